```python
import math
import jax, jax.numpy as jnp
from jax import lax
import numpy as np

D_MODEL = 2048
BATCH = 8
SEQ = 2048
DEPTH = 2

HYENA_WIDTH = 1024
HYENA_ORDER = 2
HYENA_POS_BANDS = 16
HYENA_POS_DIM = 2 * HYENA_POS_BANDS + 1
HYENA_FILTER_HIDDEN = 64
HYENA_DECAY_TARGET = 1e-2
HYENA_FAST_DECAY_PCT = 0.3
HYENA_SLOW_DECAY_PCT = 1.5
SHORT_CONV = 3

MLSTM_HEADS = 8
MLSTM_HEAD_DIM = 128
MLSTM_WIDTH = MLSTM_HEADS * MLSTM_HEAD_DIM
MLSTM_CHUNK = 64

N_BRANCHES = 2
OFF_QK = (HYENA_ORDER + 1) * HYENA_WIDTH
OFF_V = OFF_QK + 2 * MLSTM_WIDTH
OFF_O = OFF_V + MLSTM_WIDTH
OFF_GATES = OFF_O + MLSTM_WIDTH
OFF_BR = OFF_GATES + 4 * MLSTM_HEADS
IN_WIDTH = OFF_BR + N_BRANCHES * D_MODEL

D_FF = 5632
N_EXPERTS = 8
TOP_K = 2
MOE_BLOCK = 512
N_DENSE = (DEPTH + 1) // 2
N_MOE = DEPTH // 2
EPS = 1e-6

kernel_name = 'hybrid_hyena_mlstm_moe_encoder'


def rms_norm(x, g):
    xf = x.astype(jnp.float32)
    y = xf * lax.rsqrt(jnp.mean(xf * xf, axis=-1, keepdims=True) + EPS)
    return (y * g.astype(jnp.float32)).astype(x.dtype)


def short_conv(u, w, b):
    pad = SHORT_CONV // 2
    s = u.shape[1]
    up = jnp.pad(u, ((0, 0), (pad, pad), (0, 0)))
    out = b
    for j in range(SHORT_CONV):
        out = out + up[:, j:j + s] * w[j]
    return out


def swiglu(h, wg, wu, wd):
    return (jax.nn.silu(h @ wg) * (h @ wu)) @ wd


def hyena_filter_spectrum(L, w1, b1, w2, b2, w3, freq, decay):
    f32 = jnp.float32
    t = jnp.linspace(0.0, 1.0, L, dtype=f32)[:, None]
    n = jnp.arange(L, dtype=f32)[:, None]
    bands = jnp.linspace(1e-4, HYENA_POS_BANDS - 1, HYENA_POS_BANDS, dtype=f32)[None, :]
    ang = (2.0 * math.pi / L) * n * bands
    z = jnp.concatenate([t, jnp.cos(ang), -jnp.sin(ang)], axis=-1)
    fr = freq.astype(f32)
    hid = jnp.sin(fr * (z @ w1.astype(f32) + b1.astype(f32)))
    hid = jnp.sin(fr * (hid @ w2.astype(f32) + b2.astype(f32)))
    filt = (hid @ w3.astype(f32)) * jnp.exp(-t * jnp.abs(decay.astype(f32)))
    filt = filt.reshape(L, HYENA_ORDER, 2, HYENA_WIDTH)
    fwd, bwd = filt[:, :, 0], filt[:, :, 1]
    two_sided = jnp.concatenate(
        [fwd.at[0].add(bwd[0]), jnp.zeros_like(fwd[:1]), bwd[:0:-1]], axis=0)
    two_sided = two_sided / jnp.sum(jnp.abs(two_sided), axis=0, keepdims=True)
    return jnp.fft.rfft(two_sided, axis=0)


def hyena_branch(u, conv_w, conv_b, w1, b1, w2, b2, w3, freq, decay, skip):
    L = u.shape[1]
    uc = short_conv(u, conv_w, conv_b).astype(jnp.float32)
    v, *gates = jnp.split(uc, HYENA_ORDER + 1, axis=-1)
    spec = hyena_filter_spectrum(L, w1, b1, w2, b2, w3, freq, decay)
    z = v
    for o in range(HYENA_ORDER):
        zf = jnp.fft.rfft(z, n=2 * L, axis=1)
        conv = jnp.fft.irfft(zf * spec[:, o][None], n=2 * L, axis=1)[:, :L]
        z = gates[o] * (conv + skip[o].astype(jnp.float32) * z)
    return z


def mlstm_chunkwise(q, k, v, i_pre, f_pre):
    B, H, S, dk = q.shape
    dv = v.shape[-1]
    nc = S // MLSTM_CHUNK
    q = q.reshape(B, H, nc, MLSTM_CHUNK, dk)
    k = k.reshape(B, H, nc, MLSTM_CHUNK, dk)
    v = v.reshape(B, H, nc, MLSTM_CHUNK, dv)
    log_f = jax.nn.log_sigmoid(f_pre).reshape(B, H, nc, MLSTM_CHUNK)
    log_i = i_pre.reshape(B, H, nc, MLSTM_CHUNK)
    b = jnp.cumsum(log_f, axis=-1)
    g = b[..., -1]
    a = g[..., None] - b + log_i
    m_loc = jnp.max(a, axis=-1)
    wgt = jnp.exp(a - m_loc[..., None])
    kv = jnp.einsum('bhcs,bhcsv,bhcsk->bhcvk', wgt, v, k)
    ks = jnp.einsum('bhcs,bhcsk->bhck', wgt, k)

    def step(carry, xs):
        C, nvec, m = carry
        kv_c, ks_c, g_c, ml_c = xs
        m_new = jnp.maximum(g_c + m, ml_c)
        a_old = jnp.exp(g_c + m - m_new)
        a_new = jnp.exp(ml_c - m_new)
        C_new = a_old[..., None, None] * C + a_new[..., None, None] * kv_c
        n_new = a_old[..., None] * nvec + a_new[..., None] * ks_c
        return (C_new, n_new, m_new), (C, nvec, m)

    init = (jnp.zeros((B, H, dv, dk), jnp.float32), jnp.zeros((B, H, dk), jnp.float32),
            jnp.zeros((B, H), jnp.float32))
    xs = (jnp.moveaxis(kv, 2, 0), jnp.moveaxis(ks, 2, 0), jnp.moveaxis(g, 2, 0),
          jnp.moveaxis(m_loc, 2, 0))
    _, (C_prev, n_prev, m_prev) = lax.scan(step, init, xs)
    C_prev = jnp.moveaxis(C_prev, 0, 2)
    n_prev = jnp.moveaxis(n_prev, 0, 2)
    m_prev = jnp.moveaxis(m_prev, 0, 2)

    mask = jnp.tril(jnp.ones((MLSTM_CHUNK, MLSTM_CHUNK), dtype=bool))
    dmat = jnp.where(mask, b[..., :, None] - b[..., None, :] + log_i[..., None, :], -jnp.inf)
    inter = b + m_prev[..., None]
    m_j = jnp.maximum(inter, jnp.max(dmat, axis=-1))
    scores = jnp.einsum('bhcjk,bhcsk->bhcjs', q, k) * jnp.exp(dmat - m_j[..., None])
    s_inter = jnp.exp(inter - m_j)
    num = jnp.einsum('bhcjs,bhcsv->bhcjv', scores, v) + s_inter[..., None] * jnp.einsum('bhcvk,bhcjk->bhcjv', C_prev, q)
    den = jnp.sum(scores, axis=-1) + s_inter * jnp.einsum('bhck,bhcjk->bhcj', n_prev, q)
    h = num / jnp.maximum(jnp.abs(den), jnp.exp(-m_j))[..., None]
    return h.reshape(B, H, S, dv)


def mlstm_branch(p_qk, p_v, p_o, p_gates, conv_w, conv_b, norm_g):
    f32 = jnp.float32
    B, S, _ = p_v.shape
    qk = jax.nn.silu(short_conv(p_qk, conv_w, conv_b)).astype(f32)

    def heads(t):
        return t.reshape(B, S, MLSTM_HEADS, MLSTM_HEAD_DIM).transpose(0, 2, 1, 3)

    q = heads(qk[..., :MLSTM_WIDTH]) * (MLSTM_HEAD_DIM ** -0.5)
    k = heads(qk[..., MLSTM_WIDTH:])
    v = heads(p_v.astype(f32))
    gt = p_gates.astype(f32).reshape(B, S, 4, MLSTM_HEADS).transpose(2, 0, 3, 1)

    def rev(t):
        return jnp.flip(t, axis=2)

    h_fwd = mlstm_chunkwise(q, k, v, gt[0], gt[1])
    h_bwd = rev(mlstm_chunkwise(rev(q), rev(k), rev(v), rev(gt[2]), rev(gt[3])))
    h = (h_fwd + h_bwd).transpose(0, 2, 1, 3)
    h = h * lax.rsqrt(jnp.mean(h * h, axis=-1, keepdims=True) + EPS)
    h = h.reshape(B, S, MLSTM_WIDTH) * norm_g.astype(f32)
    return h * jax.nn.sigmoid(p_o.astype(f32))


def hybrid_mixer(h, w_in, b_in, hy_conv_w, hy_conv_b, hy_w1, hy_b1, hy_w2, hy_b2, hy_w3,
                 hy_freq, hy_decay, hy_skip, ml_conv_w, ml_conv_b, ml_norm_g, w_a, w_b, w_o):
    p = h @ w_in + b_in
    y_hy = hyena_branch(p[..., :OFF_QK], hy_conv_w, hy_conv_b, hy_w1, hy_b1, hy_w2, hy_b2,
                        hy_w3, hy_freq, hy_decay, hy_skip).astype(h.dtype)
    y_ml = mlstm_branch(p[..., OFF_QK:OFF_V], p[..., OFF_V:OFF_O], p[..., OFF_O:OFF_GATES],
                        p[..., OFF_GATES:OFF_BR], ml_conv_w, ml_conv_b, ml_norm_g).astype(h.dtype)
    gate_hy, gate_ml = jnp.split(jax.nn.sigmoid(p[..., OFF_BR:]), N_BRANCHES, axis=-1)
    merged = gate_hy * (y_hy @ w_a) + gate_ml * (y_ml @ w_b)
    return merged @ w_o


def moe_swiglu(h, router_w, router_b, w_gate, w_up, w_down):
    Bt, S, D = h.shape
    xt = h.reshape(-1, D)
    N = xt.shape[0]
    NK = N * TOP_K
    logits = (xt @ router_w).astype(jnp.float32) + router_b.astype(jnp.float32)
    top_logit, top_e = lax.top_k(logits, TOP_K)
    gate = jax.nn.softmax(top_logit, axis=-1)
    flat_e = top_e.reshape(-1)
    flat_tok = jnp.arange(NK, dtype=jnp.int32) // TOP_K
    order = jnp.argsort(flat_e)
    e_sorted = flat_e[order]
    tok_sorted = flat_tok[order]
    counts = jnp.bincount(flat_e, length=N_EXPERTS)
    padded = (counts + MOE_BLOCK - 1) // MOE_BLOCK * MOE_BLOCK
    start = jnp.cumsum(counts) - counts
    p_end = jnp.cumsum(padded)
    p_start = p_end - padded
    dest = p_start[e_sorted] + jnp.arange(NK, dtype=jnp.int32) - start[e_sorted]
    n_blocks = -(-NK // MOE_BLOCK) + N_EXPERTS
    slot_tok = jnp.full((n_blocks * MOE_BLOCK,), N, dtype=jnp.int32).at[dest].set(tok_sorted)
    block_e = jnp.minimum(jnp.searchsorted(p_end, jnp.arange(n_blocks) * MOE_BLOCK, side='right'),
                          N_EXPERTS - 1)
    x_pad = jnp.concatenate([xt, jnp.zeros((1, D), xt.dtype)], axis=0)
    xb = x_pad[slot_tok].reshape(n_blocks, MOE_BLOCK, D)

    def expert_block(args):
        xblk, e = args
        return swiglu(xblk, w_gate[e], w_up[e], w_down[e])

    yb = lax.map(expert_block, (xb, block_e)).reshape(-1, D)
    y_assign = yb[dest] * gate.reshape(-1)[order][:, None].astype(yb.dtype)
    out = jnp.zeros_like(xt).at[tok_sorted].add(y_assign)
    return out.reshape(Bt, S, D)


def setup_inputs(seed: int = 0) -> dict:
    key = jax.random.key(seed)
    ks = jax.random.split(key, 40)
    f32 = jnp.float32
    D, Dh, Dm, H = D_MODEL, HYENA_WIDTH, MLSTM_WIDTH, MLSTM_HEADS

    def nrm(k, shape, scale):
        return jax.random.normal(k, shape, f32) * scale

    x = nrm(ks[0], (BATCH, SEQ, D), 1.0)
    mix_norm_g = 1.0 + nrm(ks[1], (DEPTH, D), 0.1)
    mix_w_in = nrm(ks[2], (DEPTH, D, IN_WIDTH), D ** -0.5)
    mix_b_in = nrm(ks[3], (DEPTH, IN_WIDTH), 0.01)
    f_bias = jnp.linspace(3.0, 6.0, H, dtype=f32)[None, :] + nrm(ks[4], (DEPTH, H), 0.1)
    mix_b_in = mix_b_in.at[:, OFF_GATES + H:OFF_GATES + 2 * H].set(f_bias)
    mix_b_in = mix_b_in.at[:, OFF_GATES + 3 * H:OFF_GATES + 4 * H].set(f_bias[:, ::-1])
    hy_conv_w = nrm(ks[5], (DEPTH, SHORT_CONV, (HYENA_ORDER + 1) * Dh), SHORT_CONV ** -0.5)
    hy_conv_b = nrm(ks[6], (DEPTH, (HYENA_ORDER + 1) * Dh), 0.01)
    hy_filt_w1 = nrm(ks[7], (DEPTH, HYENA_POS_DIM, HYENA_FILTER_HIDDEN), HYENA_POS_DIM ** -0.5)
    hy_filt_b1 = nrm(ks[8], (DEPTH, HYENA_FILTER_HIDDEN), 0.1)
    hy_filt_w2 = nrm(ks[9], (DEPTH, HYENA_FILTER_HIDDEN, HYENA_FILTER_HIDDEN), HYENA_FILTER_HIDDEN ** -0.5)
    hy_filt_b2 = nrm(ks[10], (DEPTH, HYENA_FILTER_HIDDEN), 0.1)
    hy_filt_w3 = nrm(ks[11], (DEPTH, HYENA_FILTER_HIDDEN, 2 * HYENA_ORDER * Dh), HYENA_FILTER_HIDDEN ** -0.5)
    hy_filt_freq = 1.0 + nrm(ks[12], (DEPTH, HYENA_FILTER_HIDDEN), 0.1)
    max_decay = math.log(HYENA_DECAY_TARGET) / HYENA_FAST_DECAY_PCT
    min_decay = math.log(HYENA_DECAY_TARGET) / HYENA_SLOW_DECAY_PCT
    base_decay = jnp.tile(jnp.linspace(min_decay, max_decay, Dh, dtype=f32), 2 * HYENA_ORDER)
    hy_filt_decay = base_decay[None, :] + nrm(ks[13], (DEPTH, 2 * HYENA_ORDER * Dh), 0.1)
    hy_skip = nrm(ks[14], (DEPTH, HYENA_ORDER, Dh), 1.0)
    ml_conv_w = nrm(ks[15], (DEPTH, SHORT_CONV, 2 * Dm), SHORT_CONV ** -0.5)
    ml_conv_b = nrm(ks[16], (DEPTH, 2 * Dm), 0.01)
    ml_norm_g = 1.0 + nrm(ks[17], (DEPTH, Dm), 0.1)
    mix_w_a = nrm(ks[18], (DEPTH, Dh, D), Dh ** -0.5)
    mix_w_b = nrm(ks[19], (DEPTH, Dm, D), Dm ** -0.5)
    mix_w_o = nrm(ks[20], (DEPTH, D, D), D ** -0.5)
    ffn_norm_g = 1.0 + nrm(ks[21], (DEPTH, D), 0.1)
    dense_w_gate = nrm(ks[22], (N_DENSE, D, D_FF), D ** -0.5)
    dense_w_up = nrm(ks[23], (N_DENSE, D, D_FF), D ** -0.5)
    dense_w_down = nrm(ks[24], (N_DENSE, D_FF, D), D_FF ** -0.5)
    moe_router_w = nrm(ks[25], (N_MOE, D, N_EXPERTS), D ** -0.5)
    moe_router_b = nrm(ks[26], (N_MOE, N_EXPERTS), 0.01)
    moe_w_gate = nrm(ks[27], (N_MOE, N_EXPERTS, D, D_FF), D ** -0.5)
    moe_w_up = nrm(ks[28], (N_MOE, N_EXPERTS, D, D_FF), D ** -0.5)
    moe_w_down = nrm(ks[29], (N_MOE, N_EXPERTS, D_FF, D), D_FF ** -0.5)
    final_norm_g = 1.0 + nrm(ks[30], (D,), 0.1)
    return {'x': x, 'mix_norm_g': mix_norm_g, 'mix_w_in': mix_w_in, 'mix_b_in': mix_b_in,
            'hy_conv_w': hy_conv_w, 'hy_conv_b': hy_conv_b, 'hy_filt_w1': hy_filt_w1,
            'hy_filt_b1': hy_filt_b1, 'hy_filt_w2': hy_filt_w2, 'hy_filt_b2': hy_filt_b2,
            'hy_filt_w3': hy_filt_w3, 'hy_filt_freq': hy_filt_freq, 'hy_filt_decay': hy_filt_decay,
            'hy_skip': hy_skip, 'ml_conv_w': ml_conv_w, 'ml_conv_b': ml_conv_b,
            'ml_norm_g': ml_norm_g, 'mix_w_a': mix_w_a, 'mix_w_b': mix_w_b, 'mix_w_o': mix_w_o,
            'ffn_norm_g': ffn_norm_g, 'dense_w_gate': dense_w_gate, 'dense_w_up': dense_w_up,
            'dense_w_down': dense_w_down, 'moe_router_w': moe_router_w,
            'moe_router_b': moe_router_b, 'moe_w_gate': moe_w_gate, 'moe_w_up': moe_w_up,
            'moe_w_down': moe_w_down, 'final_norm_g': final_norm_g}


def reference(x, mix_norm_g, mix_w_in, mix_b_in, hy_conv_w, hy_conv_b, hy_filt_w1, hy_filt_b1,
              hy_filt_w2, hy_filt_b2, hy_filt_w3, hy_filt_freq, hy_filt_decay, hy_skip,
              ml_conv_w, ml_conv_b, ml_norm_g, mix_w_a, mix_w_b, mix_w_o, ffn_norm_g,
              dense_w_gate, dense_w_up, dense_w_down, moe_router_w, moe_router_b,
              moe_w_gate, moe_w_up, moe_w_down, final_norm_g):
    for layer in range(DEPTH):
        h = rms_norm(x, mix_norm_g[layer])
        x = x + hybrid_mixer(h, mix_w_in[layer], mix_b_in[layer], hy_conv_w[layer], hy_conv_b[layer],
                             hy_filt_w1[layer], hy_filt_b1[layer], hy_filt_w2[layer], hy_filt_b2[layer],
                             hy_filt_w3[layer], hy_filt_freq[layer], hy_filt_decay[layer], hy_skip[layer],
                             ml_conv_w[layer], ml_conv_b[layer], ml_norm_g[layer],
                             mix_w_a[layer], mix_w_b[layer], mix_w_o[layer])
        h = rms_norm(x, ffn_norm_g[layer])
        j = layer // 2
        if layer % 2 == 0:
            x = x + swiglu(h, dense_w_gate[j], dense_w_up[j], dense_w_down[j])
        else:
            x = x + moe_swiglu(h, moe_router_w[j], moe_router_b[j], moe_w_gate[j], moe_w_up[j], moe_w_down[j])
    return rms_norm(x, final_norm_g)
```

```python
import functools
import math

import jax
import jax.numpy as jnp
from jax import lax
from jax.experimental import pallas as pl
from jax.experimental.pallas import tpu as pltpu

F32, BF16, I32 = jnp.float32, jnp.bfloat16, jnp.int32
HIGHEST = lax.Precision.HIGHEST

HYENA_ORDER = 2
HYENA_POS_BANDS = 16
MLSTM_HEADS = 8
N_EXPERTS = 8
TOP_K = 2
EPS = 1e-6

V7X_VMEM_BYTES = 64 * 1024 * 1024
LANES = 128
MXU_DIM = 256

MLSTM_CHUNK = 256
HYENA_CT = MXU_DIM
HYENA_RC = 512
MOE_BLOCK = 512
GATHER_ROWS = 1024


def _cparams(semantics, vmem_mib):
    assert vmem_mib * 1024 * 1024 <= V7X_VMEM_BYTES
    return pltpu.CompilerParams(dimension_semantics=semantics, vmem_limit_bytes=vmem_mib * 1024 * 1024)


def _const_spec(shape):
    nd = len(shape)
    return pl.BlockSpec(shape, lambda *_: (0,) * nd, pipeline_mode=pl.Buffered(1))


def _rms(x, g):
    return x * lax.rsqrt(jnp.mean(x * x, axis=-1, keepdims=True) + EPS) * g


def _shift_rows(u, direction):
    n = u.shape[0]
    row = lax.broadcasted_iota(I32, u.shape, 0)
    if direction > 0:
        return jnp.where(row == 0, 0.0, pltpu.roll(u, 1, axis=0))
    return jnp.where(row == n - 1, 0.0, pltpu.roll(u, n - 1, axis=0))


def _short_conv(u, w, b):
    return b + _shift_rows(u, 1) * w[0:1] + u * w[1:2] + _shift_rows(u, -1) * w[2:3]


def _log_sigmoid(x):
    return jnp.minimum(x, 0.0) - jnp.log1p(jnp.exp(-jnp.abs(x)))


def _norm_gates_kernel(x_ref, g_ref, wg_ref, bg_ref, h_ref, gates_ref):
    hn = _rms(x_ref[...], g_ref[...])
    h_ref[...] = hn.astype(h_ref.dtype)
    gates_ref[...] = jnp.dot(hn, wg_ref[...], precision=HIGHEST, preferred_element_type=F32) + bg_ref[...]


def _norm_gates(x2, g, w_gates, b_gates, tm=512):
    n, d = x2.shape
    ng = w_gates.shape[1]
    return pl.pallas_call(
        _norm_gates_kernel,
        grid=(n // tm,),
        in_specs=[pl.BlockSpec((tm, d), lambda i: (i, 0)),
                  pl.BlockSpec((1, d), lambda i: (0, 0)),
                  pl.BlockSpec((d, ng), lambda i: (0, 0)),
                  pl.BlockSpec((1, ng), lambda i: (0, 0))],
        out_specs=[pl.BlockSpec((tm, d), lambda i: (i, 0)),
                   pl.BlockSpec((tm, ng), lambda i: (i, 0))],
        out_shape=[jax.ShapeDtypeStruct((n, d), BF16), jax.ShapeDtypeStruct((n, ng), F32)],
        compiler_params=_cparams(("parallel",), 32),
        name="norm_gates",
    )(x2, g, w_gates, b_gates)


def _matmul_bias_kernel(a_ref, w_ref, b_ref, o_ref):
    acc = jnp.dot(a_ref[...], w_ref[...], preferred_element_type=F32) + b_ref[...]
    o_ref[...] = acc.astype(o_ref.dtype)


def _matmul_bias(a, w, b, out_dtype, tm=1024, tn=1024):
    m, k = a.shape
    nc = w.shape[1]
    return pl.pallas_call(
        _matmul_bias_kernel,
        grid=(m // tm, nc // tn),
        in_specs=[pl.BlockSpec((tm, k), lambda i, j: (i, 0)),
                  pl.BlockSpec((k, tn), lambda i, j: (0, j)),
                  pl.BlockSpec((1, tn), lambda i, j: (0, j))],
        out_specs=pl.BlockSpec((tm, tn), lambda i, j: (i, j)),
        out_shape=jax.ShapeDtypeStruct((m, nc), out_dtype),
        compiler_params=_cparams(("parallel", "arbitrary"), 40),
        name="in_proj",
    )(a, w, b)


def _dft_tables(length):
    k = jnp.arange(length, dtype=I32)
    kn = (k[:, None] * k[None, :]) % (2 * length)
    ang = kn.astype(F32) * (math.pi / length)
    return jnp.cos(ang).astype(BF16), jnp.sin(ang).astype(BF16)


def _hyena_positions(length):
    t = jnp.linspace(0.0, 1.0, length, dtype=F32)[:, None]
    n = jnp.arange(length, dtype=F32)[:, None]
    bands = jnp.linspace(1e-4, HYENA_POS_BANDS - 1, HYENA_POS_BANDS, dtype=F32)[None, :]
    ang = (2.0 * math.pi / length) * n * bands
    return jnp.concatenate([t, jnp.cos(ang), -jnp.sin(ang)], axis=-1)


def _hyena_spectrum_kernel(z_ref, w1_ref, b1_ref, w2_ref, b2_ref, fr_ref, w3f_ref, w3b_ref,
                           decf_ref, decb_ref, c_ref, s_ref, h_ref, nyq_ref):
    length = c_ref.shape[0]
    dot_hi = functools.partial(jnp.dot, precision=HIGHEST, preferred_element_type=F32)
    z = z_ref[...]
    fr = fr_ref[...]
    hid = jnp.sin(fr * (dot_hi(z, w1_ref[...]) + b1_ref[...]))
    hid = jnp.sin(fr * (dot_hi(hid, w2_ref[...]) + b2_ref[...]))
    t = z[:, 0:1]
    ff = dot_hi(hid, w3f_ref[...]) * jnp.exp(-t * jnp.abs(decf_ref[...]))
    fb = dot_hi(hid, w3b_ref[...]) * jnp.exp(-t * jnp.abs(decb_ref[...]))
    row = lax.broadcasted_iota(I32, (length, 1), 0)
    l1 = jnp.sum(jnp.where(row == 0, jnp.abs(ff + fb), jnp.abs(ff) + jnp.abs(fb)), axis=0, keepdims=True)
    inv = 1.0 / l1
    even = (ff + fb) * inv
    odd = (ff - fb) * inv
    sign = jnp.where(row % 2 == 0, 1.0, -1.0)
    scale = jnp.where(row == 0, 0.5 / length, 1.0 / length)
    h_ref[0] = jnp.dot(c_ref[...], even.astype(BF16), preferred_element_type=F32) * scale
    h_ref[1] = -jnp.dot(s_ref[...], odd.astype(BF16), preferred_element_type=F32) * scale
    nyq_ref[...] = jnp.sum(sign * even, axis=0, keepdims=True) * (0.5 / length)


def _hyena_spectrum(z, w1, b1, w2, b2, freq, w3, decay, ctab, stab, width, ct):
    length = ctab.shape[0]
    kp = w1.shape[1]
    nct = width // ct
    col_f = lambda o, c: (0, o * 2 * nct + c)
    col_b = lambda o, c: (0, o * 2 * nct + nct + c)
    return pl.pallas_call(
        _hyena_spectrum_kernel,
        grid=(HYENA_ORDER, nct),
        in_specs=[_const_spec(z.shape), _const_spec(w1.shape), _const_spec(b1.shape),
                  _const_spec(w2.shape), _const_spec(b2.shape), _const_spec(freq.shape),
                  pl.BlockSpec((kp, ct), col_f), pl.BlockSpec((kp, ct), col_b),
                  pl.BlockSpec((1, ct), col_f), pl.BlockSpec((1, ct), col_b),
                  _const_spec(ctab.shape), _const_spec(stab.shape)],
        out_specs=[pl.BlockSpec((None, 2, length, ct), lambda o, c: (o, 0, 0, c)),
                   pl.BlockSpec((None, 1, ct), lambda o, c: (o, 0, c))],
        out_shape=[jax.ShapeDtypeStruct((HYENA_ORDER, 2, length, width), F32),
                   jax.ShapeDtypeStruct((HYENA_ORDER, 1, width), F32)],
        compiler_params=_cparams(("parallel", "parallel"), 48),
        name="hyena_spectrum",
    )(z, w1, b1, w2, b2, freq, w3, w3, decay, decay, ctab, stab)


def _hyena_conv_kernel(pv_ref, p1_ref, p2_ref, wv_ref, w1_ref, w2_ref, bv_ref, b1_ref, b2_ref,
                       h_ref, nyq_ref, skip_ref, c_ref, s_ref, y_ref,
                       z_s, g1_s, g2_s, zb_s, a_s, b_s, *, rc):
    length = c_ref.shape[0]
    z_s[...] = _short_conv(pv_ref[...].astype(F32), wv_ref[...], bv_ref[...])
    g1_s[...] = _short_conv(p1_ref[...].astype(F32), w1_ref[...], b1_ref[...])
    g2_s[...] = _short_conv(p2_ref[...].astype(F32), w2_ref[...], b2_ref[...])
    gates = (g1_s, g2_s)
    row = lax.broadcasted_iota(I32, (length, 1), 0)
    sign = jnp.where(row % 2 == 0, 1.0, -1.0)
    for o in range(HYENA_ORDER):
        z = z_s[...]
        zb_s[...] = z.astype(BF16)
        nyq = jnp.sum(sign * z, axis=0, keepdims=True) * nyq_ref[o]
        for r in range(length // rc):
            rows = slice(r * rc, (r + 1) * rc)
            re = jnp.dot(c_ref[rows, :], zb_s[...], preferred_element_type=F32)
            pim = jnp.dot(s_ref[rows, :], zb_s[...], preferred_element_type=F32)
            hre = h_ref[o, 0, rows, :]
            him = h_ref[o, 1, rows, :]
            a_s[rows, :] = (hre * re + him * pim).astype(BF16)
            b_s[rows, :] = (hre * pim - him * re).astype(BF16)
        for r in range(length // rc):
            rows = slice(r * rc, (r + 1) * rc)
            conv = (jnp.dot(c_ref[rows, :], a_s[...], preferred_element_type=F32)
                    + jnp.dot(s_ref[rows, :], b_s[...], preferred_element_type=F32)
                    + sign[rows] * nyq)
            znew = gates[o][rows, :] * (conv + skip_ref[o] * z_s[rows, :])
            if o == HYENA_ORDER - 1:
                y_ref[rows, :] = znew.astype(y_ref.dtype)
            else:
                z_s[rows, :] = znew


def _hyena_conv(p3, col0, conv_w, conv_b, hspec, hnyq, skip, ctab, stab, width, ct, rc):
    bsz, length, _ = p3.shape
    nct = width // ct
    c0 = col0 // ct

    def pspec(part):
        return pl.BlockSpec((None, length, ct), lambda c, b: (b, 0, c0 + part * nct + c))

    def wspec(rows, part):
        return pl.BlockSpec((rows, ct), lambda c, b: (0, part * nct + c))

    scratch = [pltpu.VMEM((length, ct), F32)] * 3 + [pltpu.VMEM((length, ct), BF16)] * 3
    return pl.pallas_call(
        functools.partial(_hyena_conv_kernel, rc=rc),
        grid=(nct, bsz),
        in_specs=[pspec(0), pspec(1), pspec(2),
                  wspec(3, 0), wspec(3, 1), wspec(3, 2), wspec(1, 0), wspec(1, 1), wspec(1, 2),
                  pl.BlockSpec((HYENA_ORDER, 2, length, ct), lambda c, b: (0, 0, 0, c),
                               pipeline_mode=pl.Buffered(1)),
                  pl.BlockSpec((HYENA_ORDER, 1, ct), lambda c, b: (0, 0, c)),
                  pl.BlockSpec((HYENA_ORDER, 1, ct), lambda c, b: (0, 0, c)),
                  _const_spec(ctab.shape), _const_spec(stab.shape)],
        out_specs=pl.BlockSpec((None, length, ct), lambda c, b: (b, 0, c)),
        out_shape=jax.ShapeDtypeStruct((bsz, length, width), BF16),
        scratch_shapes=scratch,
        compiler_params=_cparams(("parallel", "parallel"), 56),
        name="hyena_conv",
    )(p3, p3, p3, conv_w, conv_w, conv_w, conv_b, conv_b, conv_b, hspec, hnyq, skip, ctab, stab)


def _gate_prep_kernel(gc_ref, gr_ref, oc_ref, or_ref, *, chunk):
    length, nch = gc_ref.shape
    nh = nch // 4
    r = lax.broadcasted_iota(I32, (chunk, chunk), 0)
    c = lax.broadcasted_iota(I32, (chunk, chunk), 1)
    lower = (c <= r).astype(F32)
    upper = (c >= r).astype(F32)
    dot_hi = functools.partial(jnp.dot, precision=HIGHEST, preferred_element_type=F32)
    ccol = lax.broadcasted_iota(I32, (chunk, nch), 1)
    crow = lax.broadcasted_iota(I32, (nch, chunk), 0)
    for ch in range(length // chunk):
        rows = slice(ch * chunk, (ch + 1) * chunk)
        x = gc_ref[rows, :]
        lf = _log_sigmoid(x)
        pre = dot_hi(lower, lf)
        suf = dot_hi(upper, lf)
        oc_ref[rows, :] = jnp.where((ccol // nh) % 2 == 0, x, jnp.where(ccol < 2 * nh, pre, suf))
        xr = gr_ref[:, rows]
        lfr = _log_sigmoid(xr)
        pre_r = dot_hi(lfr, upper)
        suf_r = dot_hi(lfr, lower)
        or_ref[:, rows] = jnp.where((crow // nh) % 2 == 0, xr, jnp.where(crow < 2 * nh, pre_r, suf_r))


def _gate_prep(gates3, chunk):
    bsz, length, nch = gates3.shape
    gates_t = jnp.swapaxes(gates3, 1, 2)
    return pl.pallas_call(
        functools.partial(_gate_prep_kernel, chunk=chunk),
        grid=(bsz,),
        in_specs=[pl.BlockSpec((None, length, nch), lambda b: (b, 0, 0)),
                  pl.BlockSpec((None, nch, length), lambda b: (b, 0, 0))],
        out_specs=[pl.BlockSpec((None, length, nch), lambda b: (b, 0, 0)),
                   pl.BlockSpec((None, nch, length), lambda b: (b, 0, 0))],
        out_shape=[jax.ShapeDtypeStruct((bsz, length, nch), F32),
                   jax.ShapeDtypeStruct((bsz, nch, length), F32)],
        compiler_params=_cparams(("parallel",), 32),
        name="mlstm_gate_prep",
    )(gates3, gates_t)


def _mlstm_kernel(pq_ref, pk_ref, pv_ref, po_ref, cwq_ref, cwk_ref, cbq_ref, cbk_ref, ng_ref,
                  gcol_ref, grow_ref, y_ref, q_s, k_s, hf_s, *, chunk):
    length, dk = pq_ref.shape
    nc = length // chunk

    def conv_silu(p_ref, w_ref, b_ref):
        c = _short_conv(p_ref[...].astype(F32), w_ref[...], b_ref[...])
        return c * jax.nn.sigmoid(c)

    q_s[...] = (conv_silu(pq_ref, cwq_ref, cbq_ref) * (dk ** -0.5)).astype(BF16)
    k_s[...] = conv_silu(pk_ref, cwk_ref, cbk_ref).astype(BF16)

    row_i = lax.broadcasted_iota(I32, (chunk, chunk), 0)
    col_i = lax.broadcasted_iota(I32, (chunk, chunk), 1)

    def run_direction(li_idx, b_idx, reverse, emit):
        mask = (col_i >= row_i) if reverse else (col_i <= row_i)

        def body(step, carry):
            state, nvec, m = carry
            c = (nc - 1 - step) if reverse else step
            rows = pl.ds(pl.multiple_of(c * chunk, chunk), chunk)
            q = q_s[rows, :]
            k = k_s[rows, :]
            v = pv_ref[rows, :]
            gc = gcol_ref[rows, :]
            li_col = gc[:, li_idx:li_idx + 1]
            b_col = gc[:, b_idx:b_idx + 1]
            gr = grow_ref[c]
            li_row = gr[li_idx:li_idx + 1, :]
            b_row = gr[b_idx:b_idx + 1, :]
            g = b_col[0:1, :] if reverse else b_col[chunk - 1:chunk, :]
            a_col = g - b_col + li_col
            m_loc = jnp.max(a_col, axis=0, keepdims=True)
            kw = k.astype(F32) * jnp.exp(a_col - m_loc)
            ks = jnp.sum(kw, axis=0, keepdims=True)
            kv_t = lax.dot_general(kw.astype(BF16), v, (((0,), (0,)), ((), ())),
                                   preferred_element_type=F32)
            dmat = jnp.where(mask, b_col - b_row + li_row, -jnp.inf)
            inter = b_col + m
            m_j = jnp.maximum(inter, jnp.max(dmat, axis=1, keepdims=True))
            qk = lax.dot_general(q, k, (((1,), (1,)), ((), ())), preferred_element_type=F32)
            sc = qk * jnp.exp(dmat - m_j)
            s_inter = jnp.exp(inter - m_j)
            num = (jnp.dot(sc.astype(BF16), v, preferred_element_type=F32)
                   + s_inter * jnp.dot(q, state.astype(BF16), preferred_element_type=F32))
            den = (jnp.sum(sc, axis=1, keepdims=True)
                   + s_inter * jnp.sum(q.astype(F32) * nvec, axis=1, keepdims=True))
            emit(rows, num / jnp.maximum(jnp.abs(den), jnp.exp(-m_j)))
            m_new = jnp.maximum(g + m, m_loc)
            a_old = jnp.exp(g + m - m_new)
            a_new = jnp.exp(m_loc - m_new)
            return a_old * state + a_new * kv_t, a_old * nvec + a_new * ks, m_new

        init = (jnp.zeros((dk, pv_ref.shape[1]), F32), jnp.zeros((1, dk), F32), jnp.zeros((1, 1), F32))
        lax.fori_loop(0, nc, body, init)

    def emit_fwd(rows, h):
        hf_s[rows, :] = h

    def emit_bwd(rows, h):
        ht = hf_s[rows, :] + h
        hn = _rms(ht, ng_ref[...]) * jax.nn.sigmoid(po_ref[rows, :].astype(F32))
        y_ref[rows, :] = hn.astype(y_ref.dtype)

    run_direction(0, 1, False, emit_fwd)
    run_direction(2, 3, True, emit_bwd)


def _mlstm(p3, col_q, col_k, col_v, col_o, conv_w, conv_b, norm_g, gcol, grow, chunk):
    bsz, length, _ = p3.shape
    nh, dh = MLSTM_HEADS, norm_g.shape[1] // MLSTM_HEADS
    nc = length // chunk

    def pspec(col):
        return pl.BlockSpec((None, length, dh), lambda b, h: (b, 0, col // dh + h))

    def wspec(rows, part):
        return pl.BlockSpec((rows, dh), lambda b, h: (0, part * nh + h))

    return pl.pallas_call(
        functools.partial(_mlstm_kernel, chunk=chunk),
        grid=(bsz, nh),
        in_specs=[pspec(col_q), pspec(col_k), pspec(col_v), pspec(col_o),
                  wspec(3, 0), wspec(3, 1), wspec(1, 0), wspec(1, 1),
                  pl.BlockSpec((1, dh), lambda b, h: (0, h)),
                  pl.BlockSpec((None, None, length, 4), lambda b, h: (b, h, 0, 0)),
                  pl.BlockSpec((None, None, nc, 4, chunk), lambda b, h: (b, h, 0, 0, 0))],
        out_specs=pl.BlockSpec((None, length, dh), lambda b, h: (b, 0, h)),
        out_shape=jax.ShapeDtypeStruct((bsz, length, nh * dh), BF16),
        scratch_shapes=[pltpu.VMEM((length, dh), BF16), pltpu.VMEM((length, dh), BF16),
                        pltpu.VMEM((length, dh), F32)],
        compiler_params=_cparams(("parallel", "parallel"), 32),
        name="mlstm",
    )(p3, p3, p3, p3, conv_w, conv_w, conv_b, conv_b, norm_g, gcol, grow)


def _merge_kernel(yh_ref, ym_ref, gh_ref, gm_ref, x_ref, wa_ref, wb_ref, wo_ref, o_ref):
    a = jnp.dot(yh_ref[...], wa_ref[...], preferred_element_type=F32)
    b = jnp.dot(ym_ref[...], wb_ref[...], preferred_element_type=F32)
    t = jax.nn.sigmoid(gh_ref[...].astype(F32)) * a + jax.nn.sigmoid(gm_ref[...].astype(F32)) * b
    o_ref[...] = x_ref[...] + jnp.dot(t.astype(BF16), wo_ref[...], preferred_element_type=F32)


def _merge(y_hy, y_ml, p_big, x2, w_a, w_b, w_o, tm=256):
    n, d = x2.shape
    wh, wm = y_hy.shape[1], y_ml.shape[1]
    return pl.pallas_call(
        _merge_kernel,
        grid=(n // tm,),
        in_specs=[pl.BlockSpec((tm, wh), lambda i: (i, 0)),
                  pl.BlockSpec((tm, wm), lambda i: (i, 0)),
                  pl.BlockSpec((tm, d), lambda i: (i, 0)),
                  pl.BlockSpec((tm, d), lambda i: (i, 1)),
                  pl.BlockSpec((tm, d), lambda i: (i, 0)),
                  _const_spec(w_a.shape), _const_spec(w_b.shape), _const_spec(w_o.shape)],
        out_specs=pl.BlockSpec((tm, d), lambda i: (i, 0)),
        out_shape=jax.ShapeDtypeStruct((n, d), F32),
        compiler_params=_cparams(("parallel",), 48),
        name="branch_merge",
    )(y_hy, y_ml, p_big, p_big, x2, w_a, w_b, w_o)


def _dense_ffn_kernel(x_ref, g_ref, wg_ref, wu_ref, wd_ref, o_ref, h_s):
    @pl.when(pl.program_id(1) == 0)
    def _():
        x = x_ref[...]
        h_s[...] = _rms(x, g_ref[...]).astype(BF16)
        o_ref[...] = x

    h = h_s[...]
    gt = jnp.dot(h, wg_ref[...], preferred_element_type=F32)
    up = jnp.dot(h, wu_ref[...], preferred_element_type=F32)
    act = (gt * jax.nn.sigmoid(gt) * up).astype(BF16)
    o_ref[...] += jnp.dot(act, wd_ref[...], preferred_element_type=F32)


def _dense_ffn(x2, g, w_gate, w_up, w_down, tm=512, tf=512):
    n, d = x2.shape
    f = w_gate.shape[1]
    return pl.pallas_call(
        _dense_ffn_kernel,
        grid=(n // tm, f // tf),
        in_specs=[pl.BlockSpec((tm, d), lambda i, j: (i, 0)),
                  pl.BlockSpec((1, d), lambda i, j: (0, 0)),
                  pl.BlockSpec((d, tf), lambda i, j: (0, j)),
                  pl.BlockSpec((d, tf), lambda i, j: (0, j)),
                  pl.BlockSpec((tf, d), lambda i, j: (j, 0))],
        out_specs=pl.BlockSpec((tm, d), lambda i, j: (i, 0)),
        out_shape=jax.ShapeDtypeStruct((n, d), F32),
        scratch_shapes=[pltpu.VMEM((tm, d), BF16)],
        compiler_params=_cparams(("parallel", "arbitrary"), 48),
        name="dense_swiglu",
    )(x2, g, w_gate, w_up, w_down)


def _router_kernel(x_ref, g_ref, rw_ref, rb_ref, h_ref, e_ref, gate_ref):
    hn = _rms(x_ref[...], g_ref[...])
    h_ref[...] = hn
    logits = jnp.dot(hn, rw_ref[...], precision=HIGHEST, preferred_element_type=F32) + rb_ref[...]
    lane = lax.broadcasted_iota(I32, logits.shape, 1)
    logits = jnp.where(lane < N_EXPERTS, logits, -jnp.inf)
    m1 = jnp.max(logits, axis=-1, keepdims=True)
    i1 = jnp.min(jnp.where(logits == m1, lane, LANES), axis=-1, keepdims=True)
    rest = jnp.where(lane == i1, -jnp.inf, logits)
    m2 = jnp.max(rest, axis=-1, keepdims=True)
    i2 = jnp.min(jnp.where(rest == m2, lane, LANES), axis=-1, keepdims=True)
    e = jnp.exp(m2 - m1)
    slot = lax.broadcasted_iota(I32, e_ref.shape, 1)
    e_ref[...] = jnp.where(slot == 0, i1, i2)
    gate_ref[...] = jnp.where(slot == 0, 1.0 / (1.0 + e), e / (1.0 + e))


def _router(x2, g, rw_pad, rb_pad, tm=512):
    n, d = x2.shape
    return pl.pallas_call(
        _router_kernel,
        grid=(n // tm,),
        in_specs=[pl.BlockSpec((tm, d), lambda i: (i, 0)),
                  pl.BlockSpec((1, d), lambda i: (0, 0)),
                  pl.BlockSpec((d, LANES), lambda i: (0, 0)),
                  pl.BlockSpec((1, LANES), lambda i: (0, 0))],
        out_specs=[pl.BlockSpec((tm, d), lambda i: (i, 0)),
                   pl.BlockSpec((tm, TOP_K), lambda i: (i, 0)),
                   pl.BlockSpec((tm, TOP_K), lambda i: (i, 0))],
        out_shape=[jax.ShapeDtypeStruct((n, d), F32),
                   jax.ShapeDtypeStruct((n, TOP_K), I32),
                   jax.ShapeDtypeStruct((n, TOP_K), F32)],
        compiler_params=_cparams(("parallel",), 32),
        name="moe_router",
    )(x2, g, rw_pad, rb_pad)


def _gather_rows_kernel(idx_ref, src_ref, out_ref, sem, *, rows_per_step):
    base = pl.program_id(0) * rows_per_step

    def row_copy(r):
        return pltpu.make_async_copy(src_ref.at[pl.ds(idx_ref[base + r], 1)],
                                     out_ref.at[pl.ds(base + r, 1)], sem)

    def start(r, carry):
        row_copy(r).start()
        return carry

    def wait(r, carry):
        row_copy(r).wait()
        return carry

    lax.fori_loop(0, rows_per_step, start, 0)
    lax.fori_loop(0, rows_per_step, wait, 0)


def _gather_rows(src, idx, rows_per_step):
    m = idx.shape[0]
    return pl.pallas_call(
        functools.partial(_gather_rows_kernel, rows_per_step=rows_per_step),
        grid_spec=pltpu.PrefetchScalarGridSpec(
            num_scalar_prefetch=1,
            grid=(m // rows_per_step,),
            in_specs=[pl.BlockSpec(memory_space=pl.ANY)],
            out_specs=pl.BlockSpec(memory_space=pl.ANY),
            scratch_shapes=[pltpu.SemaphoreType.DMA(())],
        ),
        out_shape=jax.ShapeDtypeStruct((m, src.shape[1]), src.dtype),
        compiler_params=_cparams(("arbitrary",), 16),
        name="gather_rows",
    )(idx, src)


def _expert_ffn_kernel(be_ref, nu_ref, xb_ref, gate_ref, wg_ref, wu_ref, wd_ref, o_ref, h_s):
    i, j = pl.program_id(0), pl.program_id(1)
    used = i < nu_ref[0]

    @pl.when(j == 0)
    def _():
        h_s[...] = xb_ref[...].astype(BF16)
        o_ref[...] = jnp.zeros_like(o_ref)

    @pl.when(used)
    def _():
        h = h_s[...]
        gt = jnp.dot(h, wg_ref[...], preferred_element_type=F32)
        up = jnp.dot(h, wu_ref[...], preferred_element_type=F32)
        act = (gt * jax.nn.sigmoid(gt) * up).astype(BF16)
        o_ref[...] += jnp.dot(act, wd_ref[...], preferred_element_type=F32)

    @pl.when(jnp.logical_and(used, j == pl.num_programs(1) - 1))
    def _():
        o_ref[...] = o_ref[...] * gate_ref[...]


def _expert_ffn(xb, slot_gate, block_e, n_used, w_gate, w_up, w_down, blk, tf=512):
    ns, d = xb.shape
    f = w_gate.shape[2]
    nj = f // tf

    def jj(i, j, nu):
        return jnp.where(i < nu[0], j, nj - 1)

    return pl.pallas_call(
        _expert_ffn_kernel,
        grid_spec=pltpu.PrefetchScalarGridSpec(
            num_scalar_prefetch=2,
            grid=(ns // blk, nj),
            in_specs=[pl.BlockSpec((blk, d), lambda i, j, be, nu: (i, 0)),
                      pl.BlockSpec((blk, 1), lambda i, j, be, nu: (i, 0)),
                      pl.BlockSpec((None, d, tf), lambda i, j, be, nu: (be[i], 0, jj(i, j, nu))),
                      pl.BlockSpec((None, d, tf), lambda i, j, be, nu: (be[i], 0, jj(i, j, nu))),
                      pl.BlockSpec((None, tf, d), lambda i, j, be, nu: (be[i], jj(i, j, nu), 0))],
            out_specs=pl.BlockSpec((blk, d), lambda i, j, be, nu: (i, 0)),
            scratch_shapes=[pltpu.VMEM((blk, d), BF16)],
        ),
        out_shape=jax.ShapeDtypeStruct((ns, d), F32),
        compiler_params=_cparams(("arbitrary", "arbitrary"), 48),
        name="expert_swiglu",
    )(block_e, n_used, xb, slot_gate, w_gate, w_up, w_down)


def _combine_kernel(x_ref, y0_ref, y1_ref, g_ref, o_ref, *, final_norm):
    s = x_ref[...] + y0_ref[...] + y1_ref[...]
    o_ref[...] = _rms(s, g_ref[...]) if final_norm else s


def _combine(x2, y2, g, final_norm, tm=512):
    n, d = x2.shape
    return pl.pallas_call(
        functools.partial(_combine_kernel, final_norm=final_norm),
        grid=(n // tm,),
        in_specs=[pl.BlockSpec((tm, d), lambda i: (i, 0)),
                  pl.BlockSpec((None, tm, d), lambda i: (0, i, 0)),
                  pl.BlockSpec((None, tm, d), lambda i: (1, i, 0)),
                  pl.BlockSpec((1, d), lambda i: (0, 0))],
        out_specs=pl.BlockSpec((tm, d), lambda i: (i, 0)),
        out_shape=jax.ShapeDtypeStruct((n, d), F32),
        compiler_params=_cparams(("parallel",), 48),
        name="moe_combine",
    )(x2, y2, y2, g)


def _final_norm_kernel(x_ref, g_ref, o_ref):
    o_ref[...] = _rms(x_ref[...], g_ref[...])


def _final_norm(x2, g, tm=512):
    n, d = x2.shape
    return pl.pallas_call(
        _final_norm_kernel,
        grid=(n // tm,),
        in_specs=[pl.BlockSpec((tm, d), lambda i: (i, 0)), pl.BlockSpec((1, d), lambda i: (0, 0))],
        out_specs=pl.BlockSpec((tm, d), lambda i: (i, 0)),
        out_shape=jax.ShapeDtypeStruct((n, d), F32),
        compiler_params=_cparams(("parallel",), 32),
        name="final_norm",
    )(x2, g)


def _routing_tables(top_e, gate, blk):
    n = top_e.shape[0]
    nk = n * TOP_K
    flat_e = top_e.reshape(-1)
    onehot = (flat_e[:, None] == jnp.arange(N_EXPERTS, dtype=I32)[None, :]).astype(I32)
    rank = jnp.sum((jnp.cumsum(onehot, axis=0) - onehot) * onehot, axis=1)
    counts = jnp.sum(onehot, axis=0)
    padded = (counts + blk - 1) // blk * blk
    p_end = jnp.cumsum(padded)
    p_start = p_end - padded
    dest = (p_start[flat_e] + rank).astype(I32)
    n_blocks = -(-nk // blk) + N_EXPERTS
    flat_tok = jnp.arange(nk, dtype=I32) // TOP_K
    slot_tok = jnp.zeros((n_blocks * blk,), I32).at[dest].set(flat_tok)
    slot_gate = jnp.zeros((n_blocks * blk,), F32).at[dest].set(gate.reshape(-1))
    n_used = (p_end[-1] // blk).astype(I32)
    blocks = jnp.arange(n_blocks, dtype=I32)
    block_e = jnp.minimum(jnp.searchsorted(p_end, blocks * blk, side="right"), N_EXPERTS - 1).astype(I32)
    block_e = jnp.where(blocks < n_used, block_e, block_e[n_used - 1])
    dest_by_k = dest.reshape(n, TOP_K).T.reshape(-1)
    return slot_tok, slot_gate[:, None], block_e, n_used.reshape(1), dest_by_k


def _moe_layer(x2, norm_g, router_w, router_b, w_gate, w_up, w_down, out_g, final_norm):
    n, d = x2.shape
    rw = jnp.zeros((d, LANES), F32).at[:, :N_EXPERTS].set(router_w)
    rb = jnp.zeros((1, LANES), F32).at[0, :N_EXPERTS].set(router_b)
    h, top_e, gate = _router(x2, norm_g[None, :], rw, rb)
    slot_tok, slot_gate, block_e, n_used, dest_by_k = _routing_tables(top_e, gate, MOE_BLOCK)
    xb = _gather_rows(h, slot_tok, GATHER_ROWS)
    yb = _expert_ffn(xb, slot_gate, block_e, n_used, w_gate.astype(BF16), w_up.astype(BF16),
                     w_down.astype(BF16), MOE_BLOCK)
    y2 = _gather_rows(yb, dest_by_k, GATHER_ROWS).reshape(TOP_K, n, d)
    return _combine(x2, y2, out_g[None, :], final_norm)


def _mixer_layer(x2, bsz, length, norm_g, w_in, b_in, hy_conv_w, hy_conv_b, hy_w1, hy_b1, hy_w2, hy_b2,
                 hy_w3, hy_freq, hy_decay, hy_skip, ml_conv_w, ml_conv_b, ml_norm_g, w_a, w_b, w_o,
                 ctab, stab, zpos):
    n, d = x2.shape
    wh = hy_skip.shape[1]
    wm = ml_norm_g.shape[0]
    off_qk = (HYENA_ORDER + 1) * wh
    off_v = off_qk + 2 * wm
    off_o = off_v + wm
    off_gates = off_o + wm
    off_br = off_gates + 4 * MLSTM_HEADS
    w_cat = jnp.concatenate([w_in[:, off_br:], w_in[:, :off_gates]], axis=1).astype(BF16)
    b_cat = jnp.concatenate([b_in[off_br:], b_in[:off_gates]])[None, :]
    col_hy = 2 * d
    col_q = col_hy + off_qk
    col_k = col_q + wm
    col_v = col_hy + off_v
    col_o = col_hy + off_o

    h, gates = _norm_gates(x2, norm_g[None, :], w_in[:, off_gates:off_br], b_in[None, off_gates:off_br])
    p_big = _matmul_bias(h, w_cat, b_cat, BF16)
    p3 = p_big.reshape(bsz, length, p_big.shape[1])

    kp = LANES
    pad2 = lambda a, r, c: jnp.zeros((r, c), F32).at[:a.shape[0], :a.shape[1]].set(a)
    zp = pad2(zpos, length, kp)
    hspec, hnyq = _hyena_spectrum(
        zp, pad2(hy_w1, kp, kp), pad2(hy_b1[None, :], 1, kp), pad2(hy_w2, kp, kp), pad2(hy_b2[None, :], 1, kp),
        pad2(hy_freq[None, :], 1, kp), pad2(hy_w3, kp, hy_w3.shape[1]), hy_decay[None, :],
        ctab, stab, wh, HYENA_CT)
    y_hy = _hyena_conv(p3, col_hy, hy_conv_w, hy_conv_b[None, :], hspec, hnyq, hy_skip[:, None, :],
                       ctab, stab, wh, HYENA_CT, HYENA_RC)

    nh = MLSTM_HEADS
    nc = length // MLSTM_CHUNK
    gcol, grow = _gate_prep(gates.reshape(bsz, length, 4 * nh), MLSTM_CHUNK)
    gcol = gcol.reshape(bsz, length, 4, nh).transpose(0, 3, 1, 2)
    grow = grow.reshape(bsz, 4, nh, nc, MLSTM_CHUNK).transpose(0, 2, 3, 1, 4)
    y_ml = _mlstm(p3, col_q, col_k, col_v, col_o, ml_conv_w, ml_conv_b[None, :], ml_norm_g[None, :],
                  gcol, grow, MLSTM_CHUNK)

    return _merge(y_hy.reshape(n, wh), y_ml.reshape(n, wm), p_big, x2,
                  w_a.astype(BF16), w_b.astype(BF16), w_o.astype(BF16))


def kernel(x, mix_norm_g, mix_w_in, mix_b_in, hy_conv_w, hy_conv_b, hy_filt_w1, hy_filt_b1, hy_filt_w2,
           hy_filt_b2, hy_filt_w3, hy_filt_freq, hy_filt_decay, hy_skip, ml_conv_w, ml_conv_b, ml_norm_g,
           mix_w_a, mix_w_b, mix_w_o, ffn_norm_g, dense_w_gate, dense_w_up, dense_w_down, moe_router_w,
           moe_router_b, moe_w_gate, moe_w_up, moe_w_down, final_norm_g):
    bsz, length, d = x.shape
    depth = mix_norm_g.shape[0]
    x2 = x.reshape(bsz * length, d)
    ctab, stab = _dft_tables(length)
    zpos = _hyena_positions(length)
    normed = False
    for layer in range(depth):
        x2 = _mixer_layer(x2, bsz, length, mix_norm_g[layer], mix_w_in[layer], mix_b_in[layer],
                          hy_conv_w[layer], hy_conv_b[layer], hy_filt_w1[layer], hy_filt_b1[layer],
                          hy_filt_w2[layer], hy_filt_b2[layer], hy_filt_w3[layer], hy_filt_freq[layer],
                          hy_filt_decay[layer], hy_skip[layer], ml_conv_w[layer], ml_conv_b[layer],
                          ml_norm_g[layer], mix_w_a[layer], mix_w_b[layer], mix_w_o[layer], ctab, stab, zpos)
        j = layer // 2
        if layer % 2 == 0:
            x2 = _dense_ffn(x2, ffn_norm_g[layer][None, :], dense_w_gate[j].astype(BF16),
                            dense_w_up[j].astype(BF16), dense_w_down[j].astype(BF16))
        else:
            normed = layer == depth - 1
            x2 = _moe_layer(x2, ffn_norm_g[layer], moe_router_w[j], moe_router_b[j], moe_w_gate[j],
                            moe_w_up[j], moe_w_down[j], final_norm_g, normed)
    if not normed:
        x2 = _final_norm(x2, final_norm_g[None, :])
    return x2.reshape(bsz, length, d)
```

```python
import functools
import math

import jax
import jax.numpy as jnp
from jax import lax
from jax.experimental import pallas as pl
from jax.experimental.pallas import tpu as pltpu

F32, BF16, I32 = jnp.float32, jnp.bfloat16, jnp.int32
HIGHEST = lax.Precision.HIGHEST

HYENA_ORDER = 2
HYENA_POS_BANDS = 16
MLSTM_HEADS = 8
N_EXPERTS = 8
TOP_K = 2
EPS = 1e-6

V7X_VMEM_BYTES = 64 * 1024 * 1024
LANES = 128
MXU_DIM = 256

MLSTM_CHUNK = 256
HYENA_CT = MXU_DIM
HYENA_RC = 512
MOE_BLOCK = 512
GATHER_ROWS = 1024


def _cparams(semantics, vmem_mib):
    assert vmem_mib * 1024 * 1024 <= V7X_VMEM_BYTES
    return pltpu.CompilerParams(dimension_semantics=semantics, vmem_limit_bytes=vmem_mib * 1024 * 1024)


def _const_spec(shape):
    nd = len(shape)
    return pl.BlockSpec(shape, lambda *_: (0,) * nd, pipeline_mode=pl.Buffered(1))


def _rms(x, g):
    return x * lax.rsqrt(jnp.mean(x * x, axis=-1, keepdims=True) + EPS) * g


def _shift_rows(u, direction):
    n = u.shape[0]
    row = lax.broadcasted_iota(I32, u.shape, 0)
    if direction > 0:
        return jnp.where(row == 0, 0.0, pltpu.roll(u, 1, axis=0))
    return jnp.where(row == n - 1, 0.0, pltpu.roll(u, n - 1, axis=0))


def _short_conv(u, w, b):
    return b + _shift_rows(u, 1) * w[0:1] + u * w[1:2] + _shift_rows(u, -1) * w[2:3]


def _log_sigmoid(x):
    return jnp.minimum(x, 0.0) - jnp.log1p(jnp.exp(-jnp.abs(x)))


def _norm_gates_kernel(x_ref, g_ref, wg_ref, bg_ref, h_ref, gates_ref):
    hn = _rms(x_ref[...], g_ref[...])
    h_ref[...] = hn.astype(h_ref.dtype)
    gates_ref[...] = jnp.dot(hn, wg_ref[...], precision=HIGHEST, preferred_element_type=F32) + bg_ref[...]


def _norm_gates(x2, g, w_gates, b_gates, tm=512):
    n, d = x2.shape
    ng = w_gates.shape[1]
    return pl.pallas_call(
        _norm_gates_kernel,
        grid=(n // tm,),
        in_specs=[pl.BlockSpec((tm, d), lambda i: (i, 0)),
                  pl.BlockSpec((1, d), lambda i: (0, 0)),
                  pl.BlockSpec((d, ng), lambda i: (0, 0)),
                  pl.BlockSpec((1, ng), lambda i: (0, 0))],
        out_specs=[pl.BlockSpec((tm, d), lambda i: (i, 0)),
                   pl.BlockSpec((tm, ng), lambda i: (i, 0))],
        out_shape=[jax.ShapeDtypeStruct((n, d), BF16), jax.ShapeDtypeStruct((n, ng), F32)],
        compiler_params=_cparams(("parallel",), 32),
        name="norm_gates",
    )(x2, g, w_gates, b_gates)


def _matmul_bias_kernel(a_ref, w_ref, b_ref, o_ref):
    acc = jnp.dot(a_ref[...], w_ref[...], preferred_element_type=F32) + b_ref[...]
    o_ref[...] = acc.astype(o_ref.dtype)


def _matmul_bias(a, w, b, out_dtype, tm=1024, tn=1024):
    m, k = a.shape
    nc = w.shape[1]
    return pl.pallas_call(
        _matmul_bias_kernel,
        grid=(m // tm, nc // tn),
        in_specs=[pl.BlockSpec((tm, k), lambda i, j: (i, 0)),
                  pl.BlockSpec((k, tn), lambda i, j: (0, j)),
                  pl.BlockSpec((1, tn), lambda i, j: (0, j))],
        out_specs=pl.BlockSpec((tm, tn), lambda i, j: (i, j)),
        out_shape=jax.ShapeDtypeStruct((m, nc), out_dtype),
        compiler_params=_cparams(("parallel", "arbitrary"), 40),
        name="in_proj",
    )(a, w, b)


def _dft_tables(length):
    k = jnp.arange(length, dtype=I32)
    kn = (k[:, None] * k[None, :]) % (2 * length)
    ang = kn.astype(F32) * (math.pi / length)
    return jnp.cos(ang).astype(BF16), jnp.sin(ang).astype(BF16)


def _hyena_positions(length):
    t = jnp.linspace(0.0, 1.0, length, dtype=F32)[:, None]
    n = jnp.arange(length, dtype=F32)[:, None]
    bands = jnp.linspace(1e-4, HYENA_POS_BANDS - 1, HYENA_POS_BANDS, dtype=F32)[None, :]
    ang = (2.0 * math.pi / length) * n * bands
    return jnp.concatenate([t, jnp.cos(ang), -jnp.sin(ang)], axis=-1)


def _hyena_spectrum_kernel(z_ref, w1_ref, b1_ref, w2_ref, b2_ref, fr_ref, w3f_ref, w3b_ref,
                           decf_ref, decb_ref, c_ref, s_ref, h_ref, nyq_ref):
    length = c_ref.shape[0]
    dot_hi = functools.partial(jnp.dot, precision=HIGHEST, preferred_element_type=F32)
    z = z_ref[...]
    fr = fr_ref[...]
    hid = jnp.sin(fr * (dot_hi(z, w1_ref[...]) + b1_ref[...]))
    hid = jnp.sin(fr * (dot_hi(hid, w2_ref[...]) + b2_ref[...]))
    t = z[:, 0:1]
    ff = dot_hi(hid, w3f_ref[...]) * jnp.exp(-t * jnp.abs(decf_ref[...]))
    fb = dot_hi(hid, w3b_ref[...]) * jnp.exp(-t * jnp.abs(decb_ref[...]))
    row = lax.broadcasted_iota(I32, (length, 1), 0)
    l1 = jnp.sum(jnp.where(row == 0, jnp.abs(ff + fb), jnp.abs(ff) + jnp.abs(fb)), axis=0, keepdims=True)
    inv = 1.0 / l1
    even = (ff + fb) * inv
    odd = (ff - fb) * inv
    sign = jnp.where(row % 2 == 0, 1.0, -1.0)
    scale = jnp.where(row == 0, 0.5 / length, 1.0 / length)
    h_ref[0] = jnp.dot(c_ref[...], even.astype(BF16), preferred_element_type=F32) * scale
    h_ref[1] = -jnp.dot(s_ref[...], odd.astype(BF16), preferred_element_type=F32) * scale
    nyq_ref[...] = jnp.sum(sign * even, axis=0, keepdims=True) * (0.5 / length)


def _hyena_spectrum(z, w1, b1, w2, b2, freq, w3, decay, ctab, stab, width, ct):
    length = ctab.shape[0]
    kp = w1.shape[1]
    nct = width // ct
    col_f = lambda o, c: (0, o * 2 * nct + c)
    col_b = lambda o, c: (0, o * 2 * nct + nct + c)
    return pl.pallas_call(
        _hyena_spectrum_kernel,
        grid=(HYENA_ORDER, nct),
        in_specs=[_const_spec(z.shape), _const_spec(w1.shape), _const_spec(b1.shape),
                  _const_spec(w2.shape), _const_spec(b2.shape), _const_spec(freq.shape),
                  pl.BlockSpec((kp, ct), col_f), pl.BlockSpec((kp, ct), col_b),
                  pl.BlockSpec((1, ct), col_f), pl.BlockSpec((1, ct), col_b),
                  _const_spec(ctab.shape), _const_spec(stab.shape)],
        out_specs=[pl.BlockSpec((None, 2, length, ct), lambda o, c: (o, 0, 0, c)),
                   pl.BlockSpec((None, 1, ct), lambda o, c: (o, 0, c))],
        out_shape=[jax.ShapeDtypeStruct((HYENA_ORDER, 2, length, width), F32),
                   jax.ShapeDtypeStruct((HYENA_ORDER, 1, width), F32)],
        compiler_params=_cparams(("parallel", "parallel"), 48),
        name="hyena_spectrum",
    )(z, w1, b1, w2, b2, freq, w3, w3, decay, decay, ctab, stab)


def _hyena_conv_kernel(pv_ref, p1_ref, p2_ref, wv_ref, w1_ref, w2_ref, bv_ref, b1_ref, b2_ref,
                       h_ref, nyq_ref, skip_ref, c_ref, s_ref, y_ref,
                       z_s, g1_s, g2_s, zb_s, a_s, b_s, *, rc):
    length = c_ref.shape[0]
    z_s[...] = _short_conv(pv_ref[...].astype(F32), wv_ref[...], bv_ref[...])
    g1_s[...] = _short_conv(p1_ref[...].astype(F32), w1_ref[...], b1_ref[...])
    g2_s[...] = _short_conv(p2_ref[...].astype(F32), w2_ref[...], b2_ref[...])
    gates = (g1_s, g2_s)
    row = lax.broadcasted_iota(I32, (length, 1), 0)
    sign = jnp.where(row % 2 == 0, 1.0, -1.0)
    for o in range(HYENA_ORDER):
        z = z_s[...]
        zb_s[...] = z.astype(BF16)
        nyq = jnp.sum(sign * z, axis=0, keepdims=True) * nyq_ref[o]
        for r in range(length // rc):
            rows = slice(r * rc, (r + 1) * rc)
            re = jnp.dot(c_ref[rows, :], zb_s[...], preferred_element_type=F32)
            pim = jnp.dot(s_ref[rows, :], zb_s[...], preferred_element_type=F32)
            hre = h_ref[o, 0, rows, :]
            him = h_ref[o, 1, rows, :]
            a_s[rows, :] = (hre * re + him * pim).astype(BF16)
            b_s[rows, :] = (hre * pim - him * re).astype(BF16)
        for r in range(length // rc):
            rows = slice(r * rc, (r + 1) * rc)
            conv = (jnp.dot(c_ref[rows, :], a_s[...], preferred_element_type=F32)
                    + jnp.dot(s_ref[rows, :], b_s[...], preferred_element_type=F32)
                    + sign[rows] * nyq)
            znew = gates[o][rows, :] * (conv + skip_ref[o] * z_s[rows, :])
            if o == HYENA_ORDER - 1:
                y_ref[rows, :] = znew.astype(y_ref.dtype)
            else:
                z_s[rows, :] = znew


def _hyena_conv(p3, col0, conv_w, conv_b, hspec, hnyq, skip, ctab, stab, width, ct, rc):
    bsz, length, _ = p3.shape
    nct = width // ct
    c0 = col0 // ct

    def pspec(part):
        return pl.BlockSpec((None, length, ct), lambda c, b: (b, 0, c0 + part * nct + c))

    def wspec(rows, part):
        return pl.BlockSpec((rows, ct), lambda c, b: (0, part * nct + c))

    scratch = [pltpu.VMEM((length, ct), F32)] * 3 + [pltpu.VMEM((length, ct), BF16)] * 3
    return pl.pallas_call(
        functools.partial(_hyena_conv_kernel, rc=rc),
        grid=(nct, bsz),
        in_specs=[pspec(0), pspec(1), pspec(2),
                  wspec(3, 0), wspec(3, 1), wspec(3, 2), wspec(1, 0), wspec(1, 1), wspec(1, 2),
                  pl.BlockSpec((HYENA_ORDER, 2, length, ct), lambda c, b: (0, 0, 0, c),
                               pipeline_mode=pl.Buffered(1)),
                  pl.BlockSpec((HYENA_ORDER, 1, ct), lambda c, b: (0, 0, c)),
                  pl.BlockSpec((HYENA_ORDER, 1, ct), lambda c, b: (0, 0, c)),
                  _const_spec(ctab.shape), _const_spec(stab.shape)],
        out_specs=pl.BlockSpec((None, length, ct), lambda c, b: (b, 0, c)),
        out_shape=jax.ShapeDtypeStruct((bsz, length, width), BF16),
        scratch_shapes=scratch,
        compiler_params=_cparams(("parallel", "parallel"), 56),
        name="hyena_conv",
    )(p3, p3, p3, conv_w, conv_w, conv_w, conv_b, conv_b, conv_b, hspec, hnyq, skip, ctab, stab)


def _gate_prep_kernel(gc_ref, gr_ref, oc_ref, or_ref, *, chunk):
    length, nch = gc_ref.shape
    nh = nch // 4
    r = lax.broadcasted_iota(I32, (chunk, chunk), 0)
    c = lax.broadcasted_iota(I32, (chunk, chunk), 1)
    lower = (c <= r).astype(F32)
    upper = (c >= r).astype(F32)
    dot_hi = functools.partial(jnp.dot, precision=HIGHEST, preferred_element_type=F32)
    ccol = lax.broadcasted_iota(I32, (chunk, nch), 1)
    crow = lax.broadcasted_iota(I32, (nch, chunk), 0)
    for ch in range(length // chunk):
        rows = slice(ch * chunk, (ch + 1) * chunk)
        x = gc_ref[rows, :]
        lf = _log_sigmoid(x)
        pre = dot_hi(lower, lf)
        suf = dot_hi(upper, lf)
        oc_ref[rows, :] = jnp.where((ccol // nh) % 2 == 0, x, jnp.where(ccol < 2 * nh, pre, suf))
        xr = gr_ref[:, rows]
        lfr = _log_sigmoid(xr)
        pre_r = dot_hi(lfr, upper)
        suf_r = dot_hi(lfr, lower)
        or_ref[:, rows] = jnp.where((crow // nh) % 2 == 0, xr, jnp.where(crow < 2 * nh, pre_r, suf_r))


def _gate_prep(gates3, chunk):
    bsz, length, nch = gates3.shape
    gates_t = jnp.swapaxes(gates3, 1, 2)
    return pl.pallas_call(
        functools.partial(_gate_prep_kernel, chunk=chunk),
        grid=(bsz,),
        in_specs=[pl.BlockSpec((None, length, nch), lambda b: (b, 0, 0)),
                  pl.BlockSpec((None, nch, length), lambda b: (b, 0, 0))],
        out_specs=[pl.BlockSpec((None, length, nch), lambda b: (b, 0, 0)),
                   pl.BlockSpec((None, nch, length), lambda b: (b, 0, 0))],
        out_shape=[jax.ShapeDtypeStruct((bsz, length, nch), F32),
                   jax.ShapeDtypeStruct((bsz, nch, length), F32)],
        compiler_params=_cparams(("parallel",), 32),
        name="mlstm_gate_prep",
    )(gates3, gates_t)


def _mlstm_kernel(pq_ref, pk_ref, pv_ref, po_ref, cwq_ref, cwk_ref, cbq_ref, cbk_ref, ng_ref,
                  gcol_ref, grow_ref, y_ref, q_s, k_s, hf_s, *, chunk):
    length, dk = pq_ref.shape
    nc = length // chunk

    def conv_silu(p_ref, w_ref, b_ref):
        c = _short_conv(p_ref[...].astype(F32), w_ref[...], b_ref[...])
        return c * jax.nn.sigmoid(c)

    q_s[...] = (conv_silu(pq_ref, cwq_ref, cbq_ref) * (dk ** -0.5)).astype(BF16)
    k_s[...] = conv_silu(pk_ref, cwk_ref, cbk_ref).astype(BF16)

    row_i = lax.broadcasted_iota(I32, (chunk, chunk), 0)
    col_i = lax.broadcasted_iota(I32, (chunk, chunk), 1)

    def run_direction(li_idx, b_idx, reverse, emit):
        mask = (col_i >= row_i) if reverse else (col_i <= row_i)

        def body(step, carry):
            state, nvec, m = carry
            c = (nc - 1 - step) if reverse else step
            rows = pl.ds(pl.multiple_of(c * chunk, chunk), chunk)
            q = q_s[rows, :]
            k = k_s[rows, :]
            v = pv_ref[rows, :]
            gc = gcol_ref[rows, :]
            li_col = gc[:, li_idx:li_idx + 1]
            b_col = gc[:, b_idx:b_idx + 1]
            gr = grow_ref[c]
            li_row = gr[li_idx:li_idx + 1, :]
            b_row = gr[b_idx:b_idx + 1, :]
            g = b_col[0:1, :] if reverse else b_col[chunk - 1:chunk, :]
            a_col = g - b_col + li_col
            m_loc = jnp.max(a_col, axis=0, keepdims=True)
            kw = k.astype(F32) * jnp.exp(a_col - m_loc)
            ks = jnp.sum(kw, axis=0, keepdims=True)
            kv_t = lax.dot_general(kw.astype(BF16), v, (((0,), (0,)), ((), ())),
                                   preferred_element_type=F32)
            dmat = jnp.where(mask, b_col - b_row + li_row, -jnp.inf)
            inter = b_col + m
            m_j = jnp.maximum(inter, jnp.max(dmat, axis=1, keepdims=True))
            qk = lax.dot_general(q, k, (((1,), (1,)), ((), ())), preferred_element_type=F32)
            sc = qk * jnp.exp(dmat - m_j)
            s_inter = jnp.exp(inter - m_j)
            num = (jnp.dot(sc.astype(BF16), v, preferred_element_type=F32)
                   + s_inter * jnp.dot(q, state.astype(BF16), preferred_element_type=F32))
            den = (jnp.sum(sc, axis=1, keepdims=True)
                   + s_inter * jnp.sum(q.astype(F32) * nvec, axis=1, keepdims=True))
            emit(rows, num / jnp.maximum(jnp.abs(den), jnp.exp(-m_j)))
            m_new = jnp.maximum(g + m, m_loc)
            a_old = jnp.exp(g + m - m_new)
            a_new = jnp.exp(m_loc - m_new)
            return a_old * state + a_new * kv_t, a_old * nvec + a_new * ks, m_new

        init = (jnp.zeros((dk, pv_ref.shape[1]), F32), jnp.zeros((1, dk), F32), jnp.zeros((1, 1), F32))
        lax.fori_loop(0, nc, body, init)

    def emit_fwd(rows, h):
        hf_s[rows, :] = h

    def emit_bwd(rows, h):
        ht = hf_s[rows, :] + h
        hn = _rms(ht, ng_ref[...]) * jax.nn.sigmoid(po_ref[rows, :].astype(F32))
        y_ref[rows, :] = hn.astype(y_ref.dtype)

    run_direction(0, 1, False, emit_fwd)
    run_direction(2, 3, True, emit_bwd)


def _mlstm(p3, col_q, col_k, col_v, col_o, conv_w, conv_b, norm_g, gcol, grow, chunk):
    bsz, length, _ = p3.shape
    nh, dh = MLSTM_HEADS, norm_g.shape[1] // MLSTM_HEADS
    nc = length // chunk

    def pspec(col):
        return pl.BlockSpec((None, length, dh), lambda b, h: (b, 0, col // dh + h))

    def wspec(rows, part):
        return pl.BlockSpec((rows, dh), lambda b, h: (0, part * nh + h))

    return pl.pallas_call(
        functools.partial(_mlstm_kernel, chunk=chunk),
        grid=(bsz, nh),
        in_specs=[pspec(col_q), pspec(col_k), pspec(col_v), pspec(col_o),
                  wspec(3, 0), wspec(3, 1), wspec(1, 0), wspec(1, 1),
                  pl.BlockSpec((1, dh), lambda b, h: (0, h)),
                  pl.BlockSpec((None, None, length, 4), lambda b, h: (b, h, 0, 0)),
                  pl.BlockSpec((None, None, nc, 4, chunk), lambda b, h: (b, h, 0, 0, 0))],
        out_specs=pl.BlockSpec((None, length, dh), lambda b, h: (b, 0, h)),
        out_shape=jax.ShapeDtypeStruct((bsz, length, nh * dh), BF16),
        scratch_shapes=[pltpu.VMEM((length, dh), BF16), pltpu.VMEM((length, dh), BF16),
                        pltpu.VMEM((length, dh), F32)],
        compiler_params=_cparams(("parallel", "parallel"), 32),
        name="mlstm",
    )(p3, p3, p3, p3, conv_w, conv_w, conv_b, conv_b, norm_g, gcol, grow)


def _merge_kernel(yh_ref, ym_ref, gh_ref, gm_ref, x_ref, wa_ref, wb_ref, wo_ref, o_ref):
    a = jnp.dot(yh_ref[...], wa_ref[...], preferred_element_type=F32)
    b = jnp.dot(ym_ref[...], wb_ref[...], preferred_element_type=F32)
    t = jax.nn.sigmoid(gh_ref[...].astype(F32)) * a + jax.nn.sigmoid(gm_ref[...].astype(F32)) * b
    o_ref[...] = x_ref[...] + jnp.dot(t.astype(BF16), wo_ref[...], preferred_element_type=F32)


def _merge(y_hy, y_ml, p_big, x2, w_a, w_b, w_o, tm=256):
    n, d = x2.shape
    wh, wm = y_hy.shape[1], y_ml.shape[1]
    return pl.pallas_call(
        _merge_kernel,
        grid=(n // tm,),
        in_specs=[pl.BlockSpec((tm, wh), lambda i: (i, 0)),
                  pl.BlockSpec((tm, wm), lambda i: (i, 0)),
                  pl.BlockSpec((tm, d), lambda i: (i, 0)),
                  pl.BlockSpec((tm, d), lambda i: (i, 1)),
                  pl.BlockSpec((tm, d), lambda i: (i, 0)),
                  _const_spec(w_a.shape), _const_spec(w_b.shape), _const_spec(w_o.shape)],
        out_specs=pl.BlockSpec((tm, d), lambda i: (i, 0)),
        out_shape=jax.ShapeDtypeStruct((n, d), F32),
        compiler_params=_cparams(("parallel",), 48),
        name="branch_merge",
    )(y_hy, y_ml, p_big, p_big, x2, w_a, w_b, w_o)


def _dense_ffn_kernel(x_ref, g_ref, wg_ref, wu_ref, wd_ref, o_ref, h_s):
    @pl.when(pl.program_id(1) == 0)
    def _():
        x = x_ref[...]
        h_s[...] = _rms(x, g_ref[...]).astype(BF16)
        o_ref[...] = x

    h = h_s[...]
    gt = jnp.dot(h, wg_ref[...], preferred_element_type=F32)
    up = jnp.dot(h, wu_ref[...], preferred_element_type=F32)
    act = (gt * jax.nn.sigmoid(gt) * up).astype(BF16)
    o_ref[...] += jnp.dot(act, wd_ref[...], preferred_element_type=F32)


def _dense_ffn(x2, g, w_gate, w_up, w_down, tm=512, tf=512):
    n, d = x2.shape
    f = w_gate.shape[1]
    return pl.pallas_call(
        _dense_ffn_kernel,
        grid=(n // tm, f // tf),
        in_specs=[pl.BlockSpec((tm, d), lambda i, j: (i, 0)),
                  pl.BlockSpec((1, d), lambda i, j: (0, 0)),
                  pl.BlockSpec((d, tf), lambda i, j: (0, j)),
                  pl.BlockSpec((d, tf), lambda i, j: (0, j)),
                  pl.BlockSpec((tf, d), lambda i, j: (j, 0))],
        out_specs=pl.BlockSpec((tm, d), lambda i, j: (i, 0)),
        out_shape=jax.ShapeDtypeStruct((n, d), F32),
        scratch_shapes=[pltpu.VMEM((tm, d), BF16)],
        compiler_params=_cparams(("parallel", "arbitrary"), 48),
        name="dense_swiglu",
    )(x2, g, w_gate, w_up, w_down)


def _store_rows_3d(ref3, val2):
    for s in range(ref3.shape[1]):
        ref3[:, s, :] = val2[:, s * LANES:(s + 1) * LANES]


def _router_kernel(x_ref, g_ref, rw_ref, rb_ref, h_ref, e_ref, gate_ref):
    hn = _rms(x_ref[...], g_ref[...])
    _store_rows_3d(h_ref, hn)
    logits = jnp.dot(hn, rw_ref[...], precision=HIGHEST, preferred_element_type=F32) + rb_ref[...]
    lane = lax.broadcasted_iota(I32, logits.shape, 1)
    logits = jnp.where(lane < N_EXPERTS, logits, -jnp.inf)
    m1 = jnp.max(logits, axis=-1, keepdims=True)
    i1 = jnp.min(jnp.where(logits == m1, lane, LANES), axis=-1, keepdims=True)
    rest = jnp.where(lane == i1, -jnp.inf, logits)
    m2 = jnp.max(rest, axis=-1, keepdims=True)
    i2 = jnp.min(jnp.where(rest == m2, lane, LANES), axis=-1, keepdims=True)
    e = jnp.exp(m2 - m1)
    slot = lax.broadcasted_iota(I32, e_ref.shape, 1)
    e_ref[...] = jnp.where(slot == 0, i1, i2)
    gate_ref[...] = jnp.where(slot == 0, 1.0 / (1.0 + e), e / (1.0 + e))


def _router(x2, g, rw_pad, rb_pad, tm=512):
    n, d = x2.shape
    s = d // LANES
    return pl.pallas_call(
        _router_kernel,
        grid=(n // tm,),
        in_specs=[pl.BlockSpec((tm, d), lambda i: (i, 0)),
                  pl.BlockSpec((1, d), lambda i: (0, 0)),
                  pl.BlockSpec((d, LANES), lambda i: (0, 0)),
                  pl.BlockSpec((1, LANES), lambda i: (0, 0))],
        out_specs=[pl.BlockSpec((tm, s, LANES), lambda i: (i, 0, 0)),
                   pl.BlockSpec((tm, TOP_K), lambda i: (i, 0)),
                   pl.BlockSpec((tm, TOP_K), lambda i: (i, 0))],
        out_shape=[jax.ShapeDtypeStruct((n, s, LANES), F32),
                   jax.ShapeDtypeStruct((n, TOP_K), I32),
                   jax.ShapeDtypeStruct((n, TOP_K), F32)],
        compiler_params=_cparams(("parallel",), 32),
        name="moe_router",
    )(x2, g, rw_pad, rb_pad)


def _gather_rows_kernel(idx_ref, src_ref, out_ref, sem, *, rows_per_step):
    base = pl.program_id(0) * rows_per_step

    def row_copy(r):
        return pltpu.make_async_copy(src_ref.at[idx_ref[base + r]], out_ref.at[base + r], sem)

    def start(r, carry):
        row_copy(r).start()
        return carry

    def wait(r, carry):
        row_copy(r).wait()
        return carry

    lax.fori_loop(0, rows_per_step, start, 0)
    lax.fori_loop(0, rows_per_step, wait, 0)


def _gather_rows(src, idx, rows_per_step):
    m = idx.shape[0]
    return pl.pallas_call(
        functools.partial(_gather_rows_kernel, rows_per_step=rows_per_step),
        grid_spec=pltpu.PrefetchScalarGridSpec(
            num_scalar_prefetch=1,
            grid=(m // rows_per_step,),
            in_specs=[pl.BlockSpec(memory_space=pl.ANY)],
            out_specs=pl.BlockSpec(memory_space=pl.ANY),
            scratch_shapes=[pltpu.SemaphoreType.DMA(())],
        ),
        out_shape=jax.ShapeDtypeStruct((m,) + src.shape[1:], src.dtype),
        compiler_params=_cparams(("arbitrary",), 16),
        name="gather_rows",
    )(idx, src)


def _expert_ffn_kernel(be_ref, nu_ref, xb_ref, gate_ref, wg_ref, wu_ref, wd_ref, o_ref, h_s, acc_s):
    i, j = pl.program_id(0), pl.program_id(1)

    @pl.when(j == 0)
    def _():
        for s in range(xb_ref.shape[1]):
            h_s[:, s * LANES:(s + 1) * LANES] = xb_ref[:, s, :].astype(BF16)
        acc_s[...] = jnp.zeros_like(acc_s)

    @pl.when(i < nu_ref[0])
    def _():
        h = h_s[...]
        gt = jnp.dot(h, wg_ref[...], preferred_element_type=F32)
        up = jnp.dot(h, wu_ref[...], preferred_element_type=F32)
        act = (gt * jax.nn.sigmoid(gt) * up).astype(BF16)
        acc_s[...] += jnp.dot(act, wd_ref[...], preferred_element_type=F32)

    @pl.when(j == pl.num_programs(1) - 1)
    def _():
        gate = jnp.broadcast_to(gate_ref[...], (gate_ref.shape[0], LANES))
        for s in range(o_ref.shape[1]):
            o_ref[:, s, :] = acc_s[:, s * LANES:(s + 1) * LANES] * gate


def _expert_ffn(xb, slot_gate, block_e, n_used, w_gate, w_up, w_down, blk, tf=512):
    ns, s, _ = xb.shape
    d = s * LANES
    f = w_gate.shape[2]
    nj = f // tf

    def jj(i, j, nu):
        return jnp.where(i < nu[0], j, nj - 1)

    return pl.pallas_call(
        _expert_ffn_kernel,
        grid_spec=pltpu.PrefetchScalarGridSpec(
            num_scalar_prefetch=2,
            grid=(ns // blk, nj),
            in_specs=[pl.BlockSpec((blk, s, LANES), lambda i, j, be, nu: (i, 0, 0)),
                      pl.BlockSpec((blk, 1), lambda i, j, be, nu: (i, 0)),
                      pl.BlockSpec((None, d, tf), lambda i, j, be, nu: (be[i], 0, jj(i, j, nu))),
                      pl.BlockSpec((None, d, tf), lambda i, j, be, nu: (be[i], 0, jj(i, j, nu))),
                      pl.BlockSpec((None, tf, d), lambda i, j, be, nu: (be[i], jj(i, j, nu), 0))],
            out_specs=pl.BlockSpec((blk, s, LANES), lambda i, j, be, nu: (i, 0, 0)),
            scratch_shapes=[pltpu.VMEM((blk, d), BF16), pltpu.VMEM((blk, d), F32)],
        ),
        out_shape=jax.ShapeDtypeStruct((ns, s, LANES), F32),
        compiler_params=_cparams(("arbitrary", "arbitrary"), 48),
        name="expert_swiglu",
    )(block_e, n_used, xb, slot_gate, w_gate, w_up, w_down)


def _combine_kernel(x_ref, y0_ref, y1_ref, g_ref, o_ref, *, final_norm):
    for s in range(y0_ref.shape[1]):
        cols = slice(s * LANES, (s + 1) * LANES)
        o_ref[:, cols] = x_ref[:, cols] + y0_ref[:, s, :] + y1_ref[:, s, :]
    if final_norm:
        o_ref[...] = _rms(o_ref[...], g_ref[...])


def _combine(x2, y2, g, final_norm, tm=512):
    n, d = x2.shape
    s = d // LANES
    return pl.pallas_call(
        functools.partial(_combine_kernel, final_norm=final_norm),
        grid=(n // tm,),
        in_specs=[pl.BlockSpec((tm, d), lambda i: (i, 0)),
                  pl.BlockSpec((None, tm, s, LANES), lambda i: (0, i, 0, 0)),
                  pl.BlockSpec((None, tm, s, LANES), lambda i: (1, i, 0, 0)),
                  pl.BlockSpec((1, d), lambda i: (0, 0))],
        out_specs=pl.BlockSpec((tm, d), lambda i: (i, 0)),
        out_shape=jax.ShapeDtypeStruct((n, d), F32),
        compiler_params=_cparams(("parallel",), 48),
        name="moe_combine",
    )(x2, y2, y2, g)


def _final_norm_kernel(x_ref, g_ref, o_ref):
    o_ref[...] = _rms(x_ref[...], g_ref[...])


def _final_norm(x2, g, tm=512):
    n, d = x2.shape
    return pl.pallas_call(
        _final_norm_kernel,
        grid=(n // tm,),
        in_specs=[pl.BlockSpec((tm, d), lambda i: (i, 0)), pl.BlockSpec((1, d), lambda i: (0, 0))],
        out_specs=pl.BlockSpec((tm, d), lambda i: (i, 0)),
        out_shape=jax.ShapeDtypeStruct((n, d), F32),
        compiler_params=_cparams(("parallel",), 32),
        name="final_norm",
    )(x2, g)


def _routing_tables(top_e, gate, blk):
    n = top_e.shape[0]
    nk = n * TOP_K
    flat_e = top_e.reshape(-1)
    onehot = (flat_e[:, None] == jnp.arange(N_EXPERTS, dtype=I32)[None, :]).astype(I32)
    rank = jnp.sum((jnp.cumsum(onehot, axis=0) - onehot) * onehot, axis=1)
    counts = jnp.sum(onehot, axis=0)
    padded = (counts + blk - 1) // blk * blk
    p_end = jnp.cumsum(padded)
    p_start = p_end - padded
    dest = (p_start[flat_e] + rank).astype(I32)
    n_blocks = -(-nk // blk) + N_EXPERTS
    flat_tok = jnp.arange(nk, dtype=I32) // TOP_K
    slot_tok = jnp.zeros((n_blocks * blk,), I32).at[dest].set(flat_tok)
    slot_gate = jnp.zeros((n_blocks * blk,), F32).at[dest].set(gate.reshape(-1))
    n_used = (p_end[-1] // blk).astype(I32)
    blocks = jnp.arange(n_blocks, dtype=I32)
    block_e = jnp.sum((p_end[None, :] <= (blocks * blk)[:, None]).astype(I32), axis=1)
    block_e = jnp.minimum(block_e, N_EXPERTS - 1)
    block_e = jnp.where(blocks < n_used, block_e, block_e[n_used - 1])
    dest_by_k = dest.reshape(n, TOP_K).T.reshape(-1)
    return slot_tok, slot_gate[:, None], block_e, n_used.reshape(1), dest_by_k


def _moe_layer(x2, norm_g, router_w, router_b, w_gate, w_up, w_down, out_g, final_norm):
    n, d = x2.shape
    rw = jnp.zeros((d, LANES), F32).at[:, :N_EXPERTS].set(router_w)
    rb = jnp.zeros((1, LANES), F32).at[0, :N_EXPERTS].set(router_b)
    h, top_e, gate = _router(x2, norm_g[None, :], rw, rb)
    slot_tok, slot_gate, block_e, n_used, dest_by_k = _routing_tables(top_e, gate, MOE_BLOCK)
    xb = _gather_rows(h, slot_tok, GATHER_ROWS)
    yb = _expert_ffn(xb, slot_gate, block_e, n_used, w_gate.astype(BF16), w_up.astype(BF16),
                     w_down.astype(BF16), MOE_BLOCK)
    y2 = _gather_rows(yb, dest_by_k, GATHER_ROWS)
    y2 = y2.reshape((TOP_K, n) + y2.shape[1:])
    return _combine(x2, y2, out_g[None, :], final_norm)


def _mixer_layer(x2, bsz, length, norm_g, w_in, b_in, hy_conv_w, hy_conv_b, hy_w1, hy_b1, hy_w2, hy_b2,
                 hy_w3, hy_freq, hy_decay, hy_skip, ml_conv_w, ml_conv_b, ml_norm_g, w_a, w_b, w_o,
                 ctab, stab, zpos):
    n, d = x2.shape
    wh = hy_skip.shape[1]
    wm = ml_norm_g.shape[0]
    off_qk = (HYENA_ORDER + 1) * wh
    off_v = off_qk + 2 * wm
    off_o = off_v + wm
    off_gates = off_o + wm
    off_br = off_gates + 4 * MLSTM_HEADS
    w_cat = jnp.concatenate([w_in[:, off_br:], w_in[:, :off_gates]], axis=1).astype(BF16)
    b_cat = jnp.concatenate([b_in[off_br:], b_in[:off_gates]])[None, :]
    col_hy = 2 * d
    col_q = col_hy + off_qk
    col_k = col_q + wm
    col_v = col_hy + off_v
    col_o = col_hy + off_o

    h, gates = _norm_gates(x2, norm_g[None, :], w_in[:, off_gates:off_br], b_in[None, off_gates:off_br])
    p_big = _matmul_bias(h, w_cat, b_cat, BF16)
    p3 = p_big.reshape(bsz, length, p_big.shape[1])

    kp = LANES
    pad2 = lambda a, r, c: jnp.zeros((r, c), F32).at[:a.shape[0], :a.shape[1]].set(a)
    zp = pad2(zpos, length, kp)
    hspec, hnyq = _hyena_spectrum(
        zp, pad2(hy_w1, kp, kp), pad2(hy_b1[None, :], 1, kp), pad2(hy_w2, kp, kp), pad2(hy_b2[None, :], 1, kp),
        pad2(hy_freq[None, :], 1, kp), pad2(hy_w3, kp, hy_w3.shape[1]), hy_decay[None, :],
        ctab, stab, wh, HYENA_CT)
    y_hy = _hyena_conv(p3, col_hy, hy_conv_w, hy_conv_b[None, :], hspec, hnyq, hy_skip[:, None, :],
                       ctab, stab, wh, HYENA_CT, HYENA_RC)

    nh = MLSTM_HEADS
    nc = length // MLSTM_CHUNK
    gcol, grow = _gate_prep(gates.reshape(bsz, length, 4 * nh), MLSTM_CHUNK)
    gcol = gcol.reshape(bsz, length, 4, nh).transpose(0, 3, 1, 2)
    grow = grow.reshape(bsz, 4, nh, nc, MLSTM_CHUNK).transpose(0, 2, 3, 1, 4)
    y_ml = _mlstm(p3, col_q, col_k, col_v, col_o, ml_conv_w, ml_conv_b[None, :], ml_norm_g[None, :],
                  gcol, grow, MLSTM_CHUNK)

    return _merge(y_hy.reshape(n, wh), y_ml.reshape(n, wm), p_big, x2,
                  w_a.astype(BF16), w_b.astype(BF16), w_o.astype(BF16))


def kernel(x, mix_norm_g, mix_w_in, mix_b_in, hy_conv_w, hy_conv_b, hy_filt_w1, hy_filt_b1, hy_filt_w2,
           hy_filt_b2, hy_filt_w3, hy_filt_freq, hy_filt_decay, hy_skip, ml_conv_w, ml_conv_b, ml_norm_g,
           mix_w_a, mix_w_b, mix_w_o, ffn_norm_g, dense_w_gate, dense_w_up, dense_w_down, moe_router_w,
           moe_router_b, moe_w_gate, moe_w_up, moe_w_down, final_norm_g):
    bsz, length, d = x.shape
    depth = mix_norm_g.shape[0]
    x2 = x.reshape(bsz * length, d)
    ctab, stab = _dft_tables(length)
    zpos = _hyena_positions(length)
    normed = False
    for layer in range(depth):
        x2 = _mixer_layer(x2, bsz, length, mix_norm_g[layer], mix_w_in[layer], mix_b_in[layer],
                          hy_conv_w[layer], hy_conv_b[layer], hy_filt_w1[layer], hy_filt_b1[layer],
                          hy_filt_w2[layer], hy_filt_b2[layer], hy_filt_w3[layer], hy_filt_freq[layer],
                          hy_filt_decay[layer], hy_skip[layer], ml_conv_w[layer], ml_conv_b[layer],
                          ml_norm_g[layer], mix_w_a[layer], mix_w_b[layer], mix_w_o[layer], ctab, stab, zpos)
        j = layer // 2
        if layer % 2 == 0:
            x2 = _dense_ffn(x2, ffn_norm_g[layer][None, :], dense_w_gate[j].astype(BF16),
                            dense_w_up[j].astype(BF16), dense_w_down[j].astype(BF16))
        else:
            normed = layer == depth - 1
            x2 = _moe_layer(x2, ffn_norm_g[layer], moe_router_w[j], moe_router_b[j], moe_w_gate[j],
                            moe_w_up[j], moe_w_down[j], final_norm_g, normed)
    if not normed:
        x2 = _final_norm(x2, final_norm_g[None, :])
    return x2.reshape(bsz, length, d)
```

```python
import functools
import math

import jax
import jax.numpy as jnp
from jax import lax
from jax.experimental import pallas as pl
from jax.experimental.pallas import tpu as pltpu

F32, BF16, I32 = jnp.float32, jnp.bfloat16, jnp.int32
HIGHEST = lax.Precision.HIGHEST

HYENA_ORDER = 2
HYENA_POS_BANDS = 16
MLSTM_HEADS = 8
N_EXPERTS = 8
TOP_K = 2
EPS = 1e-6

V7X_VMEM_BYTES = 64 * 1024 * 1024
LANES = 128
MXU_DIM = 256

MLSTM_CHUNK = 256
HYENA_CT = MXU_DIM
HYENA_RC = 512
MOE_BLOCK = 512


def _cparams(semantics, vmem_mib):
    assert vmem_mib * 1024 * 1024 <= V7X_VMEM_BYTES
    return pltpu.CompilerParams(dimension_semantics=semantics, vmem_limit_bytes=vmem_mib * 1024 * 1024)


def _const_spec(shape):
    nd = len(shape)
    return pl.BlockSpec(shape, lambda *_: (0,) * nd, pipeline_mode=pl.Buffered(1))


def _rms(x, g):
    return x * lax.rsqrt(jnp.mean(x * x, axis=-1, keepdims=True) + EPS) * g


def _shift_rows(u, direction):
    n = u.shape[0]
    row = lax.broadcasted_iota(I32, u.shape, 0)
    if direction > 0:
        return jnp.where(row == 0, 0.0, pltpu.roll(u, 1, axis=0))
    return jnp.where(row == n - 1, 0.0, pltpu.roll(u, n - 1, axis=0))


def _short_conv(u, w, b):
    return b + _shift_rows(u, 1) * w[0:1] + u * w[1:2] + _shift_rows(u, -1) * w[2:3]


def _log_sigmoid(x):
    return jnp.minimum(x, 0.0) - jnp.log1p(jnp.exp(-jnp.abs(x)))


def _norm_gates_kernel(x_ref, g_ref, wg_ref, bg_ref, h_ref, gates_ref):
    hn = _rms(x_ref[...], g_ref[...])
    h_ref[...] = hn.astype(h_ref.dtype)
    gates_ref[...] = jnp.dot(hn, wg_ref[...], precision=HIGHEST, preferred_element_type=F32) + bg_ref[...]


def _norm_gates(x2, g, w_gates, b_gates, tm=512):
    n, d = x2.shape
    ng = w_gates.shape[1]
    return pl.pallas_call(
        _norm_gates_kernel,
        grid=(n // tm,),
        in_specs=[pl.BlockSpec((tm, d), lambda i: (i, 0)),
                  pl.BlockSpec((1, d), lambda i: (0, 0)),
                  pl.BlockSpec((d, ng), lambda i: (0, 0)),
                  pl.BlockSpec((1, ng), lambda i: (0, 0))],
        out_specs=[pl.BlockSpec((tm, d), lambda i: (i, 0)),
                   pl.BlockSpec((tm, ng), lambda i: (i, 0))],
        out_shape=[jax.ShapeDtypeStruct((n, d), BF16), jax.ShapeDtypeStruct((n, ng), F32)],
        compiler_params=_cparams(("parallel",), 32),
        name="norm_gates",
    )(x2, g, w_gates, b_gates)


def _matmul_bias_kernel(a_ref, w_ref, b_ref, o_ref):
    acc = jnp.dot(a_ref[...], w_ref[...], preferred_element_type=F32) + b_ref[...]
    o_ref[...] = acc.astype(o_ref.dtype)


def _matmul_bias(a, w, b, out_dtype, tm=1024, tn=1024):
    m, k = a.shape
    nc = w.shape[1]
    return pl.pallas_call(
        _matmul_bias_kernel,
        grid=(m // tm, nc // tn),
        in_specs=[pl.BlockSpec((tm, k), lambda i, j: (i, 0)),
                  pl.BlockSpec((k, tn), lambda i, j: (0, j)),
                  pl.BlockSpec((1, tn), lambda i, j: (0, j))],
        out_specs=pl.BlockSpec((tm, tn), lambda i, j: (i, j)),
        out_shape=jax.ShapeDtypeStruct((m, nc), out_dtype),
        compiler_params=_cparams(("parallel", "arbitrary"), 40),
        name="in_proj",
    )(a, w, b)


def _dft_tables(length):
    k = jnp.arange(length, dtype=I32)
    kn = (k[:, None] * k[None, :]) % (2 * length)
    ang = kn.astype(F32) * (math.pi / length)
    return jnp.cos(ang).astype(BF16), jnp.sin(ang).astype(BF16)


def _hyena_positions(length):
    t = jnp.linspace(0.0, 1.0, length, dtype=F32)[:, None]
    n = jnp.arange(length, dtype=F32)[:, None]
    bands = jnp.linspace(1e-4, HYENA_POS_BANDS - 1, HYENA_POS_BANDS, dtype=F32)[None, :]
    ang = (2.0 * math.pi / length) * n * bands
    return jnp.concatenate([t, jnp.cos(ang), -jnp.sin(ang)], axis=-1)


def _hyena_spectrum_kernel(z_ref, w1_ref, b1_ref, w2_ref, b2_ref, fr_ref, w3f_ref, w3b_ref,
                           decf_ref, decb_ref, c_ref, s_ref, h_ref, nyq_ref):
    length = c_ref.shape[0]
    dot_hi = functools.partial(jnp.dot, precision=HIGHEST, preferred_element_type=F32)
    z = z_ref[...]
    fr = fr_ref[...]
    hid = jnp.sin(fr * (dot_hi(z, w1_ref[...]) + b1_ref[...]))
    hid = jnp.sin(fr * (dot_hi(hid, w2_ref[...]) + b2_ref[...]))
    t = z[:, 0:1]
    ff = dot_hi(hid, w3f_ref[...]) * jnp.exp(-t * jnp.abs(decf_ref[...]))
    fb = dot_hi(hid, w3b_ref[...]) * jnp.exp(-t * jnp.abs(decb_ref[...]))
    row = lax.broadcasted_iota(I32, (length, 1), 0)
    l1 = jnp.sum(jnp.where(row == 0, jnp.abs(ff + fb), jnp.abs(ff) + jnp.abs(fb)), axis=0, keepdims=True)
    inv = 1.0 / l1
    even = (ff + fb) * inv
    odd = (ff - fb) * inv
    sign = jnp.where(row % 2 == 0, 1.0, -1.0)
    scale = jnp.where(row == 0, 0.5 / length, 1.0 / length)
    h_ref[0] = jnp.dot(c_ref[...], even.astype(BF16), preferred_element_type=F32) * scale
    h_ref[1] = -jnp.dot(s_ref[...], odd.astype(BF16), preferred_element_type=F32) * scale
    nyq_ref[...] = jnp.sum(sign * even, axis=0, keepdims=True) * (0.5 / length)


def _hyena_spectrum(z, w1, b1, w2, b2, freq, w3, decay, ctab, stab, width, ct):
    length = ctab.shape[0]
    kp = w1.shape[1]
    nct = width // ct
    col_f = lambda o, c: (0, o * 2 * nct + c)
    col_b = lambda o, c: (0, o * 2 * nct + nct + c)
    return pl.pallas_call(
        _hyena_spectrum_kernel,
        grid=(HYENA_ORDER, nct),
        in_specs=[_const_spec(z.shape), _const_spec(w1.shape), _const_spec(b1.shape),
                  _const_spec(w2.shape), _const_spec(b2.shape), _const_spec(freq.shape),
                  pl.BlockSpec((kp, ct), col_f), pl.BlockSpec((kp, ct), col_b),
                  pl.BlockSpec((1, ct), col_f), pl.BlockSpec((1, ct), col_b),
                  _const_spec(ctab.shape), _const_spec(stab.shape)],
        out_specs=[pl.BlockSpec((None, 2, length, ct), lambda o, c: (o, 0, 0, c)),
                   pl.BlockSpec((None, 1, ct), lambda o, c: (o, 0, c))],
        out_shape=[jax.ShapeDtypeStruct((HYENA_ORDER, 2, length, width), F32),
                   jax.ShapeDtypeStruct((HYENA_ORDER, 1, width), F32)],
        compiler_params=_cparams(("parallel", "parallel"), 48),
        name="hyena_spectrum",
    )(z, w1, b1, w2, b2, freq, w3, w3, decay, decay, ctab, stab)


def _hyena_conv_kernel(pv_ref, p1_ref, p2_ref, wv_ref, w1_ref, w2_ref, bv_ref, b1_ref, b2_ref,
                       h_ref, nyq_ref, skip_ref, c_ref, s_ref, y_ref,
                       z_s, g1_s, g2_s, zb_s, a_s, b_s, *, rc):
    length = c_ref.shape[0]
    z_s[...] = _short_conv(pv_ref[...].astype(F32), wv_ref[...], bv_ref[...])
    g1_s[...] = _short_conv(p1_ref[...].astype(F32), w1_ref[...], b1_ref[...])
    g2_s[...] = _short_conv(p2_ref[...].astype(F32), w2_ref[...], b2_ref[...])
    gates = (g1_s, g2_s)
    row = lax.broadcasted_iota(I32, (length, 1), 0)
    sign = jnp.where(row % 2 == 0, 1.0, -1.0)
    for o in range(HYENA_ORDER):
        z = z_s[...]
        zb_s[...] = z.astype(BF16)
        nyq = jnp.sum(sign * z, axis=0, keepdims=True) * nyq_ref[o]
        for r in range(length // rc):
            rows = slice(r * rc, (r + 1) * rc)
            re = jnp.dot(c_ref[rows, :], zb_s[...], preferred_element_type=F32)
            pim = jnp.dot(s_ref[rows, :], zb_s[...], preferred_element_type=F32)
            hre = h_ref[o, 0, rows, :]
            him = h_ref[o, 1, rows, :]
            a_s[rows, :] = (hre * re + him * pim).astype(BF16)
            b_s[rows, :] = (hre * pim - him * re).astype(BF16)
        for r in range(length // rc):
            rows = slice(r * rc, (r + 1) * rc)
            conv = (jnp.dot(c_ref[rows, :], a_s[...], preferred_element_type=F32)
                    + jnp.dot(s_ref[rows, :], b_s[...], preferred_element_type=F32)
                    + sign[rows] * nyq)
            znew = gates[o][rows, :] * (conv + skip_ref[o] * z_s[rows, :])
            if o == HYENA_ORDER - 1:
                y_ref[rows, :] = znew.astype(y_ref.dtype)
            else:
                z_s[rows, :] = znew


def _hyena_conv(p3, col0, conv_w, conv_b, hspec, hnyq, skip, ctab, stab, width, ct, rc):
    bsz, length, _ = p3.shape
    nct = width // ct
    c0 = col0 // ct

    def pspec(part):
        return pl.BlockSpec((None, length, ct), lambda c, b: (b, 0, c0 + part * nct + c))

    def wspec(rows, part):
        return pl.BlockSpec((rows, ct), lambda c, b: (0, part * nct + c))

    scratch = [pltpu.VMEM((length, ct), F32)] * 3 + [pltpu.VMEM((length, ct), BF16)] * 3
    return pl.pallas_call(
        functools.partial(_hyena_conv_kernel, rc=rc),
        grid=(nct, bsz),
        in_specs=[pspec(0), pspec(1), pspec(2),
                  wspec(3, 0), wspec(3, 1), wspec(3, 2), wspec(1, 0), wspec(1, 1), wspec(1, 2),
                  pl.BlockSpec((HYENA_ORDER, 2, length, ct), lambda c, b: (0, 0, 0, c),
                               pipeline_mode=pl.Buffered(1)),
                  pl.BlockSpec((HYENA_ORDER, 1, ct), lambda c, b: (0, 0, c)),
                  pl.BlockSpec((HYENA_ORDER, 1, ct), lambda c, b: (0, 0, c)),
                  _const_spec(ctab.shape), _const_spec(stab.shape)],
        out_specs=pl.BlockSpec((None, length, ct), lambda c, b: (b, 0, c)),
        out_shape=jax.ShapeDtypeStruct((bsz, length, width), BF16),
        scratch_shapes=scratch,
        compiler_params=_cparams(("parallel", "parallel"), 56),
        name="hyena_conv",
    )(p3, p3, p3, conv_w, conv_w, conv_w, conv_b, conv_b, conv_b, hspec, hnyq, skip, ctab, stab)


def _gate_prep_kernel(gc_ref, gr_ref, oc_ref, or_ref, *, chunk):
    length, nch = gc_ref.shape
    nh = nch // 4
    r = lax.broadcasted_iota(I32, (chunk, chunk), 0)
    c = lax.broadcasted_iota(I32, (chunk, chunk), 1)
    lower = (c <= r).astype(F32)
    upper = (c >= r).astype(F32)
    dot_hi = functools.partial(jnp.dot, precision=HIGHEST, preferred_element_type=F32)
    ccol = lax.broadcasted_iota(I32, (chunk, nch), 1)
    crow = lax.broadcasted_iota(I32, (nch, chunk), 0)
    for ch in range(length // chunk):
        rows = slice(ch * chunk, (ch + 1) * chunk)
        x = gc_ref[rows, :]
        lf = _log_sigmoid(x)
        pre = dot_hi(lower, lf)
        suf = dot_hi(upper, lf)
        oc_ref[rows, :] = jnp.where((ccol // nh) % 2 == 0, x, jnp.where(ccol < 2 * nh, pre, suf))
        xr = gr_ref[:, rows]
        lfr = _log_sigmoid(xr)
        pre_r = dot_hi(lfr, upper)
        suf_r = dot_hi(lfr, lower)
        or_ref[:, rows] = jnp.where((crow // nh) % 2 == 0, xr, jnp.where(crow < 2 * nh, pre_r, suf_r))


def _gate_prep(gates3, chunk):
    bsz, length, nch = gates3.shape
    gates_t = jnp.swapaxes(gates3, 1, 2)
    return pl.pallas_call(
        functools.partial(_gate_prep_kernel, chunk=chunk),
        grid=(bsz,),
        in_specs=[pl.BlockSpec((None, length, nch), lambda b: (b, 0, 0)),
                  pl.BlockSpec((None, nch, length), lambda b: (b, 0, 0))],
        out_specs=[pl.BlockSpec((None, length, nch), lambda b: (b, 0, 0)),
                   pl.BlockSpec((None, nch, length), lambda b: (b, 0, 0))],
        out_shape=[jax.ShapeDtypeStruct((bsz, length, nch), F32),
                   jax.ShapeDtypeStruct((bsz, nch, length), F32)],
        compiler_params=_cparams(("parallel",), 32),
        name="mlstm_gate_prep",
    )(gates3, gates_t)


def _mlstm_kernel(pq_ref, pk_ref, pv_ref, po_ref, cwq_ref, cwk_ref, cbq_ref, cbk_ref, ng_ref,
                  gcol_ref, grow_ref, y_ref, q_s, k_s, hf_s, *, chunk):
    length, dk = pq_ref.shape
    nc = length // chunk

    def conv_silu(p_ref, w_ref, b_ref):
        c = _short_conv(p_ref[...].astype(F32), w_ref[...], b_ref[...])
        return c * jax.nn.sigmoid(c)

    q_s[...] = (conv_silu(pq_ref, cwq_ref, cbq_ref) * (dk ** -0.5)).astype(BF16)
    k_s[...] = conv_silu(pk_ref, cwk_ref, cbk_ref).astype(BF16)

    row_i = lax.broadcasted_iota(I32, (chunk, chunk), 0)
    col_i = lax.broadcasted_iota(I32, (chunk, chunk), 1)

    def run_direction(li_idx, b_idx, reverse, emit):
        mask = (col_i >= row_i) if reverse else (col_i <= row_i)

        def body(step, carry):
            state, nvec, m = carry
            c = (nc - 1 - step) if reverse else step
            rows = pl.ds(pl.multiple_of(c * chunk, chunk), chunk)
            q = q_s[rows, :]
            k = k_s[rows, :]
            v = pv_ref[rows, :]
            gc = gcol_ref[rows, :]
            li_col = gc[:, li_idx:li_idx + 1]
            b_col = gc[:, b_idx:b_idx + 1]
            gr = grow_ref[c]
            li_row = gr[li_idx:li_idx + 1, :]
            b_row = gr[b_idx:b_idx + 1, :]
            g = b_col[0:1, :] if reverse else b_col[chunk - 1:chunk, :]
            a_col = g - b_col + li_col
            m_loc = jnp.max(a_col, axis=0, keepdims=True)
            kw = k.astype(F32) * jnp.exp(a_col - m_loc)
            ks = jnp.sum(kw, axis=0, keepdims=True)
            kv_t = lax.dot_general(kw.astype(BF16), v, (((0,), (0,)), ((), ())),
                                   preferred_element_type=F32)
            dmat = jnp.where(mask, b_col - b_row + li_row, -jnp.inf)
            inter = b_col + m
            m_j = jnp.maximum(inter, jnp.max(dmat, axis=1, keepdims=True))
            qk = lax.dot_general(q, k, (((1,), (1,)), ((), ())), preferred_element_type=F32)
            sc = qk * jnp.exp(dmat - m_j)
            s_inter = jnp.exp(inter - m_j)
            num = (jnp.dot(sc.astype(BF16), v, preferred_element_type=F32)
                   + s_inter * jnp.dot(q, state.astype(BF16), preferred_element_type=F32))
            den = (jnp.sum(sc, axis=1, keepdims=True)
                   + s_inter * jnp.sum(q.astype(F32) * nvec, axis=1, keepdims=True))
            emit(rows, num / jnp.maximum(jnp.abs(den), jnp.exp(-m_j)))
            m_new = jnp.maximum(g + m, m_loc)
            a_old = jnp.exp(g + m - m_new)
            a_new = jnp.exp(m_loc - m_new)
            return a_old * state + a_new * kv_t, a_old * nvec + a_new * ks, m_new

        init = (jnp.zeros((dk, pv_ref.shape[1]), F32), jnp.zeros((1, dk), F32), jnp.zeros((1, 1), F32))
        lax.fori_loop(0, nc, body, init)

    def emit_fwd(rows, h):
        hf_s[rows, :] = h

    def emit_bwd(rows, h):
        ht = hf_s[rows, :] + h
        hn = _rms(ht, ng_ref[...]) * jax.nn.sigmoid(po_ref[rows, :].astype(F32))
        y_ref[rows, :] = hn.astype(y_ref.dtype)

    run_direction(0, 1, False, emit_fwd)
    run_direction(2, 3, True, emit_bwd)


def _mlstm(p3, col_q, col_k, col_v, col_o, conv_w, conv_b, norm_g, gcol, grow, chunk):
    bsz, length, _ = p3.shape
    nh, dh = MLSTM_HEADS, norm_g.shape[1] // MLSTM_HEADS
    nc = length // chunk

    def pspec(col):
        return pl.BlockSpec((None, length, dh), lambda b, h: (b, 0, col // dh + h))

    def wspec(rows, part):
        return pl.BlockSpec((rows, dh), lambda b, h: (0, part * nh + h))

    return pl.pallas_call(
        functools.partial(_mlstm_kernel, chunk=chunk),
        grid=(bsz, nh),
        in_specs=[pspec(col_q), pspec(col_k), pspec(col_v), pspec(col_o),
                  wspec(3, 0), wspec(3, 1), wspec(1, 0), wspec(1, 1),
                  pl.BlockSpec((1, dh), lambda b, h: (0, h)),
                  pl.BlockSpec((None, None, length, 4), lambda b, h: (b, h, 0, 0)),
                  pl.BlockSpec((None, None, nc, 4, chunk), lambda b, h: (b, h, 0, 0, 0))],
        out_specs=pl.BlockSpec((None, length, dh), lambda b, h: (b, 0, h)),
        out_shape=jax.ShapeDtypeStruct((bsz, length, nh * dh), BF16),
        scratch_shapes=[pltpu.VMEM((length, dh), BF16), pltpu.VMEM((length, dh), BF16),
                        pltpu.VMEM((length, dh), F32)],
        compiler_params=_cparams(("parallel", "parallel"), 32),
        name="mlstm",
    )(p3, p3, p3, p3, conv_w, conv_w, conv_b, conv_b, norm_g, gcol, grow)


def _merge_kernel(yh_ref, ym_ref, gh_ref, gm_ref, x_ref, wa_ref, wb_ref, wo_ref, o_ref):
    a = jnp.dot(yh_ref[...], wa_ref[...], preferred_element_type=F32)
    b = jnp.dot(ym_ref[...], wb_ref[...], preferred_element_type=F32)
    t = jax.nn.sigmoid(gh_ref[...].astype(F32)) * a + jax.nn.sigmoid(gm_ref[...].astype(F32)) * b
    o_ref[...] = x_ref[...] + jnp.dot(t.astype(BF16), wo_ref[...], preferred_element_type=F32)


def _merge(y_hy, y_ml, p_big, x2, w_a, w_b, w_o, tm=256):
    n, d = x2.shape
    wh, wm = y_hy.shape[1], y_ml.shape[1]
    return pl.pallas_call(
        _merge_kernel,
        grid=(n // tm,),
        in_specs=[pl.BlockSpec((tm, wh), lambda i: (i, 0)),
                  pl.BlockSpec((tm, wm), lambda i: (i, 0)),
                  pl.BlockSpec((tm, d), lambda i: (i, 0)),
                  pl.BlockSpec((tm, d), lambda i: (i, 1)),
                  pl.BlockSpec((tm, d), lambda i: (i, 0)),
                  _const_spec(w_a.shape), _const_spec(w_b.shape), _const_spec(w_o.shape)],
        out_specs=pl.BlockSpec((tm, d), lambda i: (i, 0)),
        out_shape=jax.ShapeDtypeStruct((n, d), F32),
        compiler_params=_cparams(("parallel",), 48),
        name="branch_merge",
    )(y_hy, y_ml, p_big, p_big, x2, w_a, w_b, w_o)


def _dense_ffn_kernel(x_ref, g_ref, wg_ref, wu_ref, wd_ref, o_ref, h_s):
    @pl.when(pl.program_id(1) == 0)
    def _():
        x = x_ref[...]
        h_s[...] = _rms(x, g_ref[...]).astype(BF16)
        o_ref[...] = x

    h = h_s[...]
    gt = jnp.dot(h, wg_ref[...], preferred_element_type=F32)
    up = jnp.dot(h, wu_ref[...], preferred_element_type=F32)
    act = (gt * jax.nn.sigmoid(gt) * up).astype(BF16)
    o_ref[...] += jnp.dot(act, wd_ref[...], preferred_element_type=F32)


def _dense_ffn(x2, g, w_gate, w_up, w_down, tm=512, tf=512):
    n, d = x2.shape
    f = w_gate.shape[1]
    return pl.pallas_call(
        _dense_ffn_kernel,
        grid=(n // tm, f // tf),
        in_specs=[pl.BlockSpec((tm, d), lambda i, j: (i, 0)),
                  pl.BlockSpec((1, d), lambda i, j: (0, 0)),
                  pl.BlockSpec((d, tf), lambda i, j: (0, j)),
                  pl.BlockSpec((d, tf), lambda i, j: (0, j)),
                  pl.BlockSpec((tf, d), lambda i, j: (j, 0))],
        out_specs=pl.BlockSpec((tm, d), lambda i, j: (i, 0)),
        out_shape=jax.ShapeDtypeStruct((n, d), F32),
        scratch_shapes=[pltpu.VMEM((tm, d), BF16)],
        compiler_params=_cparams(("parallel", "arbitrary"), 48),
        name="dense_swiglu",
    )(x2, g, w_gate, w_up, w_down)


def _store_rows_3d(ref3, val2):
    for s in range(ref3.shape[1]):
        ref3[:, s, :] = val2[:, s * LANES:(s + 1) * LANES]


def _router_kernel(x_ref, g_ref, rw_ref, rb_ref, h_ref, e_ref, gate_ref):
    hn = _rms(x_ref[...], g_ref[...])
    _store_rows_3d(h_ref, hn)
    logits = jnp.dot(hn, rw_ref[...], precision=HIGHEST, preferred_element_type=F32) + rb_ref[...]
    lane = lax.broadcasted_iota(I32, logits.shape, 1)
    logits = jnp.where(lane < N_EXPERTS, logits, -jnp.inf)
    m1 = jnp.max(logits, axis=-1, keepdims=True)
    i1 = jnp.min(jnp.where(logits == m1, lane, LANES), axis=-1, keepdims=True)
    rest = jnp.where(lane == i1, -jnp.inf, logits)
    m2 = jnp.max(rest, axis=-1, keepdims=True)
    i2 = jnp.min(jnp.where(rest == m2, lane, LANES), axis=-1, keepdims=True)
    e = jnp.exp(m2 - m1)
    slot = lax.broadcasted_iota(I32, e_ref.shape, 1)
    e_ref[...] = jnp.where(slot == 0, i1, i2)
    gate_ref[...] = jnp.where(slot == 0, 1.0 / (1.0 + e), e / (1.0 + e))


def _router(x2, g, rw_pad, rb_pad, tm=512):
    n, d = x2.shape
    s = d // LANES
    return pl.pallas_call(
        _router_kernel,
        grid=(n // tm,),
        in_specs=[pl.BlockSpec((tm, d), lambda i: (i, 0)),
                  pl.BlockSpec((1, d), lambda i: (0, 0)),
                  pl.BlockSpec((d, LANES), lambda i: (0, 0)),
                  pl.BlockSpec((1, LANES), lambda i: (0, 0))],
        out_specs=[pl.BlockSpec((tm, s, LANES), lambda i: (i, 0, 0)),
                   pl.BlockSpec((tm, TOP_K), lambda i: (i, 0)),
                   pl.BlockSpec((tm, TOP_K), lambda i: (i, 0))],
        out_shape=[jax.ShapeDtypeStruct((n, s, LANES), F32),
                   jax.ShapeDtypeStruct((n, TOP_K), I32),
                   jax.ShapeDtypeStruct((n, TOP_K), F32)],
        compiler_params=_cparams(("parallel",), 32),
        name="moe_router",
    )(x2, g, rw_pad, rb_pad)


def _row_gather(src_hbm, idx_ref, idx_base, dst, sem, rows):
    def copy(r):
        return pltpu.make_async_copy(src_hbm.at[idx_ref[idx_base + r]], dst.at[r], sem)

    def start_one(r, carry):
        copy(r).start()
        return carry

    def wait_one(r, carry):
        copy(r).wait()
        return carry

    return (lambda: lax.fori_loop(0, rows, start_one, 0)), (lambda: lax.fori_loop(0, rows, wait_one, 0))


def _expert_ffn_kernel(be_ref, nu_ref, tok_ref, h_hbm, gate_ref, wg_ref, wu_ref, wd_ref, o_ref,
                       xg_s, x2_s, h_s, acc_s, sems):
    i, j = pl.program_id(0), pl.program_id(1)
    blk = xg_s.shape[1]
    n_used = nu_ref[0]

    def gather(block):
        slot = block % 2
        return _row_gather(h_hbm, tok_ref, block * blk, xg_s.at[slot], sems.at[slot], blk)

    @pl.when(jnp.logical_and(i == 0, j == 0))
    def _():
        gather(0)[0]()

    @pl.when(jnp.logical_and(j == 0, i + 1 < n_used))
    def _():
        gather(i + 1)[0]()

    @pl.when(jnp.logical_and(j == 0, i < n_used))
    def _():
        gather(i)[1]()
        slot = i % 2
        for s in range(xg_s.shape[2]):
            x2_s[:, s * LANES:(s + 1) * LANES] = xg_s[slot, :, s, :]
        h_s[...] = x2_s[...].astype(BF16)

    @pl.when(j == 0)
    def _():
        acc_s[...] = jnp.zeros_like(acc_s)

    @pl.when(i < n_used)
    def _():
        h = h_s[...]
        gt = jnp.dot(h, wg_ref[...], preferred_element_type=F32)
        up = jnp.dot(h, wu_ref[...], preferred_element_type=F32)
        act = (gt * jax.nn.sigmoid(gt) * up).astype(BF16)
        acc_s[...] += jnp.dot(act, wd_ref[...], preferred_element_type=F32)

    @pl.when(j == pl.num_programs(1) - 1)
    def _():
        gate = jnp.broadcast_to(gate_ref[...], (gate_ref.shape[0], LANES))
        for s in range(o_ref.shape[1]):
            o_ref[:, s, :] = acc_s[:, s * LANES:(s + 1) * LANES] * gate


def _expert_ffn(h3, slot_tok, slot_gate, block_e, n_used, w_gate, w_up, w_down, blk, tf=512):
    ns = slot_tok.shape[0]
    s = h3.shape[1]
    d = s * LANES
    f = w_gate.shape[2]
    nj = f // tf

    def jj(i, j, nu):
        return jnp.where(i < nu[0], j, nj - 1)

    return pl.pallas_call(
        _expert_ffn_kernel,
        grid_spec=pltpu.PrefetchScalarGridSpec(
            num_scalar_prefetch=3,
            grid=(ns // blk, nj),
            in_specs=[pl.BlockSpec(memory_space=pl.ANY),
                      pl.BlockSpec((blk, 1), lambda i, j, be, nu, tok: (i, 0)),
                      pl.BlockSpec((None, d, tf), lambda i, j, be, nu, tok: (be[i], 0, jj(i, j, nu))),
                      pl.BlockSpec((None, d, tf), lambda i, j, be, nu, tok: (be[i], 0, jj(i, j, nu))),
                      pl.BlockSpec((None, tf, d), lambda i, j, be, nu, tok: (be[i], jj(i, j, nu), 0))],
            out_specs=pl.BlockSpec((blk, s, LANES), lambda i, j, be, nu, tok: (i, 0, 0)),
            scratch_shapes=[pltpu.VMEM((2, blk, s, LANES), F32), pltpu.VMEM((blk, d), F32),
                            pltpu.VMEM((blk, d), BF16), pltpu.VMEM((blk, d), F32),
                            pltpu.SemaphoreType.DMA((2,))],
        ),
        out_shape=jax.ShapeDtypeStruct((ns, s, LANES), F32),
        compiler_params=_cparams(("arbitrary", "arbitrary"), 52),
        name="expert_swiglu",
    )(block_e, n_used, slot_tok, h3, slot_gate, w_gate, w_up, w_down)


def _combine_kernel(dest_ref, x_ref, y_hbm, g_ref, o_ref, yg_s, sems, *, final_norm):
    i, n_tiles = pl.program_id(0), pl.num_programs(0)
    tm = x_ref.shape[0]

    def gather(tile):
        slot = tile % 2
        parts = [_row_gather(y_hbm, dest_ref, k * n_tiles * tm + tile * tm, yg_s.at[slot, k], sems.at[slot], tm)
                 for k in range(TOP_K)]
        return (lambda: [p[0]() for p in parts]), (lambda: [p[1]() for p in parts])

    @pl.when(i == 0)
    def _():
        gather(0)[0]()

    @pl.when(i + 1 < n_tiles)
    def _():
        gather(i + 1)[0]()

    gather(i)[1]()
    slot = i % 2
    for s in range(yg_s.shape[3]):
        cols = slice(s * LANES, (s + 1) * LANES)
        acc = x_ref[:, cols]
        for k in range(TOP_K):
            acc = acc + yg_s[slot, k, :, s, :]
        o_ref[:, cols] = acc
    if final_norm:
        o_ref[...] = _rms(o_ref[...], g_ref[...])


def _combine(x2, yb, dest_by_k, g, final_norm, tm=512):
    n, d = x2.shape
    s = d // LANES
    return pl.pallas_call(
        functools.partial(_combine_kernel, final_norm=final_norm),
        grid_spec=pltpu.PrefetchScalarGridSpec(
            num_scalar_prefetch=1,
            grid=(n // tm,),
            in_specs=[pl.BlockSpec((tm, d), lambda i, dest: (i, 0)),
                      pl.BlockSpec(memory_space=pl.ANY),
                      pl.BlockSpec((1, d), lambda i, dest: (0, 0))],
            out_specs=pl.BlockSpec((tm, d), lambda i, dest: (i, 0)),
            scratch_shapes=[pltpu.VMEM((2, TOP_K, tm, s, LANES), F32), pltpu.SemaphoreType.DMA((2,))],
        ),
        out_shape=jax.ShapeDtypeStruct((n, d), F32),
        compiler_params=_cparams(("arbitrary",), 48),
        name="moe_combine",
    )(dest_by_k, x2, yb, g)


def _final_norm_kernel(x_ref, g_ref, o_ref):
    o_ref[...] = _rms(x_ref[...], g_ref[...])


def _final_norm(x2, g, tm=512):
    n, d = x2.shape
    return pl.pallas_call(
        _final_norm_kernel,
        grid=(n // tm,),
        in_specs=[pl.BlockSpec((tm, d), lambda i: (i, 0)), pl.BlockSpec((1, d), lambda i: (0, 0))],
        out_specs=pl.BlockSpec((tm, d), lambda i: (i, 0)),
        out_shape=jax.ShapeDtypeStruct((n, d), F32),
        compiler_params=_cparams(("parallel",), 32),
        name="final_norm",
    )(x2, g)


def _routing_tables(top_e, gate, blk):
    n = top_e.shape[0]
    nk = n * TOP_K
    flat_e = top_e.reshape(-1)
    onehot = (flat_e[:, None] == jnp.arange(N_EXPERTS, dtype=I32)[None, :]).astype(I32)
    rank = jnp.sum((jnp.cumsum(onehot, axis=0) - onehot) * onehot, axis=1)
    counts = jnp.sum(onehot, axis=0)
    padded = (counts + blk - 1) // blk * blk
    p_end = jnp.cumsum(padded)
    p_start = p_end - padded
    dest = (p_start[flat_e] + rank).astype(I32)
    n_blocks = -(-nk // blk) + N_EXPERTS
    flat_tok = jnp.arange(nk, dtype=I32) // TOP_K
    slot_tok = jnp.zeros((n_blocks * blk,), I32).at[dest].set(flat_tok)
    slot_gate = jnp.zeros((n_blocks * blk,), F32).at[dest].set(gate.reshape(-1))
    n_used = (p_end[-1] // blk).astype(I32)
    blocks = jnp.arange(n_blocks, dtype=I32)
    block_e = jnp.sum((p_end[None, :] <= (blocks * blk)[:, None]).astype(I32), axis=1)
    block_e = jnp.minimum(block_e, N_EXPERTS - 1)
    block_e = jnp.where(blocks < n_used, block_e, block_e[n_used - 1])
    dest_by_k = dest.reshape(n, TOP_K).T.reshape(-1)
    return slot_tok, slot_gate[:, None], block_e, n_used.reshape(1), dest_by_k


def _moe_layer(x2, norm_g, router_w, router_b, w_gate, w_up, w_down, out_g, final_norm):
    n, d = x2.shape
    rw = jnp.zeros((d, LANES), F32).at[:, :N_EXPERTS].set(router_w)
    rb = jnp.zeros((1, LANES), F32).at[0, :N_EXPERTS].set(router_b)
    h, top_e, gate = _router(x2, norm_g[None, :], rw, rb)
    slot_tok, slot_gate, block_e, n_used, dest_by_k = _routing_tables(top_e, gate, MOE_BLOCK)
    yb = _expert_ffn(h, slot_tok, slot_gate, block_e, n_used, w_gate.astype(BF16), w_up.astype(BF16),
                     w_down.astype(BF16), MOE_BLOCK)
    return _combine(x2, yb, dest_by_k, out_g[None, :], final_norm)


def _mixer_layer(x2, bsz, length, norm_g, w_in, b_in, hy_conv_w, hy_conv_b, hy_w1, hy_b1, hy_w2, hy_b2,
                 hy_w3, hy_freq, hy_decay, hy_skip, ml_conv_w, ml_conv_b, ml_norm_g, w_a, w_b, w_o,
                 ctab, stab, zpos):
    n, d = x2.shape
    wh = hy_skip.shape[1]
    wm = ml_norm_g.shape[0]
    off_qk = (HYENA_ORDER + 1) * wh
    off_v = off_qk + 2 * wm
    off_o = off_v + wm
    off_gates = off_o + wm
    off_br = off_gates + 4 * MLSTM_HEADS
    w_cat = jnp.concatenate([w_in[:, off_br:], w_in[:, :off_gates]], axis=1).astype(BF16)
    b_cat = jnp.concatenate([b_in[off_br:], b_in[:off_gates]])[None, :]
    col_hy = 2 * d
    col_q = col_hy + off_qk
    col_k = col_q + wm
    col_v = col_hy + off_v
    col_o = col_hy + off_o

    h, gates = _norm_gates(x2, norm_g[None, :], w_in[:, off_gates:off_br], b_in[None, off_gates:off_br])
    p_big = _matmul_bias(h, w_cat, b_cat, BF16)
    p3 = p_big.reshape(bsz, length, p_big.shape[1])

    kp = LANES
    pad2 = lambda a, r, c: jnp.zeros((r, c), F32).at[:a.shape[0], :a.shape[1]].set(a)
    zp = pad2(zpos, length, kp)
    hspec, hnyq = _hyena_spectrum(
        zp, pad2(hy_w1, kp, kp), pad2(hy_b1[None, :], 1, kp), pad2(hy_w2, kp, kp), pad2(hy_b2[None, :], 1, kp),
        pad2(hy_freq[None, :], 1, kp), pad2(hy_w3, kp, hy_w3.shape[1]), hy_decay[None, :],
        ctab, stab, wh, HYENA_CT)
    y_hy = _hyena_conv(p3, col_hy, hy_conv_w, hy_conv_b[None, :], hspec, hnyq, hy_skip[:, None, :],
                       ctab, stab, wh, HYENA_CT, HYENA_RC)

    nh = MLSTM_HEADS
    nc = length // MLSTM_CHUNK
    gcol, grow = _gate_prep(gates.reshape(bsz, length, 4 * nh), MLSTM_CHUNK)
    gcol = gcol.reshape(bsz, length, 4, nh).transpose(0, 3, 1, 2)
    grow = grow.reshape(bsz, 4, nh, nc, MLSTM_CHUNK).transpose(0, 2, 3, 1, 4)
    y_ml = _mlstm(p3, col_q, col_k, col_v, col_o, ml_conv_w, ml_conv_b[None, :], ml_norm_g[None, :],
                  gcol, grow, MLSTM_CHUNK)

    return _merge(y_hy.reshape(n, wh), y_ml.reshape(n, wm), p_big, x2,
                  w_a.astype(BF16), w_b.astype(BF16), w_o.astype(BF16))


def kernel(x, mix_norm_g, mix_w_in, mix_b_in, hy_conv_w, hy_conv_b, hy_filt_w1, hy_filt_b1, hy_filt_w2,
           hy_filt_b2, hy_filt_w3, hy_filt_freq, hy_filt_decay, hy_skip, ml_conv_w, ml_conv_b, ml_norm_g,
           mix_w_a, mix_w_b, mix_w_o, ffn_norm_g, dense_w_gate, dense_w_up, dense_w_down, moe_router_w,
           moe_router_b, moe_w_gate, moe_w_up, moe_w_down, final_norm_g):
    bsz, length, d = x.shape
    depth = mix_norm_g.shape[0]
    x2 = x.reshape(bsz * length, d)
    ctab, stab = _dft_tables(length)
    zpos = _hyena_positions(length)
    normed = False
    for layer in range(depth):
        x2 = _mixer_layer(x2, bsz, length, mix_norm_g[layer], mix_w_in[layer], mix_b_in[layer],
                          hy_conv_w[layer], hy_conv_b[layer], hy_filt_w1[layer], hy_filt_b1[layer],
                          hy_filt_w2[layer], hy_filt_b2[layer], hy_filt_w3[layer], hy_filt_freq[layer],
                          hy_filt_decay[layer], hy_skip[layer], ml_conv_w[layer], ml_conv_b[layer],
                          ml_norm_g[layer], mix_w_a[layer], mix_w_b[layer], mix_w_o[layer], ctab, stab, zpos)
        j = layer // 2
        if layer % 2 == 0:
            x2 = _dense_ffn(x2, ffn_norm_g[layer][None, :], dense_w_gate[j].astype(BF16),
                            dense_w_up[j].astype(BF16), dense_w_down[j].astype(BF16))
        else:
            normed = layer == depth - 1
            x2 = _moe_layer(x2, ffn_norm_g[layer], moe_router_w[j], moe_router_b[j], moe_w_gate[j],
                            moe_w_up[j], moe_w_down[j], final_norm_g, normed)
    if not normed:
        x2 = _final_norm(x2, final_norm_g[None, :])
    return x2.reshape(bsz, length, d)
```

```python
import functools
import math

import jax
import jax.numpy as jnp
import numpy as np
from jax import lax
from jax.experimental import pallas as pl
from jax.experimental.pallas import tpu as pltpu

F32, BF16, I32 = jnp.float32, jnp.bfloat16, jnp.int32
HIGHEST = lax.Precision.HIGHEST

HYENA_ORDER = 2
HYENA_POS_BANDS = 16
MLSTM_HEADS = 8
N_EXPERTS = 8
TOP_K = 2
EPS = 1e-6

V7X_VMEM_BYTES = 64 * 1024 * 1024
LANES = 128
MXU_DIM = 256

MLSTM_CHUNK = 256
HYENA_CT = MXU_DIM
HYENA_RC = 1024
MOE_BLOCK = 512


def _cparams(semantics, vmem_mib):
    assert vmem_mib * 1024 * 1024 <= V7X_VMEM_BYTES
    return pltpu.CompilerParams(dimension_semantics=semantics, vmem_limit_bytes=vmem_mib * 1024 * 1024)


def _const_spec(shape):
    nd = len(shape)
    return pl.BlockSpec(shape, lambda *_: (0,) * nd, pipeline_mode=pl.Buffered(1))


def _rms(x, g):
    return x * lax.rsqrt(jnp.mean(x * x, axis=-1, keepdims=True) + EPS) * g


def _shift_rows(u, direction):
    n = u.shape[0]
    row = lax.broadcasted_iota(I32, u.shape, 0)
    if direction > 0:
        return jnp.where(row == 0, 0.0, pltpu.roll(u, 1, axis=0))
    return jnp.where(row == n - 1, 0.0, pltpu.roll(u, n - 1, axis=0))


def _short_conv(u, w, b):
    return b + _shift_rows(u, 1) * w[0:1] + u * w[1:2] + _shift_rows(u, -1) * w[2:3]


def _log_sigmoid(x):
    return jnp.minimum(x, 0.0) - jnp.log1p(jnp.exp(-jnp.abs(x)))


def _dot_split(a, w_hi, w_lo):
    a_hi = a.astype(BF16)
    a_lo = (a - a_hi.astype(F32)).astype(BF16)
    return (jnp.dot(a_hi, w_hi, preferred_element_type=F32) + jnp.dot(a_lo, w_hi, preferred_element_type=F32)
            + jnp.dot(a_hi, w_lo, preferred_element_type=F32))


def _split_bf16(w):
    w_hi = w.astype(BF16)
    return w_hi, (w - w_hi.astype(F32)).astype(BF16)


def _in_proj_kernel(x_ref, g_ref, wgh_ref, wgl_ref, bg_ref, w_ref, b_ref, o_ref, gates_ref, h_s):
    @pl.when(pl.program_id(1) == 0)
    def _():
        hn = _rms(x_ref[...], g_ref[...])
        h_s[...] = hn.astype(BF16)
        gates_ref[...] = _dot_split(hn, wgh_ref[...], wgl_ref[...]) + bg_ref[...]

    acc = jnp.dot(h_s[...], w_ref[...], preferred_element_type=F32) + b_ref[...]
    o_ref[...] = acc.astype(o_ref.dtype)


def _in_proj(x2, g, w_gates, b_gates, w, b, tm=1024, tn=1024):
    n, d = x2.shape
    nc = w.shape[1]
    ng = w_gates.shape[1]
    wg_hi, wg_lo = _split_bf16(w_gates)
    return pl.pallas_call(
        _in_proj_kernel,
        grid=(n // tm, nc // tn),
        in_specs=[pl.BlockSpec((tm, d), lambda i, j: (i, 0)),
                  pl.BlockSpec((1, d), lambda i, j: (0, 0)),
                  pl.BlockSpec((d, ng), lambda i, j: (0, 0)),
                  pl.BlockSpec((d, ng), lambda i, j: (0, 0)),
                  pl.BlockSpec((1, ng), lambda i, j: (0, 0)),
                  pl.BlockSpec((d, tn), lambda i, j: (0, j)),
                  pl.BlockSpec((1, tn), lambda i, j: (0, j))],
        out_specs=[pl.BlockSpec((tm, tn), lambda i, j: (i, j)),
                   pl.BlockSpec((tm, ng), lambda i, j: (i, 0))],
        out_shape=[jax.ShapeDtypeStruct((n, nc), BF16), jax.ShapeDtypeStruct((n, ng), F32)],
        scratch_shapes=[pltpu.VMEM((tm, d), BF16)],
        compiler_params=_cparams(("parallel", "arbitrary"), 48),
        name="in_proj",
    )(x2, g, wg_hi, wg_lo, b_gates, w, b)


def _dft_tables(length):
    k = jnp.arange(length, dtype=I32)
    kn = (k[:, None] * k[None, :]) % (2 * length)
    ang = kn.astype(F32) * (math.pi / length)
    return jnp.cos(ang).astype(BF16), jnp.sin(ang).astype(BF16)


def _hyena_positions(length):
    t = np.linspace(0.0, 1.0, length)[:, None]
    n = np.arange(length, dtype=np.float64)[:, None]
    bands = np.linspace(1e-4, HYENA_POS_BANDS - 1, HYENA_POS_BANDS)[None, :]
    ang = (2.0 * np.pi / length) * n * bands
    return jnp.asarray(np.concatenate([t, np.cos(ang), -np.sin(ang)], axis=-1), dtype=F32)


def _hyena_spectrum_kernel(z_ref, w1_ref, b1_ref, w2_ref, b2_ref, fr_ref, w3f_ref, w3b_ref,
                           decf_ref, decb_ref, c_ref, s_ref, h_ref, nyq_ref):
    length = c_ref.shape[0]
    dot_hi = functools.partial(jnp.dot, precision=HIGHEST, preferred_element_type=F32)
    z = z_ref[...]
    fr = fr_ref[...]
    hid = jnp.sin(fr * (dot_hi(z, w1_ref[...]) + b1_ref[...]))
    hid = jnp.sin(fr * (dot_hi(hid, w2_ref[...]) + b2_ref[...]))
    t = z[:, 0:1]
    ff = dot_hi(hid, w3f_ref[...]) * jnp.exp(-t * jnp.abs(decf_ref[...]))
    fb = dot_hi(hid, w3b_ref[...]) * jnp.exp(-t * jnp.abs(decb_ref[...]))
    row = lax.broadcasted_iota(I32, (length, 1), 0)
    l1 = jnp.sum(jnp.where(row == 0, jnp.abs(ff + fb), jnp.abs(ff) + jnp.abs(fb)), axis=0, keepdims=True)
    inv = 1.0 / l1
    even = (ff + fb) * inv
    odd = (ff - fb) * inv
    sign = jnp.where(row % 2 == 0, 1.0, -1.0)
    scale = jnp.where(row == 0, 0.5 / length, 1.0 / length)
    h_ref[0] = jnp.dot(c_ref[...], even.astype(BF16), preferred_element_type=F32) * scale
    h_ref[1] = -jnp.dot(s_ref[...], odd.astype(BF16), preferred_element_type=F32) * scale
    nyq_ref[...] = jnp.sum(sign * even, axis=0, keepdims=True) * (0.5 / length)


def _hyena_spectrum(z, w1, b1, w2, b2, freq, w3, decay, ctab, stab, width, ct):
    length = ctab.shape[0]
    kp = w1.shape[1]
    nct = width // ct
    col_f = lambda o, c: (0, o * 2 * nct + c)
    col_b = lambda o, c: (0, o * 2 * nct + nct + c)
    return pl.pallas_call(
        _hyena_spectrum_kernel,
        grid=(HYENA_ORDER, nct),
        in_specs=[_const_spec(z.shape), _const_spec(w1.shape), _const_spec(b1.shape),
                  _const_spec(w2.shape), _const_spec(b2.shape), _const_spec(freq.shape),
                  pl.BlockSpec((kp, ct), col_f), pl.BlockSpec((kp, ct), col_b),
                  pl.BlockSpec((1, ct), col_f), pl.BlockSpec((1, ct), col_b),
                  _const_spec(ctab.shape), _const_spec(stab.shape)],
        out_specs=[pl.BlockSpec((None, 2, length, ct), lambda o, c: (o, 0, 0, c)),
                   pl.BlockSpec((None, 1, ct), lambda o, c: (o, 0, c))],
        out_shape=[jax.ShapeDtypeStruct((HYENA_ORDER, 2, length, width), F32),
                   jax.ShapeDtypeStruct((HYENA_ORDER, 1, width), F32)],
        compiler_params=_cparams(("parallel", "parallel"), 48),
        name="hyena_spectrum",
    )(z, w1, b1, w2, b2, freq, w3, w3, decay, decay, ctab, stab)


def _hyena_conv_kernel(pv_ref, p1_ref, p2_ref, wv_ref, w1_ref, w2_ref, bv_ref, b1_ref, b2_ref,
                       h_ref, nyq_ref, skip_ref, c_ref, s_ref, y_ref,
                       z_s, g1_s, g2_s, zb_s, a_s, b_s, *, rc):
    length = c_ref.shape[0]
    z_s[...] = _short_conv(pv_ref[...].astype(F32), wv_ref[...], bv_ref[...])
    g1_s[...] = _short_conv(p1_ref[...].astype(F32), w1_ref[...], b1_ref[...])
    g2_s[...] = _short_conv(p2_ref[...].astype(F32), w2_ref[...], b2_ref[...])
    gates = (g1_s, g2_s)
    row = lax.broadcasted_iota(I32, (length, 1), 0)
    sign = jnp.where(row % 2 == 0, 1.0, -1.0)
    for o in range(HYENA_ORDER):
        z = z_s[...]
        zb_s[...] = z.astype(BF16)
        nyq = jnp.sum(sign * z, axis=0, keepdims=True) * nyq_ref[o]
        for r in range(length // rc):
            rows = slice(r * rc, (r + 1) * rc)
            re = jnp.dot(c_ref[rows, :], zb_s[...], preferred_element_type=F32)
            pim = jnp.dot(s_ref[rows, :], zb_s[...], preferred_element_type=F32)
            hre = h_ref[o, 0, rows, :]
            him = h_ref[o, 1, rows, :]
            a_s[rows, :] = (hre * re + him * pim).astype(BF16)
            b_s[rows, :] = (hre * pim - him * re).astype(BF16)
        for r in range(length // rc):
            rows = slice(r * rc, (r + 1) * rc)
            conv = (jnp.dot(c_ref[rows, :], a_s[...], preferred_element_type=F32)
                    + jnp.dot(s_ref[rows, :], b_s[...], preferred_element_type=F32)
                    + sign[rows] * nyq)
            znew = gates[o][rows, :] * (conv + skip_ref[o] * z_s[rows, :])
            if o == HYENA_ORDER - 1:
                y_ref[rows, :] = znew.astype(y_ref.dtype)
            else:
                z_s[rows, :] = znew


def _hyena_conv(p3, col0, conv_w, conv_b, hspec, hnyq, skip, ctab, stab, width, ct, rc):
    bsz, length, _ = p3.shape
    nct = width // ct
    c0 = col0 // ct

    def pspec(part):
        return pl.BlockSpec((None, length, ct), lambda c, b: (b, 0, c0 + part * nct + c))

    def wspec(rows, part):
        return pl.BlockSpec((rows, ct), lambda c, b: (0, part * nct + c))

    scratch = [pltpu.VMEM((length, ct), F32)] * 3 + [pltpu.VMEM((length, ct), BF16)] * 3
    return pl.pallas_call(
        functools.partial(_hyena_conv_kernel, rc=rc),
        grid=(nct, bsz),
        in_specs=[pspec(0), pspec(1), pspec(2),
                  wspec(3, 0), wspec(3, 1), wspec(3, 2), wspec(1, 0), wspec(1, 1), wspec(1, 2),
                  pl.BlockSpec((HYENA_ORDER, 2, length, ct), lambda c, b: (0, 0, 0, c),
                               pipeline_mode=pl.Buffered(1)),
                  pl.BlockSpec((HYENA_ORDER, 1, ct), lambda c, b: (0, 0, c)),
                  pl.BlockSpec((HYENA_ORDER, 1, ct), lambda c, b: (0, 0, c)),
                  _const_spec(ctab.shape), _const_spec(stab.shape)],
        out_specs=pl.BlockSpec((None, length, ct), lambda c, b: (b, 0, c)),
        out_shape=jax.ShapeDtypeStruct((bsz, length, width), BF16),
        scratch_shapes=scratch,
        compiler_params=_cparams(("parallel", "parallel"), 56),
        name="hyena_conv",
    )(p3, p3, p3, conv_w, conv_w, conv_w, conv_b, conv_b, conv_b, hspec, hnyq, skip, ctab, stab)


def _gate_prep_kernel(ic_ref, fc_ref, ir_ref, fr_ref, col_ref, row_ref, *, chunk):
    length, nch = ic_ref.shape
    nh = nch // 2
    r = lax.broadcasted_iota(I32, (chunk, chunk), 0)
    c = lax.broadcasted_iota(I32, (chunk, chunk), 1)
    lower = (c <= r).astype(F32)
    upper = (c >= r).astype(F32)
    dot_hi = functools.partial(jnp.dot, precision=HIGHEST, preferred_element_type=F32)
    fwd_col = lax.broadcasted_iota(I32, (chunk, nch), 1) < nh
    fwd_row = lax.broadcasted_iota(I32, (nch, chunk), 0) < nh
    pos = lax.broadcasted_iota(I32, (chunk, nch), 0)
    for ch in range(length // chunk):
        rows = slice(ch * chunk, (ch + 1) * chunk)
        lf = _log_sigmoid(fc_ref[rows, :])
        b = jnp.where(fwd_col, dot_hi(lower, lf), dot_hi(upper, lf))
        u = ic_ref[rows, :] - b
        cm_f, cm_b = u, u
        shift = 1
        while shift < chunk:
            cm_f = jnp.maximum(cm_f, jnp.where(pos >= shift, pltpu.roll(cm_f, shift, axis=0), -jnp.inf))
            cm_b = jnp.maximum(cm_b, jnp.where(pos + shift < chunk, pltpu.roll(cm_b, chunk - shift, axis=0), -jnp.inf))
            shift *= 2
        col_ref[0, rows, :] = b
        col_ref[1, rows, :] = u
        col_ref[2, rows, :] = jnp.where(fwd_col, cm_f, cm_b)
        lfr = _log_sigmoid(fr_ref[:, rows])
        b_r = jnp.where(fwd_row, dot_hi(lfr, upper), dot_hi(lfr, lower))
        row_ref[:, rows] = ir_ref[:, rows] - b_r


def _gate_prep(gates3, chunk):
    bsz, length, nch4 = gates3.shape
    nh = nch4 // 4
    i_col = jnp.concatenate([gates3[..., :nh], gates3[..., 2 * nh:3 * nh]], axis=-1)
    f_col = jnp.concatenate([gates3[..., nh:2 * nh], gates3[..., 3 * nh:]], axis=-1)
    nch = 2 * nh
    cspec = pl.BlockSpec((None, length, nch), lambda b: (b, 0, 0))
    rspec = pl.BlockSpec((None, nch, length), lambda b: (b, 0, 0))
    col, row = pl.pallas_call(
        functools.partial(_gate_prep_kernel, chunk=chunk),
        grid=(bsz,),
        in_specs=[cspec, cspec, rspec, rspec],
        out_specs=[pl.BlockSpec((None, 3, length, nch), lambda b: (b, 0, 0, 0)), rspec],
        out_shape=[jax.ShapeDtypeStruct((bsz, 3, length, nch), F32),
                   jax.ShapeDtypeStruct((bsz, nch, length), F32)],
        compiler_params=_cparams(("parallel",), 32),
        name="mlstm_gate_prep",
    )(i_col, f_col, jnp.swapaxes(i_col, 1, 2), jnp.swapaxes(f_col, 1, 2))
    col = col.reshape(bsz, 3, length, 2, nh).transpose(0, 4, 2, 3, 1).reshape(bsz, nh, length, 6)
    row = row.reshape(bsz, 2, nh, length).transpose(0, 2, 1, 3)
    return col, row


def _mlstm_kernel(pq_ref, pk_ref, pv_ref, po_ref, cwq_ref, cwk_ref, cbq_ref, cbk_ref, ng_ref,
                  gcol_ref, grow_ref, y_ref, q_s, k_s, v1_s, tab_s, hf_s, hb_s, *, chunk):
    length, dk = pq_ref.shape
    nc = length // chunk
    rep = chunk // LANES

    def conv_silu(p_ref, w_ref, b_ref):
        c = _short_conv(p_ref[...].astype(F32), w_ref[...], b_ref[...])
        return c * jax.nn.sigmoid(c)

    q_s[...] = (conv_silu(pq_ref, cwq_ref, cbq_ref) * (dk ** -0.5)).astype(BF16)
    k_s[...] = conv_silu(pk_ref, cwk_ref, cbk_ref).astype(BF16)
    v1_s[:, :dk] = pv_ref[...]
    v1_s[:, dk:] = jnp.ones((length, dk), BF16)
    for t in range(tab_s.shape[0]):
        tab_s[t] = jnp.broadcast_to(gcol_ref[:, t:t + 1], (length, LANES))

    row_i = lax.broadcasted_iota(I32, (chunk, chunk), 0)
    col_i = lax.broadcasted_iota(I32, (chunk, chunk), 1)
    wide = lambda a: jnp.concatenate([a] * rep, axis=1)
    both = lambda a: jnp.concatenate([a, a], axis=1)

    def step(c, reverse, state, m):
        d = 1 if reverse else 0
        rows = slice(c * chunk, (c + 1) * chunk)
        edge = c * chunk if reverse else (c + 1) * chunk - 1
        q, k, v1 = q_s[rows, :], k_s[rows, :], v1_s[rows, :]
        b_t, u_t, cm_t = tab_s[3 * d, rows, :], tab_s[3 * d + 1, rows, :], tab_s[3 * d + 2, rows, :]
        u_row = grow_ref[d:d + 1, rows]
        g = tab_s[3 * d, edge:edge + 1, :]
        u_max = tab_s[3 * d + 2, edge:edge + 1, :]
        mask = (col_i >= row_i) if reverse else (col_i <= row_i)
        mm = jnp.maximum(cm_t, m)
        p = jnp.exp(jnp.where(mask, u_row - wide(mm), -jnp.inf))
        qk = lax.dot_general(q, k, (((1,), (1,)), ((), ())), preferred_element_type=F32)
        intra = jnp.dot((qk * p).astype(BF16), v1, preferred_element_type=F32)
        inter = jnp.dot(q, state.astype(BF16), preferred_element_type=F32)
        nd = intra + both(jnp.exp(m - mm)) * inter
        h = nd[:, :dk] / jnp.maximum(jnp.abs(nd[:, dk:]), jnp.exp(-(b_t + mm)))
        kw = (k.astype(F32) * jnp.exp(u_t - u_max)).astype(BF16)
        upd = lax.dot_general(kw, v1, (((0,), (0,)), ((), ())), preferred_element_type=F32)
        m_loc = g + u_max
        m_new = jnp.maximum(g + m, m_loc)
        state = both(jnp.exp(g + m - m_new)) * state + both(jnp.exp(m_loc - m_new)) * upd
        return h, state, m_new

    zero = (jnp.zeros((dk, 2 * dk), F32), jnp.zeros((1, LANES), F32))
    st_f, m_f = zero
    st_b, m_b = zero
    for c in range(nc):
        h, st_f, m_f = step(c, False, st_f, m_f)
        hf_s[c * chunk:(c + 1) * chunk, :] = h
        cb = nc - 1 - c
        h, st_b, m_b = step(cb, True, st_b, m_b)
        hb_s[cb * chunk:(cb + 1) * chunk, :] = h

    ht = hf_s[...] + hb_s[...]
    y_ref[...] = (_rms(ht, ng_ref[...]) * jax.nn.sigmoid(po_ref[...].astype(F32))).astype(y_ref.dtype)


def _mlstm(p3, col_q, col_k, col_v, col_o, conv_w, conv_b, norm_g, gcol, grow, chunk):
    bsz, length, _ = p3.shape
    nh, dh = MLSTM_HEADS, norm_g.shape[1] // MLSTM_HEADS
    assert dh == LANES and chunk % LANES == 0

    def pspec(col):
        return pl.BlockSpec((None, length, dh), lambda b, h: (b, 0, col // dh + h))

    def wspec(rows, part):
        return pl.BlockSpec((rows, dh), lambda b, h: (0, part * nh + h))

    return pl.pallas_call(
        functools.partial(_mlstm_kernel, chunk=chunk),
        grid=(bsz, nh),
        in_specs=[pspec(col_q), pspec(col_k), pspec(col_v), pspec(col_o),
                  wspec(3, 0), wspec(3, 1), wspec(1, 0), wspec(1, 1),
                  pl.BlockSpec((1, dh), lambda b, h: (0, h)),
                  pl.BlockSpec((None, None, length, 6), lambda b, h: (b, h, 0, 0)),
                  pl.BlockSpec((None, None, 2, length), lambda b, h: (b, h, 0, 0))],
        out_specs=pl.BlockSpec((None, length, dh), lambda b, h: (b, 0, h)),
        out_shape=jax.ShapeDtypeStruct((bsz, length, nh * dh), BF16),
        scratch_shapes=[pltpu.VMEM((length, dh), BF16), pltpu.VMEM((length, dh), BF16),
                        pltpu.VMEM((length, 2 * dh), BF16), pltpu.VMEM((6, length, LANES), F32),
                        pltpu.VMEM((length, dh), F32), pltpu.VMEM((length, dh), F32)],
        compiler_params=_cparams(("parallel", "parallel"), 40),
        name="mlstm",
    )(p3, p3, p3, p3, conv_w, conv_w, conv_b, conv_b, norm_g, gcol, grow)


def _merge_kernel(yh_ref, ym_ref, gh_ref, gm_ref, x_ref, wa_ref, wb_ref, wo_ref, o_ref):
    a = jnp.dot(yh_ref[...], wa_ref[...], preferred_element_type=F32)
    b = jnp.dot(ym_ref[...], wb_ref[...], preferred_element_type=F32)
    t = jax.nn.sigmoid(gh_ref[...].astype(F32)) * a + jax.nn.sigmoid(gm_ref[...].astype(F32)) * b
    o_ref[...] = x_ref[...] + jnp.dot(t.astype(BF16), wo_ref[...], preferred_element_type=F32)


def _merge(y_hy, y_ml, p_big, x2, w_a, w_b, w_o, tm=256):
    n, d = x2.shape
    wh, wm = y_hy.shape[1], y_ml.shape[1]
    return pl.pallas_call(
        _merge_kernel,
        grid=(n // tm,),
        in_specs=[pl.BlockSpec((tm, wh), lambda i: (i, 0)),
                  pl.BlockSpec((tm, wm), lambda i: (i, 0)),
                  pl.BlockSpec((tm, d), lambda i: (i, 0)),
                  pl.BlockSpec((tm, d), lambda i: (i, 1)),
                  pl.BlockSpec((tm, d), lambda i: (i, 0)),
                  _const_spec(w_a.shape), _const_spec(w_b.shape), _const_spec(w_o.shape)],
        out_specs=pl.BlockSpec((tm, d), lambda i: (i, 0)),
        out_shape=jax.ShapeDtypeStruct((n, d), F32),
        compiler_params=_cparams(("parallel",), 48),
        name="branch_merge",
    )(y_hy, y_ml, p_big, p_big, x2, w_a, w_b, w_o)


def _dense_ffn_kernel(x_ref, g_ref, wg_ref, wu_ref, wd_ref, o_ref, h_s):
    @pl.when(pl.program_id(1) == 0)
    def _():
        x = x_ref[...]
        h_s[...] = _rms(x, g_ref[...]).astype(BF16)
        o_ref[...] = x

    h = h_s[...]
    gt = jnp.dot(h, wg_ref[...], preferred_element_type=F32)
    up = jnp.dot(h, wu_ref[...], preferred_element_type=F32)
    act = (gt * jax.nn.sigmoid(gt) * up).astype(BF16)
    o_ref[...] += jnp.dot(act, wd_ref[...], preferred_element_type=F32)


def _dense_ffn(x2, g, w_gate, w_up, w_down, tm=1024, tf=512):
    n, d = x2.shape
    f = w_gate.shape[1]
    return pl.pallas_call(
        _dense_ffn_kernel,
        grid=(n // tm, f // tf),
        in_specs=[pl.BlockSpec((tm, d), lambda i, j: (i, 0), pipeline_mode=pl.Buffered(1)),
                  pl.BlockSpec((1, d), lambda i, j: (0, 0)),
                  pl.BlockSpec((d, tf), lambda i, j: (0, j)),
                  pl.BlockSpec((d, tf), lambda i, j: (0, j)),
                  pl.BlockSpec((tf, d), lambda i, j: (j, 0))],
        out_specs=pl.BlockSpec((tm, d), lambda i, j: (i, 0)),
        out_shape=jax.ShapeDtypeStruct((n, d), F32),
        scratch_shapes=[pltpu.VMEM((tm, d), BF16)],
        compiler_params=_cparams(("parallel", "arbitrary"), 56),
        name="dense_swiglu",
    )(x2, g, w_gate, w_up, w_down)


def _store_rows_3d(ref3, val2):
    for s in range(ref3.shape[1]):
        ref3[:, s, :] = val2[:, s * LANES:(s + 1) * LANES]


def _router_kernel(x_ref, g_ref, rwh_ref, rwl_ref, rb_ref, h_ref, e_ref, gate_ref, rank_ref, cnt_ref, base_s):
    @pl.when(pl.program_id(0) == 0)
    def _():
        base_s[...] = jnp.zeros_like(base_s)

    hn = _rms(x_ref[...], g_ref[...])
    _store_rows_3d(h_ref, hn)
    logits = _dot_split(hn, rwh_ref[...], rwl_ref[...]) + rb_ref[...]
    lane = lax.broadcasted_iota(I32, logits.shape, 1)
    logits = jnp.where(lane < N_EXPERTS, logits, -jnp.inf)
    m1 = jnp.max(logits, axis=-1, keepdims=True)
    i1 = jnp.min(jnp.where(logits == m1, lane, LANES), axis=-1, keepdims=True)
    rest = jnp.where(lane == i1, -jnp.inf, logits)
    m2 = jnp.max(rest, axis=-1, keepdims=True)
    i2 = jnp.min(jnp.where(rest == m2, lane, LANES), axis=-1, keepdims=True)
    e = jnp.exp(m2 - m1)
    slot = lax.broadcasted_iota(I32, e_ref.shape, 1)
    e_ref[...] = jnp.where(slot == 0, i1, i2)
    gate_ref[...] = jnp.where(slot == 0, 1.0 / (1.0 + e), e / (1.0 + e))
    tm = logits.shape[0]
    oh1 = (lane == i1).astype(F32)
    oh2 = (lane == i2).astype(F32)
    both = oh1 + oh2
    earlier = (lax.broadcasted_iota(I32, (tm, tm), 1) < lax.broadcasted_iota(I32, (tm, tm), 0)).astype(BF16)
    before = jnp.dot(earlier, both.astype(BF16), preferred_element_type=F32) + base_s[...]
    r1 = jnp.sum(before * oh1, axis=-1, keepdims=True)
    r2 = jnp.sum(before * oh2, axis=-1, keepdims=True)
    rank_ref[...] = jnp.where(slot == 0, r1, r2).astype(I32)
    base_s[...] += jnp.sum(both, axis=0, keepdims=True)
    cnt_ref[...] = base_s[...].astype(I32)


def _router(x2, g, rw_pad, rb_pad, tm=512):
    n, d = x2.shape
    s = d // LANES
    rw_hi, rw_lo = _split_bf16(rw_pad)
    return pl.pallas_call(
        _router_kernel,
        grid=(n // tm,),
        in_specs=[pl.BlockSpec((tm, d), lambda i: (i, 0)),
                  pl.BlockSpec((1, d), lambda i: (0, 0)),
                  pl.BlockSpec((d, LANES), lambda i: (0, 0)),
                  pl.BlockSpec((d, LANES), lambda i: (0, 0)),
                  pl.BlockSpec((1, LANES), lambda i: (0, 0))],
        out_specs=[pl.BlockSpec((tm, s, LANES), lambda i: (i, 0, 0)),
                   pl.BlockSpec((tm, TOP_K), lambda i: (i, 0)),
                   pl.BlockSpec((tm, TOP_K), lambda i: (i, 0)),
                   pl.BlockSpec((tm, TOP_K), lambda i: (i, 0)),
                   pl.BlockSpec((1, LANES), lambda i: (0, 0))],
        out_shape=[jax.ShapeDtypeStruct((n, s, LANES), F32),
                   jax.ShapeDtypeStruct((n, TOP_K), I32),
                   jax.ShapeDtypeStruct((n, TOP_K), F32),
                   jax.ShapeDtypeStruct((n, TOP_K), I32),
                   jax.ShapeDtypeStruct((1, LANES), I32)],
        scratch_shapes=[pltpu.VMEM((1, LANES), F32)],
        compiler_params=_cparams(("arbitrary",), 32),
        name="moe_router",
    )(x2, g, rw_hi, rw_lo, rb_pad)


def _slab_pitch(s):
    return s if (s // 8) % 2 == 1 else s + 8


def _row_gather(src_hbm, idx_ref, idx_base, dst, sem, rows):
    slab = src_hbm.shape[1]

    def copy(r):
        return pltpu.make_async_copy(src_hbm.at[idx_ref[idx_base + r]], dst.at[r, pl.ds(0, slab)], sem)

    def start_one(r, carry):
        copy(r).start()
        return carry

    def wait_one(r, carry):
        copy(r).wait()
        return carry

    return (lambda: lax.fori_loop(0, rows, start_one, 0, unroll=8),
            lambda: lax.fori_loop(0, rows, wait_one, 0, unroll=8))


def _expert_ffn_kernel(be_ref, nu_ref, tok_ref, h_hbm, wg_ref, wu_ref, wd_ref, o_ref,
                       xg_s, x2_s, h_s, acc_s, sems):
    i, j = pl.program_id(0), pl.program_id(1)
    blk = xg_s.shape[1]
    n_used = nu_ref[0]

    def gather(block):
        slot = block % 2
        return _row_gather(h_hbm, tok_ref, block * blk, xg_s.at[slot], sems.at[slot], blk)

    @pl.when(jnp.logical_and(i == 0, j == 0))
    def _():
        gather(0)[0]()

    @pl.when(jnp.logical_and(j == 0, i + 1 < n_used))
    def _():
        gather(i + 1)[0]()

    @pl.when(jnp.logical_and(j == 0, i < n_used))
    def _():
        gather(i)[1]()
        slot = i % 2
        for s in range(h_hbm.shape[1]):
            x2_s[:, s * LANES:(s + 1) * LANES] = xg_s[slot, :, s, :]
        h_s[...] = x2_s[...].astype(BF16)

    @pl.when(j == 0)
    def _():
        acc_s[...] = jnp.zeros_like(acc_s)

    @pl.when(i < n_used)
    def _():
        h = h_s[...]
        gt = jnp.dot(h, wg_ref[...], preferred_element_type=F32)
        up = jnp.dot(h, wu_ref[...], preferred_element_type=F32)
        act = (gt * jax.nn.sigmoid(gt) * up).astype(BF16)
        acc_s[...] += jnp.dot(act, wd_ref[...], preferred_element_type=F32)

    @pl.when(j == pl.num_programs(1) - 1)
    def _():
        for s in range(o_ref.shape[1]):
            o_ref[:, s, :] = acc_s[:, s * LANES:(s + 1) * LANES]


def _expert_ffn(h3, slot_tok, block_e, n_used, w_gate, w_up, w_down, blk, tf=512):
    ns = slot_tok.shape[0]
    s = h3.shape[1]
    d = s * LANES
    f = w_gate.shape[2]
    nj = f // tf

    def jj(i, j, nu):
        return jnp.where(i < nu[0], j, nj - 1)

    return pl.pallas_call(
        _expert_ffn_kernel,
        grid_spec=pltpu.PrefetchScalarGridSpec(
            num_scalar_prefetch=3,
            grid=(ns // blk, nj),
            in_specs=[pl.BlockSpec(memory_space=pl.ANY),
                      pl.BlockSpec((None, d, tf), lambda i, j, be, nu, tok: (be[i], 0, jj(i, j, nu))),
                      pl.BlockSpec((None, d, tf), lambda i, j, be, nu, tok: (be[i], 0, jj(i, j, nu))),
                      pl.BlockSpec((None, tf, d), lambda i, j, be, nu, tok: (be[i], jj(i, j, nu), 0))],
            out_specs=pl.BlockSpec((blk, s, LANES), lambda i, j, be, nu, tok: (i, 0, 0)),
            scratch_shapes=[pltpu.VMEM((2, blk, _slab_pitch(s), LANES), F32), pltpu.VMEM((blk, d), F32),
                            pltpu.VMEM((blk, d), BF16), pltpu.VMEM((blk, d), F32),
                            pltpu.SemaphoreType.DMA((2,))],
        ),
        out_shape=jax.ShapeDtypeStruct((ns, s, LANES), F32),
        compiler_params=_cparams(("arbitrary", "arbitrary"), 52),
        name="expert_swiglu",
    )(block_e, n_used, slot_tok, h3, w_gate, w_up, w_down)


def _combine_kernel(dest_ref, x_ref, gate_ref, y_hbm, g_ref, o_ref, yg_s, sems, *, final_norm):
    i, n_tiles = pl.program_id(0), pl.num_programs(0)
    tm = x_ref.shape[0]

    def gather(tile):
        slot = tile % 2
        parts = [_row_gather(y_hbm, dest_ref, k * n_tiles * tm + tile * tm, yg_s.at[slot, k], sems.at[slot], tm)
                 for k in range(TOP_K)]
        return (lambda: [p[0]() for p in parts]), (lambda: [p[1]() for p in parts])

    @pl.when(i == 0)
    def _():
        gather(0)[0]()

    @pl.when(i + 1 < n_tiles)
    def _():
        gather(i + 1)[0]()

    gather(i)[1]()
    slot = i % 2
    gates = [jnp.broadcast_to(gate_ref[:, k:k + 1], (tm, LANES)) for k in range(TOP_K)]
    for s in range(y_hbm.shape[1]):
        cols = slice(s * LANES, (s + 1) * LANES)
        acc = x_ref[:, cols]
        for k in range(TOP_K):
            acc = acc + gates[k] * yg_s[slot, k, :, s, :]
        o_ref[:, cols] = acc
    if final_norm:
        o_ref[...] = _rms(o_ref[...], g_ref[...])


def _combine(x2, gate, yb, dest_by_k, g, final_norm, tm=512):
    n, d = x2.shape
    s = d // LANES
    return pl.pallas_call(
        functools.partial(_combine_kernel, final_norm=final_norm),
        grid_spec=pltpu.PrefetchScalarGridSpec(
            num_scalar_prefetch=1,
            grid=(n // tm,),
            in_specs=[pl.BlockSpec((tm, d), lambda i, dest: (i, 0)),
                      pl.BlockSpec((tm, TOP_K), lambda i, dest: (i, 0)),
                      pl.BlockSpec(memory_space=pl.ANY),
                      pl.BlockSpec((1, d), lambda i, dest: (0, 0))],
            out_specs=pl.BlockSpec((tm, d), lambda i, dest: (i, 0)),
            scratch_shapes=[pltpu.VMEM((2, TOP_K, tm, _slab_pitch(s), LANES), F32),
                            pltpu.SemaphoreType.DMA((2,))],
        ),
        out_shape=jax.ShapeDtypeStruct((n, d), F32),
        compiler_params=_cparams(("arbitrary",), 48),
        name="moe_combine",
    )(dest_by_k, x2, gate, yb, g)


def _final_norm_kernel(x_ref, g_ref, o_ref):
    o_ref[...] = _rms(x_ref[...], g_ref[...])


def _final_norm(x2, g, tm=512):
    n, d = x2.shape
    return pl.pallas_call(
        _final_norm_kernel,
        grid=(n // tm,),
        in_specs=[pl.BlockSpec((tm, d), lambda i: (i, 0)), pl.BlockSpec((1, d), lambda i: (0, 0))],
        out_specs=pl.BlockSpec((tm, d), lambda i: (i, 0)),
        out_shape=jax.ShapeDtypeStruct((n, d), F32),
        compiler_params=_cparams(("parallel",), 32),
        name="final_norm",
    )(x2, g)


def _routing_tables(top_e, rank, counts, blk):
    n = top_e.shape[0]
    nk = n * TOP_K
    flat_e = top_e.reshape(-1)
    padded = (counts + blk - 1) // blk * blk
    p_end = jnp.cumsum(padded)
    p_start = p_end - padded
    dest = (p_start[flat_e] + rank.reshape(-1)).astype(I32)
    n_blocks = -(-nk // blk) + N_EXPERTS
    flat_tok = jnp.arange(nk, dtype=I32) // TOP_K
    slot_tok = jnp.zeros((n_blocks * blk,), I32).at[dest].set(flat_tok)
    n_used = (p_end[-1] // blk).astype(I32)
    blocks = jnp.arange(n_blocks, dtype=I32)
    block_e = jnp.sum((p_end[None, :] <= (blocks * blk)[:, None]).astype(I32), axis=1)
    block_e = jnp.minimum(block_e, N_EXPERTS - 1)
    block_e = jnp.where(blocks < n_used, block_e, block_e[n_used - 1])
    dest_by_k = dest.reshape(n, TOP_K).T.reshape(-1)
    return slot_tok, block_e, n_used.reshape(1), dest_by_k


def _moe_layer(x2, norm_g, router_w, router_b, w_gate, w_up, w_down, out_g, final_norm):
    n, d = x2.shape
    rw = jnp.zeros((d, LANES), F32).at[:, :N_EXPERTS].set(router_w)
    rb = jnp.zeros((1, LANES), F32).at[0, :N_EXPERTS].set(router_b)
    h, top_e, gate, rank, counts = _router(x2, norm_g[None, :], rw, rb)
    slot_tok, block_e, n_used, dest_by_k = _routing_tables(top_e, rank, counts[0, :N_EXPERTS], MOE_BLOCK)
    yb = _expert_ffn(h, slot_tok, block_e, n_used, w_gate.astype(BF16), w_up.astype(BF16),
                     w_down.astype(BF16), MOE_BLOCK)
    return _combine(x2, gate, yb, dest_by_k, out_g[None, :], final_norm)


def _mixer_layer(x2, bsz, length, norm_g, w_in, b_in, hy_conv_w, hy_conv_b, hy_w1, hy_b1, hy_w2, hy_b2,
                 hy_w3, hy_freq, hy_decay, hy_skip, ml_conv_w, ml_conv_b, ml_norm_g, w_a, w_b, w_o,
                 ctab, stab, zpos):
    n, d = x2.shape
    wh = hy_skip.shape[1]
    wm = ml_norm_g.shape[0]
    off_qk = (HYENA_ORDER + 1) * wh
    off_v = off_qk + 2 * wm
    off_o = off_v + wm
    off_gates = off_o + wm
    off_br = off_gates + 4 * MLSTM_HEADS
    w_cat = jnp.concatenate([w_in[:, off_br:], w_in[:, :off_gates]], axis=1).astype(BF16)
    b_cat = jnp.concatenate([b_in[off_br:], b_in[:off_gates]])[None, :]
    col_hy = 2 * d
    col_q = col_hy + off_qk
    col_k = col_q + wm
    col_v = col_hy + off_v
    col_o = col_hy + off_o

    p_big, gates = _in_proj(x2, norm_g[None, :], w_in[:, off_gates:off_br], b_in[None, off_gates:off_br],
                            w_cat, b_cat)
    p3 = p_big.reshape(bsz, length, p_big.shape[1])

    kp = LANES
    pad2 = lambda a, r, c: jnp.zeros((r, c), F32).at[:a.shape[0], :a.shape[1]].set(a)
    zp = pad2(zpos, length, kp)
    hspec, hnyq = _hyena_spectrum(
        zp, pad2(hy_w1, kp, kp), pad2(hy_b1[None, :], 1, kp), pad2(hy_w2, kp, kp), pad2(hy_b2[None, :], 1, kp),
        pad2(hy_freq[None, :], 1, kp), pad2(hy_w3, kp, hy_w3.shape[1]), hy_decay[None, :],
        ctab, stab, wh, HYENA_CT)
    y_hy = _hyena_conv(p3, col_hy, hy_conv_w, hy_conv_b[None, :], hspec, hnyq, hy_skip[:, None, :],
                       ctab, stab, wh, HYENA_CT, HYENA_RC)

    gcol, grow = _gate_prep(gates.reshape(bsz, length, 4 * MLSTM_HEADS), MLSTM_CHUNK)
    y_ml = _mlstm(p3, col_q, col_k, col_v, col_o, ml_conv_w, ml_conv_b[None, :], ml_norm_g[None, :],
                  gcol, grow, MLSTM_CHUNK)

    return _merge(y_hy.reshape(n, wh), y_ml.reshape(n, wm), p_big, x2,
                  w_a.astype(BF16), w_b.astype(BF16), w_o.astype(BF16))


def kernel(x, mix_norm_g, mix_w_in, mix_b_in, hy_conv_w, hy_conv_b, hy_filt_w1, hy_filt_b1, hy_filt_w2,
           hy_filt_b2, hy_filt_w3, hy_filt_freq, hy_filt_decay, hy_skip, ml_conv_w, ml_conv_b, ml_norm_g,
           mix_w_a, mix_w_b, mix_w_o, ffn_norm_g, dense_w_gate, dense_w_up, dense_w_down, moe_router_w,
           moe_router_b, moe_w_gate, moe_w_up, moe_w_down, final_norm_g):
    bsz, length, d = x.shape
    depth = mix_norm_g.shape[0]
    x2 = x.reshape(bsz * length, d)
    ctab, stab = _dft_tables(length)
    zpos = _hyena_positions(length)
    normed = False
    for layer in range(depth):
        x2 = _mixer_layer(x2, bsz, length, mix_norm_g[layer], mix_w_in[layer], mix_b_in[layer],
                          hy_conv_w[layer], hy_conv_b[layer], hy_filt_w1[layer], hy_filt_b1[layer],
                          hy_filt_w2[layer], hy_filt_b2[layer], hy_filt_w3[layer], hy_filt_freq[layer],
                          hy_filt_decay[layer], hy_skip[layer], ml_conv_w[layer], ml_conv_b[layer],
                          ml_norm_g[layer], mix_w_a[layer], mix_w_b[layer], mix_w_o[layer], ctab, stab, zpos)
        j = layer // 2
        if layer % 2 == 0:
            x2 = _dense_ffn(x2, ffn_norm_g[layer][None, :], dense_w_gate[j].astype(BF16),
                            dense_w_up[j].astype(BF16), dense_w_down[j].astype(BF16))
        else:
            normed = layer == depth - 1
            x2 = _moe_layer(x2, ffn_norm_g[layer], moe_router_w[j], moe_router_b[j], moe_w_gate[j],
                            moe_w_up[j], moe_w_down[j], final_norm_g, normed)
    if not normed:
        x2 = _final_norm(x2, final_norm_g[None, :])
    return x2.reshape(bsz, length, d)
```

```python
import functools
import math

import jax
import jax.numpy as jnp
import numpy as np
from jax import lax
from jax.experimental import pallas as pl
from jax.experimental.pallas import tpu as pltpu

F32, BF16, I32 = jnp.float32, jnp.bfloat16, jnp.int32
HIGHEST = lax.Precision.HIGHEST

HYENA_ORDER = 2
HYENA_POS_BANDS = 16
MLSTM_HEADS = 8
N_EXPERTS = 8
TOP_K = 2
EPS = 1e-6

V7X_VMEM_BYTES = 64 * 1024 * 1024
LANES = 128
MXU_DIM = 256

MLSTM_CHUNK = 256
HYENA_CT = MXU_DIM
HYENA_RC = 512
MOE_BLOCK = 512


def _cparams(semantics, vmem_mib):
    assert vmem_mib * 1024 * 1024 <= V7X_VMEM_BYTES
    return pltpu.CompilerParams(dimension_semantics=semantics, vmem_limit_bytes=vmem_mib * 1024 * 1024)


def _const_spec(shape):
    nd = len(shape)
    return pl.BlockSpec(shape, lambda *_: (0,) * nd, pipeline_mode=pl.Buffered(1))


def _rms(x, g):
    return x * lax.rsqrt(jnp.mean(x * x, axis=-1, keepdims=True) + EPS) * g


def _shift_rows(u, direction):
    n = u.shape[0]
    row = lax.broadcasted_iota(I32, u.shape, 0)
    if direction > 0:
        return jnp.where(row == 0, 0.0, pltpu.roll(u, 1, axis=0))
    return jnp.where(row == n - 1, 0.0, pltpu.roll(u, n - 1, axis=0))


def _short_conv(u, w, b):
    return b + _shift_rows(u, 1) * w[0:1] + u * w[1:2] + _shift_rows(u, -1) * w[2:3]


def _log_sigmoid(x):
    return jnp.minimum(x, 0.0) - jnp.log1p(jnp.exp(-jnp.abs(x)))


def _dot_split(a, w_hi, w_lo):
    a_hi = a.astype(BF16)
    a_lo = (a - a_hi.astype(F32)).astype(BF16)
    return (jnp.dot(a_hi, w_hi, preferred_element_type=F32) + jnp.dot(a_lo, w_hi, preferred_element_type=F32)
            + jnp.dot(a_hi, w_lo, preferred_element_type=F32))


def _split_bf16(w):
    w_hi = w.astype(BF16)
    return w_hi, (w - w_hi.astype(F32)).astype(BF16)


def _in_proj_kernel(x_ref, g_ref, wg_ref, bg_ref, w_ref, b_ref, o_ref, gates_ref, h_s):
    @pl.when(pl.program_id(1) == 0)
    def _():
        h_s[...] = _rms(x_ref[...], g_ref[...]).astype(BF16)
        gates_ref[...] = jnp.dot(h_s[...], wg_ref[...], preferred_element_type=F32) + bg_ref[...]

    acc = jnp.dot(h_s[...], w_ref[...], preferred_element_type=F32) + b_ref[...]
    o_ref[...] = acc.astype(o_ref.dtype)


def _in_proj(x2, g, w_gates, b_gates, w, b, tm=1024, tn=1024):
    n, d = x2.shape
    nc = w.shape[1]
    ng = w_gates.shape[1]
    return pl.pallas_call(
        _in_proj_kernel,
        grid=(n // tm, nc // tn),
        in_specs=[pl.BlockSpec((tm, d), lambda i, j: (i, 0)),
                  pl.BlockSpec((1, d), lambda i, j: (0, 0)),
                  pl.BlockSpec((d, ng), lambda i, j: (0, 0)),
                  pl.BlockSpec((1, ng), lambda i, j: (0, 0)),
                  pl.BlockSpec((d, tn), lambda i, j: (0, j)),
                  pl.BlockSpec((1, tn), lambda i, j: (0, j))],
        out_specs=[pl.BlockSpec((tm, tn), lambda i, j: (i, j)),
                   pl.BlockSpec((tm, ng), lambda i, j: (i, 0))],
        out_shape=[jax.ShapeDtypeStruct((n, nc), BF16), jax.ShapeDtypeStruct((n, ng), F32)],
        scratch_shapes=[pltpu.VMEM((tm, d), BF16)],
        compiler_params=_cparams(("parallel", "arbitrary"), 48),
        name="in_proj",
    )(x2, g, w_gates.astype(BF16), b_gates, w, b)


def _dft_tables(half):
    k = jnp.arange(half, dtype=I32)
    ang = ((k[:, None] * k[None, :]) % (2 * half)).astype(F32) * (math.pi / half)
    tw = k.astype(F32)[:, None] * (math.pi / (2 * half))
    twiddle = jnp.stack([jnp.broadcast_to(jnp.cos(tw), (half, LANES)), jnp.broadcast_to(jnp.sin(tw), (half, LANES))])
    return jnp.cos(ang).astype(BF16), jnp.sin(ang).astype(BF16), twiddle


def _hyena_positions(length):
    t = np.linspace(0.0, 1.0, length)[:, None]
    n = np.arange(length, dtype=np.float64)[:, None]
    bands = np.linspace(1e-4, HYENA_POS_BANDS - 1, HYENA_POS_BANDS)[None, :]
    ang = (2.0 * np.pi / length) * n * bands
    return jnp.asarray(np.concatenate([t, np.cos(ang), -np.sin(ang)], axis=-1), dtype=F32)


def _split_rows(val, slab_s):
    half = val.shape[0] // 2
    n_slabs = slab_s.shape[0]
    for s in range(n_slabs):
        slab_s[s] = val[:, s * LANES:(s + 1) * LANES]
    pick = lambda p: jnp.concatenate([slab_s[s, pl.ds(p, half, stride=2), :] for s in range(n_slabs)], axis=1)
    return pick(0), pick(1)


def _merge_rows(even, odd, slab_s):
    half = even.shape[0]
    n_slabs = slab_s.shape[0]
    for s in range(n_slabs):
        slab_s[s, pl.ds(0, half, stride=2), :] = even[:, s * LANES:(s + 1) * LANES]
        slab_s[s, pl.ds(1, half, stride=2), :] = odd[:, s * LANES:(s + 1) * LANES]
    return jnp.concatenate([slab_s[s] for s in range(n_slabs)], axis=1)


def _alt_sum(x):
    row = lax.broadcasted_iota(I32, (x.shape[0], 1), 0)
    return jnp.sum(jnp.where(row % 2 == 0, x, -x), axis=0, keepdims=True)


def _hyena_spectrum_kernel(z_ref, w1_ref, b1_ref, w2_ref, b2_ref, fr_ref, w3f_ref, w3b_ref,
                           decf_ref, decb_ref, c_ref, s_ref, tw_ref, tab_ref, mid_ref, slab_s):
    half = c_ref.shape[0]
    length = 2 * half
    ct = w3f_ref.shape[1]
    dot_hi = functools.partial(jnp.dot, precision=HIGHEST, preferred_element_type=F32)
    dot = functools.partial(jnp.dot, preferred_element_type=F32)
    z = z_ref[...]
    fr = fr_ref[...]
    hid = jnp.sin(fr * (dot_hi(z, w1_ref[...]) + b1_ref[...]))
    hid = jnp.sin(fr * (dot_hi(hid, w2_ref[...]) + b2_ref[...]))
    t = z[:, 0:1]
    ff = dot_hi(hid, w3f_ref[...]) * jnp.exp(-t * jnp.abs(decf_ref[...]))
    fb = dot_hi(hid, w3b_ref[...]) * jnp.exp(-t * jnp.abs(decb_ref[...]))
    row = lax.broadcasted_iota(I32, (length, 1), 0)
    l1 = jnp.sum(jnp.where(row == 0, jnp.abs(ff + fb), jnp.abs(ff) + jnp.abs(fb)), axis=0, keepdims=True)
    inv = 1.0 / l1
    se, so = _split_rows((ff + fb) * inv, slab_s)
    de, do = _split_rows((ff - fb) * inv, slab_s)
    wide = lambda a: jnp.concatenate([a] * (ct // LANES), axis=1)
    cw, sw = wide(tw_ref[0]), wide(tw_ref[1])
    c_t, s_t = c_ref[...], s_ref[...]
    bf = lambda a: a.astype(BF16)
    e_r = dot(c_t, bf(se))
    o_r, o_s = dot(c_t, bf(so)), dot(s_t, bf(so))
    t_r = cw * o_r - sw * o_s
    d_s = dot(s_t, bf(de))
    p_r, p_s = dot(c_t, bf(do)), dot(s_t, bf(do))
    t_i = -(cw * p_s + sw * p_r)
    krow = lax.broadcasted_iota(I32, (half, 1), 0)
    scale = jnp.where(krow == 0, 1.0 / (2 * length), 1.0 / length)
    lo_r, lo_i = (e_r + t_r) * scale, (t_i - d_s) * scale
    hi_r, hi_i = (e_r - t_r) * scale, (t_i + d_s) * scale

    def emit(p, lr, li, hr, hi):
        a_r, a_i = lr + hr, li - hi
        d_r, d_i = lr - hr, li + hi
        tab_ref[4 * p + 0] = a_r
        tab_ref[4 * p + 1] = a_i
        tab_ref[4 * p + 2] = d_r * cw + d_i * sw
        tab_ref[4 * p + 3] = d_i * cw - d_r * sw

    emit(0, lo_r, lo_i, hi_r, hi_i)
    emit(1, lo_r * cw - lo_i * sw, lo_r * sw + lo_i * cw, -(hi_r * cw + hi_i * sw), -(hi_i * cw - hi_r * sw))
    mid_ref[0:1, :] = _alt_sum(se) * (1.0 / length)
    mid_ref[1:2, :] = -_alt_sum(do) * (1.0 / length)


def _hyena_spectrum(z, w1, b1, w2, b2, freq, w3, decay, ctab, stab, twiddle, width, ct):
    half = ctab.shape[0]
    kp = w1.shape[1]
    nct = width // ct
    col_f = lambda o, c: (0, o * 2 * nct + c)
    col_b = lambda o, c: (0, o * 2 * nct + nct + c)
    return pl.pallas_call(
        _hyena_spectrum_kernel,
        grid=(HYENA_ORDER, nct),
        in_specs=[_const_spec(z.shape), _const_spec(w1.shape), _const_spec(b1.shape),
                  _const_spec(w2.shape), _const_spec(b2.shape), _const_spec(freq.shape),
                  pl.BlockSpec((kp, ct), col_f), pl.BlockSpec((kp, ct), col_b),
                  pl.BlockSpec((1, ct), col_f), pl.BlockSpec((1, ct), col_b),
                  _const_spec(ctab.shape), _const_spec(stab.shape), _const_spec(twiddle.shape)],
        out_specs=[pl.BlockSpec((None, 8, half, ct), lambda o, c: (o, 0, 0, c)),
                   pl.BlockSpec((None, 2, ct), lambda o, c: (o, 0, c))],
        out_shape=[jax.ShapeDtypeStruct((HYENA_ORDER, 8, half, width), F32),
                   jax.ShapeDtypeStruct((HYENA_ORDER, 2, width), F32)],
        scratch_shapes=[pltpu.VMEM((ct // LANES, 2 * half, LANES), F32)],
        compiler_params=_cparams(("parallel", "parallel"), 48),
        name="hyena_spectrum",
    )(z, w1, b1, w2, b2, freq, w3, w3, decay, decay, ctab, stab, twiddle)


def _hyena_conv_kernel(pv_ref, p1_ref, p2_ref, wv_ref, w1_ref, w2_ref, bv_ref, b1_ref, b2_ref,
                       tab_ref, mid_ref, skip_ref, c_ref, s_ref, y_ref,
                       slab_s, z_s, g1_s, g2_s, zb_s, gr_s, gn_s, *, rc):
    half = c_ref.shape[0]
    dot = functools.partial(jnp.dot, preferred_element_type=F32)
    for dst, p_ref, w_ref, b_ref in ((z_s, pv_ref, wv_ref, bv_ref), (g1_s, p1_ref, w1_ref, b1_ref),
                                     (g2_s, p2_ref, w2_ref, b2_ref)):
        even, odd = _split_rows(_short_conv(p_ref[...].astype(F32), w_ref[...], b_ref[...]), slab_s)
        dst[0] = even
        dst[1] = odd
    gates = (g1_s, g2_s)
    row = lax.broadcasted_iota(I32, (half, 1), 0)
    sign = jnp.where(row % 2 == 0, 1.0, -1.0)
    for o in range(HYENA_ORDER):
        zb_s[0] = z_s[0].astype(BF16)
        zb_s[1] = z_s[1].astype(BF16)
        a_e, a_o = _alt_sum(z_s[0]), _alt_sum(z_s[1])
        mid = (mid_ref[o, 0:1, :] * a_e + mid_ref[o, 1:2, :] * a_o,
               mid_ref[o, 0:1, :] * a_o - mid_ref[o, 1:2, :] * a_e)
        for r in range(half // rc):
            rows = slice(r * rc, (r + 1) * rc)
            e_r, e_s = dot(c_ref[rows, :], zb_s[0]), dot(s_ref[rows, :], zb_s[0])
            o_r, o_s = dot(c_ref[rows, :], zb_s[1]), dot(s_ref[rows, :], zb_s[1])
            for p in range(2):
                a_r, a_i = tab_ref[o, 4 * p + 0, rows, :], tab_ref[o, 4 * p + 1, rows, :]
                b_r, b_i = tab_ref[o, 4 * p + 2, rows, :], tab_ref[o, 4 * p + 3, rows, :]
                gr_s[p, rows, :] = (a_r * e_r + a_i * e_s + b_r * o_r + b_i * o_s).astype(BF16)
                gn_s[p, rows, :] = (a_r * e_s - a_i * e_r + b_r * o_s - b_i * o_r).astype(BF16)
        for p in range(2):
            for r in range(half // rc):
                rows = slice(r * rc, (r + 1) * rc)
                conv = dot(c_ref[rows, :], gr_s[p]) + dot(s_ref[rows, :], gn_s[p]) + sign[rows] * mid[p]
                z_s[p, rows, :] = gates[o][p, rows, :] * (conv + skip_ref[o] * z_s[p, rows, :])
    y_ref[...] = _merge_rows(z_s[0], z_s[1], slab_s).astype(y_ref.dtype)


def _hyena_conv(p3, col0, conv_w, conv_b, tab, mid, skip, ctab, stab, width, ct, rc):
    bsz, length, _ = p3.shape
    half = length // 2
    nct = width // ct
    c0 = col0 // ct

    def pspec(part):
        return pl.BlockSpec((None, length, ct), lambda c, b: (b, 0, c0 + part * nct + c))

    def wspec(rows, part):
        return pl.BlockSpec((rows, ct), lambda c, b: (0, part * nct + c))

    scratch = ([pltpu.VMEM((ct // LANES, length, LANES), F32)] + [pltpu.VMEM((2, half, ct), F32)] * 3
               + [pltpu.VMEM((2, half, ct), BF16)] * 3)
    return pl.pallas_call(
        functools.partial(_hyena_conv_kernel, rc=rc),
        grid=(nct, bsz),
        in_specs=[pspec(0), pspec(1), pspec(2),
                  wspec(3, 0), wspec(3, 1), wspec(3, 2), wspec(1, 0), wspec(1, 1), wspec(1, 2),
                  pl.BlockSpec((HYENA_ORDER, 8, half, ct), lambda c, b: (0, 0, 0, c),
                               pipeline_mode=pl.Buffered(1)),
                  pl.BlockSpec((HYENA_ORDER, 2, ct), lambda c, b: (0, 0, c)),
                  pl.BlockSpec((HYENA_ORDER, 1, ct), lambda c, b: (0, 0, c)),
                  _const_spec(ctab.shape), _const_spec(stab.shape)],
        out_specs=pl.BlockSpec((None, length, ct), lambda c, b: (b, 0, c)),
        out_shape=jax.ShapeDtypeStruct((bsz, length, width), BF16),
        scratch_shapes=scratch,
        compiler_params=_cparams(("parallel", "parallel"), 56),
        name="hyena_conv",
    )(p3, p3, p3, conv_w, conv_w, conv_w, conv_b, conv_b, conv_b, tab, mid, skip, ctab, stab)


def _gate_prep_kernel(ic_ref, fc_ref, ir_ref, fr_ref, col_ref, row_ref, *, chunk):
    length, nch = ic_ref.shape
    nh = nch // 2
    r = lax.broadcasted_iota(I32, (chunk, chunk), 0)
    c = lax.broadcasted_iota(I32, (chunk, chunk), 1)
    lower = (c <= r).astype(F32)
    upper = (c >= r).astype(F32)
    dot_hi = functools.partial(jnp.dot, precision=HIGHEST, preferred_element_type=F32)
    fwd_col = lax.broadcasted_iota(I32, (chunk, nch), 1) < nh
    fwd_row = lax.broadcasted_iota(I32, (nch, chunk), 0) < nh
    pos = lax.broadcasted_iota(I32, (chunk, nch), 0)
    for ch in range(length // chunk):
        rows = slice(ch * chunk, (ch + 1) * chunk)
        lf = _log_sigmoid(fc_ref[rows, :])
        b = jnp.where(fwd_col, dot_hi(lower, lf), dot_hi(upper, lf))
        u = ic_ref[rows, :] - b
        cm_f, cm_b = u, u
        shift = 1
        while shift < chunk:
            cm_f = jnp.maximum(cm_f, jnp.where(pos >= shift, pltpu.roll(cm_f, shift, axis=0), -jnp.inf))
            cm_b = jnp.maximum(cm_b, jnp.where(pos + shift < chunk, pltpu.roll(cm_b, chunk - shift, axis=0), -jnp.inf))
            shift *= 2
        col_ref[0, rows, :] = b
        col_ref[1, rows, :] = u
        col_ref[2, rows, :] = jnp.where(fwd_col, cm_f, cm_b)
        lfr = _log_sigmoid(fr_ref[:, rows])
        b_r = jnp.where(fwd_row, dot_hi(lfr, upper), dot_hi(lfr, lower))
        row_ref[:, rows] = ir_ref[:, rows] - b_r


def _gate_prep(gates3, chunk):
    bsz, length, nch4 = gates3.shape
    nh = nch4 // 4
    i_col = jnp.concatenate([gates3[..., :nh], gates3[..., 2 * nh:3 * nh]], axis=-1)
    f_col = jnp.concatenate([gates3[..., nh:2 * nh], gates3[..., 3 * nh:]], axis=-1)
    nch = 2 * nh
    cspec = pl.BlockSpec((None, length, nch), lambda b: (b, 0, 0))
    rspec = pl.BlockSpec((None, nch, length), lambda b: (b, 0, 0))
    col, row = pl.pallas_call(
        functools.partial(_gate_prep_kernel, chunk=chunk),
        grid=(bsz,),
        in_specs=[cspec, cspec, rspec, rspec],
        out_specs=[pl.BlockSpec((None, 3, length, nch), lambda b: (b, 0, 0, 0)), rspec],
        out_shape=[jax.ShapeDtypeStruct((bsz, 3, length, nch), F32),
                   jax.ShapeDtypeStruct((bsz, nch, length), F32)],
        compiler_params=_cparams(("parallel",), 32),
        name="mlstm_gate_prep",
    )(i_col, f_col, jnp.swapaxes(i_col, 1, 2), jnp.swapaxes(f_col, 1, 2))
    col = col.reshape(bsz, 3, length, 2, nh).transpose(0, 4, 2, 3, 1).reshape(bsz, nh, length, 6)
    row = row.reshape(bsz, 2, nh, length).transpose(0, 2, 1, 3)
    return col, row


def _mlstm_kernel(pq_ref, pk_ref, pv_ref, po_ref, cwq_ref, cwk_ref, cbq_ref, cbk_ref, ng_ref,
                  gcol_ref, grow_ref, y_ref, q_s, k_s, v1_s, tab_s, hf_s, hb_s, *, chunk):
    length, dk = pq_ref.shape
    nc = length // chunk
    rep = chunk // LANES

    def conv_silu(p_ref, w_ref, b_ref):
        c = _short_conv(p_ref[...].astype(F32), w_ref[...], b_ref[...])
        return c * jax.nn.sigmoid(c)

    q_s[...] = (conv_silu(pq_ref, cwq_ref, cbq_ref) * (dk ** -0.5)).astype(BF16)
    k_s[...] = conv_silu(pk_ref, cwk_ref, cbk_ref).astype(BF16)
    v1_s[:, :dk] = pv_ref[...]
    v1_s[:, dk:] = jnp.ones((length, dk), BF16)
    for t in range(tab_s.shape[0]):
        tab_s[t] = jnp.broadcast_to(gcol_ref[:, t:t + 1], (length, LANES))

    row_i = lax.broadcasted_iota(I32, (chunk, chunk), 0)
    col_i = lax.broadcasted_iota(I32, (chunk, chunk), 1)
    wide = lambda a: jnp.concatenate([a] * rep, axis=1)
    both = lambda a: jnp.concatenate([a, a], axis=1)

    def step(c, reverse, state, m):
        d = 1 if reverse else 0
        rows = slice(c * chunk, (c + 1) * chunk)
        edge = c * chunk if reverse else (c + 1) * chunk - 1
        q, k, v1 = q_s[rows, :], k_s[rows, :], v1_s[rows, :]
        b_t, u_t, cm_t = tab_s[3 * d, rows, :], tab_s[3 * d + 1, rows, :], tab_s[3 * d + 2, rows, :]
        u_row = grow_ref[d:d + 1, rows]
        g = tab_s[3 * d, edge:edge + 1, :]
        u_max = tab_s[3 * d + 2, edge:edge + 1, :]
        mask = (col_i >= row_i) if reverse else (col_i <= row_i)
        mm = jnp.maximum(cm_t, m)
        p = jnp.exp(jnp.where(mask, u_row - wide(mm), -jnp.inf))
        qk = lax.dot_general(q, k, (((1,), (1,)), ((), ())), preferred_element_type=F32)
        intra = jnp.dot((qk * p).astype(BF16), v1, preferred_element_type=F32)
        inter = jnp.dot(q, state.astype(BF16), preferred_element_type=F32)
        nd = intra + both(jnp.exp(m - mm)) * inter
        h = nd[:, :dk] / jnp.maximum(jnp.abs(nd[:, dk:]), jnp.exp(-(b_t + mm)))
        kw = (k.astype(F32) * jnp.exp(u_t - u_max)).astype(BF16)
        upd = lax.dot_general(kw, v1, (((0,), (0,)), ((), ())), preferred_element_type=F32)
        m_loc = g + u_max
        m_new = jnp.maximum(g + m, m_loc)
        state = both(jnp.exp(g + m - m_new)) * state + both(jnp.exp(m_loc - m_new)) * upd
        return h, state, m_new

    zero = (jnp.zeros((dk, 2 * dk), F32), jnp.zeros((1, LANES), F32))
    st_f, m_f = zero
    st_b, m_b = zero
    for c in range(nc):
        h, st_f, m_f = step(c, False, st_f, m_f)
        hf_s[c * chunk:(c + 1) * chunk, :] = h
        cb = nc - 1 - c
        h, st_b, m_b = step(cb, True, st_b, m_b)
        hb_s[cb * chunk:(cb + 1) * chunk, :] = h

    ht = hf_s[...] + hb_s[...]
    y_ref[...] = (_rms(ht, ng_ref[...]) * jax.nn.sigmoid(po_ref[...].astype(F32))).astype(y_ref.dtype)


def _mlstm(p3, col_q, col_k, col_v, col_o, conv_w, conv_b, norm_g, gcol, grow, chunk):
    bsz, length, _ = p3.shape
    nh, dh = MLSTM_HEADS, norm_g.shape[1] // MLSTM_HEADS
    assert dh == LANES and chunk % LANES == 0

    def pspec(col):
        return pl.BlockSpec((None, length, dh), lambda b, h: (b, 0, col // dh + h))

    def wspec(rows, part):
        return pl.BlockSpec((rows, dh), lambda b, h: (0, part * nh + h))

    return pl.pallas_call(
        functools.partial(_mlstm_kernel, chunk=chunk),
        grid=(bsz, nh),
        in_specs=[pspec(col_q), pspec(col_k), pspec(col_v), pspec(col_o),
                  wspec(3, 0), wspec(3, 1), wspec(1, 0), wspec(1, 1),
                  pl.BlockSpec((1, dh), lambda b, h: (0, h)),
                  pl.BlockSpec((None, None, length, 6), lambda b, h: (b, h, 0, 0)),
                  pl.BlockSpec((None, None, 2, length), lambda b, h: (b, h, 0, 0))],
        out_specs=pl.BlockSpec((None, length, dh), lambda b, h: (b, 0, h)),
        out_shape=jax.ShapeDtypeStruct((bsz, length, nh * dh), BF16),
        scratch_shapes=[pltpu.VMEM((length, dh), BF16), pltpu.VMEM((length, dh), BF16),
                        pltpu.VMEM((length, 2 * dh), BF16), pltpu.VMEM((6, length, LANES), F32),
                        pltpu.VMEM((length, dh), F32), pltpu.VMEM((length, dh), F32)],
        compiler_params=_cparams(("parallel", "parallel"), 40),
        name="mlstm",
    )(p3, p3, p3, p3, conv_w, conv_w, conv_b, conv_b, norm_g, gcol, grow)


def _merge_kernel(yh_ref, ym_ref, gh_ref, gm_ref, x_ref, wa_ref, wb_ref, wo_ref, o_ref):
    a = jnp.dot(yh_ref[...], wa_ref[...], preferred_element_type=F32)
    b = jnp.dot(ym_ref[...], wb_ref[...], preferred_element_type=F32)
    t = jax.nn.sigmoid(gh_ref[...].astype(F32)) * a + jax.nn.sigmoid(gm_ref[...].astype(F32)) * b
    o_ref[...] = x_ref[...] + jnp.dot(t.astype(BF16), wo_ref[...], preferred_element_type=F32)


def _merge(y_hy, y_ml, p_big, x2, w_a, w_b, w_o, tm=256):
    n, d = x2.shape
    wh, wm = y_hy.shape[1], y_ml.shape[1]
    return pl.pallas_call(
        _merge_kernel,
        grid=(n // tm,),
        in_specs=[pl.BlockSpec((tm, wh), lambda i: (i, 0)),
                  pl.BlockSpec((tm, wm), lambda i: (i, 0)),
                  pl.BlockSpec((tm, d), lambda i: (i, 0)),
                  pl.BlockSpec((tm, d), lambda i: (i, 1)),
                  pl.BlockSpec((tm, d), lambda i: (i, 0)),
                  _const_spec(w_a.shape), _const_spec(w_b.shape), _const_spec(w_o.shape)],
        out_specs=pl.BlockSpec((tm, d), lambda i: (i, 0)),
        out_shape=jax.ShapeDtypeStruct((n, d), F32),
        compiler_params=_cparams(("parallel",), 48),
        name="branch_merge",
    )(y_hy, y_ml, p_big, p_big, x2, w_a, w_b, w_o)


def _dense_ffn_kernel(x_ref, g_ref, wg_ref, wu_ref, wd_ref, o_ref, h_s):
    @pl.when(pl.program_id(1) == 0)
    def _():
        x = x_ref[...]
        h_s[...] = _rms(x, g_ref[...]).astype(BF16)
        o_ref[...] = x

    h = h_s[...]
    gt = jnp.dot(h, wg_ref[...], preferred_element_type=F32)
    up = jnp.dot(h, wu_ref[...], preferred_element_type=F32)
    act = (gt * jax.nn.sigmoid(gt) * up).astype(BF16)
    o_ref[...] += jnp.dot(act, wd_ref[...], preferred_element_type=F32)


def _dense_ffn(x2, g, w_gate, w_up, w_down, tm=512, tf=512):
    n, d = x2.shape
    f = w_gate.shape[1]
    return pl.pallas_call(
        _dense_ffn_kernel,
        grid=(n // tm, f // tf),
        in_specs=[pl.BlockSpec((tm, d), lambda i, j: (i, 0)),
                  pl.BlockSpec((1, d), lambda i, j: (0, 0)),
                  pl.BlockSpec((d, tf), lambda i, j: (0, j)),
                  pl.BlockSpec((d, tf), lambda i, j: (0, j)),
                  pl.BlockSpec((tf, d), lambda i, j: (j, 0))],
        out_specs=pl.BlockSpec((tm, d), lambda i, j: (i, 0)),
        out_shape=jax.ShapeDtypeStruct((n, d), F32),
        scratch_shapes=[pltpu.VMEM((tm, d), BF16)],
        compiler_params=_cparams(("parallel", "arbitrary"), 56),
        name="dense_swiglu",
    )(x2, g, w_gate, w_up, w_down)


def _store_rows_3d(ref3, val2):
    for s in range(ref3.shape[1]):
        ref3[:, s, :] = val2[:, s * LANES:(s + 1) * LANES]


def _router_kernel(x_ref, g_ref, rwh_ref, rwl_ref, rb_ref, h_ref, e_ref, gate_ref, rank_ref, cnt_ref, base_s):
    @pl.when(pl.program_id(0) == 0)
    def _():
        base_s[...] = jnp.zeros_like(base_s)

    hn = _rms(x_ref[...], g_ref[...])
    _store_rows_3d(h_ref, hn)
    logits = _dot_split(hn, rwh_ref[...], rwl_ref[...]) + rb_ref[...]
    lane = lax.broadcasted_iota(I32, logits.shape, 1)
    logits = jnp.where(lane < N_EXPERTS, logits, -jnp.inf)
    m1 = jnp.max(logits, axis=-1, keepdims=True)
    i1 = jnp.min(jnp.where(logits == m1, lane, LANES), axis=-1, keepdims=True)
    rest = jnp.where(lane == i1, -jnp.inf, logits)
    m2 = jnp.max(rest, axis=-1, keepdims=True)
    i2 = jnp.min(jnp.where(rest == m2, lane, LANES), axis=-1, keepdims=True)
    e = jnp.exp(m2 - m1)
    slot = lax.broadcasted_iota(I32, e_ref.shape, 1)
    e_ref[...] = jnp.where(slot == 0, i1, i2)
    gate_ref[...] = jnp.where(slot == 0, 1.0 / (1.0 + e), e / (1.0 + e))
    tm = logits.shape[0]
    oh1 = (lane == i1).astype(F32)
    oh2 = (lane == i2).astype(F32)
    both = oh1 + oh2
    earlier = (lax.broadcasted_iota(I32, (tm, tm), 1) < lax.broadcasted_iota(I32, (tm, tm), 0)).astype(BF16)
    before = jnp.dot(earlier, both.astype(BF16), preferred_element_type=F32) + base_s[...]
    r1 = jnp.sum(before * oh1, axis=-1, keepdims=True)
    r2 = jnp.sum(before * oh2, axis=-1, keepdims=True)
    rank_ref[...] = jnp.where(slot == 0, r1, r2).astype(I32)
    base_s[...] += jnp.sum(both, axis=0, keepdims=True)
    cnt_ref[...] = base_s[...].astype(I32)


def _router(x2, g, rw_pad, rb_pad, tm=512):
    n, d = x2.shape
    s = d // LANES
    rw_hi, rw_lo = _split_bf16(rw_pad)
    return pl.pallas_call(
        _router_kernel,
        grid=(n // tm,),
        in_specs=[pl.BlockSpec((tm, d), lambda i: (i, 0)),
                  pl.BlockSpec((1, d), lambda i: (0, 0)),
                  pl.BlockSpec((d, LANES), lambda i: (0, 0)),
                  pl.BlockSpec((d, LANES), lambda i: (0, 0)),
                  pl.BlockSpec((1, LANES), lambda i: (0, 0))],
        out_specs=[pl.BlockSpec((tm, s, LANES), lambda i: (i, 0, 0)),
                   pl.BlockSpec((tm, TOP_K), lambda i: (i, 0)),
                   pl.BlockSpec((tm, TOP_K), lambda i: (i, 0)),
                   pl.BlockSpec((tm, TOP_K), lambda i: (i, 0)),
                   pl.BlockSpec((1, LANES), lambda i: (0, 0))],
        out_shape=[jax.ShapeDtypeStruct((n, s, LANES), F32),
                   jax.ShapeDtypeStruct((n, TOP_K), I32),
                   jax.ShapeDtypeStruct((n, TOP_K), F32),
                   jax.ShapeDtypeStruct((n, TOP_K), I32),
                   jax.ShapeDtypeStruct((1, LANES), I32)],
        scratch_shapes=[pltpu.VMEM((1, LANES), F32)],
        compiler_params=_cparams(("arbitrary",), 32),
        name="moe_router",
    )(x2, g, rw_hi, rw_lo, rb_pad)


def _slab_pitch(s):
    return s if (s // 8) % 2 == 1 else s + 8


def _row_gather(src_hbm, idx_ref, idx_base, dst, sem, rows):
    slab = src_hbm.shape[1]

    def copy(r):
        return pltpu.make_async_copy(src_hbm.at[idx_ref[idx_base + r]], dst.at[r, pl.ds(0, slab)], sem)

    def start_one(r, carry):
        copy(r).start()
        return carry

    def wait_one(r, carry):
        copy(r).wait()
        return carry

    return (lambda: lax.fori_loop(0, rows, start_one, 0, unroll=8),
            lambda: lax.fori_loop(0, rows, wait_one, 0, unroll=8))


def _expert_ffn_kernel(be_ref, nu_ref, tok_ref, h_hbm, wg_ref, wu_ref, wd_ref, o_ref,
                       xg_s, x2_s, h_s, acc_s, sems):
    i, j = pl.program_id(0), pl.program_id(1)
    blk = xg_s.shape[1]
    n_used = nu_ref[0]

    def gather(block):
        slot = block % 2
        return _row_gather(h_hbm, tok_ref, block * blk, xg_s.at[slot], sems.at[slot], blk)

    @pl.when(jnp.logical_and(i == 0, j == 0))
    def _():
        gather(0)[0]()

    @pl.when(jnp.logical_and(j == 0, i + 1 < n_used))
    def _():
        gather(i + 1)[0]()

    @pl.when(jnp.logical_and(j == 0, i < n_used))
    def _():
        gather(i)[1]()
        slot = i % 2
        for s in range(h_hbm.shape[1]):
            x2_s[:, s * LANES:(s + 1) * LANES] = xg_s[slot, :, s, :]
        h_s[...] = x2_s[...].astype(BF16)

    @pl.when(j == 0)
    def _():
        acc_s[...] = jnp.zeros_like(acc_s)

    @pl.when(i < n_used)
    def _():
        h = h_s[...]
        gt = jnp.dot(h, wg_ref[...], preferred_element_type=F32)
        up = jnp.dot(h, wu_ref[...], preferred_element_type=F32)
        act = (gt * jax.nn.sigmoid(gt) * up).astype(BF16)
        acc_s[...] += jnp.dot(act, wd_ref[...], preferred_element_type=F32)

    @pl.when(j == pl.num_programs(1) - 1)
    def _():
        for s in range(o_ref.shape[1]):
            o_ref[:, s, :] = acc_s[:, s * LANES:(s + 1) * LANES]


def _expert_ffn(h3, slot_tok, block_e, n_used, w_gate, w_up, w_down, blk, tf=512):
    ns = slot_tok.shape[0]
    s = h3.shape[1]
    d = s * LANES
    f = w_gate.shape[2]
    nj = f // tf

    def jj(i, j, nu):
        return jnp.where(i < nu[0], j, nj - 1)

    return pl.pallas_call(
        _expert_ffn_kernel,
        grid_spec=pltpu.PrefetchScalarGridSpec(
            num_scalar_prefetch=3,
            grid=(ns // blk, nj),
            in_specs=[pl.BlockSpec(memory_space=pl.ANY),
                      pl.BlockSpec((None, d, tf), lambda i, j, be, nu, tok: (be[i], 0, jj(i, j, nu))),
                      pl.BlockSpec((None, d, tf), lambda i, j, be, nu, tok: (be[i], 0, jj(i, j, nu))),
                      pl.BlockSpec((None, tf, d), lambda i, j, be, nu, tok: (be[i], jj(i, j, nu), 0))],
            out_specs=pl.BlockSpec((blk, s, LANES), lambda i, j, be, nu, tok: (i, 0, 0)),
            scratch_shapes=[pltpu.VMEM((2, blk, _slab_pitch(s), LANES), F32), pltpu.VMEM((blk, d), F32),
                            pltpu.VMEM((blk, d), BF16), pltpu.VMEM((blk, d), F32),
                            pltpu.SemaphoreType.DMA((2,))],
        ),
        out_shape=jax.ShapeDtypeStruct((ns, s, LANES), F32),
        compiler_params=_cparams(("arbitrary", "arbitrary"), 52),
        name="expert_swiglu",
    )(block_e, n_used, slot_tok, h3, w_gate, w_up, w_down)


def _combine_kernel(dest_ref, x_ref, gate_ref, y_hbm, g_ref, o_ref, yg_s, sems, *, final_norm):
    i, n_tiles = pl.program_id(0), pl.num_programs(0)
    tm = x_ref.shape[0]

    def gather(tile):
        slot = tile % 2
        parts = [_row_gather(y_hbm, dest_ref, k * n_tiles * tm + tile * tm, yg_s.at[slot, k], sems.at[slot], tm)
                 for k in range(TOP_K)]
        return (lambda: [p[0]() for p in parts]), (lambda: [p[1]() for p in parts])

    @pl.when(i == 0)
    def _():
        gather(0)[0]()

    @pl.when(i + 1 < n_tiles)
    def _():
        gather(i + 1)[0]()

    gather(i)[1]()
    slot = i % 2
    gates = [jnp.broadcast_to(gate_ref[:, k:k + 1], (tm, LANES)) for k in range(TOP_K)]
    for s in range(y_hbm.shape[1]):
        cols = slice(s * LANES, (s + 1) * LANES)
        acc = x_ref[:, cols]
        for k in range(TOP_K):
            acc = acc + gates[k] * yg_s[slot, k, :, s, :]
        o_ref[:, cols] = acc
    if final_norm:
        o_ref[...] = _rms(o_ref[...], g_ref[...])


def _combine(x2, gate, yb, dest_by_k, g, final_norm, tm=512):
    n, d = x2.shape
    s = d // LANES
    return pl.pallas_call(
        functools.partial(_combine_kernel, final_norm=final_norm),
        grid_spec=pltpu.PrefetchScalarGridSpec(
            num_scalar_prefetch=1,
            grid=(n // tm,),
            in_specs=[pl.BlockSpec((tm, d), lambda i, dest: (i, 0)),
                      pl.BlockSpec((tm, TOP_K), lambda i, dest: (i, 0)),
                      pl.BlockSpec(memory_space=pl.ANY),
                      pl.BlockSpec((1, d), lambda i, dest: (0, 0))],
            out_specs=pl.BlockSpec((tm, d), lambda i, dest: (i, 0)),
            scratch_shapes=[pltpu.VMEM((2, TOP_K, tm, _slab_pitch(s), LANES), F32),
                            pltpu.SemaphoreType.DMA((2,))],
        ),
        out_shape=jax.ShapeDtypeStruct((n, d), F32),
        compiler_params=_cparams(("arbitrary",), 48),
        name="moe_combine",
    )(dest_by_k, x2, gate, yb, g)


def _final_norm_kernel(x_ref, g_ref, o_ref):
    o_ref[...] = _rms(x_ref[...], g_ref[...])


def _final_norm(x2, g, tm=512):
    n, d = x2.shape
    return pl.pallas_call(
        _final_norm_kernel,
        grid=(n // tm,),
        in_specs=[pl.BlockSpec((tm, d), lambda i: (i, 0)), pl.BlockSpec((1, d), lambda i: (0, 0))],
        out_specs=pl.BlockSpec((tm, d), lambda i: (i, 0)),
        out_shape=jax.ShapeDtypeStruct((n, d), F32),
        compiler_params=_cparams(("parallel",), 32),
        name="final_norm",
    )(x2, g)


def _routing_tables(top_e, rank, counts, blk):
    n = top_e.shape[0]
    nk = n * TOP_K
    flat_e = top_e.reshape(-1)
    padded = (counts + blk - 1) // blk * blk
    p_end = jnp.cumsum(padded)
    p_start = p_end - padded
    dest = (p_start[flat_e] + rank.reshape(-1)).astype(I32)
    n_blocks = -(-nk // blk) + N_EXPERTS
    flat_tok = jnp.arange(nk, dtype=I32) // TOP_K
    slot_tok = jnp.zeros((n_blocks * blk,), I32).at[dest].set(flat_tok)
    n_used = (p_end[-1] // blk).astype(I32)
    blocks = jnp.arange(n_blocks, dtype=I32)
    block_e = jnp.sum((p_end[None, :] <= (blocks * blk)[:, None]).astype(I32), axis=1)
    block_e = jnp.minimum(block_e, N_EXPERTS - 1)
    block_e = jnp.where(blocks < n_used, block_e, block_e[n_used - 1])
    dest_by_k = dest.reshape(n, TOP_K).T.reshape(-1)
    return slot_tok, block_e, n_used.reshape(1), dest_by_k


def _moe_layer(x2, norm_g, router_w, router_b, w_gate, w_up, w_down, out_g, final_norm):
    n, d = x2.shape
    rw = jnp.zeros((d, LANES), F32).at[:, :N_EXPERTS].set(router_w)
    rb = jnp.zeros((1, LANES), F32).at[0, :N_EXPERTS].set(router_b)
    h, top_e, gate, rank, counts = _router(x2, norm_g[None, :], rw, rb)
    slot_tok, block_e, n_used, dest_by_k = _routing_tables(top_e, rank, counts[0, :N_EXPERTS], MOE_BLOCK)
    yb = _expert_ffn(h, slot_tok, block_e, n_used, w_gate.astype(BF16), w_up.astype(BF16),
                     w_down.astype(BF16), MOE_BLOCK)
    return _combine(x2, gate, yb, dest_by_k, out_g[None, :], final_norm)


def _mixer_layer(x2, bsz, length, norm_g, w_in, b_in, hy_conv_w, hy_conv_b, hy_w1, hy_b1, hy_w2, hy_b2,
                 hy_w3, hy_freq, hy_decay, hy_skip, ml_conv_w, ml_conv_b, ml_norm_g, w_a, w_b, w_o,
                 ctab, stab, twiddle, zpos):
    n, d = x2.shape
    wh = hy_skip.shape[1]
    wm = ml_norm_g.shape[0]
    off_qk = (HYENA_ORDER + 1) * wh
    off_v = off_qk + 2 * wm
    off_o = off_v + wm
    off_gates = off_o + wm
    off_br = off_gates + 4 * MLSTM_HEADS
    w_cat = jnp.concatenate([w_in[:, off_br:], w_in[:, :off_gates]], axis=1).astype(BF16)
    b_cat = jnp.concatenate([b_in[off_br:], b_in[:off_gates]])[None, :]
    col_hy = 2 * d
    col_q = col_hy + off_qk
    col_k = col_q + wm
    col_v = col_hy + off_v
    col_o = col_hy + off_o

    p_big, gates = _in_proj(x2, norm_g[None, :], w_in[:, off_gates:off_br], b_in[None, off_gates:off_br],
                            w_cat, b_cat)
    p3 = p_big.reshape(bsz, length, p_big.shape[1])

    kp = LANES
    pad2 = lambda a, r, c: jnp.zeros((r, c), F32).at[:a.shape[0], :a.shape[1]].set(a)
    zp = pad2(zpos, length, kp)
    htab, hmid = _hyena_spectrum(
        zp, pad2(hy_w1, kp, kp), pad2(hy_b1[None, :], 1, kp), pad2(hy_w2, kp, kp), pad2(hy_b2[None, :], 1, kp),
        pad2(hy_freq[None, :], 1, kp), pad2(hy_w3, kp, hy_w3.shape[1]), hy_decay[None, :],
        ctab, stab, twiddle, wh, HYENA_CT)
    y_hy = _hyena_conv(p3, col_hy, hy_conv_w, hy_conv_b[None, :], htab, hmid, hy_skip[:, None, :],
                       ctab, stab, wh, HYENA_CT, HYENA_RC)

    gcol, grow = _gate_prep(gates.reshape(bsz, length, 4 * MLSTM_HEADS), MLSTM_CHUNK)
    y_ml = _mlstm(p3, col_q, col_k, col_v, col_o, ml_conv_w, ml_conv_b[None, :], ml_norm_g[None, :],
                  gcol, grow, MLSTM_CHUNK)

    return _merge(y_hy.reshape(n, wh), y_ml.reshape(n, wm), p_big, x2,
                  w_a.astype(BF16), w_b.astype(BF16), w_o.astype(BF16))


def kernel(x, mix_norm_g, mix_w_in, mix_b_in, hy_conv_w, hy_conv_b, hy_filt_w1, hy_filt_b1, hy_filt_w2,
           hy_filt_b2, hy_filt_w3, hy_filt_freq, hy_filt_decay, hy_skip, ml_conv_w, ml_conv_b, ml_norm_g,
           mix_w_a, mix_w_b, mix_w_o, ffn_norm_g, dense_w_gate, dense_w_up, dense_w_down, moe_router_w,
           moe_router_b, moe_w_gate, moe_w_up, moe_w_down, final_norm_g):
    bsz, length, d = x.shape
    depth = mix_norm_g.shape[0]
    x2 = x.reshape(bsz * length, d)
    ctab, stab, twiddle = _dft_tables(length // 2)
    zpos = _hyena_positions(length)
    normed = False
    for layer in range(depth):
        x2 = _mixer_layer(x2, bsz, length, mix_norm_g[layer], mix_w_in[layer], mix_b_in[layer],
                          hy_conv_w[layer], hy_conv_b[layer], hy_filt_w1[layer], hy_filt_b1[layer],
                          hy_filt_w2[layer], hy_filt_b2[layer], hy_filt_w3[layer], hy_filt_freq[layer],
                          hy_filt_decay[layer], hy_skip[layer], ml_conv_w[layer], ml_conv_b[layer],
                          ml_norm_g[layer], mix_w_a[layer], mix_w_b[layer], mix_w_o[layer], ctab, stab, twiddle,
                          zpos)
        j = layer // 2
        if layer % 2 == 0:
            x2 = _dense_ffn(x2, ffn_norm_g[layer][None, :], dense_w_gate[j].astype(BF16),
                            dense_w_up[j].astype(BF16), dense_w_down[j].astype(BF16))
        else:
            normed = layer == depth - 1
            x2 = _moe_layer(x2, ffn_norm_g[layer], moe_router_w[j], moe_router_b[j], moe_w_gate[j],
                            moe_w_up[j], moe_w_down[j], final_norm_g, normed)
    if not normed:
        x2 = _final_norm(x2, final_norm_g[None, :])
    return x2.reshape(bsz, length, d)
```

```python
import functools
import math

import jax
import jax.numpy as jnp
import numpy as np
from jax import lax
from jax.experimental import pallas as pl
from jax.experimental.pallas import tpu as pltpu

F32, BF16, I32 = jnp.float32, jnp.bfloat16, jnp.int32
HIGHEST = lax.Precision.HIGHEST

HYENA_ORDER = 2
HYENA_POS_BANDS = 16
MLSTM_HEADS = 8
N_EXPERTS = 8
TOP_K = 2
EPS = 1e-6

V7X_VMEM_BYTES = 64 * 1024 * 1024
LANES = 128
MXU_DIM = 256

MLSTM_CHUNK = 256
HYENA_CT = MXU_DIM
HYENA_RC = 512
MOE_BLOCK = 512


def _cparams(semantics, vmem_mib):
    assert vmem_mib * 1024 * 1024 <= V7X_VMEM_BYTES
    return pltpu.CompilerParams(dimension_semantics=semantics, vmem_limit_bytes=vmem_mib * 1024 * 1024)


def _const_spec(shape):
    nd = len(shape)
    return pl.BlockSpec(shape, lambda *_: (0,) * nd, pipeline_mode=pl.Buffered(1))


def _rms(x, g):
    return x * lax.rsqrt(jnp.mean(x * x, axis=-1, keepdims=True) + EPS) * g


def _shift_rows(u, direction):
    n = u.shape[0]
    row = lax.broadcasted_iota(I32, u.shape, 0)
    if direction > 0:
        return jnp.where(row == 0, 0.0, pltpu.roll(u, 1, axis=0))
    return jnp.where(row == n - 1, 0.0, pltpu.roll(u, n - 1, axis=0))


def _short_conv(u, w, b):
    return b + _shift_rows(u, 1) * w[0:1] + u * w[1:2] + _shift_rows(u, -1) * w[2:3]


def _log_sigmoid(x):
    return jnp.minimum(x, 0.0) - jnp.log1p(jnp.exp(-jnp.abs(x)))


def _dot_split(a, w_hi, w_lo):
    a_hi = a.astype(BF16)
    a_lo = (a - a_hi.astype(F32)).astype(BF16)
    return (jnp.dot(a_hi, w_hi, preferred_element_type=F32) + jnp.dot(a_lo, w_hi, preferred_element_type=F32)
            + jnp.dot(a_hi, w_lo, preferred_element_type=F32))


def _split_bf16(w):
    w_hi = w.astype(BF16)
    return w_hi, (w - w_hi.astype(F32)).astype(BF16)


def _in_proj_kernel(x_ref, g_ref, wg_ref, bg_ref, w_ref, b_ref, o_ref, gates_ref, h_s):
    @pl.when(pl.program_id(1) == 0)
    def _():
        h_s[...] = _rms(x_ref[...], g_ref[...]).astype(BF16)
        gates_ref[...] = jnp.dot(h_s[...], wg_ref[...], preferred_element_type=F32) + bg_ref[...]

    acc = jnp.dot(h_s[...], w_ref[...], preferred_element_type=F32) + b_ref[...]
    o_ref[...] = acc.astype(o_ref.dtype)


def _in_proj(x2, g, w_gates, b_gates, w, b, tm=1024, tn=1024):
    n, d = x2.shape
    nc = w.shape[1]
    ng = w_gates.shape[1]
    return pl.pallas_call(
        _in_proj_kernel,
        grid=(n // tm, nc // tn),
        in_specs=[pl.BlockSpec((tm, d), lambda i, j: (i, 0)),
                  pl.BlockSpec((1, d), lambda i, j: (0, 0)),
                  pl.BlockSpec((d, ng), lambda i, j: (0, 0)),
                  pl.BlockSpec((1, ng), lambda i, j: (0, 0)),
                  pl.BlockSpec((d, tn), lambda i, j: (0, j)),
                  pl.BlockSpec((1, tn), lambda i, j: (0, j))],
        out_specs=[pl.BlockSpec((tm, tn), lambda i, j: (i, j)),
                   pl.BlockSpec((tm, ng), lambda i, j: (i, 0))],
        out_shape=[jax.ShapeDtypeStruct((n, nc), BF16), jax.ShapeDtypeStruct((n, ng), F32)],
        scratch_shapes=[pltpu.VMEM((tm, d), BF16)],
        compiler_params=_cparams(("parallel", "arbitrary"), 48),
        name="in_proj",
    )(x2, g, w_gates.astype(BF16), b_gates, w, b)


def _dft_tables(half):
    k = jnp.arange(half, dtype=I32)
    ang = ((k[:, None] * k[None, :]) % (2 * half)).astype(F32) * (math.pi / half)
    tw = k.astype(F32)[:, None] * (math.pi / (2 * half))
    twiddle = jnp.stack([jnp.broadcast_to(jnp.cos(tw), (half, LANES)), jnp.broadcast_to(jnp.sin(tw), (half, LANES))])
    return jnp.cos(ang).astype(BF16), jnp.sin(ang).astype(BF16), twiddle


def _hyena_positions(length):
    t = np.linspace(0.0, 1.0, length)[:, None]
    n = np.arange(length, dtype=np.float64)[:, None]
    bands = np.linspace(1e-4, HYENA_POS_BANDS - 1, HYENA_POS_BANDS)[None, :]
    ang = (2.0 * np.pi / length) * n * bands
    return jnp.asarray(np.concatenate([t, np.cos(ang), -np.sin(ang)], axis=-1), dtype=F32)


def _split_rows(val, slab_s):
    half = val.shape[0] // 2
    n_slabs = slab_s.shape[0]
    for s in range(n_slabs):
        slab_s[s] = val[:, s * LANES:(s + 1) * LANES]
    pick = lambda p: jnp.concatenate([slab_s[s, pl.ds(p, half, stride=2), :] for s in range(n_slabs)], axis=1)
    return pick(0), pick(1)


def _merge_rows(even, odd, slab_s):
    half = even.shape[0]
    n_slabs = slab_s.shape[0]
    for s in range(n_slabs):
        slab_s[s, pl.ds(0, half, stride=2), :] = even[:, s * LANES:(s + 1) * LANES]
        slab_s[s, pl.ds(1, half, stride=2), :] = odd[:, s * LANES:(s + 1) * LANES]
    return jnp.concatenate([slab_s[s] for s in range(n_slabs)], axis=1)


def _alt_sum(x):
    row = lax.broadcasted_iota(I32, (x.shape[0], 1), 0)
    return jnp.sum(jnp.where(row % 2 == 0, x, -x), axis=0, keepdims=True)


def _hyena_spectrum_kernel(z_ref, w1_ref, b1_ref, w2_ref, b2_ref, fr_ref, w3f_ref, w3b_ref,
                           decf_ref, decb_ref, c_ref, s_ref, tw_ref, tab_ref, mid_ref, slab_s, hid_s):
    half = c_ref.shape[0]
    length = 2 * half
    ct = w3f_ref.shape[1]
    dot_hi = functools.partial(jnp.dot, precision=HIGHEST, preferred_element_type=F32)
    dot = functools.partial(jnp.dot, preferred_element_type=F32)
    z = z_ref[...]

    @pl.when(jnp.logical_and(pl.program_id(0) == 0, pl.program_id(1) == 0))
    def _():
        fr = fr_ref[...]
        hid = jnp.sin(fr * (dot_hi(z, w1_ref[...]) + b1_ref[...]))
        hid_s[...] = jnp.sin(fr * (dot_hi(hid, w2_ref[...]) + b2_ref[...]))

    hid = hid_s[...]
    t = z[:, 0:1]
    ff = dot_hi(hid, w3f_ref[...]) * jnp.exp(-t * jnp.abs(decf_ref[...]))
    fb = dot_hi(hid, w3b_ref[...]) * jnp.exp(-t * jnp.abs(decb_ref[...]))
    row = lax.broadcasted_iota(I32, (length, 1), 0)
    l1 = jnp.sum(jnp.where(row == 0, jnp.abs(ff + fb), jnp.abs(ff) + jnp.abs(fb)), axis=0, keepdims=True)
    inv = 1.0 / l1
    se, so = _split_rows((ff + fb) * inv, slab_s)
    de, do = _split_rows((ff - fb) * inv, slab_s)
    wide = lambda a: jnp.concatenate([a] * (ct // LANES), axis=1)
    cw, sw = wide(tw_ref[0]), wide(tw_ref[1])
    c_t, s_t = c_ref[...], s_ref[...]
    bf = lambda a: a.astype(BF16)
    e_r = dot(c_t, bf(se))
    o_r, o_s = dot(c_t, bf(so)), dot(s_t, bf(so))
    t_r = cw * o_r - sw * o_s
    d_s = dot(s_t, bf(de))
    p_r, p_s = dot(c_t, bf(do)), dot(s_t, bf(do))
    t_i = -(cw * p_s + sw * p_r)
    krow = lax.broadcasted_iota(I32, (half, 1), 0)
    scale = jnp.where(krow == 0, 1.0 / (2 * length), 1.0 / length)
    lo_r, lo_i = (e_r + t_r) * scale, (t_i - d_s) * scale
    hi_r, hi_i = (e_r - t_r) * scale, (t_i + d_s) * scale

    def emit(p, lr, li, hr, hi):
        a_r, a_i = lr + hr, li - hi
        d_r, d_i = lr - hr, li + hi
        tab_ref[4 * p + 0] = a_r
        tab_ref[4 * p + 1] = a_i
        tab_ref[4 * p + 2] = d_r * cw + d_i * sw
        tab_ref[4 * p + 3] = d_i * cw - d_r * sw

    emit(0, lo_r, lo_i, hi_r, hi_i)
    emit(1, lo_r * cw - lo_i * sw, lo_r * sw + lo_i * cw, -(hi_r * cw + hi_i * sw), -(hi_i * cw - hi_r * sw))
    mid_ref[0:1, :] = _alt_sum(se) * (1.0 / length)
    mid_ref[1:2, :] = -_alt_sum(do) * (1.0 / length)


def _hyena_spectrum(z, w1, b1, w2, b2, freq, w3, decay, ctab, stab, twiddle, width, ct):
    half = ctab.shape[0]
    kp = w1.shape[1]
    nct = width // ct
    col_f = lambda o, c: (0, o * 2 * nct + c)
    col_b = lambda o, c: (0, o * 2 * nct + nct + c)
    return pl.pallas_call(
        _hyena_spectrum_kernel,
        grid=(HYENA_ORDER, nct),
        in_specs=[_const_spec(z.shape), _const_spec(w1.shape), _const_spec(b1.shape),
                  _const_spec(w2.shape), _const_spec(b2.shape), _const_spec(freq.shape),
                  pl.BlockSpec((kp, ct), col_f), pl.BlockSpec((kp, ct), col_b),
                  pl.BlockSpec((1, ct), col_f), pl.BlockSpec((1, ct), col_b),
                  _const_spec(ctab.shape), _const_spec(stab.shape), _const_spec(twiddle.shape)],
        out_specs=[pl.BlockSpec((None, 8, half, ct), lambda o, c: (o, 0, 0, c)),
                   pl.BlockSpec((None, 2, ct), lambda o, c: (o, 0, c))],
        out_shape=[jax.ShapeDtypeStruct((HYENA_ORDER, 8, half, width), F32),
                   jax.ShapeDtypeStruct((HYENA_ORDER, 2, width), F32)],
        scratch_shapes=[pltpu.VMEM((ct // LANES, 2 * half, LANES), F32), pltpu.VMEM((2 * half, kp), F32)],
        compiler_params=_cparams(("arbitrary", "arbitrary"), 48),
        name="hyena_spectrum",
    )(z, w1, b1, w2, b2, freq, w3, w3, decay, decay, ctab, stab, twiddle)


def _hyena_conv_kernel(pv_ref, p1_ref, p2_ref, wv_ref, w1_ref, w2_ref, bv_ref, b1_ref, b2_ref,
                       tab_ref, mid_ref, skip_ref, c_ref, s_ref, y_ref,
                       slab_s, z_s, g1_s, g2_s, zb_s, gr_s, gn_s, *, rc):
    half = c_ref.shape[0]
    dot = functools.partial(jnp.dot, preferred_element_type=F32)
    for dst, p_ref, w_ref, b_ref in ((z_s, pv_ref, wv_ref, bv_ref), (g1_s, p1_ref, w1_ref, b1_ref),
                                     (g2_s, p2_ref, w2_ref, b2_ref)):
        even, odd = _split_rows(_short_conv(p_ref[...].astype(F32), w_ref[...], b_ref[...]), slab_s)
        dst[0] = even
        dst[1] = odd
    gates = (g1_s, g2_s)
    row = lax.broadcasted_iota(I32, (half, 1), 0)
    sign = jnp.where(row % 2 == 0, 1.0, -1.0)
    for o in range(HYENA_ORDER):
        zb_s[0] = z_s[0].astype(BF16)
        zb_s[1] = z_s[1].astype(BF16)
        a_e, a_o = _alt_sum(z_s[0]), _alt_sum(z_s[1])
        mid = (mid_ref[o, 0:1, :] * a_e + mid_ref[o, 1:2, :] * a_o,
               mid_ref[o, 0:1, :] * a_o - mid_ref[o, 1:2, :] * a_e)
        for r in range(half // rc):
            rows = slice(r * rc, (r + 1) * rc)
            e_r, e_s = dot(c_ref[rows, :], zb_s[0]), dot(s_ref[rows, :], zb_s[0])
            o_r, o_s = dot(c_ref[rows, :], zb_s[1]), dot(s_ref[rows, :], zb_s[1])
            for p in range(2):
                a_r, a_i = tab_ref[o, 4 * p + 0, rows, :], tab_ref[o, 4 * p + 1, rows, :]
                b_r, b_i = tab_ref[o, 4 * p + 2, rows, :], tab_ref[o, 4 * p + 3, rows, :]
                gr_s[p, rows, :] = (a_r * e_r + a_i * e_s + b_r * o_r + b_i * o_s).astype(BF16)
                gn_s[p, rows, :] = (a_r * e_s - a_i * e_r + b_r * o_s - b_i * o_r).astype(BF16)
        for p in range(2):
            for r in range(half // rc):
                rows = slice(r * rc, (r + 1) * rc)
                conv = dot(c_ref[rows, :], gr_s[p]) + dot(s_ref[rows, :], gn_s[p]) + sign[rows] * mid[p]
                z_s[p, rows, :] = gates[o][p, rows, :] * (conv + skip_ref[o] * z_s[p, rows, :])
    y_ref[...] = _merge_rows(z_s[0], z_s[1], slab_s).astype(y_ref.dtype)


def _hyena_conv(p3, col0, conv_w, conv_b, tab, mid, skip, ctab, stab, width, ct, rc):
    bsz, length, _ = p3.shape
    half = length // 2
    nct = width // ct
    c0 = col0 // ct

    def pspec(part):
        return pl.BlockSpec((None, length, ct), lambda c, b: (b, 0, c0 + part * nct + c))

    def wspec(rows, part):
        return pl.BlockSpec((rows, ct), lambda c, b: (0, part * nct + c))

    scratch = ([pltpu.VMEM((ct // LANES, length, LANES), F32)] + [pltpu.VMEM((2, half, ct), F32)] * 3
               + [pltpu.VMEM((2, half, ct), BF16)] * 3)
    return pl.pallas_call(
        functools.partial(_hyena_conv_kernel, rc=rc),
        grid=(nct, bsz),
        in_specs=[pspec(0), pspec(1), pspec(2),
                  wspec(3, 0), wspec(3, 1), wspec(3, 2), wspec(1, 0), wspec(1, 1), wspec(1, 2),
                  pl.BlockSpec((HYENA_ORDER, 8, half, ct), lambda c, b: (0, 0, 0, c),
                               pipeline_mode=pl.Buffered(1)),
                  pl.BlockSpec((HYENA_ORDER, 2, ct), lambda c, b: (0, 0, c)),
                  pl.BlockSpec((HYENA_ORDER, 1, ct), lambda c, b: (0, 0, c)),
                  _const_spec(ctab.shape), _const_spec(stab.shape)],
        out_specs=pl.BlockSpec((None, length, ct), lambda c, b: (b, 0, c)),
        out_shape=jax.ShapeDtypeStruct((bsz, length, width), BF16),
        scratch_shapes=scratch,
        compiler_params=_cparams(("parallel", "parallel"), 56),
        name="hyena_conv",
    )(p3, p3, p3, conv_w, conv_w, conv_w, conv_b, conv_b, conv_b, tab, mid, skip, ctab, stab)


def _gate_prep_kernel(ic_ref, fc_ref, ir_ref, fr_ref, col_ref, row_ref, *, chunk):
    length, nch = ic_ref.shape
    nh = nch // 2
    r = lax.broadcasted_iota(I32, (chunk, chunk), 0)
    c = lax.broadcasted_iota(I32, (chunk, chunk), 1)
    lower = (c <= r).astype(F32)
    upper = (c >= r).astype(F32)
    dot_hi = functools.partial(jnp.dot, precision=HIGHEST, preferred_element_type=F32)
    fwd_col = lax.broadcasted_iota(I32, (chunk, nch), 1) < nh
    fwd_row = lax.broadcasted_iota(I32, (nch, chunk), 0) < nh
    pos = lax.broadcasted_iota(I32, (chunk, nch), 0)
    for ch in range(length // chunk):
        rows = slice(ch * chunk, (ch + 1) * chunk)
        lf = _log_sigmoid(fc_ref[rows, :])
        b = jnp.where(fwd_col, dot_hi(lower, lf), dot_hi(upper, lf))
        u = ic_ref[rows, :] - b
        cm_f, cm_b = u, u
        shift = 1
        while shift < chunk:
            cm_f = jnp.maximum(cm_f, jnp.where(pos >= shift, pltpu.roll(cm_f, shift, axis=0), -jnp.inf))
            cm_b = jnp.maximum(cm_b, jnp.where(pos + shift < chunk, pltpu.roll(cm_b, chunk - shift, axis=0), -jnp.inf))
            shift *= 2
        col_ref[0, rows, :] = b
        col_ref[1, rows, :] = u
        col_ref[2, rows, :] = jnp.where(fwd_col, cm_f, cm_b)
        lfr = _log_sigmoid(fr_ref[:, rows])
        b_r = jnp.where(fwd_row, dot_hi(lfr, upper), dot_hi(lfr, lower))
        row_ref[:, rows] = ir_ref[:, rows] - b_r


def _gate_prep(gates3, chunk):
    bsz, length, nch4 = gates3.shape
    nh = nch4 // 4
    i_col = jnp.concatenate([gates3[..., :nh], gates3[..., 2 * nh:3 * nh]], axis=-1)
    f_col = jnp.concatenate([gates3[..., nh:2 * nh], gates3[..., 3 * nh:]], axis=-1)
    nch = 2 * nh
    cspec = pl.BlockSpec((None, length, nch), lambda b: (b, 0, 0))
    rspec = pl.BlockSpec((None, nch, length), lambda b: (b, 0, 0))
    col, row = pl.pallas_call(
        functools.partial(_gate_prep_kernel, chunk=chunk),
        grid=(bsz,),
        in_specs=[cspec, cspec, rspec, rspec],
        out_specs=[pl.BlockSpec((None, 3, length, nch), lambda b: (b, 0, 0, 0)), rspec],
        out_shape=[jax.ShapeDtypeStruct((bsz, 3, length, nch), F32),
                   jax.ShapeDtypeStruct((bsz, nch, length), F32)],
        compiler_params=_cparams(("parallel",), 32),
        name="mlstm_gate_prep",
    )(i_col, f_col, jnp.swapaxes(i_col, 1, 2), jnp.swapaxes(f_col, 1, 2))
    col = col.reshape(bsz, 3, length, 2, nh).transpose(0, 4, 2, 3, 1).reshape(bsz, nh, length, 6)
    row = row.reshape(bsz, 2, nh, length).transpose(0, 2, 1, 3)
    return col, row


def _mlstm_kernel(pq_ref, pk_ref, pv_ref, po_ref, cwq_ref, cwk_ref, cbq_ref, cbk_ref, ng_ref,
                  gcol_ref, grow_ref, y_ref, q_s, k_s, v1_s, tab_s, hf_s, hb_s, *, chunk):
    length, dk = pq_ref.shape
    nc = length // chunk
    rep = chunk // LANES

    def conv_silu(p_ref, w_ref, b_ref):
        c = _short_conv(p_ref[...].astype(F32), w_ref[...], b_ref[...])
        return c * jax.nn.sigmoid(c)

    q_s[...] = (conv_silu(pq_ref, cwq_ref, cbq_ref) * (dk ** -0.5)).astype(BF16)
    k_s[...] = conv_silu(pk_ref, cwk_ref, cbk_ref).astype(BF16)
    v1_s[:, :dk] = pv_ref[...]
    v1_s[:, dk:] = jnp.ones((length, dk), BF16)
    for t in range(tab_s.shape[0]):
        tab_s[t] = jnp.broadcast_to(gcol_ref[:, t:t + 1], (length, LANES))

    row_i = lax.broadcasted_iota(I32, (chunk, chunk), 0)
    col_i = lax.broadcasted_iota(I32, (chunk, chunk), 1)
    wide = lambda a: jnp.concatenate([a] * rep, axis=1)
    both = lambda a: jnp.concatenate([a, a], axis=1)

    def step(c, reverse, state, m):
        d = 1 if reverse else 0
        rows = slice(c * chunk, (c + 1) * chunk)
        edge = c * chunk if reverse else (c + 1) * chunk - 1
        q, k, v1 = q_s[rows, :], k_s[rows, :], v1_s[rows, :]
        b_t, u_t, cm_t = tab_s[3 * d, rows, :], tab_s[3 * d + 1, rows, :], tab_s[3 * d + 2, rows, :]
        u_row = grow_ref[d:d + 1, rows]
        g = tab_s[3 * d, edge:edge + 1, :]
        u_max = tab_s[3 * d + 2, edge:edge + 1, :]
        mask = (col_i >= row_i) if reverse else (col_i <= row_i)
        mm = jnp.maximum(cm_t, m)
        p = jnp.exp(jnp.where(mask, u_row - wide(mm), -jnp.inf))
        qk = lax.dot_general(q, k, (((1,), (1,)), ((), ())), preferred_element_type=F32)
        intra = jnp.dot((qk * p).astype(BF16), v1, preferred_element_type=F32)
        inter = jnp.dot(q, state.astype(BF16), preferred_element_type=F32)
        nd = intra + both(jnp.exp(m - mm)) * inter
        h = nd[:, :dk] / jnp.maximum(jnp.abs(nd[:, dk:]), jnp.exp(-(b_t + mm)))
        kw = (k.astype(F32) * jnp.exp(u_t - u_max)).astype(BF16)
        upd = lax.dot_general(kw, v1, (((0,), (0,)), ((), ())), preferred_element_type=F32)
        m_loc = g + u_max
        m_new = jnp.maximum(g + m, m_loc)
        state = both(jnp.exp(g + m - m_new)) * state + both(jnp.exp(m_loc - m_new)) * upd
        return h, state, m_new

    zero = (jnp.zeros((dk, 2 * dk), F32), jnp.zeros((1, LANES), F32))
    st_f, m_f = zero
    st_b, m_b = zero
    for c in range(nc):
        h, st_f, m_f = step(c, False, st_f, m_f)
        hf_s[c * chunk:(c + 1) * chunk, :] = h
        cb = nc - 1 - c
        h, st_b, m_b = step(cb, True, st_b, m_b)
        hb_s[cb * chunk:(cb + 1) * chunk, :] = h

    ht = hf_s[...] + hb_s[...]
    y_ref[...] = (_rms(ht, ng_ref[...]) * jax.nn.sigmoid(po_ref[...].astype(F32))).astype(y_ref.dtype)


def _mlstm(p3, col_q, col_k, col_v, col_o, conv_w, conv_b, norm_g, gcol, grow, chunk):
    bsz, length, _ = p3.shape
    nh, dh = MLSTM_HEADS, norm_g.shape[1] // MLSTM_HEADS
    assert dh == LANES and chunk % LANES == 0

    def pspec(col):
        return pl.BlockSpec((None, length, dh), lambda b, h: (b, 0, col // dh + h))

    def wspec(rows, part):
        return pl.BlockSpec((rows, dh), lambda b, h: (0, part * nh + h))

    return pl.pallas_call(
        functools.partial(_mlstm_kernel, chunk=chunk),
        grid=(bsz, nh),
        in_specs=[pspec(col_q), pspec(col_k), pspec(col_v), pspec(col_o),
                  wspec(3, 0), wspec(3, 1), wspec(1, 0), wspec(1, 1),
                  pl.BlockSpec((1, dh), lambda b, h: (0, h)),
                  pl.BlockSpec((None, None, length, 6), lambda b, h: (b, h, 0, 0)),
                  pl.BlockSpec((None, None, 2, length), lambda b, h: (b, h, 0, 0))],
        out_specs=pl.BlockSpec((None, length, dh), lambda b, h: (b, 0, h)),
        out_shape=jax.ShapeDtypeStruct((bsz, length, nh * dh), BF16),
        scratch_shapes=[pltpu.VMEM((length, dh), BF16), pltpu.VMEM((length, dh), BF16),
                        pltpu.VMEM((length, 2 * dh), BF16), pltpu.VMEM((6, length, LANES), F32),
                        pltpu.VMEM((length, dh), F32), pltpu.VMEM((length, dh), F32)],
        compiler_params=_cparams(("parallel", "parallel"), 40),
        name="mlstm",
    )(p3, p3, p3, p3, conv_w, conv_w, conv_b, conv_b, norm_g, gcol, grow)


def _merge_kernel(yh_ref, ym_ref, gh_ref, gm_ref, x_ref, wa_ref, wb_ref, wo_ref, o_ref):
    a = jnp.dot(yh_ref[...], wa_ref[...], preferred_element_type=F32)
    b = jnp.dot(ym_ref[...], wb_ref[...], preferred_element_type=F32)
    t = jax.nn.sigmoid(gh_ref[...].astype(F32)) * a + jax.nn.sigmoid(gm_ref[...].astype(F32)) * b
    o_ref[...] = x_ref[...] + jnp.dot(t.astype(BF16), wo_ref[...], preferred_element_type=F32)


def _merge(y_hy, y_ml, p_big, x2, w_a, w_b, w_o, tm=256):
    n, d = x2.shape
    wh, wm = y_hy.shape[1], y_ml.shape[1]
    return pl.pallas_call(
        _merge_kernel,
        grid=(n // tm,),
        in_specs=[pl.BlockSpec((tm, wh), lambda i: (i, 0)),
                  pl.BlockSpec((tm, wm), lambda i: (i, 0)),
                  pl.BlockSpec((tm, d), lambda i: (i, 0)),
                  pl.BlockSpec((tm, d), lambda i: (i, 1)),
                  pl.BlockSpec((tm, d), lambda i: (i, 0)),
                  _const_spec(w_a.shape), _const_spec(w_b.shape), _const_spec(w_o.shape)],
        out_specs=pl.BlockSpec((tm, d), lambda i: (i, 0)),
        out_shape=jax.ShapeDtypeStruct((n, d), F32),
        compiler_params=_cparams(("parallel",), 48),
        name="branch_merge",
    )(y_hy, y_ml, p_big, p_big, x2, w_a, w_b, w_o)


def _dense_ffn_kernel(x_ref, g_ref, wg_ref, wu_ref, wd_ref, o_ref, h_s):
    @pl.when(pl.program_id(1) == 0)
    def _():
        x = x_ref[...]
        h_s[...] = _rms(x, g_ref[...]).astype(BF16)
        o_ref[...] = x

    h = h_s[...]
    gt = jnp.dot(h, wg_ref[...], preferred_element_type=F32)
    up = jnp.dot(h, wu_ref[...], preferred_element_type=F32)
    act = (gt * jax.nn.sigmoid(gt) * up).astype(BF16)
    o_ref[...] += jnp.dot(act, wd_ref[...], preferred_element_type=F32)


def _dense_ffn(x2, g, w_gate, w_up, w_down, tm=512, tf=512):
    n, d = x2.shape
    f = w_gate.shape[1]
    return pl.pallas_call(
        _dense_ffn_kernel,
        grid=(n // tm, f // tf),
        in_specs=[pl.BlockSpec((tm, d), lambda i, j: (i, 0)),
                  pl.BlockSpec((1, d), lambda i, j: (0, 0)),
                  pl.BlockSpec((d, tf), lambda i, j: (0, j)),
                  pl.BlockSpec((d, tf), lambda i, j: (0, j)),
                  pl.BlockSpec((tf, d), lambda i, j: (j, 0))],
        out_specs=pl.BlockSpec((tm, d), lambda i, j: (i, 0)),
        out_shape=jax.ShapeDtypeStruct((n, d), F32),
        scratch_shapes=[pltpu.VMEM((tm, d), BF16)],
        compiler_params=_cparams(("parallel", "arbitrary"), 56),
        name="dense_swiglu",
    )(x2, g, w_gate, w_up, w_down)


def _store_rows_3d(ref3, val2):
    for s in range(ref3.shape[1]):
        ref3[:, s, :] = val2[:, s * LANES:(s + 1) * LANES]


def _router_kernel(x_ref, g_ref, rwh_ref, rwl_ref, rb_ref, h_ref, e_ref, gate_ref, rank_ref, cnt_ref, base_s):
    @pl.when(pl.program_id(0) == 0)
    def _():
        base_s[...] = jnp.zeros_like(base_s)

    hn = _rms(x_ref[...], g_ref[...])
    _store_rows_3d(h_ref, hn)
    logits = _dot_split(hn, rwh_ref[...], rwl_ref[...]) + rb_ref[...]
    lane = lax.broadcasted_iota(I32, logits.shape, 1)
    logits = jnp.where(lane < N_EXPERTS, logits, -jnp.inf)
    m1 = jnp.max(logits, axis=-1, keepdims=True)
    i1 = jnp.min(jnp.where(logits == m1, lane, LANES), axis=-1, keepdims=True)
    rest = jnp.where(lane == i1, -jnp.inf, logits)
    m2 = jnp.max(rest, axis=-1, keepdims=True)
    i2 = jnp.min(jnp.where(rest == m2, lane, LANES), axis=-1, keepdims=True)
    e = jnp.exp(m2 - m1)
    slot = lax.broadcasted_iota(I32, e_ref.shape, 1)
    e_ref[...] = jnp.where(slot == 0, i1, i2)
    gate_ref[...] = jnp.where(slot == 0, 1.0 / (1.0 + e), e / (1.0 + e))
    tm = logits.shape[0]
    oh1 = (lane == i1).astype(F32)
    oh2 = (lane == i2).astype(F32)
    both = oh1 + oh2
    earlier = (lax.broadcasted_iota(I32, (tm, tm), 1) < lax.broadcasted_iota(I32, (tm, tm), 0)).astype(BF16)
    before = jnp.dot(earlier, both.astype(BF16), preferred_element_type=F32) + base_s[...]
    r1 = jnp.sum(before * oh1, axis=-1, keepdims=True)
    r2 = jnp.sum(before * oh2, axis=-1, keepdims=True)
    rank_ref[...] = jnp.where(slot == 0, r1, r2).astype(I32)
    base_s[...] += jnp.sum(both, axis=0, keepdims=True)
    cnt_ref[...] = base_s[...].astype(I32)


def _router(x2, g, rw_pad, rb_pad, tm=512):
    n, d = x2.shape
    s = d // LANES
    rw_hi, rw_lo = _split_bf16(rw_pad)
    return pl.pallas_call(
        _router_kernel,
        grid=(n // tm,),
        in_specs=[pl.BlockSpec((tm, d), lambda i: (i, 0)),
                  pl.BlockSpec((1, d), lambda i: (0, 0)),
                  pl.BlockSpec((d, LANES), lambda i: (0, 0)),
                  pl.BlockSpec((d, LANES), lambda i: (0, 0)),
                  pl.BlockSpec((1, LANES), lambda i: (0, 0))],
        out_specs=[pl.BlockSpec((tm, s, LANES), lambda i: (i, 0, 0)),
                   pl.BlockSpec((tm, TOP_K), lambda i: (i, 0)),
                   pl.BlockSpec((tm, TOP_K), lambda i: (i, 0)),
                   pl.BlockSpec((tm, TOP_K), lambda i: (i, 0)),
                   pl.BlockSpec((1, LANES), lambda i: (0, 0))],
        out_shape=[jax.ShapeDtypeStruct((n, s, LANES), F32),
                   jax.ShapeDtypeStruct((n, TOP_K), I32),
                   jax.ShapeDtypeStruct((n, TOP_K), F32),
                   jax.ShapeDtypeStruct((n, TOP_K), I32),
                   jax.ShapeDtypeStruct((1, LANES), I32)],
        scratch_shapes=[pltpu.VMEM((1, LANES), F32)],
        compiler_params=_cparams(("arbitrary",), 32),
        name="moe_router",
    )(x2, g, rw_hi, rw_lo, rb_pad)


def _slab_pitch(s):
    return s if (s // 8) % 2 == 1 else s + 8


def _row_gather(src_hbm, idx_ref, idx_base, dst, sem, rows):
    slab = src_hbm.shape[1]

    def copy(r):
        return pltpu.make_async_copy(src_hbm.at[idx_ref[idx_base + r]], dst.at[r, pl.ds(0, slab)], sem)

    def start_one(r, carry):
        copy(r).start()
        return carry

    def wait_one(r, carry):
        copy(r).wait()
        return carry

    return (lambda: lax.fori_loop(0, rows, start_one, 0, unroll=8),
            lambda: lax.fori_loop(0, rows, wait_one, 0, unroll=8))


def _expert_ffn_kernel(be_ref, nu_ref, tok_ref, h_hbm, wg_ref, wu_ref, wd_ref, o_ref,
                       xg_s, x2_s, h_s, acc_s, sems):
    i, j = pl.program_id(0), pl.program_id(1)
    blk = xg_s.shape[1]
    n_used = nu_ref[0]
    last = pl.num_programs(1) - 1
    mid = pl.num_programs(1) // 2
    used = i < n_used
    has_next = i + 1 < n_used
    both = jnp.logical_and

    def gather(block):
        slot = block % 2
        return _row_gather(h_hbm, tok_ref, block * blk, xg_s.at[slot], sems.at[slot], blk)

    def unpack(block):
        slot = block % 2
        for s in range(h_hbm.shape[1]):
            x2_s[:, s * LANES:(s + 1) * LANES] = xg_s[slot, :, s, :]
        h_s[slot] = x2_s[...].astype(BF16)

    def partial_out():
        h = h_s[i % 2]
        gt = jnp.dot(h, wg_ref[...], preferred_element_type=F32)
        up = jnp.dot(h, wu_ref[...], preferred_element_type=F32)
        act = (gt * jax.nn.sigmoid(gt) * up).astype(BF16)
        return jnp.dot(act, wd_ref[...], preferred_element_type=F32)

    @pl.when(both(i == 0, j == 0))
    def _():
        start, wait = gather(0)
        start()
        wait()
        unpack(0)

    @pl.when(both(j == 0, has_next))
    def _():
        gather(i + 1)[0]()

    @pl.when(both(used, j == 0))
    def _():
        acc_s[...] = partial_out()

    @pl.when(both(used, both(both(j > 0, j < last), jnp.logical_or(j != mid, jnp.logical_not(has_next)))))
    def _():
        acc_s[...] += partial_out()

    @pl.when(both(both(used, has_next), j == mid))
    def _():
        gather(i + 1)[1]()
        unpack(i + 1)
        acc_s[...] += partial_out()

    @pl.when(both(used, j == last))
    def _():
        res = acc_s[...] + partial_out()
        for s in range(o_ref.shape[1]):
            o_ref[:, s, :] = res[:, s * LANES:(s + 1) * LANES]

    @pl.when(both(jnp.logical_not(used), j == last))
    def _():
        o_ref[...] = jnp.zeros_like(o_ref)


def _expert_ffn(h3, slot_tok, block_e, n_used, w_gate, w_up, w_down, blk, tf=512):
    ns = slot_tok.shape[0]
    s = h3.shape[1]
    d = s * LANES
    f = w_gate.shape[2]
    nj = f // tf
    assert nj >= 3

    def jj(i, j, nu):
        return jnp.where(i < nu[0], j, nj - 1)

    return pl.pallas_call(
        _expert_ffn_kernel,
        grid_spec=pltpu.PrefetchScalarGridSpec(
            num_scalar_prefetch=3,
            grid=(ns // blk, nj),
            in_specs=[pl.BlockSpec(memory_space=pl.ANY),
                      pl.BlockSpec((None, d, tf), lambda i, j, be, nu, tok: (be[i], 0, jj(i, j, nu))),
                      pl.BlockSpec((None, d, tf), lambda i, j, be, nu, tok: (be[i], 0, jj(i, j, nu))),
                      pl.BlockSpec((None, tf, d), lambda i, j, be, nu, tok: (be[i], jj(i, j, nu), 0))],
            out_specs=pl.BlockSpec((blk, s, LANES), lambda i, j, be, nu, tok: (i, 0, 0)),
            scratch_shapes=[pltpu.VMEM((2, blk, _slab_pitch(s), LANES), F32), pltpu.VMEM((blk, d), F32),
                            pltpu.VMEM((2, blk, d), BF16), pltpu.VMEM((blk, d), F32),
                            pltpu.SemaphoreType.DMA((2,))],
        ),
        out_shape=jax.ShapeDtypeStruct((ns, s, LANES), F32),
        compiler_params=_cparams(("arbitrary", "arbitrary"), 52),
        name="expert_swiglu",
    )(block_e, n_used, slot_tok, h3, w_gate, w_up, w_down)


def _combine_kernel(dest_ref, x_ref, gate_ref, y_hbm, g_ref, o_ref, yg_s, sems, *, final_norm):
    i, n_tiles = pl.program_id(0), pl.num_programs(0)
    tm = x_ref.shape[0]

    def gather(tile):
        slot = tile % 2
        parts = [_row_gather(y_hbm, dest_ref, k * n_tiles * tm + tile * tm, yg_s.at[slot, k], sems.at[slot], tm)
                 for k in range(TOP_K)]
        return (lambda: [p[0]() for p in parts]), (lambda: [p[1]() for p in parts])

    @pl.when(i == 0)
    def _():
        gather(0)[0]()

    @pl.when(i + 1 < n_tiles)
    def _():
        gather(i + 1)[0]()

    gather(i)[1]()
    slot = i % 2
    gates = [jnp.broadcast_to(gate_ref[:, k:k + 1], (tm, LANES)) for k in range(TOP_K)]
    for s in range(y_hbm.shape[1]):
        cols = slice(s * LANES, (s + 1) * LANES)
        acc = x_ref[:, cols]
        for k in range(TOP_K):
            acc = acc + gates[k] * yg_s[slot, k, :, s, :]
        o_ref[:, cols] = acc
    if final_norm:
        o_ref[...] = _rms(o_ref[...], g_ref[...])


def _combine(x2, gate, yb, dest_by_k, g, final_norm, tm=512):
    n, d = x2.shape
    s = d // LANES
    return pl.pallas_call(
        functools.partial(_combine_kernel, final_norm=final_norm),
        grid_spec=pltpu.PrefetchScalarGridSpec(
            num_scalar_prefetch=1,
            grid=(n // tm,),
            in_specs=[pl.BlockSpec((tm, d), lambda i, dest: (i, 0)),
                      pl.BlockSpec((tm, TOP_K), lambda i, dest: (i, 0)),
                      pl.BlockSpec(memory_space=pl.ANY),
                      pl.BlockSpec((1, d), lambda i, dest: (0, 0))],
            out_specs=pl.BlockSpec((tm, d), lambda i, dest: (i, 0)),
            scratch_shapes=[pltpu.VMEM((2, TOP_K, tm, _slab_pitch(s), LANES), F32),
                            pltpu.SemaphoreType.DMA((2,))],
        ),
        out_shape=jax.ShapeDtypeStruct((n, d), F32),
        compiler_params=_cparams(("arbitrary",), 48),
        name="moe_combine",
    )(dest_by_k, x2, gate, yb, g)


def _final_norm_kernel(x_ref, g_ref, o_ref):
    o_ref[...] = _rms(x_ref[...], g_ref[...])


def _final_norm(x2, g, tm=512):
    n, d = x2.shape
    return pl.pallas_call(
        _final_norm_kernel,
        grid=(n // tm,),
        in_specs=[pl.BlockSpec((tm, d), lambda i: (i, 0)), pl.BlockSpec((1, d), lambda i: (0, 0))],
        out_specs=pl.BlockSpec((tm, d), lambda i: (i, 0)),
        out_shape=jax.ShapeDtypeStruct((n, d), F32),
        compiler_params=_cparams(("parallel",), 32),
        name="final_norm",
    )(x2, g)


def _routing_tables(top_e, rank, counts, blk):
    n = top_e.shape[0]
    nk = n * TOP_K
    flat_e = top_e.reshape(-1)
    padded = (counts + blk - 1) // blk * blk
    p_end = jnp.cumsum(padded)
    p_start = p_end - padded
    dest = (p_start[flat_e] + rank.reshape(-1)).astype(I32)
    n_blocks = -(-nk // blk) + N_EXPERTS
    flat_tok = jnp.arange(nk, dtype=I32) // TOP_K
    slot_tok = jnp.zeros((n_blocks * blk,), I32).at[dest].set(flat_tok)
    n_used = (p_end[-1] // blk).astype(I32)
    blocks = jnp.arange(n_blocks, dtype=I32)
    block_e = jnp.sum((p_end[None, :] <= (blocks * blk)[:, None]).astype(I32), axis=1)
    block_e = jnp.minimum(block_e, N_EXPERTS - 1)
    block_e = jnp.where(blocks < n_used, block_e, block_e[n_used - 1])
    dest_by_k = dest.reshape(n, TOP_K).T.reshape(-1)
    return slot_tok, block_e, n_used.reshape(1), dest_by_k


def _moe_layer(x2, norm_g, router_w, router_b, w_gate, w_up, w_down, out_g, final_norm):
    n, d = x2.shape
    rw = jnp.zeros((d, LANES), F32).at[:, :N_EXPERTS].set(router_w)
    rb = jnp.zeros((1, LANES), F32).at[0, :N_EXPERTS].set(router_b)
    h, top_e, gate, rank, counts = _router(x2, norm_g[None, :], rw, rb)
    slot_tok, block_e, n_used, dest_by_k = _routing_tables(top_e, rank, counts[0, :N_EXPERTS], MOE_BLOCK)
    yb = _expert_ffn(h, slot_tok, block_e, n_used, w_gate.astype(BF16), w_up.astype(BF16),
                     w_down.astype(BF16), MOE_BLOCK)
    return _combine(x2, gate, yb, dest_by_k, out_g[None, :], final_norm)


def _mixer_layer(x2, bsz, length, norm_g, w_in, b_in, hy_conv_w, hy_conv_b, hy_w1, hy_b1, hy_w2, hy_b2,
                 hy_w3, hy_freq, hy_decay, hy_skip, ml_conv_w, ml_conv_b, ml_norm_g, w_a, w_b, w_o,
                 ctab, stab, twiddle, zpos):
    n, d = x2.shape
    wh = hy_skip.shape[1]
    wm = ml_norm_g.shape[0]
    off_qk = (HYENA_ORDER + 1) * wh
    off_v = off_qk + 2 * wm
    off_o = off_v + wm
    off_gates = off_o + wm
    off_br = off_gates + 4 * MLSTM_HEADS
    w_cat = jnp.concatenate([w_in[:, off_br:], w_in[:, :off_gates]], axis=1).astype(BF16)
    b_cat = jnp.concatenate([b_in[off_br:], b_in[:off_gates]])[None, :]
    col_hy = 2 * d
    col_q = col_hy + off_qk
    col_k = col_q + wm
    col_v = col_hy + off_v
    col_o = col_hy + off_o

    p_big, gates = _in_proj(x2, norm_g[None, :], w_in[:, off_gates:off_br], b_in[None, off_gates:off_br],
                            w_cat, b_cat)
    p3 = p_big.reshape(bsz, length, p_big.shape[1])

    kp = LANES
    pad2 = lambda a, r, c: jnp.zeros((r, c), F32).at[:a.shape[0], :a.shape[1]].set(a)
    zp = pad2(zpos, length, kp)
    htab, hmid = _hyena_spectrum(
        zp, pad2(hy_w1, kp, kp), pad2(hy_b1[None, :], 1, kp), pad2(hy_w2, kp, kp), pad2(hy_b2[None, :], 1, kp),
        pad2(hy_freq[None, :], 1, kp), pad2(hy_w3, kp, hy_w3.shape[1]), hy_decay[None, :],
        ctab, stab, twiddle, wh, HYENA_CT)
    y_hy = _hyena_conv(p3, col_hy, hy_conv_w, hy_conv_b[None, :], htab, hmid, hy_skip[:, None, :],
                       ctab, stab, wh, HYENA_CT, HYENA_RC)

    gcol, grow = _gate_prep(gates.reshape(bsz, length, 4 * MLSTM_HEADS), MLSTM_CHUNK)
    y_ml = _mlstm(p3, col_q, col_k, col_v, col_o, ml_conv_w, ml_conv_b[None, :], ml_norm_g[None, :],
                  gcol, grow, MLSTM_CHUNK)

    return _merge(y_hy.reshape(n, wh), y_ml.reshape(n, wm), p_big, x2,
                  w_a.astype(BF16), w_b.astype(BF16), w_o.astype(BF16))


def kernel(x, mix_norm_g, mix_w_in, mix_b_in, hy_conv_w, hy_conv_b, hy_filt_w1, hy_filt_b1, hy_filt_w2,
           hy_filt_b2, hy_filt_w3, hy_filt_freq, hy_filt_decay, hy_skip, ml_conv_w, ml_conv_b, ml_norm_g,
           mix_w_a, mix_w_b, mix_w_o, ffn_norm_g, dense_w_gate, dense_w_up, dense_w_down, moe_router_w,
           moe_router_b, moe_w_gate, moe_w_up, moe_w_down, final_norm_g):
    bsz, length, d = x.shape
    depth = mix_norm_g.shape[0]
    x2 = x.reshape(bsz * length, d)
    ctab, stab, twiddle = _dft_tables(length // 2)
    zpos = _hyena_positions(length)
    normed = False
    for layer in range(depth):
        x2 = _mixer_layer(x2, bsz, length, mix_norm_g[layer], mix_w_in[layer], mix_b_in[layer],
                          hy_conv_w[layer], hy_conv_b[layer], hy_filt_w1[layer], hy_filt_b1[layer],
                          hy_filt_w2[layer], hy_filt_b2[layer], hy_filt_w3[layer], hy_filt_freq[layer],
                          hy_filt_decay[layer], hy_skip[layer], ml_conv_w[layer], ml_conv_b[layer],
                          ml_norm_g[layer], mix_w_a[layer], mix_w_b[layer], mix_w_o[layer], ctab, stab, twiddle,
                          zpos)
        j = layer // 2
        if layer % 2 == 0:
            x2 = _dense_ffn(x2, ffn_norm_g[layer][None, :], dense_w_gate[j].astype(BF16),
                            dense_w_up[j].astype(BF16), dense_w_down[j].astype(BF16))
        else:
            normed = layer == depth - 1
            x2 = _moe_layer(x2, ffn_norm_g[layer], moe_router_w[j], moe_router_b[j], moe_w_gate[j],
                            moe_w_up[j], moe_w_down[j], final_norm_g, normed)
    if not normed:
        x2 = _final_norm(x2, final_norm_g[None, :])
    return x2.reshape(bsz, length, d)
```

```python
import functools
import math

import jax
import jax.numpy as jnp
import numpy as np
from jax import lax
from jax.experimental import pallas as pl
from jax.experimental.pallas import tpu as pltpu

F32, BF16, I32 = jnp.float32, jnp.bfloat16, jnp.int32
HIGHEST = lax.Precision.HIGHEST

HYENA_ORDER = 2
HYENA_POS_BANDS = 16
MLSTM_HEADS = 8
N_EXPERTS = 8
TOP_K = 2
EPS = 1e-6

V7X_VMEM_BYTES = 64 * 1024 * 1024
LANES = 128
MXU_DIM = 256

MLSTM_CHUNK = 256
HYENA_CT = MXU_DIM
HYENA_RC = 512
MOE_BLOCK = 512


def _cparams(semantics, vmem_mib):
    assert vmem_mib * 1024 * 1024 <= V7X_VMEM_BYTES
    return pltpu.CompilerParams(dimension_semantics=semantics, vmem_limit_bytes=vmem_mib * 1024 * 1024)


def _const_spec(shape):
    nd = len(shape)
    return pl.BlockSpec(shape, lambda *_: (0,) * nd, pipeline_mode=pl.Buffered(1))


def _rms(x, g):
    return x * lax.rsqrt(jnp.mean(x * x, axis=-1, keepdims=True) + EPS) * g


def _shift_rows(u, direction):
    n = u.shape[0]
    row = lax.broadcasted_iota(I32, u.shape, 0)
    if direction > 0:
        return jnp.where(row == 0, 0.0, pltpu.roll(u, 1, axis=0))
    return jnp.where(row == n - 1, 0.0, pltpu.roll(u, n - 1, axis=0))


def _short_conv(u, w, b):
    return b + _shift_rows(u, 1) * w[0:1] + u * w[1:2] + _shift_rows(u, -1) * w[2:3]


def _log_sigmoid(x):
    return jnp.minimum(x, 0.0) - jnp.log1p(jnp.exp(-jnp.abs(x)))


def _dot_split(a, w_hi, w_lo):
    a_hi = a.astype(BF16)
    a_lo = (a - a_hi.astype(F32)).astype(BF16)
    return (jnp.dot(a_hi, w_hi, preferred_element_type=F32) + jnp.dot(a_lo, w_hi, preferred_element_type=F32)
            + jnp.dot(a_hi, w_lo, preferred_element_type=F32))


def _split_bf16(w):
    w_hi = w.astype(BF16)
    return w_hi, (w - w_hi.astype(F32)).astype(BF16)


def _in_proj_kernel(x_ref, g_ref, wg_ref, bg_ref, w_ref, b_ref, o_ref, gates_ref, h_s):
    @pl.when(pl.program_id(1) == 0)
    def _():
        h_s[...] = _rms(x_ref[...], g_ref[...]).astype(BF16)
        gates_ref[...] = jnp.dot(h_s[...], wg_ref[...], preferred_element_type=F32) + bg_ref[...]

    acc = jnp.dot(h_s[...], w_ref[...], preferred_element_type=F32) + b_ref[...]
    o_ref[...] = acc.astype(o_ref.dtype)


def _in_proj(x2, g, w_gates, b_gates, w, b, tm=1024, tn=1024):
    n, d = x2.shape
    nc = w.shape[1]
    ng = w_gates.shape[1]
    return pl.pallas_call(
        _in_proj_kernel,
        grid=(n // tm, nc // tn),
        in_specs=[pl.BlockSpec((tm, d), lambda i, j: (i, 0)),
                  pl.BlockSpec((1, d), lambda i, j: (0, 0)),
                  pl.BlockSpec((d, ng), lambda i, j: (0, 0)),
                  pl.BlockSpec((1, ng), lambda i, j: (0, 0)),
                  pl.BlockSpec((d, tn), lambda i, j: (0, j)),
                  pl.BlockSpec((1, tn), lambda i, j: (0, j))],
        out_specs=[pl.BlockSpec((tm, tn), lambda i, j: (i, j)),
                   pl.BlockSpec((tm, ng), lambda i, j: (i, 0))],
        out_shape=[jax.ShapeDtypeStruct((n, nc), BF16), jax.ShapeDtypeStruct((n, ng), F32)],
        scratch_shapes=[pltpu.VMEM((tm, d), BF16)],
        compiler_params=_cparams(("parallel", "arbitrary"), 48),
        name="in_proj",
    )(x2, g, w_gates.astype(BF16), b_gates, w, b)


def _dft_tables(half):
    k = jnp.arange(half, dtype=I32)
    ang = ((k[:, None] * k[None, :]) % (2 * half)).astype(F32) * (math.pi / half)
    tw = k.astype(F32)[:, None] * (math.pi / (2 * half))
    twiddle = jnp.stack([jnp.broadcast_to(jnp.cos(tw), (half, LANES)), jnp.broadcast_to(jnp.sin(tw), (half, LANES))])
    return jnp.cos(ang).astype(BF16), jnp.sin(ang).astype(BF16), twiddle


def _hyena_positions(length):
    t = np.linspace(0.0, 1.0, length)[:, None]
    n = np.arange(length, dtype=np.float64)[:, None]
    bands = np.linspace(1e-4, HYENA_POS_BANDS - 1, HYENA_POS_BANDS)[None, :]
    ang = (2.0 * np.pi / length) * n * bands
    return jnp.asarray(np.concatenate([t, np.cos(ang), -np.sin(ang)], axis=-1), dtype=F32)


def _split_rows(val, slab_s):
    half = val.shape[0] // 2
    n_slabs = slab_s.shape[0]
    for s in range(n_slabs):
        slab_s[s] = val[:, s * LANES:(s + 1) * LANES]
    pick = lambda p: jnp.concatenate([slab_s[s, pl.ds(p, half, stride=2), :] for s in range(n_slabs)], axis=1)
    return pick(0), pick(1)


def _merge_rows(even, odd, slab_s):
    half = even.shape[0]
    n_slabs = slab_s.shape[0]
    for s in range(n_slabs):
        slab_s[s, pl.ds(0, half, stride=2), :] = even[:, s * LANES:(s + 1) * LANES]
        slab_s[s, pl.ds(1, half, stride=2), :] = odd[:, s * LANES:(s + 1) * LANES]
    return jnp.concatenate([slab_s[s] for s in range(n_slabs)], axis=1)


def _alt_sum(x):
    row = lax.broadcasted_iota(I32, (x.shape[0], 1), 0)
    return jnp.sum(jnp.where(row % 2 == 0, x, -x), axis=0, keepdims=True)


def _hyena_spectrum_kernel(z_ref, w1_ref, b1_ref, w2_ref, b2_ref, fr_ref, w3f_ref, w3b_ref,
                           decf_ref, decb_ref, c_ref, s_ref, tw_ref, tab_ref, mid_ref, slab_s, hid_s):
    half = c_ref.shape[0]
    length = 2 * half
    ct = w3f_ref.shape[1]
    dot_hi = functools.partial(jnp.dot, precision=HIGHEST, preferred_element_type=F32)
    dot = functools.partial(jnp.dot, preferred_element_type=F32)
    z = z_ref[...]

    @pl.when(jnp.logical_and(pl.program_id(0) == 0, pl.program_id(1) == 0))
    def _():
        fr = fr_ref[...]
        hid = jnp.sin(fr * (dot_hi(z, w1_ref[...]) + b1_ref[...]))
        hid_s[...] = jnp.sin(fr * (dot_hi(hid, w2_ref[...]) + b2_ref[...]))

    hid = hid_s[...]
    t = z[:, 0:1]
    ff = dot_hi(hid, w3f_ref[...]) * jnp.exp(-t * jnp.abs(decf_ref[...]))
    fb = dot_hi(hid, w3b_ref[...]) * jnp.exp(-t * jnp.abs(decb_ref[...]))
    row = lax.broadcasted_iota(I32, (length, 1), 0)
    l1 = jnp.sum(jnp.where(row == 0, jnp.abs(ff + fb), jnp.abs(ff) + jnp.abs(fb)), axis=0, keepdims=True)
    inv = 1.0 / l1
    se, so = _split_rows((ff + fb) * inv, slab_s)
    de, do = _split_rows((ff - fb) * inv, slab_s)
    wide = lambda a: jnp.concatenate([a] * (ct // LANES), axis=1)
    cw, sw = wide(tw_ref[0]), wide(tw_ref[1])
    c_t, s_t = c_ref[...], s_ref[...]
    bf = lambda a: a.astype(BF16)
    e_r = dot(c_t, bf(se))
    o_r, o_s = dot(c_t, bf(so)), dot(s_t, bf(so))
    t_r = cw * o_r - sw * o_s
    d_s = dot(s_t, bf(de))
    p_r, p_s = dot(c_t, bf(do)), dot(s_t, bf(do))
    t_i = -(cw * p_s + sw * p_r)
    krow = lax.broadcasted_iota(I32, (half, 1), 0)
    scale = jnp.where(krow == 0, 1.0 / (2 * length), 1.0 / length)
    lo_r, lo_i = (e_r + t_r) * scale, (t_i - d_s) * scale
    hi_r, hi_i = (e_r - t_r) * scale, (t_i + d_s) * scale

    def emit(p, lr, li, hr, hi):
        a_r, a_i = lr + hr, li - hi
        d_r, d_i = lr - hr, li + hi
        tab_ref[4 * p + 0] = a_r
        tab_ref[4 * p + 1] = a_i
        tab_ref[4 * p + 2] = d_r * cw + d_i * sw
        tab_ref[4 * p + 3] = d_i * cw - d_r * sw

    emit(0, lo_r, lo_i, hi_r, hi_i)
    emit(1, lo_r * cw - lo_i * sw, lo_r * sw + lo_i * cw, -(hi_r * cw + hi_i * sw), -(hi_i * cw - hi_r * sw))
    mid_ref[0:1, :] = _alt_sum(se) * (1.0 / length)
    mid_ref[1:2, :] = -_alt_sum(do) * (1.0 / length)


def _hyena_spectrum(z, w1, b1, w2, b2, freq, w3, decay, ctab, stab, twiddle, width, ct):
    half = ctab.shape[0]
    kp = w1.shape[1]
    nct = width // ct
    col_f = lambda o, c: (0, o * 2 * nct + c)
    col_b = lambda o, c: (0, o * 2 * nct + nct + c)
    return pl.pallas_call(
        _hyena_spectrum_kernel,
        grid=(HYENA_ORDER, nct),
        in_specs=[_const_spec(z.shape), _const_spec(w1.shape), _const_spec(b1.shape),
                  _const_spec(w2.shape), _const_spec(b2.shape), _const_spec(freq.shape),
                  pl.BlockSpec((kp, ct), col_f), pl.BlockSpec((kp, ct), col_b),
                  pl.BlockSpec((1, ct), col_f), pl.BlockSpec((1, ct), col_b),
                  _const_spec(ctab.shape), _const_spec(stab.shape), _const_spec(twiddle.shape)],
        out_specs=[pl.BlockSpec((None, 8, half, ct), lambda o, c: (o, 0, 0, c)),
                   pl.BlockSpec((None, 2, ct), lambda o, c: (o, 0, c))],
        out_shape=[jax.ShapeDtypeStruct((HYENA_ORDER, 8, half, width), F32),
                   jax.ShapeDtypeStruct((HYENA_ORDER, 2, width), F32)],
        scratch_shapes=[pltpu.VMEM((ct // LANES, 2 * half, LANES), F32), pltpu.VMEM((2 * half, kp), F32)],
        compiler_params=_cparams(("arbitrary", "arbitrary"), 48),
        name="hyena_spectrum",
    )(z, w1, b1, w2, b2, freq, w3, w3, decay, decay, ctab, stab, twiddle)


def _hyena_conv_kernel(pv_ref, p1_ref, p2_ref, wv_ref, w1_ref, w2_ref, bv_ref, b1_ref, b2_ref,
                       tab_ref, mid_ref, skip_ref, c_ref, s_ref, y_ref,
                       slab_s, z_s, g1_s, g2_s, zb_s, gr_s, gn_s, *, rc):
    half = c_ref.shape[0]
    dot = functools.partial(jnp.dot, preferred_element_type=F32)
    for dst, p_ref, w_ref, b_ref in ((z_s, pv_ref, wv_ref, bv_ref), (g1_s, p1_ref, w1_ref, b1_ref),
                                     (g2_s, p2_ref, w2_ref, b2_ref)):
        even, odd = _split_rows(_short_conv(p_ref[...].astype(F32), w_ref[...], b_ref[...]), slab_s)
        dst[0] = even
        dst[1] = odd
    gates = (g1_s, g2_s)
    row = lax.broadcasted_iota(I32, (half, 1), 0)
    sign = jnp.where(row % 2 == 0, 1.0, -1.0)
    for o in range(HYENA_ORDER):
        zb_s[0] = z_s[0].astype(BF16)
        zb_s[1] = z_s[1].astype(BF16)
        a_e, a_o = _alt_sum(z_s[0]), _alt_sum(z_s[1])
        mid = (mid_ref[o, 0:1, :] * a_e + mid_ref[o, 1:2, :] * a_o,
               mid_ref[o, 0:1, :] * a_o - mid_ref[o, 1:2, :] * a_e)
        for r in range(half // rc):
            rows = slice(r * rc, (r + 1) * rc)
            e_r, e_s = dot(c_ref[rows, :], zb_s[0]), dot(s_ref[rows, :], zb_s[0])
            o_r, o_s = dot(c_ref[rows, :], zb_s[1]), dot(s_ref[rows, :], zb_s[1])
            for p in range(2):
                a_r, a_i = tab_ref[o, 4 * p + 0, rows, :], tab_ref[o, 4 * p + 1, rows, :]
                b_r, b_i = tab_ref[o, 4 * p + 2, rows, :], tab_ref[o, 4 * p + 3, rows, :]
                gr_s[p, rows, :] = (a_r * e_r + a_i * e_s + b_r * o_r + b_i * o_s).astype(BF16)
                gn_s[p, rows, :] = (a_r * e_s - a_i * e_r + b_r * o_s - b_i * o_r).astype(BF16)
        for p in range(2):
            for r in range(half // rc):
                rows = slice(r * rc, (r + 1) * rc)
                conv = dot(c_ref[rows, :], gr_s[p]) + dot(s_ref[rows, :], gn_s[p]) + sign[rows] * mid[p]
                z_s[p, rows, :] = gates[o][p, rows, :] * (conv + skip_ref[o] * z_s[p, rows, :])
    y_ref[...] = _merge_rows(z_s[0], z_s[1], slab_s).astype(y_ref.dtype)


def _hyena_conv(p3, col0, conv_w, conv_b, tab, mid, skip, ctab, stab, width, ct, rc):
    bsz, length, _ = p3.shape
    half = length // 2
    nct = width // ct
    c0 = col0 // ct

    def pspec(part):
        return pl.BlockSpec((None, length, ct), lambda c, b: (b, 0, c0 + part * nct + c))

    def wspec(rows, part):
        return pl.BlockSpec((rows, ct), lambda c, b: (0, part * nct + c))

    scratch = ([pltpu.VMEM((ct // LANES, length, LANES), F32)] + [pltpu.VMEM((2, half, ct), F32)] * 3
               + [pltpu.VMEM((2, half, ct), BF16)] * 3)
    return pl.pallas_call(
        functools.partial(_hyena_conv_kernel, rc=rc),
        grid=(nct, bsz),
        in_specs=[pspec(0), pspec(1), pspec(2),
                  wspec(3, 0), wspec(3, 1), wspec(3, 2), wspec(1, 0), wspec(1, 1), wspec(1, 2),
                  pl.BlockSpec((HYENA_ORDER, 8, half, ct), lambda c, b: (0, 0, 0, c),
                               pipeline_mode=pl.Buffered(1)),
                  pl.BlockSpec((HYENA_ORDER, 2, ct), lambda c, b: (0, 0, c)),
                  pl.BlockSpec((HYENA_ORDER, 1, ct), lambda c, b: (0, 0, c)),
                  _const_spec(ctab.shape), _const_spec(stab.shape)],
        out_specs=pl.BlockSpec((None, length, ct), lambda c, b: (b, 0, c)),
        out_shape=jax.ShapeDtypeStruct((bsz, length, width), BF16),
        scratch_shapes=scratch,
        compiler_params=_cparams(("parallel", "parallel"), 56),
        name="hyena_conv",
    )(p3, p3, p3, conv_w, conv_w, conv_w, conv_b, conv_b, conv_b, tab, mid, skip, ctab, stab)


def _gate_prep_kernel(ic_ref, fc_ref, ir_ref, fr_ref, col_ref, row_ref, *, chunk):
    length, nch = ic_ref.shape
    nh = nch // 2
    r = lax.broadcasted_iota(I32, (chunk, chunk), 0)
    c = lax.broadcasted_iota(I32, (chunk, chunk), 1)
    lower = (c <= r).astype(F32)
    upper = (c >= r).astype(F32)
    dot_hi = functools.partial(jnp.dot, precision=HIGHEST, preferred_element_type=F32)
    fwd_col = lax.broadcasted_iota(I32, (chunk, nch), 1) < nh
    fwd_row = lax.broadcasted_iota(I32, (nch, chunk), 0) < nh
    pos = lax.broadcasted_iota(I32, (chunk, nch), 0)
    for ch in range(length // chunk):
        rows = slice(ch * chunk, (ch + 1) * chunk)
        lf = _log_sigmoid(fc_ref[rows, :])
        b = jnp.where(fwd_col, dot_hi(lower, lf), dot_hi(upper, lf))
        u = ic_ref[rows, :] - b
        cm_f, cm_b = u, u
        shift = 1
        while shift < chunk:
            cm_f = jnp.maximum(cm_f, jnp.where(pos >= shift, pltpu.roll(cm_f, shift, axis=0), -jnp.inf))
            cm_b = jnp.maximum(cm_b, jnp.where(pos + shift < chunk, pltpu.roll(cm_b, chunk - shift, axis=0), -jnp.inf))
            shift *= 2
        col_ref[0, rows, :] = b
        col_ref[1, rows, :] = u
        col_ref[2, rows, :] = jnp.where(fwd_col, cm_f, cm_b)
        lfr = _log_sigmoid(fr_ref[:, rows])
        b_r = jnp.where(fwd_row, dot_hi(lfr, upper), dot_hi(lfr, lower))
        row_ref[:, rows] = ir_ref[:, rows] - b_r


def _gate_prep(gates3, chunk):
    bsz, length, nch4 = gates3.shape
    nh = nch4 // 4
    i_col = jnp.concatenate([gates3[..., :nh], gates3[..., 2 * nh:3 * nh]], axis=-1)
    f_col = jnp.concatenate([gates3[..., nh:2 * nh], gates3[..., 3 * nh:]], axis=-1)
    nch = 2 * nh
    cspec = pl.BlockSpec((None, length, nch), lambda b: (b, 0, 0))
    rspec = pl.BlockSpec((None, nch, length), lambda b: (b, 0, 0))
    col, row = pl.pallas_call(
        functools.partial(_gate_prep_kernel, chunk=chunk),
        grid=(bsz,),
        in_specs=[cspec, cspec, rspec, rspec],
        out_specs=[pl.BlockSpec((None, 3, length, nch), lambda b: (b, 0, 0, 0)), rspec],
        out_shape=[jax.ShapeDtypeStruct((bsz, 3, length, nch), F32),
                   jax.ShapeDtypeStruct((bsz, nch, length), F32)],
        compiler_params=_cparams(("parallel",), 32),
        name="mlstm_gate_prep",
    )(i_col, f_col, jnp.swapaxes(i_col, 1, 2), jnp.swapaxes(f_col, 1, 2))
    col = col.reshape(bsz, 3, length, 2, nh).transpose(0, 4, 2, 3, 1).reshape(bsz, nh, length, 6)
    row = row.reshape(bsz, 2, nh, length).transpose(0, 2, 1, 3)
    return col, row


def _mlstm_kernel(pq_ref, pk_ref, pv_ref, po_ref, cwq_ref, cwk_ref, cbq_ref, cbk_ref, ng_ref,
                  gcol_ref, grow_ref, y_ref, q_s, k_s, v1_s, tab_s, hf_s, hb_s, *, chunk):
    length, dk = pq_ref.shape
    nc = length // chunk
    rep = chunk // LANES

    def conv_silu(p_ref, w_ref, b_ref):
        c = _short_conv(p_ref[...].astype(F32), w_ref[...], b_ref[...])
        return c * jax.nn.sigmoid(c)

    q_s[...] = (conv_silu(pq_ref, cwq_ref, cbq_ref) * (dk ** -0.5)).astype(BF16)
    k_s[...] = conv_silu(pk_ref, cwk_ref, cbk_ref).astype(BF16)
    v1_s[:, :dk] = pv_ref[...]
    v1_s[:, dk:] = jnp.ones((length, dk), BF16)
    for t in range(tab_s.shape[0]):
        tab_s[t] = jnp.broadcast_to(gcol_ref[:, t:t + 1], (length, LANES))

    row_i = lax.broadcasted_iota(I32, (chunk, chunk), 0)
    col_i = lax.broadcasted_iota(I32, (chunk, chunk), 1)
    wide = lambda a: jnp.concatenate([a] * rep, axis=1)
    both = lambda a: jnp.concatenate([a, a], axis=1)

    def step(c, reverse, state, m):
        d = 1 if reverse else 0
        rows = slice(c * chunk, (c + 1) * chunk)
        edge = c * chunk if reverse else (c + 1) * chunk - 1
        q, k, v1 = q_s[rows, :], k_s[rows, :], v1_s[rows, :]
        b_t, u_t, cm_t = tab_s[3 * d, rows, :], tab_s[3 * d + 1, rows, :], tab_s[3 * d + 2, rows, :]
        u_row = grow_ref[d:d + 1, rows]
        g = tab_s[3 * d, edge:edge + 1, :]
        u_max = tab_s[3 * d + 2, edge:edge + 1, :]
        mask = (col_i >= row_i) if reverse else (col_i <= row_i)
        mm = jnp.maximum(cm_t, m)
        p = jnp.exp(jnp.where(mask, u_row - wide(mm), -jnp.inf))
        qk = lax.dot_general(q, k, (((1,), (1,)), ((), ())), preferred_element_type=F32)
        intra = jnp.dot((qk * p).astype(BF16), v1, preferred_element_type=F32)
        inter = jnp.dot(q, state.astype(BF16), preferred_element_type=F32)
        nd = intra + both(jnp.exp(m - mm)) * inter
        h = nd[:, :dk] / jnp.maximum(jnp.abs(nd[:, dk:]), jnp.exp(-(b_t + mm)))
        kw = (k.astype(F32) * jnp.exp(u_t - u_max)).astype(BF16)
        upd = lax.dot_general(kw, v1, (((0,), (0,)), ((), ())), preferred_element_type=F32)
        m_loc = g + u_max
        m_new = jnp.maximum(g + m, m_loc)
        state = both(jnp.exp(g + m - m_new)) * state + both(jnp.exp(m_loc - m_new)) * upd
        return h, state, m_new

    zero = (jnp.zeros((dk, 2 * dk), F32), jnp.zeros((1, LANES), F32))
    st_f, m_f = zero
    st_b, m_b = zero
    for c in range(nc):
        h, st_f, m_f = step(c, False, st_f, m_f)
        hf_s[c * chunk:(c + 1) * chunk, :] = h
        cb = nc - 1 - c
        h, st_b, m_b = step(cb, True, st_b, m_b)
        hb_s[cb * chunk:(cb + 1) * chunk, :] = h

    ht = hf_s[...] + hb_s[...]
    y_ref[...] = (_rms(ht, ng_ref[...]) * jax.nn.sigmoid(po_ref[...].astype(F32))).astype(y_ref.dtype)


def _mlstm(p3, col_q, col_k, col_v, col_o, conv_w, conv_b, norm_g, gcol, grow, chunk):
    bsz, length, _ = p3.shape
    nh, dh = MLSTM_HEADS, norm_g.shape[1] // MLSTM_HEADS
    assert dh == LANES and chunk % LANES == 0

    def pspec(col):
        return pl.BlockSpec((None, length, dh), lambda b, h: (b, 0, col // dh + h))

    def wspec(rows, part):
        return pl.BlockSpec((rows, dh), lambda b, h: (0, part * nh + h))

    return pl.pallas_call(
        functools.partial(_mlstm_kernel, chunk=chunk),
        grid=(bsz, nh),
        in_specs=[pspec(col_q), pspec(col_k), pspec(col_v), pspec(col_o),
                  wspec(3, 0), wspec(3, 1), wspec(1, 0), wspec(1, 1),
                  pl.BlockSpec((1, dh), lambda b, h: (0, h)),
                  pl.BlockSpec((None, None, length, 6), lambda b, h: (b, h, 0, 0)),
                  pl.BlockSpec((None, None, 2, length), lambda b, h: (b, h, 0, 0))],
        out_specs=pl.BlockSpec((None, length, dh), lambda b, h: (b, 0, h)),
        out_shape=jax.ShapeDtypeStruct((bsz, length, nh * dh), BF16),
        scratch_shapes=[pltpu.VMEM((length, dh), BF16), pltpu.VMEM((length, dh), BF16),
                        pltpu.VMEM((length, 2 * dh), BF16), pltpu.VMEM((6, length, LANES), F32),
                        pltpu.VMEM((length, dh), F32), pltpu.VMEM((length, dh), F32)],
        compiler_params=_cparams(("parallel", "parallel"), 40),
        name="mlstm",
    )(p3, p3, p3, p3, conv_w, conv_w, conv_b, conv_b, norm_g, gcol, grow)


def _merge_kernel(yh_ref, ym_ref, gh_ref, gm_ref, x_ref, wa_ref, wb_ref, wo_ref, o_ref):
    a = jnp.dot(yh_ref[...], wa_ref[...], preferred_element_type=F32)
    b = jnp.dot(ym_ref[...], wb_ref[...], preferred_element_type=F32)
    t = jax.nn.sigmoid(gh_ref[...].astype(F32)) * a + jax.nn.sigmoid(gm_ref[...].astype(F32)) * b
    o_ref[...] = x_ref[...] + jnp.dot(t.astype(BF16), wo_ref[...], preferred_element_type=F32)


def _merge(y_hy, y_ml, p_big, x2, w_a, w_b, w_o, tm=256):
    n, d = x2.shape
    wh, wm = y_hy.shape[1], y_ml.shape[1]
    return pl.pallas_call(
        _merge_kernel,
        grid=(n // tm,),
        in_specs=[pl.BlockSpec((tm, wh), lambda i: (i, 0)),
                  pl.BlockSpec((tm, wm), lambda i: (i, 0)),
                  pl.BlockSpec((tm, d), lambda i: (i, 0)),
                  pl.BlockSpec((tm, d), lambda i: (i, 1)),
                  pl.BlockSpec((tm, d), lambda i: (i, 0)),
                  _const_spec(w_a.shape), _const_spec(w_b.shape), _const_spec(w_o.shape)],
        out_specs=pl.BlockSpec((tm, d), lambda i: (i, 0)),
        out_shape=jax.ShapeDtypeStruct((n, d), F32),
        compiler_params=_cparams(("parallel",), 48),
        name="branch_merge",
    )(y_hy, y_ml, p_big, p_big, x2, w_a, w_b, w_o)


def _dense_ffn_kernel(x_ref, g_ref, wg_ref, wu_ref, wd_ref, o_ref, h_s):
    @pl.when(pl.program_id(1) == 0)
    def _():
        x = x_ref[...]
        h_s[...] = _rms(x, g_ref[...]).astype(BF16)
        o_ref[...] = x

    h = h_s[...]
    gt = jnp.dot(h, wg_ref[...].astype(BF16), preferred_element_type=F32)
    up = jnp.dot(h, wu_ref[...].astype(BF16), preferred_element_type=F32)
    act = (gt * jax.nn.sigmoid(gt) * up).astype(BF16)
    o_ref[...] += jnp.dot(act, wd_ref[...].astype(BF16), preferred_element_type=F32)


def _dense_ffn(x2, g, w_gate, w_up, w_down, tm=1024, tf=256):
    n, d = x2.shape
    f = w_gate.shape[1]
    return pl.pallas_call(
        _dense_ffn_kernel,
        grid=(n // tm, f // tf),
        in_specs=[pl.BlockSpec((tm, d), lambda i, j: (i, 0)),
                  pl.BlockSpec((1, d), lambda i, j: (0, 0)),
                  pl.BlockSpec((d, tf), lambda i, j: (0, j)),
                  pl.BlockSpec((d, tf), lambda i, j: (0, j)),
                  pl.BlockSpec((tf, d), lambda i, j: (j, 0))],
        out_specs=pl.BlockSpec((tm, d), lambda i, j: (i, 0)),
        out_shape=jax.ShapeDtypeStruct((n, d), F32),
        scratch_shapes=[pltpu.VMEM((tm, d), BF16)],
        compiler_params=_cparams(("parallel", "arbitrary"), 56),
        name="dense_swiglu",
    )(x2, g, w_gate, w_up, w_down)


def _store_rows_3d(ref3, val2):
    for s in range(ref3.shape[1]):
        ref3[:, s, :] = val2[:, s * LANES:(s + 1) * LANES]


def _router_kernel(x_ref, g_ref, rwh_ref, rwl_ref, rb_ref, h_ref, e_ref, gate_ref, rank_ref, cnt_ref, base_s):
    @pl.when(pl.program_id(0) == 0)
    def _():
        base_s[...] = jnp.zeros_like(base_s)

    hn = _rms(x_ref[...], g_ref[...])
    _store_rows_3d(h_ref, hn)
    logits = _dot_split(hn, rwh_ref[...], rwl_ref[...]) + rb_ref[...]
    lane = lax.broadcasted_iota(I32, logits.shape, 1)
    logits = jnp.where(lane < N_EXPERTS, logits, -jnp.inf)
    m1 = jnp.max(logits, axis=-1, keepdims=True)
    i1 = jnp.min(jnp.where(logits == m1, lane, LANES), axis=-1, keepdims=True)
    rest = jnp.where(lane == i1, -jnp.inf, logits)
    m2 = jnp.max(rest, axis=-1, keepdims=True)
    i2 = jnp.min(jnp.where(rest == m2, lane, LANES), axis=-1, keepdims=True)
    e = jnp.exp(m2 - m1)
    slot = lax.broadcasted_iota(I32, e_ref.shape, 1)
    e_ref[...] = jnp.where(slot == 0, i1, i2)
    gate_ref[...] = jnp.where(slot == 0, 1.0 / (1.0 + e), e / (1.0 + e))
    tm = logits.shape[0]
    oh1 = (lane == i1).astype(F32)
    oh2 = (lane == i2).astype(F32)
    both = oh1 + oh2
    earlier = (lax.broadcasted_iota(I32, (tm, tm), 1) < lax.broadcasted_iota(I32, (tm, tm), 0)).astype(BF16)
    before = jnp.dot(earlier, both.astype(BF16), preferred_element_type=F32) + base_s[...]
    r1 = jnp.sum(before * oh1, axis=-1, keepdims=True)
    r2 = jnp.sum(before * oh2, axis=-1, keepdims=True)
    rank_ref[...] = jnp.where(slot == 0, r1, r2).astype(I32)
    base_s[...] += jnp.sum(both, axis=0, keepdims=True)
    cnt_ref[...] = base_s[...].astype(I32)


def _router(x2, g, rw_pad, rb_pad, tm=512):
    n, d = x2.shape
    s = d // LANES
    rw_hi, rw_lo = _split_bf16(rw_pad)
    return pl.pallas_call(
        _router_kernel,
        grid=(n // tm,),
        in_specs=[pl.BlockSpec((tm, d), lambda i: (i, 0)),
                  pl.BlockSpec((1, d), lambda i: (0, 0)),
                  pl.BlockSpec((d, LANES), lambda i: (0, 0)),
                  pl.BlockSpec((d, LANES), lambda i: (0, 0)),
                  pl.BlockSpec((1, LANES), lambda i: (0, 0))],
        out_specs=[pl.BlockSpec((tm, s, LANES), lambda i: (i, 0, 0)),
                   pl.BlockSpec((tm, TOP_K), lambda i: (i, 0)),
                   pl.BlockSpec((tm, TOP_K), lambda i: (i, 0)),
                   pl.BlockSpec((tm, TOP_K), lambda i: (i, 0)),
                   pl.BlockSpec((1, LANES), lambda i: (0, 0))],
        out_shape=[jax.ShapeDtypeStruct((n, s, LANES), F32),
                   jax.ShapeDtypeStruct((n, TOP_K), I32),
                   jax.ShapeDtypeStruct((n, TOP_K), F32),
                   jax.ShapeDtypeStruct((n, TOP_K), I32),
                   jax.ShapeDtypeStruct((1, LANES), I32)],
        scratch_shapes=[pltpu.VMEM((1, LANES), F32)],
        compiler_params=_cparams(("arbitrary",), 32),
        name="moe_router",
    )(x2, g, rw_hi, rw_lo, rb_pad)


def _slab_pitch(s):
    return s if (s // 8) % 2 == 1 else s + 8


def _row_gather(src_hbm, idx_ref, idx_base, dst, sem, rows):
    slab = src_hbm.shape[1]

    def copy(r):
        return pltpu.make_async_copy(src_hbm.at[idx_ref[idx_base + r]], dst.at[r, pl.ds(0, slab)], sem)

    def start_one(r, carry):
        copy(r).start()
        return carry

    def wait_one(r, carry):
        copy(r).wait()
        return carry

    return (lambda: lax.fori_loop(0, rows, start_one, 0, unroll=8),
            lambda: lax.fori_loop(0, rows, wait_one, 0, unroll=8))


def _expert_ffn_kernel(be_ref, nu_ref, tok_ref, h_hbm, wg_ref, wu_ref, wd_ref, o_ref,
                       xg_s, x2_s, h_s, acc_s, sems):
    i, j = pl.program_id(0), pl.program_id(1)
    blk = xg_s.shape[1]
    n_used = nu_ref[0]
    last = pl.num_programs(1) - 1
    mid = pl.num_programs(1) // 2
    used = i < n_used
    has_next = i + 1 < n_used
    both = jnp.logical_and

    def gather(block):
        slot = block % 2
        return _row_gather(h_hbm, tok_ref, block * blk, xg_s.at[slot], sems.at[slot], blk)

    def unpack(block):
        slot = block % 2
        for s in range(h_hbm.shape[1]):
            x2_s[:, s * LANES:(s + 1) * LANES] = xg_s[slot, :, s, :]
        h_s[slot] = x2_s[...].astype(BF16)

    def partial_out():
        h = h_s[i % 2]
        gt = jnp.dot(h, wg_ref[...], preferred_element_type=F32)
        up = jnp.dot(h, wu_ref[...], preferred_element_type=F32)
        act = (gt * jax.nn.sigmoid(gt) * up).astype(BF16)
        return jnp.dot(act, wd_ref[...], preferred_element_type=F32)

    @pl.when(both(i == 0, j == 0))
    def _():
        start, wait = gather(0)
        start()
        wait()
        unpack(0)

    @pl.when(both(j == 0, has_next))
    def _():
        gather(i + 1)[0]()

    @pl.when(both(used, j == 0))
    def _():
        acc_s[...] = partial_out()

    @pl.when(both(used, both(both(j > 0, j < last), jnp.logical_or(j != mid, jnp.logical_not(has_next)))))
    def _():
        acc_s[...] += partial_out()

    @pl.when(both(both(used, has_next), j == mid))
    def _():
        gather(i + 1)[1]()
        unpack(i + 1)
        acc_s[...] += partial_out()

    @pl.when(both(used, j == last))
    def _():
        res = acc_s[...] + partial_out()
        for s in range(o_ref.shape[1]):
            o_ref[:, s, :] = res[:, s * LANES:(s + 1) * LANES]

    @pl.when(both(jnp.logical_not(used), j == last))
    def _():
        o_ref[...] = jnp.zeros_like(o_ref)


def _expert_ffn(h3, slot_tok, block_e, n_used, w_gate, w_up, w_down, blk, tf=512):
    ns = slot_tok.shape[0]
    s = h3.shape[1]
    d = s * LANES
    f = w_gate.shape[2]
    nj = f // tf
    assert nj >= 3

    def jj(i, j, nu):
        return jnp.where(i < nu[0], j, nj - 1)

    return pl.pallas_call(
        _expert_ffn_kernel,
        grid_spec=pltpu.PrefetchScalarGridSpec(
            num_scalar_prefetch=3,
            grid=(ns // blk, nj),
            in_specs=[pl.BlockSpec(memory_space=pl.ANY),
                      pl.BlockSpec((None, d, tf), lambda i, j, be, nu, tok: (be[i], 0, jj(i, j, nu))),
                      pl.BlockSpec((None, d, tf), lambda i, j, be, nu, tok: (be[i], 0, jj(i, j, nu))),
                      pl.BlockSpec((None, tf, d), lambda i, j, be, nu, tok: (be[i], jj(i, j, nu), 0))],
            out_specs=pl.BlockSpec((blk, s, LANES), lambda i, j, be, nu, tok: (i, 0, 0)),
            scratch_shapes=[pltpu.VMEM((2, blk, _slab_pitch(s), LANES), F32), pltpu.VMEM((blk, d), F32),
                            pltpu.VMEM((2, blk, d), BF16), pltpu.VMEM((blk, d), F32),
                            pltpu.SemaphoreType.DMA((2,))],
        ),
        out_shape=jax.ShapeDtypeStruct((ns, s, LANES), F32),
        compiler_params=_cparams(("arbitrary", "arbitrary"), 52),
        name="expert_swiglu",
    )(block_e, n_used, slot_tok, h3, w_gate, w_up, w_down)


def _combine_kernel(dest_ref, x_ref, gate_ref, y_hbm, g_ref, o_ref, yg_s, sems, *, final_norm):
    i, n_tiles = pl.program_id(0), pl.num_programs(0)
    tm = x_ref.shape[0]

    def gather(tile):
        slot = tile % 2
        parts = [_row_gather(y_hbm, dest_ref, k * n_tiles * tm + tile * tm, yg_s.at[slot, k], sems.at[slot], tm)
                 for k in range(TOP_K)]
        return (lambda: [p[0]() for p in parts]), (lambda: [p[1]() for p in parts])

    @pl.when(i == 0)
    def _():
        gather(0)[0]()

    @pl.when(i + 1 < n_tiles)
    def _():
        gather(i + 1)[0]()

    gather(i)[1]()
    slot = i % 2
    gates = [jnp.broadcast_to(gate_ref[:, k:k + 1], (tm, LANES)) for k in range(TOP_K)]
    for s in range(y_hbm.shape[1]):
        cols = slice(s * LANES, (s + 1) * LANES)
        acc = x_ref[:, cols]
        for k in range(TOP_K):
            acc = acc + gates[k] * yg_s[slot, k, :, s, :]
        o_ref[:, cols] = acc
    if final_norm:
        o_ref[...] = _rms(o_ref[...], g_ref[...])


def _combine(x2, gate, yb, dest_by_k, g, final_norm, tm=512):
    n, d = x2.shape
    s = d // LANES
    return pl.pallas_call(
        functools.partial(_combine_kernel, final_norm=final_norm),
        grid_spec=pltpu.PrefetchScalarGridSpec(
            num_scalar_prefetch=1,
            grid=(n // tm,),
            in_specs=[pl.BlockSpec((tm, d), lambda i, dest: (i, 0)),
                      pl.BlockSpec((tm, TOP_K), lambda i, dest: (i, 0)),
                      pl.BlockSpec(memory_space=pl.ANY),
                      pl.BlockSpec((1, d), lambda i, dest: (0, 0))],
            out_specs=pl.BlockSpec((tm, d), lambda i, dest: (i, 0)),
            scratch_shapes=[pltpu.VMEM((2, TOP_K, tm, _slab_pitch(s), LANES), F32),
                            pltpu.SemaphoreType.DMA((2,))],
        ),
        out_shape=jax.ShapeDtypeStruct((n, d), F32),
        compiler_params=_cparams(("arbitrary",), 48),
        name="moe_combine",
    )(dest_by_k, x2, gate, yb, g)


def _final_norm_kernel(x_ref, g_ref, o_ref):
    o_ref[...] = _rms(x_ref[...], g_ref[...])


def _final_norm(x2, g, tm=512):
    n, d = x2.shape
    return pl.pallas_call(
        _final_norm_kernel,
        grid=(n // tm,),
        in_specs=[pl.BlockSpec((tm, d), lambda i: (i, 0)), pl.BlockSpec((1, d), lambda i: (0, 0))],
        out_specs=pl.BlockSpec((tm, d), lambda i: (i, 0)),
        out_shape=jax.ShapeDtypeStruct((n, d), F32),
        compiler_params=_cparams(("parallel",), 32),
        name="final_norm",
    )(x2, g)


def _routing_tables(top_e, rank, counts, blk):
    n = top_e.shape[0]
    nk = n * TOP_K
    flat_e = top_e.reshape(-1)
    padded = (counts + blk - 1) // blk * blk
    p_end = jnp.cumsum(padded)
    p_start = p_end - padded
    dest = (p_start[flat_e] + rank.reshape(-1)).astype(I32)
    n_blocks = -(-nk // blk) + N_EXPERTS
    flat_tok = jnp.arange(nk, dtype=I32) // TOP_K
    slot_tok = jnp.zeros((n_blocks * blk,), I32).at[dest].set(flat_tok)
    n_used = (p_end[-1] // blk).astype(I32)
    blocks = jnp.arange(n_blocks, dtype=I32)
    block_e = jnp.sum((p_end[None, :] <= (blocks * blk)[:, None]).astype(I32), axis=1)
    block_e = jnp.minimum(block_e, N_EXPERTS - 1)
    block_e = jnp.where(blocks < n_used, block_e, block_e[n_used - 1])
    dest_by_k = dest.reshape(n, TOP_K).T.reshape(-1)
    return slot_tok, block_e, n_used.reshape(1), dest_by_k


def _moe_layer(x2, norm_g, router_w, router_b, w_gate, w_up, w_down, out_g, final_norm):
    n, d = x2.shape
    rw = jnp.zeros((d, LANES), F32).at[:, :N_EXPERTS].set(router_w)
    rb = jnp.zeros((1, LANES), F32).at[0, :N_EXPERTS].set(router_b)
    h, top_e, gate, rank, counts = _router(x2, norm_g[None, :], rw, rb)
    slot_tok, block_e, n_used, dest_by_k = _routing_tables(top_e, rank, counts[0, :N_EXPERTS], MOE_BLOCK)
    yb = _expert_ffn(h, slot_tok, block_e, n_used, w_gate.astype(BF16), w_up.astype(BF16),
                     w_down.astype(BF16), MOE_BLOCK)
    return _combine(x2, gate, yb, dest_by_k, out_g[None, :], final_norm)


def _mixer_layer(x2, bsz, length, norm_g, w_in, b_in, hy_conv_w, hy_conv_b, hy_w1, hy_b1, hy_w2, hy_b2,
                 hy_w3, hy_freq, hy_decay, hy_skip, ml_conv_w, ml_conv_b, ml_norm_g, w_a, w_b, w_o,
                 ctab, stab, twiddle, zpos):
    n, d = x2.shape
    wh = hy_skip.shape[1]
    wm = ml_norm_g.shape[0]
    off_qk = (HYENA_ORDER + 1) * wh
    off_v = off_qk + 2 * wm
    off_o = off_v + wm
    off_gates = off_o + wm
    off_br = off_gates + 4 * MLSTM_HEADS
    w_cat = jnp.concatenate([w_in[:, off_br:], w_in[:, :off_gates]], axis=1).astype(BF16)
    b_cat = jnp.concatenate([b_in[off_br:], b_in[:off_gates]])[None, :]
    col_hy = 2 * d
    col_q = col_hy + off_qk
    col_k = col_q + wm
    col_v = col_hy + off_v
    col_o = col_hy + off_o

    p_big, gates = _in_proj(x2, norm_g[None, :], w_in[:, off_gates:off_br], b_in[None, off_gates:off_br],
                            w_cat, b_cat)
    p3 = p_big.reshape(bsz, length, p_big.shape[1])

    kp = LANES
    pad2 = lambda a, r, c: jnp.zeros((r, c), F32).at[:a.shape[0], :a.shape[1]].set(a)
    zp = pad2(zpos, length, kp)
    htab, hmid = _hyena_spectrum(
        zp, pad2(hy_w1, kp, kp), pad2(hy_b1[None, :], 1, kp), pad2(hy_w2, kp, kp), pad2(hy_b2[None, :], 1, kp),
        pad2(hy_freq[None, :], 1, kp), pad2(hy_w3, kp, hy_w3.shape[1]), hy_decay[None, :],
        ctab, stab, twiddle, wh, HYENA_CT)
    y_hy = _hyena_conv(p3, col_hy, hy_conv_w, hy_conv_b[None, :], htab, hmid, hy_skip[:, None, :],
                       ctab, stab, wh, HYENA_CT, HYENA_RC)

    gcol, grow = _gate_prep(gates.reshape(bsz, length, 4 * MLSTM_HEADS), MLSTM_CHUNK)
    y_ml = _mlstm(p3, col_q, col_k, col_v, col_o, ml_conv_w, ml_conv_b[None, :], ml_norm_g[None, :],
                  gcol, grow, MLSTM_CHUNK)

    return _merge(y_hy.reshape(n, wh), y_ml.reshape(n, wm), p_big, x2,
                  w_a.astype(BF16), w_b.astype(BF16), w_o.astype(BF16))


def kernel(x, mix_norm_g, mix_w_in, mix_b_in, hy_conv_w, hy_conv_b, hy_filt_w1, hy_filt_b1, hy_filt_w2,
           hy_filt_b2, hy_filt_w3, hy_filt_freq, hy_filt_decay, hy_skip, ml_conv_w, ml_conv_b, ml_norm_g,
           mix_w_a, mix_w_b, mix_w_o, ffn_norm_g, dense_w_gate, dense_w_up, dense_w_down, moe_router_w,
           moe_router_b, moe_w_gate, moe_w_up, moe_w_down, final_norm_g):
    bsz, length, d = x.shape
    depth = mix_norm_g.shape[0]
    x2 = x.reshape(bsz * length, d)
    ctab, stab, twiddle = _dft_tables(length // 2)
    zpos = _hyena_positions(length)
    normed = False
    for layer in range(depth):
        x2 = _mixer_layer(x2, bsz, length, mix_norm_g[layer], mix_w_in[layer], mix_b_in[layer],
                          hy_conv_w[layer], hy_conv_b[layer], hy_filt_w1[layer], hy_filt_b1[layer],
                          hy_filt_w2[layer], hy_filt_b2[layer], hy_filt_w3[layer], hy_filt_freq[layer],
                          hy_filt_decay[layer], hy_skip[layer], ml_conv_w[layer], ml_conv_b[layer],
                          ml_norm_g[layer], mix_w_a[layer], mix_w_b[layer], mix_w_o[layer], ctab, stab, twiddle,
                          zpos)
        j = layer // 2
        if layer % 2 == 0:
            x2 = _dense_ffn(x2, ffn_norm_g[layer][None, :], dense_w_gate[j], dense_w_up[j], dense_w_down[j])
        else:
            normed = layer == depth - 1
            x2 = _moe_layer(x2, ffn_norm_g[layer], moe_router_w[j], moe_router_b[j], moe_w_gate[j],
                            moe_w_up[j], moe_w_down[j], final_norm_g, normed)
    if not normed:
        x2 = _final_norm(x2, final_norm_g[None, :])
    return x2.reshape(bsz, length, d)
```

```python
import functools
import math
from typing import NamedTuple

import jax
import jax.numpy as jnp
import numpy as np
from jax import lax
from jax.experimental import pallas as pl
from jax.experimental.pallas import tpu as pltpu

F32, BF16, I32 = jnp.float32, jnp.bfloat16, jnp.int32
HIGHEST = lax.Precision.HIGHEST

HYENA_ORDER = 2
HYENA_POS_BANDS = 16
MLSTM_HEADS = 8
N_EXPERTS = 8
TOP_K = 2
EPS = 1e-6

V7X_VMEM_BYTES = 64 * 1024 * 1024
LANES = 128
SUBLANES = 8
MXU_DIM = 256


class _Tile(NamedTuple):
    rows: int
    cols: int
    vmem_mib: int


TILES = dict(
    in_proj=_Tile(1024, 1408, 48),
    hyena_spectrum=_Tile(0, MXU_DIM, 48),
    hyena_conv=_Tile(512, MXU_DIM, 56),
    mlstm_gate_prep=_Tile(256, 0, 32),
    mlstm=_Tile(256, 0, 40),
    merge=_Tile(256, 0, 48),
    dense=_Tile(1024, 256, 56),
    router=_Tile(512, 0, 32),
    expert=_Tile(512, 512, 52),
    combine=_Tile(512, 0, 48),
    final_norm=_Tile(512, 0, 32),
)


def _cparams(semantics, tile):
    assert tile.vmem_mib * 1024 * 1024 <= V7X_VMEM_BYTES
    return pltpu.CompilerParams(dimension_semantics=semantics, vmem_limit_bytes=tile.vmem_mib * 1024 * 1024)


def _const_spec(shape):
    nd = len(shape)
    return pl.BlockSpec(shape, lambda *_: (0,) * nd, pipeline_mode=pl.Buffered(1))


def _rms(x, g):
    return x * lax.rsqrt(jnp.mean(x * x, axis=-1, keepdims=True) + EPS) * g


def _shift_rows(u, direction):
    n = u.shape[0]
    row = lax.broadcasted_iota(I32, u.shape, 0)
    if direction > 0:
        return jnp.where(row == 0, 0.0, pltpu.roll(u, 1, axis=0))
    return jnp.where(row == n - 1, 0.0, pltpu.roll(u, n - 1, axis=0))


def _short_conv(u, w, b):
    return b + _shift_rows(u, 1) * w[0:1] + u * w[1:2] + _shift_rows(u, -1) * w[2:3]


def _log_sigmoid(x):
    return jnp.minimum(x, 0.0) - jnp.log1p(jnp.exp(-jnp.abs(x)))


def _dot_split(a, w_hi, w_lo):
    a_hi = a.astype(BF16)
    a_lo = (a - a_hi.astype(F32)).astype(BF16)
    return (jnp.dot(a_hi, w_hi, preferred_element_type=F32) + jnp.dot(a_lo, w_hi, preferred_element_type=F32)
            + jnp.dot(a_hi, w_lo, preferred_element_type=F32))


def _split_bf16(w):
    w_hi = w.astype(BF16)
    return w_hi, (w - w_hi.astype(F32)).astype(BF16)


def _in_proj_kernel(x_ref, g_ref, wg_ref, bg_ref, w_ref, b_ref, o_ref, gates_ref, h_s):
    @pl.when(pl.program_id(1) == 0)
    def _():
        h_s[...] = _rms(x_ref[...], g_ref[...]).astype(BF16)
        gates_ref[...] = jnp.dot(h_s[...], wg_ref[...], preferred_element_type=F32) + bg_ref[...]

    acc = jnp.dot(h_s[...], w_ref[...], preferred_element_type=F32) + b_ref[...]
    o_ref[...] = acc.astype(o_ref.dtype)


def _in_proj(x2, g, w_gates, b_gates, w, b, tile=TILES["in_proj"]):
    tm, tn = tile.rows, tile.cols
    n, d = x2.shape
    nc = w.shape[1]
    ng = w_gates.shape[1]
    return pl.pallas_call(
        _in_proj_kernel,
        grid=(n // tm, nc // tn),
        in_specs=[pl.BlockSpec((tm, d), lambda i, j: (i, 0)),
                  pl.BlockSpec((1, d), lambda i, j: (0, 0)),
                  pl.BlockSpec((d, ng), lambda i, j: (0, 0)),
                  pl.BlockSpec((1, ng), lambda i, j: (0, 0)),
                  pl.BlockSpec((d, tn), lambda i, j: (0, j)),
                  pl.BlockSpec((1, tn), lambda i, j: (0, j))],
        out_specs=[pl.BlockSpec((tm, tn), lambda i, j: (i, j)),
                   pl.BlockSpec((tm, ng), lambda i, j: (i, 0))],
        out_shape=[jax.ShapeDtypeStruct((n, nc), BF16), jax.ShapeDtypeStruct((n, ng), F32)],
        scratch_shapes=[pltpu.VMEM((tm, d), BF16)],
        compiler_params=_cparams(("parallel", "arbitrary"), tile),
        name="in_proj",
    )(x2, g, w_gates.astype(BF16), b_gates, w, b)


def _dft_tables(half):
    k = jnp.arange(half, dtype=I32)
    ang = ((k[:, None] * k[None, :]) % (2 * half)).astype(F32) * (math.pi / half)
    tw = k.astype(F32)[:, None] * (math.pi / (2 * half))
    twiddle = jnp.stack([jnp.broadcast_to(jnp.cos(tw), (half, LANES)), jnp.broadcast_to(jnp.sin(tw), (half, LANES))])
    return jnp.cos(ang).astype(BF16), jnp.sin(ang).astype(BF16), twiddle


def _hyena_positions(length):
    t = np.linspace(0.0, 1.0, length)[:, None]
    n = np.arange(length, dtype=np.float64)[:, None]
    bands = np.linspace(1e-4, HYENA_POS_BANDS - 1, HYENA_POS_BANDS)[None, :]
    ang = (2.0 * np.pi / length) * n * bands
    return jnp.asarray(np.concatenate([t, np.cos(ang), -np.sin(ang)], axis=-1), dtype=F32)


def _split_rows(val, slab_s):
    half = val.shape[0] // 2
    n_slabs = slab_s.shape[0]
    for s in range(n_slabs):
        slab_s[s] = val[:, s * LANES:(s + 1) * LANES]
    pick = lambda p: jnp.concatenate([slab_s[s, pl.ds(p, half, stride=2), :] for s in range(n_slabs)], axis=1)
    return pick(0), pick(1)


def _merge_rows(even, odd, slab_s):
    half = even.shape[0]
    n_slabs = slab_s.shape[0]
    for s in range(n_slabs):
        slab_s[s, pl.ds(0, half, stride=2), :] = even[:, s * LANES:(s + 1) * LANES]
        slab_s[s, pl.ds(1, half, stride=2), :] = odd[:, s * LANES:(s + 1) * LANES]
    return jnp.concatenate([slab_s[s] for s in range(n_slabs)], axis=1)


def _alt_sum(x):
    row = lax.broadcasted_iota(I32, (x.shape[0], 1), 0)
    return jnp.sum(jnp.where(row % 2 == 0, x, -x), axis=0, keepdims=True)


def _hyena_spectrum_kernel(z_ref, w1_ref, b1_ref, w2_ref, b2_ref, fr_ref, w3f_ref, w3b_ref,
                           decf_ref, decb_ref, c_ref, s_ref, tw_ref, tab_ref, mid_ref, slab_s, hid_s):
    half = c_ref.shape[0]
    length = 2 * half
    ct = w3f_ref.shape[1]
    dot_hi = functools.partial(jnp.dot, precision=HIGHEST, preferred_element_type=F32)
    dot = functools.partial(jnp.dot, preferred_element_type=F32)
    z = z_ref[...]

    @pl.when(jnp.logical_and(pl.program_id(0) == 0, pl.program_id(1) == 0))
    def _():
        fr = fr_ref[...]
        hid = jnp.sin(fr * (dot_hi(z, w1_ref[...]) + b1_ref[...]))
        hid_s[...] = jnp.sin(fr * (dot_hi(hid, w2_ref[...]) + b2_ref[...]))

    hid = hid_s[...]
    t = z[:, 0:1]
    ff = dot_hi(hid, w3f_ref[...]) * jnp.exp(-t * jnp.abs(decf_ref[...]))
    fb = dot_hi(hid, w3b_ref[...]) * jnp.exp(-t * jnp.abs(decb_ref[...]))
    row = lax.broadcasted_iota(I32, (length, 1), 0)
    l1 = jnp.sum(jnp.where(row == 0, jnp.abs(ff + fb), jnp.abs(ff) + jnp.abs(fb)), axis=0, keepdims=True)
    inv = 1.0 / l1
    se, so = _split_rows((ff + fb) * inv, slab_s)
    de, do = _split_rows((ff - fb) * inv, slab_s)
    wide = lambda a: jnp.concatenate([a] * (ct // LANES), axis=1)
    cw, sw = wide(tw_ref[0]), wide(tw_ref[1])
    c_t, s_t = c_ref[...], s_ref[...]
    bf = lambda a: a.astype(BF16)
    e_r = dot(c_t, bf(se))
    o_r, o_s = dot(c_t, bf(so)), dot(s_t, bf(so))
    t_r = cw * o_r - sw * o_s
    d_s = dot(s_t, bf(de))
    p_r, p_s = dot(c_t, bf(do)), dot(s_t, bf(do))
    t_i = -(cw * p_s + sw * p_r)
    krow = lax.broadcasted_iota(I32, (half, 1), 0)
    scale = jnp.where(krow == 0, 1.0 / (2 * length), 1.0 / length)
    lo_r, lo_i = (e_r + t_r) * scale, (t_i - d_s) * scale
    hi_r, hi_i = (e_r - t_r) * scale, (t_i + d_s) * scale

    def emit(p, lr, li, hr, hi):
        a_r, a_i = lr + hr, li - hi
        d_r, d_i = lr - hr, li + hi
        tab_ref[4 * p + 0] = a_r
        tab_ref[4 * p + 1] = a_i
        tab_ref[4 * p + 2] = d_r * cw + d_i * sw
        tab_ref[4 * p + 3] = d_i * cw - d_r * sw

    emit(0, lo_r, lo_i, hi_r, hi_i)
    emit(1, lo_r * cw - lo_i * sw, lo_r * sw + lo_i * cw, -(hi_r * cw + hi_i * sw), -(hi_i * cw - hi_r * sw))
    mid_ref[0:1, :] = _alt_sum(se) * (1.0 / length)
    mid_ref[1:2, :] = -_alt_sum(do) * (1.0 / length)


def _hyena_spectrum(z, w1, b1, w2, b2, freq, w3, decay, ctab, stab, twiddle, width, tile=TILES["hyena_spectrum"]):
    ct = tile.cols
    half = ctab.shape[0]
    kp = w1.shape[1]
    nct = width // ct
    col_f = lambda o, c: (0, o * 2 * nct + c)
    col_b = lambda o, c: (0, o * 2 * nct + nct + c)
    return pl.pallas_call(
        _hyena_spectrum_kernel,
        grid=(HYENA_ORDER, nct),
        in_specs=[_const_spec(z.shape), _const_spec(w1.shape), _const_spec(b1.shape),
                  _const_spec(w2.shape), _const_spec(b2.shape), _const_spec(freq.shape),
                  pl.BlockSpec((kp, ct), col_f), pl.BlockSpec((kp, ct), col_b),
                  pl.BlockSpec((1, ct), col_f), pl.BlockSpec((1, ct), col_b),
                  _const_spec(ctab.shape), _const_spec(stab.shape), _const_spec(twiddle.shape)],
        out_specs=[pl.BlockSpec((None, 8, half, ct), lambda o, c: (o, 0, 0, c)),
                   pl.BlockSpec((None, 2, ct), lambda o, c: (o, 0, c))],
        out_shape=[jax.ShapeDtypeStruct((HYENA_ORDER, 8, half, width), F32),
                   jax.ShapeDtypeStruct((HYENA_ORDER, 2, width), F32)],
        scratch_shapes=[pltpu.VMEM((ct // LANES, 2 * half, LANES), F32), pltpu.VMEM((2 * half, kp), F32)],
        compiler_params=_cparams(("arbitrary", "arbitrary"), tile),
        name="hyena_spectrum",
    )(z, w1, b1, w2, b2, freq, w3, w3, decay, decay, ctab, stab, twiddle)


def _hyena_conv_kernel(pv_ref, p1_ref, p2_ref, wv_ref, w1_ref, w2_ref, bv_ref, b1_ref, b2_ref,
                       tab_ref, mid_ref, skip_ref, c_ref, s_ref, y_ref,
                       slab_s, z_s, g1_s, g2_s, zb_s, gr_s, gn_s, *, rc):
    half = c_ref.shape[0]
    dot = functools.partial(jnp.dot, preferred_element_type=F32)
    for dst, p_ref, w_ref, b_ref in ((z_s, pv_ref, wv_ref, bv_ref), (g1_s, p1_ref, w1_ref, b1_ref),
                                     (g2_s, p2_ref, w2_ref, b2_ref)):
        even, odd = _split_rows(_short_conv(p_ref[...].astype(F32), w_ref[...], b_ref[...]), slab_s)
        dst[0] = even
        dst[1] = odd
    gates = (g1_s, g2_s)
    row = lax.broadcasted_iota(I32, (half, 1), 0)
    sign = jnp.where(row % 2 == 0, 1.0, -1.0)
    for o in range(HYENA_ORDER):
        zb_s[0] = z_s[0].astype(BF16)
        zb_s[1] = z_s[1].astype(BF16)
        a_e, a_o = _alt_sum(z_s[0]), _alt_sum(z_s[1])
        mid = (mid_ref[o, 0:1, :] * a_e + mid_ref[o, 1:2, :] * a_o,
               mid_ref[o, 0:1, :] * a_o - mid_ref[o, 1:2, :] * a_e)
        for r in range(half // rc):
            rows = slice(r * rc, (r + 1) * rc)
            e_r, e_s = dot(c_ref[rows, :], zb_s[0]), dot(s_ref[rows, :], zb_s[0])
            o_r, o_s = dot(c_ref[rows, :], zb_s[1]), dot(s_ref[rows, :], zb_s[1])
            for p in range(2):
                a_r, a_i = tab_ref[o, 4 * p + 0, rows, :], tab_ref[o, 4 * p + 1, rows, :]
                b_r, b_i = tab_ref[o, 4 * p + 2, rows, :], tab_ref[o, 4 * p + 3, rows, :]
                gr_s[p, rows, :] = (a_r * e_r + a_i * e_s + b_r * o_r + b_i * o_s).astype(BF16)
                gn_s[p, rows, :] = (a_r * e_s - a_i * e_r + b_r * o_s - b_i * o_r).astype(BF16)
        for p in range(2):
            for r in range(half // rc):
                rows = slice(r * rc, (r + 1) * rc)
                conv = dot(c_ref[rows, :], gr_s[p]) + dot(s_ref[rows, :], gn_s[p]) + sign[rows] * mid[p]
                z_s[p, rows, :] = gates[o][p, rows, :] * (conv + skip_ref[o] * z_s[p, rows, :])
    y_ref[...] = _merge_rows(z_s[0], z_s[1], slab_s).astype(y_ref.dtype)


def _hyena_conv(p3, col0, conv_w, conv_b, tab, mid, skip, ctab, stab, width, tile=TILES["hyena_conv"]):
    rc, ct = tile.rows, tile.cols
    bsz, length, _ = p3.shape
    half = length // 2
    nct = width // ct
    c0 = col0 // ct

    def pspec(part):
        return pl.BlockSpec((None, length, ct), lambda c, b: (b, 0, c0 + part * nct + c))

    def wspec(rows, part):
        return pl.BlockSpec((rows, ct), lambda c, b: (0, part * nct + c))

    scratch = ([pltpu.VMEM((ct // LANES, length, LANES), F32)] + [pltpu.VMEM((2, half, ct), F32)] * 3
               + [pltpu.VMEM((2, half, ct), BF16)] * 3)
    return pl.pallas_call(
        functools.partial(_hyena_conv_kernel, rc=rc),
        grid=(nct, bsz),
        in_specs=[pspec(0), pspec(1), pspec(2),
                  wspec(3, 0), wspec(3, 1), wspec(3, 2), wspec(1, 0), wspec(1, 1), wspec(1, 2),
                  pl.BlockSpec((HYENA_ORDER, 8, half, ct), lambda c, b: (0, 0, 0, c),
                               pipeline_mode=pl.Buffered(1)),
                  pl.BlockSpec((HYENA_ORDER, 2, ct), lambda c, b: (0, 0, c)),
                  pl.BlockSpec((HYENA_ORDER, 1, ct), lambda c, b: (0, 0, c)),
                  _const_spec(ctab.shape), _const_spec(stab.shape)],
        out_specs=pl.BlockSpec((None, length, ct), lambda c, b: (b, 0, c)),
        out_shape=jax.ShapeDtypeStruct((bsz, length, width), BF16),
        scratch_shapes=scratch,
        compiler_params=_cparams(("parallel", "parallel"), tile),
        name="hyena_conv",
    )(p3, p3, p3, conv_w, conv_w, conv_w, conv_b, conv_b, conv_b, tab, mid, skip, ctab, stab)


def _gate_prep_kernel(ic_ref, fc_ref, ir_ref, fr_ref, col_ref, row_ref, *, chunk):
    length, nch = ic_ref.shape
    nh = nch // 2
    r = lax.broadcasted_iota(I32, (chunk, chunk), 0)
    c = lax.broadcasted_iota(I32, (chunk, chunk), 1)
    lower = (c <= r).astype(F32)
    upper = (c >= r).astype(F32)
    dot_hi = functools.partial(jnp.dot, precision=HIGHEST, preferred_element_type=F32)
    fwd_col = lax.broadcasted_iota(I32, (chunk, nch), 1) < nh
    fwd_row = lax.broadcasted_iota(I32, (nch, chunk), 0) < nh
    pos = lax.broadcasted_iota(I32, (chunk, nch), 0)
    for ch in range(length // chunk):
        rows = slice(ch * chunk, (ch + 1) * chunk)
        lf = _log_sigmoid(fc_ref[rows, :])
        b = jnp.where(fwd_col, dot_hi(lower, lf), dot_hi(upper, lf))
        u = ic_ref[rows, :] - b
        cm_f, cm_b = u, u
        shift = 1
        while shift < chunk:
            cm_f = jnp.maximum(cm_f, jnp.where(pos >= shift, pltpu.roll(cm_f, shift, axis=0), -jnp.inf))
            cm_b = jnp.maximum(cm_b, jnp.where(pos + shift < chunk, pltpu.roll(cm_b, chunk - shift, axis=0), -jnp.inf))
            shift *= 2
        col_ref[0, rows, :] = b
        col_ref[1, rows, :] = u
        col_ref[2, rows, :] = jnp.where(fwd_col, cm_f, cm_b)
        lfr = _log_sigmoid(fr_ref[:, rows])
        b_r = jnp.where(fwd_row, dot_hi(lfr, upper), dot_hi(lfr, lower))
        row_ref[:, rows] = ir_ref[:, rows] - b_r


def _gate_prep(gates3, tile=TILES["mlstm_gate_prep"]):
    chunk = tile.rows
    bsz, length, nch4 = gates3.shape
    nh = nch4 // 4
    i_col = jnp.concatenate([gates3[..., :nh], gates3[..., 2 * nh:3 * nh]], axis=-1)
    f_col = jnp.concatenate([gates3[..., nh:2 * nh], gates3[..., 3 * nh:]], axis=-1)
    nch = 2 * nh
    cspec = pl.BlockSpec((None, length, nch), lambda b: (b, 0, 0))
    rspec = pl.BlockSpec((None, nch, length), lambda b: (b, 0, 0))
    col, row = pl.pallas_call(
        functools.partial(_gate_prep_kernel, chunk=chunk),
        grid=(bsz,),
        in_specs=[cspec, cspec, rspec, rspec],
        out_specs=[pl.BlockSpec((None, 3, length, nch), lambda b: (b, 0, 0, 0)), rspec],
        out_shape=[jax.ShapeDtypeStruct((bsz, 3, length, nch), F32),
                   jax.ShapeDtypeStruct((bsz, nch, length), F32)],
        compiler_params=_cparams(("parallel",), tile),
        name="mlstm_gate_prep",
    )(i_col, f_col, jnp.swapaxes(i_col, 1, 2), jnp.swapaxes(f_col, 1, 2))
    col = col.reshape(bsz, 3, length, 2, nh).transpose(0, 4, 2, 3, 1).reshape(bsz, nh, length, 6)
    row = row.reshape(bsz, 2, nh, length).transpose(0, 2, 1, 3)
    return col, row


def _mlstm_kernel(pq_ref, pk_ref, pv_ref, po_ref, cwq_ref, cwk_ref, cbq_ref, cbk_ref, ng_ref,
                  gcol_ref, grow_ref, y_ref, q_s, k_s, v1_s, tab_s, hf_s, hb_s, *, chunk):
    length, dk = pq_ref.shape
    nc = length // chunk
    rep = chunk // LANES

    def conv_silu(p_ref, w_ref, b_ref):
        c = _short_conv(p_ref[...].astype(F32), w_ref[...], b_ref[...])
        return c * jax.nn.sigmoid(c)

    q_s[...] = (conv_silu(pq_ref, cwq_ref, cbq_ref) * (dk ** -0.5)).astype(BF16)
    k_s[...] = conv_silu(pk_ref, cwk_ref, cbk_ref).astype(BF16)
    v1_s[:, :dk] = pv_ref[...]
    v1_s[:, dk:] = jnp.ones((length, dk), BF16)
    for t in range(tab_s.shape[0]):
        tab_s[t] = jnp.broadcast_to(gcol_ref[:, t:t + 1], (length, LANES))

    row_i = lax.broadcasted_iota(I32, (chunk, chunk), 0)
    col_i = lax.broadcasted_iota(I32, (chunk, chunk), 1)
    wide = lambda a: jnp.concatenate([a] * rep, axis=1)
    both = lambda a: jnp.concatenate([a, a], axis=1)

    def step(c, reverse, state, m):
        d = 1 if reverse else 0
        rows = slice(c * chunk, (c + 1) * chunk)
        edge = c * chunk if reverse else (c + 1) * chunk - 1
        q, k, v1 = q_s[rows, :], k_s[rows, :], v1_s[rows, :]
        b_t, u_t, cm_t = tab_s[3 * d, rows, :], tab_s[3 * d + 1, rows, :], tab_s[3 * d + 2, rows, :]
        u_row = grow_ref[d:d + 1, rows]
        g = tab_s[3 * d, edge:edge + 1, :]
        u_max = tab_s[3 * d + 2, edge:edge + 1, :]
        mask = (col_i >= row_i) if reverse else (col_i <= row_i)
        mm = jnp.maximum(cm_t, m)
        p = jnp.exp(jnp.where(mask, u_row - wide(mm), -jnp.inf))
        qk = lax.dot_general(q, k, (((1,), (1,)), ((), ())), preferred_element_type=F32)
        intra = jnp.dot((qk * p).astype(BF16), v1, preferred_element_type=F32)
        inter = jnp.dot(q, state.astype(BF16), preferred_element_type=F32)
        nd = intra + both(jnp.exp(m - mm)) * inter
        h = nd[:, :dk] / jnp.maximum(jnp.abs(nd[:, dk:]), jnp.exp(-(b_t + mm)))
        kw = (k.astype(F32) * jnp.exp(u_t - u_max)).astype(BF16)
        upd = lax.dot_general(kw, v1, (((0,), (0,)), ((), ())), preferred_element_type=F32)
        m_loc = g + u_max
        m_new = jnp.maximum(g + m, m_loc)
        state = both(jnp.exp(g + m - m_new)) * state + both(jnp.exp(m_loc - m_new)) * upd
        return h, state, m_new

    zero = (jnp.zeros((dk, 2 * dk), F32), jnp.zeros((1, LANES), F32))
    st_f, m_f = zero
    st_b, m_b = zero
    for c in range(nc):
        h, st_f, m_f = step(c, False, st_f, m_f)
        hf_s[c * chunk:(c + 1) * chunk, :] = h
        cb = nc - 1 - c
        h, st_b, m_b = step(cb, True, st_b, m_b)
        hb_s[cb * chunk:(cb + 1) * chunk, :] = h

    ht = hf_s[...] + hb_s[...]
    y_ref[...] = (_rms(ht, ng_ref[...]) * jax.nn.sigmoid(po_ref[...].astype(F32))).astype(y_ref.dtype)


def _mlstm(p3, col_q, col_k, col_v, col_o, conv_w, conv_b, norm_g, gcol, grow, tile=TILES["mlstm"]):
    chunk = tile.rows
    bsz, length, _ = p3.shape
    nh, dh = MLSTM_HEADS, norm_g.shape[1] // MLSTM_HEADS
    assert dh == LANES and chunk % LANES == 0

    def pspec(col):
        return pl.BlockSpec((None, length, dh), lambda b, h: (b, 0, col // dh + h))

    def wspec(rows, part):
        return pl.BlockSpec((rows, dh), lambda b, h: (0, part * nh + h))

    return pl.pallas_call(
        functools.partial(_mlstm_kernel, chunk=chunk),
        grid=(bsz, nh),
        in_specs=[pspec(col_q), pspec(col_k), pspec(col_v), pspec(col_o),
                  wspec(3, 0), wspec(3, 1), wspec(1, 0), wspec(1, 1),
                  pl.BlockSpec((1, dh), lambda b, h: (0, h)),
                  pl.BlockSpec((None, None, length, 6), lambda b, h: (b, h, 0, 0)),
                  pl.BlockSpec((None, None, 2, length), lambda b, h: (b, h, 0, 0))],
        out_specs=pl.BlockSpec((None, length, dh), lambda b, h: (b, 0, h)),
        out_shape=jax.ShapeDtypeStruct((bsz, length, nh * dh), BF16),
        scratch_shapes=[pltpu.VMEM((length, dh), BF16), pltpu.VMEM((length, dh), BF16),
                        pltpu.VMEM((length, 2 * dh), BF16), pltpu.VMEM((6, length, LANES), F32),
                        pltpu.VMEM((length, dh), F32), pltpu.VMEM((length, dh), F32)],
        compiler_params=_cparams(("parallel", "parallel"), tile),
        name="mlstm",
    )(p3, p3, p3, p3, conv_w, conv_w, conv_b, conv_b, norm_g, gcol, grow)


def _merge_kernel(yh_ref, ym_ref, gh_ref, gm_ref, x_ref, wa_ref, wb_ref, wo_ref, o_ref):
    a = jnp.dot(yh_ref[...], wa_ref[...], preferred_element_type=F32)
    b = jnp.dot(ym_ref[...], wb_ref[...], preferred_element_type=F32)
    t = jax.nn.sigmoid(gh_ref[...].astype(F32)) * a + jax.nn.sigmoid(gm_ref[...].astype(F32)) * b
    o_ref[...] = x_ref[...] + jnp.dot(t.astype(BF16), wo_ref[...], preferred_element_type=F32)


def _merge(y_hy, y_ml, p_big, x2, w_a, w_b, w_o, tile=TILES["merge"]):
    tm = tile.rows
    n, d = x2.shape
    wh, wm = y_hy.shape[1], y_ml.shape[1]
    return pl.pallas_call(
        _merge_kernel,
        grid=(n // tm,),
        in_specs=[pl.BlockSpec((tm, wh), lambda i: (i, 0)),
                  pl.BlockSpec((tm, wm), lambda i: (i, 0)),
                  pl.BlockSpec((tm, d), lambda i: (i, 0)),
                  pl.BlockSpec((tm, d), lambda i: (i, 1)),
                  pl.BlockSpec((tm, d), lambda i: (i, 0)),
                  _const_spec(w_a.shape), _const_spec(w_b.shape), _const_spec(w_o.shape)],
        out_specs=pl.BlockSpec((tm, d), lambda i: (i, 0)),
        out_shape=jax.ShapeDtypeStruct((n, d), F32),
        compiler_params=_cparams(("parallel",), tile),
        name="branch_merge",
    )(y_hy, y_ml, p_big, p_big, x2, w_a, w_b, w_o)


def _dense_ffn_kernel(x_ref, g_ref, wg_ref, wu_ref, wd_ref, o_ref, h_s):
    @pl.when(pl.program_id(1) == 0)
    def _():
        x = x_ref[...]
        h_s[...] = _rms(x, g_ref[...]).astype(BF16)
        o_ref[...] = x

    h = h_s[...]
    gt = jnp.dot(h, wg_ref[...].astype(BF16), preferred_element_type=F32)
    up = jnp.dot(h, wu_ref[...].astype(BF16), preferred_element_type=F32)
    act = (gt * jax.nn.sigmoid(gt) * up).astype(BF16)
    o_ref[...] += jnp.dot(act, wd_ref[...].astype(BF16), preferred_element_type=F32)


def _dense_ffn(x2, g, w_gate, w_up, w_down, tile=TILES["dense"]):
    tm, tf = tile.rows, tile.cols
    n, d = x2.shape
    f = w_gate.shape[1]
    return pl.pallas_call(
        _dense_ffn_kernel,
        grid=(n // tm, f // tf),
        in_specs=[pl.BlockSpec((tm, d), lambda i, j: (i, 0)),
                  pl.BlockSpec((1, d), lambda i, j: (0, 0)),
                  pl.BlockSpec((d, tf), lambda i, j: (0, j)),
                  pl.BlockSpec((d, tf), lambda i, j: (0, j)),
                  pl.BlockSpec((tf, d), lambda i, j: (j, 0))],
        out_specs=pl.BlockSpec((tm, d), lambda i, j: (i, 0)),
        out_shape=jax.ShapeDtypeStruct((n, d), F32),
        scratch_shapes=[pltpu.VMEM((tm, d), BF16)],
        compiler_params=_cparams(("parallel", "arbitrary"), tile),
        name="dense_swiglu",
    )(x2, g, w_gate, w_up, w_down)


def _store_rows_3d(ref3, val2):
    for s in range(ref3.shape[1]):
        ref3[:, s, :] = val2[:, s * LANES:(s + 1) * LANES]


def _router_kernel(x_ref, g_ref, rwh_ref, rwl_ref, rb_ref, h_ref, e_ref, gate_ref, rank_ref, cnt_ref, base_s):
    @pl.when(pl.program_id(0) == 0)
    def _():
        base_s[...] = jnp.zeros_like(base_s)

    hn = _rms(x_ref[...], g_ref[...])
    _store_rows_3d(h_ref, hn)
    logits = _dot_split(hn, rwh_ref[...], rwl_ref[...]) + rb_ref[...]
    lane = lax.broadcasted_iota(I32, logits.shape, 1)
    logits = jnp.where(lane < N_EXPERTS, logits, -jnp.inf)
    m1 = jnp.max(logits, axis=-1, keepdims=True)
    i1 = jnp.min(jnp.where(logits == m1, lane, LANES), axis=-1, keepdims=True)
    rest = jnp.where(lane == i1, -jnp.inf, logits)
    m2 = jnp.max(rest, axis=-1, keepdims=True)
    i2 = jnp.min(jnp.where(rest == m2, lane, LANES), axis=-1, keepdims=True)
    e = jnp.exp(m2 - m1)
    slot = lax.broadcasted_iota(I32, e_ref.shape, 1)
    e_ref[...] = jnp.where(slot == 0, i1, i2)
    gate_ref[...] = jnp.where(slot == 0, 1.0 / (1.0 + e), e / (1.0 + e))
    tm = logits.shape[0]
    oh1 = (lane == i1).astype(F32)
    oh2 = (lane == i2).astype(F32)
    both = oh1 + oh2
    earlier = (lax.broadcasted_iota(I32, (tm, tm), 1) < lax.broadcasted_iota(I32, (tm, tm), 0)).astype(BF16)
    before = jnp.dot(earlier, both.astype(BF16), preferred_element_type=F32) + base_s[...]
    r1 = jnp.sum(before * oh1, axis=-1, keepdims=True)
    r2 = jnp.sum(before * oh2, axis=-1, keepdims=True)
    rank_ref[...] = jnp.where(slot == 0, r1, r2).astype(I32)
    base_s[...] += jnp.sum(both, axis=0, keepdims=True)
    cnt_ref[...] = base_s[...].astype(I32)


def _router(x2, g, rw_pad, rb_pad, tile=TILES["router"]):
    tm = tile.rows
    n, d = x2.shape
    s = d // LANES
    rw_hi, rw_lo = _split_bf16(rw_pad)
    return pl.pallas_call(
        _router_kernel,
        grid=(n // tm,),
        in_specs=[pl.BlockSpec((tm, d), lambda i: (i, 0)),
                  pl.BlockSpec((1, d), lambda i: (0, 0)),
                  pl.BlockSpec((d, LANES), lambda i: (0, 0)),
                  pl.BlockSpec((d, LANES), lambda i: (0, 0)),
                  pl.BlockSpec((1, LANES), lambda i: (0, 0))],
        out_specs=[pl.BlockSpec((tm, s, LANES), lambda i: (i, 0, 0)),
                   pl.BlockSpec((tm, TOP_K), lambda i: (i, 0)),
                   pl.BlockSpec((tm, TOP_K), lambda i: (i, 0)),
                   pl.BlockSpec((tm, TOP_K), lambda i: (i, 0)),
                   pl.BlockSpec((1, LANES), lambda i: (0, 0))],
        out_shape=[jax.ShapeDtypeStruct((n, s, LANES), F32),
                   jax.ShapeDtypeStruct((n, TOP_K), I32),
                   jax.ShapeDtypeStruct((n, TOP_K), F32),
                   jax.ShapeDtypeStruct((n, TOP_K), I32),
                   jax.ShapeDtypeStruct((1, LANES), I32)],
        scratch_shapes=[pltpu.VMEM((1, LANES), F32)],
        compiler_params=_cparams(("arbitrary",), tile),
        name="moe_router",
    )(x2, g, rw_hi, rw_lo, rb_pad)


def _slab_pitch(s):
    return s if (s // SUBLANES) % 2 == 1 else s + SUBLANES


def _row_gather(src_hbm, idx_ref, idx_base, dst, sem, rows):
    slab = src_hbm.shape[1]

    def copy(r):
        return pltpu.make_async_copy(src_hbm.at[idx_ref[idx_base + r]], dst.at[r, pl.ds(0, slab)], sem)

    def start_one(r, carry):
        copy(r).start()
        return carry

    def wait_one(r, carry):
        copy(r).wait()
        return carry

    return (lambda: lax.fori_loop(0, rows, start_one, 0, unroll=8),
            lambda: lax.fori_loop(0, rows, wait_one, 0, unroll=8))


def _expert_ffn_kernel(be_ref, nu_ref, tok_ref, h_hbm, wg_ref, wu_ref, wd_ref, o_ref,
                       xg_s, x2_s, h_s, acc_s, sems):
    i, j = pl.program_id(0), pl.program_id(1)
    blk = xg_s.shape[1]
    n_used = nu_ref[0]
    last = pl.num_programs(1) - 1
    mid = pl.num_programs(1) // 2
    used = i < n_used
    has_next = i + 1 < n_used
    both = jnp.logical_and

    def gather(block):
        slot = block % 2
        return _row_gather(h_hbm, tok_ref, block * blk, xg_s.at[slot], sems.at[slot], blk)

    def unpack(block):
        slot = block % 2
        for s in range(h_hbm.shape[1]):
            x2_s[:, s * LANES:(s + 1) * LANES] = xg_s[slot, :, s, :]
        h_s[slot] = x2_s[...].astype(BF16)

    def partial_out():
        h = h_s[i % 2]
        gt = jnp.dot(h, wg_ref[...], preferred_element_type=F32)
        up = jnp.dot(h, wu_ref[...], preferred_element_type=F32)
        act = (gt * jax.nn.sigmoid(gt) * up).astype(BF16)
        return jnp.dot(act, wd_ref[...], preferred_element_type=F32)

    @pl.when(both(i == 0, j == 0))
    def _():
        start, wait = gather(0)
        start()
        wait()
        unpack(0)

    @pl.when(both(j == 0, has_next))
    def _():
        gather(i + 1)[0]()

    @pl.when(both(used, j == 0))
    def _():
        acc_s[...] = partial_out()

    @pl.when(both(used, both(both(j > 0, j < last), jnp.logical_or(j != mid, jnp.logical_not(has_next)))))
    def _():
        acc_s[...] += partial_out()

    @pl.when(both(both(used, has_next), j == mid))
    def _():
        gather(i + 1)[1]()
        unpack(i + 1)
        acc_s[...] += partial_out()

    @pl.when(both(used, j == last))
    def _():
        res = acc_s[...] + partial_out()
        for s in range(o_ref.shape[1]):
            o_ref[:, s, :] = res[:, s * LANES:(s + 1) * LANES]

    @pl.when(both(jnp.logical_not(used), j == last))
    def _():
        o_ref[...] = jnp.zeros_like(o_ref)


def _expert_ffn(h3, slot_tok, block_e, n_used, w_gate, w_up, w_down, tile=TILES["expert"]):
    blk, tf = tile.rows, tile.cols
    ns = slot_tok.shape[0]
    s = h3.shape[1]
    d = s * LANES
    f = w_gate.shape[2]
    nj = f // tf
    assert nj >= 3

    def jj(i, j, nu):
        return jnp.where(i < nu[0], j, nj - 1)

    return pl.pallas_call(
        _expert_ffn_kernel,
        grid_spec=pltpu.PrefetchScalarGridSpec(
            num_scalar_prefetch=3,
            grid=(ns // blk, nj),
            in_specs=[pl.BlockSpec(memory_space=pl.ANY),
                      pl.BlockSpec((None, d, tf), lambda i, j, be, nu, tok: (be[i], 0, jj(i, j, nu))),
                      pl.BlockSpec((None, d, tf), lambda i, j, be, nu, tok: (be[i], 0, jj(i, j, nu))),
                      pl.BlockSpec((None, tf, d), lambda i, j, be, nu, tok: (be[i], jj(i, j, nu), 0))],
            out_specs=pl.BlockSpec((blk, s, LANES), lambda i, j, be, nu, tok: (i, 0, 0)),
            scratch_shapes=[pltpu.VMEM((2, blk, _slab_pitch(s), LANES), F32), pltpu.VMEM((blk, d), F32),
                            pltpu.VMEM((2, blk, d), BF16), pltpu.VMEM((blk, d), F32),
                            pltpu.SemaphoreType.DMA((2,))],
        ),
        out_shape=jax.ShapeDtypeStruct((ns, s, LANES), F32),
        compiler_params=_cparams(("arbitrary", "arbitrary"), tile),
        name="expert_swiglu",
    )(block_e, n_used, slot_tok, h3, w_gate, w_up, w_down)


def _combine_kernel(dest_ref, x_ref, gate_ref, y_hbm, g_ref, o_ref, yg_s, sems, *, final_norm):
    i, n_tiles = pl.program_id(0), pl.num_programs(0)
    tm = x_ref.shape[0]

    def gather(tile):
        slot = tile % 2
        parts = [_row_gather(y_hbm, dest_ref, k * n_tiles * tm + tile * tm, yg_s.at[slot, k], sems.at[slot], tm)
                 for k in range(TOP_K)]
        return (lambda: [p[0]() for p in parts]), (lambda: [p[1]() for p in parts])

    @pl.when(i == 0)
    def _():
        gather(0)[0]()

    @pl.when(i + 1 < n_tiles)
    def _():
        gather(i + 1)[0]()

    gather(i)[1]()
    slot = i % 2
    gates = [jnp.broadcast_to(gate_ref[:, k:k + 1], (tm, LANES)) for k in range(TOP_K)]
    for s in range(y_hbm.shape[1]):
        cols = slice(s * LANES, (s + 1) * LANES)
        acc = x_ref[:, cols]
        for k in range(TOP_K):
            acc = acc + gates[k] * yg_s[slot, k, :, s, :]
        o_ref[:, cols] = acc
    if final_norm:
        o_ref[...] = _rms(o_ref[...], g_ref[...])


def _combine(x2, gate, yb, dest_by_k, g, final_norm, tile=TILES["combine"]):
    tm = tile.rows
    n, d = x2.shape
    s = d // LANES
    return pl.pallas_call(
        functools.partial(_combine_kernel, final_norm=final_norm),
        grid_spec=pltpu.PrefetchScalarGridSpec(
            num_scalar_prefetch=1,
            grid=(n // tm,),
            in_specs=[pl.BlockSpec((tm, d), lambda i, dest: (i, 0)),
                      pl.BlockSpec((tm, TOP_K), lambda i, dest: (i, 0)),
                      pl.BlockSpec(memory_space=pl.ANY),
                      pl.BlockSpec((1, d), lambda i, dest: (0, 0))],
            out_specs=pl.BlockSpec((tm, d), lambda i, dest: (i, 0)),
            scratch_shapes=[pltpu.VMEM((2, TOP_K, tm, _slab_pitch(s), LANES), F32),
                            pltpu.SemaphoreType.DMA((2,))],
        ),
        out_shape=jax.ShapeDtypeStruct((n, d), F32),
        compiler_params=_cparams(("arbitrary",), tile),
        name="moe_combine",
    )(dest_by_k, x2, gate, yb, g)


def _final_norm_kernel(x_ref, g_ref, o_ref):
    o_ref[...] = _rms(x_ref[...], g_ref[...])


def _final_norm(x2, g, tile=TILES["final_norm"]):
    tm = tile.rows
    n, d = x2.shape
    return pl.pallas_call(
        _final_norm_kernel,
        grid=(n // tm,),
        in_specs=[pl.BlockSpec((tm, d), lambda i: (i, 0)), pl.BlockSpec((1, d), lambda i: (0, 0))],
        out_specs=pl.BlockSpec((tm, d), lambda i: (i, 0)),
        out_shape=jax.ShapeDtypeStruct((n, d), F32),
        compiler_params=_cparams(("parallel",), tile),
        name="final_norm",
    )(x2, g)


def _routing_tables(top_e, rank, counts, blk):
    n = top_e.shape[0]
    nk = n * TOP_K
    flat_e = top_e.reshape(-1)
    padded = (counts + blk - 1) // blk * blk
    p_end = jnp.cumsum(padded)
    p_start = p_end - padded
    dest = (p_start[flat_e] + rank.reshape(-1)).astype(I32)
    n_blocks = -(-nk // blk) + N_EXPERTS
    flat_tok = jnp.arange(nk, dtype=I32) // TOP_K
    slot_tok = jnp.zeros((n_blocks * blk,), I32).at[dest].set(flat_tok, unique_indices=True)
    n_used = (p_end[-1] // blk).astype(I32)
    blocks = jnp.arange(n_blocks, dtype=I32)
    block_e = jnp.sum((p_end[None, :] <= (blocks * blk)[:, None]).astype(I32), axis=1)
    block_e = jnp.minimum(block_e, N_EXPERTS - 1)
    block_e = jnp.where(blocks < n_used, block_e, block_e[n_used - 1])
    dest_by_k = dest.reshape(n, TOP_K).T.reshape(-1)
    return slot_tok, block_e, n_used.reshape(1), dest_by_k


def _moe_layer(x2, norm_g, router_w, router_b, w_gate, w_up, w_down, out_g, final_norm):
    n, d = x2.shape
    rw = jnp.zeros((d, LANES), F32).at[:, :N_EXPERTS].set(router_w)
    rb = jnp.zeros((1, LANES), F32).at[0, :N_EXPERTS].set(router_b)
    h, top_e, gate, rank, counts = _router(x2, norm_g[None, :], rw, rb)
    slot_tok, block_e, n_used, dest_by_k = _routing_tables(top_e, rank, counts[0, :N_EXPERTS], TILES["expert"].rows)
    yb = _expert_ffn(h, slot_tok, block_e, n_used, w_gate.astype(BF16), w_up.astype(BF16), w_down.astype(BF16))
    return _combine(x2, gate, yb, dest_by_k, out_g[None, :], final_norm)


def _mixer_layer(x2, bsz, length, norm_g, w_in, b_in, hy_conv_w, hy_conv_b, hy_w1, hy_b1, hy_w2, hy_b2,
                 hy_w3, hy_freq, hy_decay, hy_skip, ml_conv_w, ml_conv_b, ml_norm_g, w_a, w_b, w_o,
                 ctab, stab, twiddle, zpos):
    n, d = x2.shape
    wh = hy_skip.shape[1]
    wm = ml_norm_g.shape[0]
    off_qk = (HYENA_ORDER + 1) * wh
    off_v = off_qk + 2 * wm
    off_o = off_v + wm
    off_gates = off_o + wm
    off_br = off_gates + 4 * MLSTM_HEADS
    w_cat = jnp.concatenate([w_in[:, off_br:], w_in[:, :off_gates]], axis=1).astype(BF16)
    b_cat = jnp.concatenate([b_in[off_br:], b_in[:off_gates]])[None, :]
    col_hy = 2 * d
    col_q = col_hy + off_qk
    col_k = col_q + wm
    col_v = col_hy + off_v
    col_o = col_hy + off_o

    p_big, gates = _in_proj(x2, norm_g[None, :], w_in[:, off_gates:off_br], b_in[None, off_gates:off_br],
                            w_cat, b_cat)
    p3 = p_big.reshape(bsz, length, p_big.shape[1])

    kp = LANES
    pad2 = lambda a, r, c: jnp.zeros((r, c), F32).at[:a.shape[0], :a.shape[1]].set(a)
    zp = pad2(zpos, length, kp)
    htab, hmid = _hyena_spectrum(
        zp, pad2(hy_w1, kp, kp), pad2(hy_b1[None, :], 1, kp), pad2(hy_w2, kp, kp), pad2(hy_b2[None, :], 1, kp),
        pad2(hy_freq[None, :], 1, kp), pad2(hy_w3, kp, hy_w3.shape[1]), hy_decay[None, :],
        ctab, stab, twiddle, wh)
    y_hy = _hyena_conv(p3, col_hy, hy_conv_w, hy_conv_b[None, :], htab, hmid, hy_skip[:, None, :], ctab, stab, wh)

    gcol, grow = _gate_prep(gates.reshape(bsz, length, 4 * MLSTM_HEADS))
    y_ml = _mlstm(p3, col_q, col_k, col_v, col_o, ml_conv_w, ml_conv_b[None, :], ml_norm_g[None, :],
                  gcol, grow)

    return _merge(y_hy.reshape(n, wh), y_ml.reshape(n, wm), p_big, x2,
                  w_a.astype(BF16), w_b.astype(BF16), w_o.astype(BF16))


def kernel(x, mix_norm_g, mix_w_in, mix_b_in, hy_conv_w, hy_conv_b, hy_filt_w1, hy_filt_b1, hy_filt_w2,
           hy_filt_b2, hy_filt_w3, hy_filt_freq, hy_filt_decay, hy_skip, ml_conv_w, ml_conv_b, ml_norm_g,
           mix_w_a, mix_w_b, mix_w_o, ffn_norm_g, dense_w_gate, dense_w_up, dense_w_down, moe_router_w,
           moe_router_b, moe_w_gate, moe_w_up, moe_w_down, final_norm_g):
    bsz, length, d = x.shape
    depth = mix_norm_g.shape[0]
    x2 = x.reshape(bsz * length, d)
    ctab, stab, twiddle = _dft_tables(length // 2)
    zpos = _hyena_positions(length)
    normed = False
    for layer in range(depth):
        x2 = _mixer_layer(x2, bsz, length, mix_norm_g[layer], mix_w_in[layer], mix_b_in[layer],
                          hy_conv_w[layer], hy_conv_b[layer], hy_filt_w1[layer], hy_filt_b1[layer],
                          hy_filt_w2[layer], hy_filt_b2[layer], hy_filt_w3[layer], hy_filt_freq[layer],
                          hy_filt_decay[layer], hy_skip[layer], ml_conv_w[layer], ml_conv_b[layer],
                          ml_norm_g[layer], mix_w_a[layer], mix_w_b[layer], mix_w_o[layer], ctab, stab, twiddle,
                          zpos)
        j = layer // 2
        if layer % 2 == 0:
            x2 = _dense_ffn(x2, ffn_norm_g[layer][None, :], dense_w_gate[j], dense_w_up[j], dense_w_down[j])
        else:
            normed = layer == depth - 1
            x2 = _moe_layer(x2, ffn_norm_g[layer], moe_router_w[j], moe_router_b[j], moe_w_gate[j],
                            moe_w_up[j], moe_w_down[j], final_norm_g, normed)
    if not normed:
        x2 = _final_norm(x2, final_norm_g[None, :])
    return x2.reshape(bsz, length, d)
```

```python
import functools
import math
from typing import NamedTuple

import jax
import jax.numpy as jnp
import numpy as np
from jax import lax
from jax.experimental import pallas as pl
from jax.experimental.pallas import tpu as pltpu

F32, BF16, I32 = jnp.float32, jnp.bfloat16, jnp.int32
HIGHEST = lax.Precision.HIGHEST

HYENA_ORDER = 2
HYENA_POS_BANDS = 16
MLSTM_HEADS = 8
N_EXPERTS = 8
TOP_K = 2
EPS = 1e-6

V7X_VMEM_BYTES = 64 * 1024 * 1024
LANES = 128
SUBLANES = 8
MXU_DIM = 256


class _Tile(NamedTuple):
    rows: int
    cols: int
    vmem_mib: int


TILES = dict(
    in_proj=_Tile(1024, 4 * MXU_DIM, 48),
    hyena_spectrum=_Tile(0, MXU_DIM, 48),
    hyena_conv=_Tile(512, MXU_DIM, 56),
    mlstm_gate_prep=_Tile(256, 0, 32),
    mlstm=_Tile(256, 0, 40),
    merge=_Tile(256, 0, 48),
    dense=_Tile(1024, 256, 56),
    router=_Tile(512, 0, 32),
    expert=_Tile(512, 512, 52),
    combine=_Tile(512, 0, 48),
    final_norm=_Tile(512, 0, 32),
)


def _cparams(semantics, tile):
    assert tile.vmem_mib * 1024 * 1024 <= V7X_VMEM_BYTES
    return pltpu.CompilerParams(dimension_semantics=semantics, vmem_limit_bytes=tile.vmem_mib * 1024 * 1024)


def _const_spec(shape):
    nd = len(shape)
    return pl.BlockSpec(shape, lambda *_: (0,) * nd, pipeline_mode=pl.Buffered(1))


def _rms(x, g):
    return x * lax.rsqrt(jnp.mean(x * x, axis=-1, keepdims=True) + EPS) * g


def _shift_rows(u, direction):
    n = u.shape[0]
    row = lax.broadcasted_iota(I32, u.shape, 0)
    if direction > 0:
        return jnp.where(row == 0, 0.0, pltpu.roll(u, 1, axis=0))
    return jnp.where(row == n - 1, 0.0, pltpu.roll(u, n - 1, axis=0))


def _short_conv(u, w, b):
    return b + _shift_rows(u, 1) * w[0:1] + u * w[1:2] + _shift_rows(u, -1) * w[2:3]


def _log_sigmoid(x):
    return jnp.minimum(x, 0.0) - jnp.log1p(jnp.exp(-jnp.abs(x)))


def _dot_split(a, w_hi, w_lo):
    a_hi = a.astype(BF16)
    a_lo = (a - a_hi.astype(F32)).astype(BF16)
    return (jnp.dot(a_hi, w_hi, preferred_element_type=F32) + jnp.dot(a_lo, w_hi, preferred_element_type=F32)
            + jnp.dot(a_hi, w_lo, preferred_element_type=F32))


def _split_bf16(w):
    w_hi = w.astype(BF16)
    return w_hi, (w - w_hi.astype(F32)).astype(BF16)


def _in_proj_kernel(x_ref, g_ref, wg_ref, bg_ref, w_ref, b_ref, o_ref, gates_ref, h_s):
    @pl.when(pl.program_id(1) == 0)
    def _():
        h_s[...] = _rms(x_ref[...], g_ref[...]).astype(BF16)
        gates_ref[...] = jnp.dot(h_s[...], wg_ref[...], preferred_element_type=F32) + bg_ref[...]

    acc = jnp.dot(h_s[...], w_ref[...], preferred_element_type=F32) + b_ref[...]
    o_ref[...] = acc.astype(o_ref.dtype)


def _in_proj(x2, g, w_gates, b_gates, w, b, tile=TILES["in_proj"]):
    tm, tn = tile.rows, tile.cols
    n, d = x2.shape
    nc = w.shape[1]
    ng = w_gates.shape[1]
    return pl.pallas_call(
        _in_proj_kernel,
        grid=(n // tm, nc // tn),
        in_specs=[pl.BlockSpec((tm, d), lambda i, j: (i, 0)),
                  pl.BlockSpec((1, d), lambda i, j: (0, 0)),
                  pl.BlockSpec((d, ng), lambda i, j: (0, 0)),
                  pl.BlockSpec((1, ng), lambda i, j: (0, 0)),
                  pl.BlockSpec((d, tn), lambda i, j: (0, j)),
                  pl.BlockSpec((1, tn), lambda i, j: (0, j))],
        out_specs=[pl.BlockSpec((tm, tn), lambda i, j: (i, j)),
                   pl.BlockSpec((tm, ng), lambda i, j: (i, 0))],
        out_shape=[jax.ShapeDtypeStruct((n, nc), BF16), jax.ShapeDtypeStruct((n, ng), F32)],
        scratch_shapes=[pltpu.VMEM((tm, d), BF16)],
        compiler_params=_cparams(("parallel", "arbitrary"), tile),
        name="in_proj",
    )(x2, g, w_gates.astype(BF16), b_gates, w, b)


def _dft_tables(half):
    k = jnp.arange(half, dtype=I32)
    ang = ((k[:, None] * k[None, :]) % (2 * half)).astype(F32) * (math.pi / half)
    tw = k.astype(F32)[:, None] * (math.pi / (2 * half))
    twiddle = jnp.stack([jnp.broadcast_to(jnp.cos(tw), (half, LANES)), jnp.broadcast_to(jnp.sin(tw), (half, LANES))])
    return jnp.cos(ang).astype(BF16), jnp.sin(ang).astype(BF16), twiddle


def _hyena_positions(length):
    t = np.linspace(0.0, 1.0, length)[:, None]
    n = np.arange(length, dtype=np.float64)[:, None]
    bands = np.linspace(1e-4, HYENA_POS_BANDS - 1, HYENA_POS_BANDS)[None, :]
    ang = (2.0 * np.pi / length) * n * bands
    return jnp.asarray(np.concatenate([t, np.cos(ang), -np.sin(ang)], axis=-1), dtype=F32)


def _split_rows(val, slab_s):
    half = val.shape[0] // 2
    n_slabs = slab_s.shape[0]
    for s in range(n_slabs):
        slab_s[s] = val[:, s * LANES:(s + 1) * LANES]
    pick = lambda p: jnp.concatenate([slab_s[s, pl.ds(p, half, stride=2), :] for s in range(n_slabs)], axis=1)
    return pick(0), pick(1)


def _merge_rows(even, odd, slab_s):
    half = even.shape[0]
    n_slabs = slab_s.shape[0]
    for s in range(n_slabs):
        slab_s[s, pl.ds(0, half, stride=2), :] = even[:, s * LANES:(s + 1) * LANES]
        slab_s[s, pl.ds(1, half, stride=2), :] = odd[:, s * LANES:(s + 1) * LANES]
    return jnp.concatenate([slab_s[s] for s in range(n_slabs)], axis=1)


def _alt_sum(x):
    row = lax.broadcasted_iota(I32, (x.shape[0], 1), 0)
    return jnp.sum(jnp.where(row % 2 == 0, x, -x), axis=0, keepdims=True)


def _hyena_spectrum_kernel(z_ref, w1_ref, b1_ref, w2_ref, b2_ref, fr_ref, w3f_ref, w3b_ref,
                           decf_ref, decb_ref, c_ref, s_ref, tw_ref, tab_ref, mid_ref, slab_s, hid_s):
    half = c_ref.shape[0]
    length = 2 * half
    ct = w3f_ref.shape[1]
    dot_hi = functools.partial(jnp.dot, precision=HIGHEST, preferred_element_type=F32)
    dot = functools.partial(jnp.dot, preferred_element_type=F32)
    z = z_ref[...]

    @pl.when(jnp.logical_and(pl.program_id(0) == 0, pl.program_id(1) == 0))
    def _():
        fr = fr_ref[...]
        hid = jnp.sin(fr * (dot_hi(z, w1_ref[...]) + b1_ref[...]))
        hid_s[...] = jnp.sin(fr * (dot_hi(hid, w2_ref[...]) + b2_ref[...]))

    hid = hid_s[...]
    t = z[:, 0:1]
    ff = dot_hi(hid, w3f_ref[...]) * jnp.exp(-t * jnp.abs(decf_ref[...]))
    fb = dot_hi(hid, w3b_ref[...]) * jnp.exp(-t * jnp.abs(decb_ref[...]))
    row = lax.broadcasted_iota(I32, (length, 1), 0)
    l1 = jnp.sum(jnp.where(row == 0, jnp.abs(ff + fb), jnp.abs(ff) + jnp.abs(fb)), axis=0, keepdims=True)
    inv = 1.0 / l1
    se, so = _split_rows((ff + fb) * inv, slab_s)
    de, do = _split_rows((ff - fb) * inv, slab_s)
    wide = lambda a: jnp.concatenate([a] * (ct // LANES), axis=1)
    cw, sw = wide(tw_ref[0]), wide(tw_ref[1])
    c_t, s_t = c_ref[...], s_ref[...]
    bf = lambda a: a.astype(BF16)
    e_r = dot(c_t, bf(se))
    o_r, o_s = dot(c_t, bf(so)), dot(s_t, bf(so))
    t_r = cw * o_r - sw * o_s
    d_s = dot(s_t, bf(de))
    p_r, p_s = dot(c_t, bf(do)), dot(s_t, bf(do))
    t_i = -(cw * p_s + sw * p_r)
    krow = lax.broadcasted_iota(I32, (half, 1), 0)
    scale = jnp.where(krow == 0, 1.0 / (2 * length), 1.0 / length)
    lo_r, lo_i = (e_r + t_r) * scale, (t_i - d_s) * scale
    hi_r, hi_i = (e_r - t_r) * scale, (t_i + d_s) * scale

    def emit(p, lr, li, hr, hi):
        a_r, a_i = lr + hr, li - hi
        d_r, d_i = lr - hr, li + hi
        tab_ref[4 * p + 0] = a_r
        tab_ref[4 * p + 1] = a_i
        tab_ref[4 * p + 2] = d_r * cw + d_i * sw
        tab_ref[4 * p + 3] = d_i * cw - d_r * sw

    emit(0, lo_r, lo_i, hi_r, hi_i)
    emit(1, lo_r * cw - lo_i * sw, lo_r * sw + lo_i * cw, -(hi_r * cw + hi_i * sw), -(hi_i * cw - hi_r * sw))
    mid_ref[0:1, :] = _alt_sum(se) * (1.0 / length)
    mid_ref[1:2, :] = -_alt_sum(do) * (1.0 / length)


def _hyena_spectrum(z, w1, b1, w2, b2, freq, w3, decay, ctab, stab, twiddle, width, tile=TILES["hyena_spectrum"]):
    ct = tile.cols
    half = ctab.shape[0]
    kp = w1.shape[1]
    nct = width // ct
    col_f = lambda o, c: (0, o * 2 * nct + c)
    col_b = lambda o, c: (0, o * 2 * nct + nct + c)
    return pl.pallas_call(
        _hyena_spectrum_kernel,
        grid=(HYENA_ORDER, nct),
        in_specs=[_const_spec(z.shape), _const_spec(w1.shape), _const_spec(b1.shape),
                  _const_spec(w2.shape), _const_spec(b2.shape), _const_spec(freq.shape),
                  pl.BlockSpec((kp, ct), col_f), pl.BlockSpec((kp, ct), col_b),
                  pl.BlockSpec((1, ct), col_f), pl.BlockSpec((1, ct), col_b),
                  _const_spec(ctab.shape), _const_spec(stab.shape), _const_spec(twiddle.shape)],
        out_specs=[pl.BlockSpec((None, 8, half, ct), lambda o, c: (o, 0, 0, c)),
                   pl.BlockSpec((None, 2, ct), lambda o, c: (o, 0, c))],
        out_shape=[jax.ShapeDtypeStruct((HYENA_ORDER, 8, half, width), F32),
                   jax.ShapeDtypeStruct((HYENA_ORDER, 2, width), F32)],
        scratch_shapes=[pltpu.VMEM((ct // LANES, 2 * half, LANES), F32), pltpu.VMEM((2 * half, kp), F32)],
        compiler_params=_cparams(("arbitrary", "arbitrary"), tile),
        name="hyena_spectrum",
    )(z, w1, b1, w2, b2, freq, w3, w3, decay, decay, ctab, stab, twiddle)


def _hyena_conv_kernel(pv_ref, p1_ref, p2_ref, wv_ref, w1_ref, w2_ref, bv_ref, b1_ref, b2_ref,
                       tab_ref, mid_ref, skip_ref, c_ref, s_ref, y_ref,
                       slab_s, z_s, g1_s, g2_s, zb_s, gr_s, gn_s, *, rc):
    half = c_ref.shape[0]
    dot = functools.partial(jnp.dot, preferred_element_type=F32)
    for dst, p_ref, w_ref, b_ref in ((z_s, pv_ref, wv_ref, bv_ref), (g1_s, p1_ref, w1_ref, b1_ref),
                                     (g2_s, p2_ref, w2_ref, b2_ref)):
        even, odd = _split_rows(_short_conv(p_ref[...].astype(F32), w_ref[...], b_ref[...]), slab_s)
        dst[0] = even
        dst[1] = odd
    gates = (g1_s, g2_s)
    row = lax.broadcasted_iota(I32, (half, 1), 0)
    sign = jnp.where(row % 2 == 0, 1.0, -1.0)
    for o in range(HYENA_ORDER):
        zb_s[0] = z_s[0].astype(BF16)
        zb_s[1] = z_s[1].astype(BF16)
        a_e, a_o = _alt_sum(z_s[0]), _alt_sum(z_s[1])
        mid = (mid_ref[o, 0:1, :] * a_e + mid_ref[o, 1:2, :] * a_o,
               mid_ref[o, 0:1, :] * a_o - mid_ref[o, 1:2, :] * a_e)
        for r in range(half // rc):
            rows = slice(r * rc, (r + 1) * rc)
            e_r, e_s = dot(c_ref[rows, :], zb_s[0]), dot(s_ref[rows, :], zb_s[0])
            o_r, o_s = dot(c_ref[rows, :], zb_s[1]), dot(s_ref[rows, :], zb_s[1])
            for p in range(2):
                a_r, a_i = tab_ref[o, 4 * p + 0, rows, :], tab_ref[o, 4 * p + 1, rows, :]
                b_r, b_i = tab_ref[o, 4 * p + 2, rows, :], tab_ref[o, 4 * p + 3, rows, :]
                gr_s[p, rows, :] = (a_r * e_r + a_i * e_s + b_r * o_r + b_i * o_s).astype(BF16)
                gn_s[p, rows, :] = (a_r * e_s - a_i * e_r + b_r * o_s - b_i * o_r).astype(BF16)
        for p in range(2):
            for r in range(half // rc):
                rows = slice(r * rc, (r + 1) * rc)
                conv = dot(c_ref[rows, :], gr_s[p]) + dot(s_ref[rows, :], gn_s[p]) + sign[rows] * mid[p]
                z_s[p, rows, :] = gates[o][p, rows, :] * (conv + skip_ref[o] * z_s[p, rows, :])
    y_ref[...] = _merge_rows(z_s[0], z_s[1], slab_s).astype(y_ref.dtype)


def _hyena_conv(p3, col0, conv_w, conv_b, tab, mid, skip, ctab, stab, width, tile=TILES["hyena_conv"]):
    rc, ct = tile.rows, tile.cols
    bsz, length, _ = p3.shape
    half = length // 2
    nct = width // ct
    c0 = col0 // ct

    def pspec(part):
        return pl.BlockSpec((None, length, ct), lambda c, b: (b, 0, c0 + part * nct + c))

    def wspec(rows, part):
        return pl.BlockSpec((rows, ct), lambda c, b: (0, part * nct + c))

    scratch = ([pltpu.VMEM((ct // LANES, length, LANES), F32)] + [pltpu.VMEM((2, half, ct), F32)] * 3
               + [pltpu.VMEM((2, half, ct), BF16)] * 3)
    return pl.pallas_call(
        functools.partial(_hyena_conv_kernel, rc=rc),
        grid=(nct, bsz),
        in_specs=[pspec(0), pspec(1), pspec(2),
                  wspec(3, 0), wspec(3, 1), wspec(3, 2), wspec(1, 0), wspec(1, 1), wspec(1, 2),
                  pl.BlockSpec((HYENA_ORDER, 8, half, ct), lambda c, b: (0, 0, 0, c),
                               pipeline_mode=pl.Buffered(1)),
                  pl.BlockSpec((HYENA_ORDER, 2, ct), lambda c, b: (0, 0, c)),
                  pl.BlockSpec((HYENA_ORDER, 1, ct), lambda c, b: (0, 0, c)),
                  _const_spec(ctab.shape), _const_spec(stab.shape)],
        out_specs=pl.BlockSpec((None, length, ct), lambda c, b: (b, 0, c)),
        out_shape=jax.ShapeDtypeStruct((bsz, length, width), BF16),
        scratch_shapes=scratch,
        compiler_params=_cparams(("parallel", "parallel"), tile),
        name="hyena_conv",
    )(p3, p3, p3, conv_w, conv_w, conv_w, conv_b, conv_b, conv_b, tab, mid, skip, ctab, stab)


def _gate_prep_kernel(ic_ref, fc_ref, ir_ref, fr_ref, col_ref, row_ref, *, chunk):
    length, nch = ic_ref.shape
    nh = nch // 2
    r = lax.broadcasted_iota(I32, (chunk, chunk), 0)
    c = lax.broadcasted_iota(I32, (chunk, chunk), 1)
    lower = (c <= r).astype(F32)
    upper = (c >= r).astype(F32)
    dot_hi = functools.partial(jnp.dot, precision=HIGHEST, preferred_element_type=F32)
    fwd_col = lax.broadcasted_iota(I32, (chunk, nch), 1) < nh
    fwd_row = lax.broadcasted_iota(I32, (nch, chunk), 0) < nh
    pos = lax.broadcasted_iota(I32, (chunk, nch), 0)
    for ch in range(length // chunk):
        rows = slice(ch * chunk, (ch + 1) * chunk)
        lf = _log_sigmoid(fc_ref[rows, :])
        b = jnp.where(fwd_col, dot_hi(lower, lf), dot_hi(upper, lf))
        u = ic_ref[rows, :] - b
        cm_f, cm_b = u, u
        shift = 1
        while shift < chunk:
            cm_f = jnp.maximum(cm_f, jnp.where(pos >= shift, pltpu.roll(cm_f, shift, axis=0), -jnp.inf))
            cm_b = jnp.maximum(cm_b, jnp.where(pos + shift < chunk, pltpu.roll(cm_b, chunk - shift, axis=0), -jnp.inf))
            shift *= 2
        col_ref[0, rows, :] = b
        col_ref[1, rows, :] = u
        col_ref[2, rows, :] = jnp.where(fwd_col, cm_f, cm_b)
        lfr = _log_sigmoid(fr_ref[:, rows])
        b_r = jnp.where(fwd_row, dot_hi(lfr, upper), dot_hi(lfr, lower))
        row_ref[:, rows] = ir_ref[:, rows] - b_r


def _gate_prep(gates3, tile=TILES["mlstm_gate_prep"]):
    chunk = tile.rows
    bsz, length, nch4 = gates3.shape
    nh = nch4 // 4
    i_col = jnp.concatenate([gates3[..., :nh], gates3[..., 2 * nh:3 * nh]], axis=-1)
    f_col = jnp.concatenate([gates3[..., nh:2 * nh], gates3[..., 3 * nh:]], axis=-1)
    nch = 2 * nh
    cspec = pl.BlockSpec((None, length, nch), lambda b: (b, 0, 0))
    rspec = pl.BlockSpec((None, nch, length), lambda b: (b, 0, 0))
    col, row = pl.pallas_call(
        functools.partial(_gate_prep_kernel, chunk=chunk),
        grid=(bsz,),
        in_specs=[cspec, cspec, rspec, rspec],
        out_specs=[pl.BlockSpec((None, 3, length, nch), lambda b: (b, 0, 0, 0)), rspec],
        out_shape=[jax.ShapeDtypeStruct((bsz, 3, length, nch), F32),
                   jax.ShapeDtypeStruct((bsz, nch, length), F32)],
        compiler_params=_cparams(("parallel",), tile),
        name="mlstm_gate_prep",
    )(i_col, f_col, jnp.swapaxes(i_col, 1, 2), jnp.swapaxes(f_col, 1, 2))
    col = col.reshape(bsz, 3, length, 2, nh).transpose(0, 4, 2, 3, 1).reshape(bsz, nh, length, 6)
    row = row.reshape(bsz, 2, nh, length).transpose(0, 2, 1, 3)
    return col, row


def _mlstm_kernel(pq_ref, pk_ref, pv_ref, po_ref, cwq_ref, cwk_ref, cbq_ref, cbk_ref, ng_ref,
                  gcol_ref, grow_ref, y_ref, q_s, k_s, v1_s, tab_s, hf_s, hb_s, *, chunk):
    length, dk = pq_ref.shape
    nc = length // chunk
    rep = chunk // LANES

    def conv_silu(p_ref, w_ref, b_ref):
        c = _short_conv(p_ref[...].astype(F32), w_ref[...], b_ref[...])
        return c * jax.nn.sigmoid(c)

    q_s[...] = (conv_silu(pq_ref, cwq_ref, cbq_ref) * (dk ** -0.5)).astype(BF16)
    k_s[...] = conv_silu(pk_ref, cwk_ref, cbk_ref).astype(BF16)
    v1_s[:, :dk] = pv_ref[...]
    v1_s[:, dk:] = jnp.ones((length, dk), BF16)
    for t in range(tab_s.shape[0]):
        tab_s[t] = jnp.broadcast_to(gcol_ref[:, t:t + 1], (length, LANES))

    row_i = lax.broadcasted_iota(I32, (chunk, chunk), 0)
    col_i = lax.broadcasted_iota(I32, (chunk, chunk), 1)
    wide = lambda a: jnp.concatenate([a] * rep, axis=1)
    both = lambda a: jnp.concatenate([a, a], axis=1)

    def step(c, reverse, state, m):
        d = 1 if reverse else 0
        rows = slice(c * chunk, (c + 1) * chunk)
        edge = c * chunk if reverse else (c + 1) * chunk - 1
        q, k, v1 = q_s[rows, :], k_s[rows, :], v1_s[rows, :]
        b_t, u_t, cm_t = tab_s[3 * d, rows, :], tab_s[3 * d + 1, rows, :], tab_s[3 * d + 2, rows, :]
        u_row = grow_ref[d:d + 1, rows]
        g = tab_s[3 * d, edge:edge + 1, :]
        u_max = tab_s[3 * d + 2, edge:edge + 1, :]
        mask = (col_i >= row_i) if reverse else (col_i <= row_i)
        mm = jnp.maximum(cm_t, m)
        p = jnp.exp(jnp.where(mask, u_row - wide(mm), -jnp.inf))
        qk = lax.dot_general(q, k, (((1,), (1,)), ((), ())), preferred_element_type=F32)
        intra = jnp.dot((qk * p).astype(BF16), v1, preferred_element_type=F32)
        inter = jnp.dot(q, state.astype(BF16), preferred_element_type=F32)
        nd = intra + both(jnp.exp(m - mm)) * inter
        h = nd[:, :dk] / jnp.maximum(jnp.abs(nd[:, dk:]), jnp.exp(-(b_t + mm)))
        kw = (k.astype(F32) * jnp.exp(u_t - u_max)).astype(BF16)
        upd = lax.dot_general(kw, v1, (((0,), (0,)), ((), ())), preferred_element_type=F32)
        m_loc = g + u_max
        m_new = jnp.maximum(g + m, m_loc)
        state = both(jnp.exp(g + m - m_new)) * state + both(jnp.exp(m_loc - m_new)) * upd
        return h, state, m_new

    zero = (jnp.zeros((dk, 2 * dk), F32), jnp.zeros((1, LANES), F32))
    st_f, m_f = zero
    st_b, m_b = zero
    for c in range(nc):
        h, st_f, m_f = step(c, False, st_f, m_f)
        hf_s[c * chunk:(c + 1) * chunk, :] = h
        cb = nc - 1 - c
        h, st_b, m_b = step(cb, True, st_b, m_b)
        hb_s[cb * chunk:(cb + 1) * chunk, :] = h

    ht = hf_s[...] + hb_s[...]
    y_ref[...] = (_rms(ht, ng_ref[...]) * jax.nn.sigmoid(po_ref[...].astype(F32))).astype(y_ref.dtype)


def _mlstm(p3, col_q, col_k, col_v, col_o, conv_w, conv_b, norm_g, gcol, grow, tile=TILES["mlstm"]):
    chunk = tile.rows
    bsz, length, _ = p3.shape
    nh, dh = MLSTM_HEADS, norm_g.shape[1] // MLSTM_HEADS
    assert dh == LANES and chunk % LANES == 0

    def pspec(col):
        return pl.BlockSpec((None, length, dh), lambda b, h: (b, 0, col // dh + h))

    def wspec(rows, part):
        return pl.BlockSpec((rows, dh), lambda b, h: (0, part * nh + h))

    return pl.pallas_call(
        functools.partial(_mlstm_kernel, chunk=chunk),
        grid=(bsz, nh),
        in_specs=[pspec(col_q), pspec(col_k), pspec(col_v), pspec(col_o),
                  wspec(3, 0), wspec(3, 1), wspec(1, 0), wspec(1, 1),
                  pl.BlockSpec((1, dh), lambda b, h: (0, h)),
                  pl.BlockSpec((None, None, length, 6), lambda b, h: (b, h, 0, 0)),
                  pl.BlockSpec((None, None, 2, length), lambda b, h: (b, h, 0, 0))],
        out_specs=pl.BlockSpec((None, length, dh), lambda b, h: (b, 0, h)),
        out_shape=jax.ShapeDtypeStruct((bsz, length, nh * dh), BF16),
        scratch_shapes=[pltpu.VMEM((length, dh), BF16), pltpu.VMEM((length, dh), BF16),
                        pltpu.VMEM((length, 2 * dh), BF16), pltpu.VMEM((6, length, LANES), F32),
                        pltpu.VMEM((length, dh), F32), pltpu.VMEM((length, dh), F32)],
        compiler_params=_cparams(("parallel", "parallel"), tile),
        name="mlstm",
    )(p3, p3, p3, p3, conv_w, conv_w, conv_b, conv_b, norm_g, gcol, grow)


def _merge_kernel(yh_ref, ym_ref, gh_ref, gm_ref, x_ref, wa_ref, wb_ref, wo_ref, o_ref):
    a = jnp.dot(yh_ref[...], wa_ref[...], preferred_element_type=F32)
    b = jnp.dot(ym_ref[...], wb_ref[...], preferred_element_type=F32)
    t = jax.nn.sigmoid(gh_ref[...].astype(F32)) * a + jax.nn.sigmoid(gm_ref[...].astype(F32)) * b
    o_ref[...] = x_ref[...] + jnp.dot(t.astype(BF16), wo_ref[...], preferred_element_type=F32)


def _merge(y_hy, y_ml, p_big, x2, w_a, w_b, w_o, tile=TILES["merge"]):
    tm = tile.rows
    n, d = x2.shape
    wh, wm = y_hy.shape[1], y_ml.shape[1]
    return pl.pallas_call(
        _merge_kernel,
        grid=(n // tm,),
        in_specs=[pl.BlockSpec((tm, wh), lambda i: (i, 0)),
                  pl.BlockSpec((tm, wm), lambda i: (i, 0)),
                  pl.BlockSpec((tm, d), lambda i: (i, 0)),
                  pl.BlockSpec((tm, d), lambda i: (i, 1)),
                  pl.BlockSpec((tm, d), lambda i: (i, 0)),
                  _const_spec(w_a.shape), _const_spec(w_b.shape), _const_spec(w_o.shape)],
        out_specs=pl.BlockSpec((tm, d), lambda i: (i, 0)),
        out_shape=jax.ShapeDtypeStruct((n, d), F32),
        compiler_params=_cparams(("parallel",), tile),
        name="branch_merge",
    )(y_hy, y_ml, p_big, p_big, x2, w_a, w_b, w_o)


def _dense_ffn_kernel(x_ref, g_ref, wg_ref, wu_ref, wd_ref, o_ref, h_s):
    @pl.when(pl.program_id(1) == 0)
    def _():
        x = x_ref[...]
        h_s[...] = _rms(x, g_ref[...]).astype(BF16)
        o_ref[...] = x

    h = h_s[...]
    gt = jnp.dot(h, wg_ref[...].astype(BF16), preferred_element_type=F32)
    up = jnp.dot(h, wu_ref[...].astype(BF16), preferred_element_type=F32)
    act = (gt * jax.nn.sigmoid(gt) * up).astype(BF16)
    o_ref[...] += jnp.dot(act, wd_ref[...].astype(BF16), preferred_element_type=F32)


def _dense_ffn(x2, g, w_gate, w_up, w_down, tile=TILES["dense"]):
    tm, tf = tile.rows, tile.cols
    n, d = x2.shape
    f = w_gate.shape[1]
    return pl.pallas_call(
        _dense_ffn_kernel,
        grid=(n // tm, f // tf),
        in_specs=[pl.BlockSpec((tm, d), lambda i, j: (i, 0)),
                  pl.BlockSpec((1, d), lambda i, j: (0, 0)),
                  pl.BlockSpec((d, tf), lambda i, j: (0, j)),
                  pl.BlockSpec((d, tf), lambda i, j: (0, j)),
                  pl.BlockSpec((tf, d), lambda i, j: (j, 0))],
        out_specs=pl.BlockSpec((tm, d), lambda i, j: (i, 0)),
        out_shape=jax.ShapeDtypeStruct((n, d), F32),
        scratch_shapes=[pltpu.VMEM((tm, d), BF16)],
        compiler_params=_cparams(("parallel", "arbitrary"), tile),
        name="dense_swiglu",
    )(x2, g, w_gate, w_up, w_down)


def _store_rows_3d(ref3, val2):
    for s in range(ref3.shape[1]):
        ref3[:, s, :] = val2[:, s * LANES:(s + 1) * LANES]


def _router_kernel(x_ref, g_ref, rwh_ref, rwl_ref, rb_ref, h_ref, e_ref, gate_ref, rank_ref, cnt_ref, base_s):
    @pl.when(pl.program_id(0) == 0)
    def _():
        base_s[...] = jnp.zeros_like(base_s)

    hn = _rms(x_ref[...], g_ref[...])
    _store_rows_3d(h_ref, hn)
    logits = _dot_split(hn, rwh_ref[...], rwl_ref[...]) + rb_ref[...]
    lane = lax.broadcasted_iota(I32, logits.shape, 1)
    logits = jnp.where(lane < N_EXPERTS, logits, -jnp.inf)
    m1 = jnp.max(logits, axis=-1, keepdims=True)
    i1 = jnp.min(jnp.where(logits == m1, lane, LANES), axis=-1, keepdims=True)
    rest = jnp.where(lane == i1, -jnp.inf, logits)
    m2 = jnp.max(rest, axis=-1, keepdims=True)
    i2 = jnp.min(jnp.where(rest == m2, lane, LANES), axis=-1, keepdims=True)
    e = jnp.exp(m2 - m1)
    slot = lax.broadcasted_iota(I32, e_ref.shape, 1)
    e_ref[...] = jnp.where(slot == 0, i1, i2)
    gate_ref[...] = jnp.where(slot == 0, 1.0 / (1.0 + e), e / (1.0 + e))
    tm = logits.shape[0]
    oh1 = (lane == i1).astype(F32)
    oh2 = (lane == i2).astype(F32)
    both = oh1 + oh2
    earlier = (lax.broadcasted_iota(I32, (tm, tm), 1) < lax.broadcasted_iota(I32, (tm, tm), 0)).astype(BF16)
    before = jnp.dot(earlier, both.astype(BF16), preferred_element_type=F32) + base_s[...]
    r1 = jnp.sum(before * oh1, axis=-1, keepdims=True)
    r2 = jnp.sum(before * oh2, axis=-1, keepdims=True)
    rank_ref[...] = jnp.where(slot == 0, r1, r2).astype(I32)
    base_s[...] += jnp.sum(both, axis=0, keepdims=True)
    cnt_ref[...] = base_s[...].astype(I32)


def _router(x2, g, rw_pad, rb_pad, tile=TILES["router"]):
    tm = tile.rows
    n, d = x2.shape
    s = d // LANES
    rw_hi, rw_lo = _split_bf16(rw_pad)
    return pl.pallas_call(
        _router_kernel,
        grid=(n // tm,),
        in_specs=[pl.BlockSpec((tm, d), lambda i: (i, 0)),
                  pl.BlockSpec((1, d), lambda i: (0, 0)),
                  pl.BlockSpec((d, LANES), lambda i: (0, 0)),
                  pl.BlockSpec((d, LANES), lambda i: (0, 0)),
                  pl.BlockSpec((1, LANES), lambda i: (0, 0))],
        out_specs=[pl.BlockSpec((tm, s, LANES), lambda i: (i, 0, 0)),
                   pl.BlockSpec((tm, TOP_K), lambda i: (i, 0)),
                   pl.BlockSpec((tm, TOP_K), lambda i: (i, 0)),
                   pl.BlockSpec((tm, TOP_K), lambda i: (i, 0)),
                   pl.BlockSpec((1, LANES), lambda i: (0, 0))],
        out_shape=[jax.ShapeDtypeStruct((n, s, LANES), F32),
                   jax.ShapeDtypeStruct((n, TOP_K), I32),
                   jax.ShapeDtypeStruct((n, TOP_K), F32),
                   jax.ShapeDtypeStruct((n, TOP_K), I32),
                   jax.ShapeDtypeStruct((1, LANES), I32)],
        scratch_shapes=[pltpu.VMEM((1, LANES), F32)],
        compiler_params=_cparams(("arbitrary",), tile),
        name="moe_router",
    )(x2, g, rw_hi, rw_lo, rb_pad)


def _slab_pitch(s):
    return s if (s // SUBLANES) % 2 == 1 else s + SUBLANES


def _row_gather(src_hbm, idx_ref, idx_base, dst, sem, rows):
    slab = src_hbm.shape[1]

    def copy(r):
        return pltpu.make_async_copy(src_hbm.at[idx_ref[idx_base + r]], dst.at[r, pl.ds(0, slab)], sem)

    def start_one(r, carry):
        copy(r).start()
        return carry

    def wait_one(r, carry):
        copy(r).wait()
        return carry

    return (lambda: lax.fori_loop(0, rows, start_one, 0, unroll=8),
            lambda: lax.fori_loop(0, rows, wait_one, 0, unroll=8))


def _expert_ffn_kernel(be_ref, nu_ref, tok_ref, h_hbm, wg_ref, wu_ref, wd_ref, o_ref,
                       xg_s, x2_s, h_s, acc_s, sems):
    i, j = pl.program_id(0), pl.program_id(1)
    blk = xg_s.shape[1]
    n_used = nu_ref[0]
    last = pl.num_programs(1) - 1
    mid = pl.num_programs(1) // 2
    used = i < n_used
    has_next = i + 1 < n_used
    both = jnp.logical_and

    def gather(block):
        slot = block % 2
        return _row_gather(h_hbm, tok_ref, block * blk, xg_s.at[slot], sems.at[slot], blk)

    def unpack(block):
        slot = block % 2
        for s in range(h_hbm.shape[1]):
            x2_s[:, s * LANES:(s + 1) * LANES] = xg_s[slot, :, s, :]
        h_s[slot] = x2_s[...].astype(BF16)

    def partial_out():
        h = h_s[i % 2]
        gt = jnp.dot(h, wg_ref[...], preferred_element_type=F32)
        up = jnp.dot(h, wu_ref[...], preferred_element_type=F32)
        act = (gt * jax.nn.sigmoid(gt) * up).astype(BF16)
        return jnp.dot(act, wd_ref[...], preferred_element_type=F32)

    @pl.when(both(i == 0, j == 0))
    def _():
        start, wait = gather(0)
        start()
        wait()
        unpack(0)

    @pl.when(both(j == 0, has_next))
    def _():
        gather(i + 1)[0]()

    @pl.when(both(used, j == 0))
    def _():
        acc_s[...] = partial_out()

    @pl.when(both(used, both(both(j > 0, j < last), jnp.logical_or(j != mid, jnp.logical_not(has_next)))))
    def _():
        acc_s[...] += partial_out()

    @pl.when(both(both(used, has_next), j == mid))
    def _():
        gather(i + 1)[1]()
        unpack(i + 1)
        acc_s[...] += partial_out()

    @pl.when(both(used, j == last))
    def _():
        res = acc_s[...] + partial_out()
        for s in range(o_ref.shape[1]):
            o_ref[:, s, :] = res[:, s * LANES:(s + 1) * LANES]

    @pl.when(both(jnp.logical_not(used), j == last))
    def _():
        o_ref[...] = jnp.zeros_like(o_ref)


def _expert_ffn(h3, slot_tok, block_e, n_used, w_gate, w_up, w_down, tile=TILES["expert"]):
    blk, tf = tile.rows, tile.cols
    ns = slot_tok.shape[0]
    s = h3.shape[1]
    d = s * LANES
    f = w_gate.shape[2]
    nj = f // tf
    assert nj >= 3

    def jj(i, j, nu):
        return jnp.where(i < nu[0], j, nj - 1)

    return pl.pallas_call(
        _expert_ffn_kernel,
        grid_spec=pltpu.PrefetchScalarGridSpec(
            num_scalar_prefetch=3,
            grid=(ns // blk, nj),
            in_specs=[pl.BlockSpec(memory_space=pl.ANY),
                      pl.BlockSpec((None, d, tf), lambda i, j, be, nu, tok: (be[i], 0, jj(i, j, nu))),
                      pl.BlockSpec((None, d, tf), lambda i, j, be, nu, tok: (be[i], 0, jj(i, j, nu))),
                      pl.BlockSpec((None, tf, d), lambda i, j, be, nu, tok: (be[i], jj(i, j, nu), 0))],
            out_specs=pl.BlockSpec((blk, s, LANES), lambda i, j, be, nu, tok: (i, 0, 0)),
            scratch_shapes=[pltpu.VMEM((2, blk, _slab_pitch(s), LANES), F32), pltpu.VMEM((blk, d), F32),
                            pltpu.VMEM((2, blk, d), BF16), pltpu.VMEM((blk, d), F32),
                            pltpu.SemaphoreType.DMA((2,))],
        ),
        out_shape=jax.ShapeDtypeStruct((ns, s, LANES), F32),
        compiler_params=_cparams(("arbitrary", "arbitrary"), tile),
        name="expert_swiglu",
    )(block_e, n_used, slot_tok, h3, w_gate, w_up, w_down)


def _combine_kernel(dest_ref, x_ref, gate_ref, y_hbm, g_ref, o_ref, yg_s, sems, *, final_norm):
    i, n_tiles = pl.program_id(0), pl.num_programs(0)
    tm = x_ref.shape[0]

    def gather(tile):
        slot = tile % 2
        parts = [_row_gather(y_hbm, dest_ref, k * n_tiles * tm + tile * tm, yg_s.at[slot, k], sems.at[slot], tm)
                 for k in range(TOP_K)]
        return (lambda: [p[0]() for p in parts]), (lambda: [p[1]() for p in parts])

    @pl.when(i == 0)
    def _():
        gather(0)[0]()

    @pl.when(i + 1 < n_tiles)
    def _():
        gather(i + 1)[0]()

    gather(i)[1]()
    slot = i % 2
    gates = [jnp.broadcast_to(gate_ref[:, k:k + 1], (tm, LANES)) for k in range(TOP_K)]
    for s in range(y_hbm.shape[1]):
        cols = slice(s * LANES, (s + 1) * LANES)
        acc = x_ref[:, cols]
        for k in range(TOP_K):
            acc = acc + gates[k] * yg_s[slot, k, :, s, :]
        o_ref[:, cols] = acc
    if final_norm:
        o_ref[...] = _rms(o_ref[...], g_ref[...])


def _combine(x2, gate, yb, dest_by_k, g, final_norm, tile=TILES["combine"]):
    tm = tile.rows
    n, d = x2.shape
    s = d // LANES
    return pl.pallas_call(
        functools.partial(_combine_kernel, final_norm=final_norm),
        grid_spec=pltpu.PrefetchScalarGridSpec(
            num_scalar_prefetch=1,
            grid=(n // tm,),
            in_specs=[pl.BlockSpec((tm, d), lambda i, dest: (i, 0)),
                      pl.BlockSpec((tm, TOP_K), lambda i, dest: (i, 0)),
                      pl.BlockSpec(memory_space=pl.ANY),
                      pl.BlockSpec((1, d), lambda i, dest: (0, 0))],
            out_specs=pl.BlockSpec((tm, d), lambda i, dest: (i, 0)),
            scratch_shapes=[pltpu.VMEM((2, TOP_K, tm, _slab_pitch(s), LANES), F32),
                            pltpu.SemaphoreType.DMA((2,))],
        ),
        out_shape=jax.ShapeDtypeStruct((n, d), F32),
        compiler_params=_cparams(("arbitrary",), tile),
        name="moe_combine",
    )(dest_by_k, x2, gate, yb, g)


def _final_norm_kernel(x_ref, g_ref, o_ref):
    o_ref[...] = _rms(x_ref[...], g_ref[...])


def _final_norm(x2, g, tile=TILES["final_norm"]):
    tm = tile.rows
    n, d = x2.shape
    return pl.pallas_call(
        _final_norm_kernel,
        grid=(n // tm,),
        in_specs=[pl.BlockSpec((tm, d), lambda i: (i, 0)), pl.BlockSpec((1, d), lambda i: (0, 0))],
        out_specs=pl.BlockSpec((tm, d), lambda i: (i, 0)),
        out_shape=jax.ShapeDtypeStruct((n, d), F32),
        compiler_params=_cparams(("parallel",), tile),
        name="final_norm",
    )(x2, g)


def _routing_tables(top_e, rank, counts, blk):
    n = top_e.shape[0]
    nk = n * TOP_K
    flat_e = top_e.reshape(-1)
    padded = (counts + blk - 1) // blk * blk
    p_end = jnp.cumsum(padded)
    p_start = p_end - padded
    dest = (p_start[flat_e] + rank.reshape(-1)).astype(I32)
    n_blocks = -(-nk // blk) + N_EXPERTS
    flat_tok = jnp.arange(nk, dtype=I32) // TOP_K
    slot_tok = jnp.zeros((n_blocks * blk,), I32).at[dest].set(flat_tok, unique_indices=True)
    n_used = (p_end[-1] // blk).astype(I32)
    blocks = jnp.arange(n_blocks, dtype=I32)
    block_e = jnp.sum((p_end[None, :] <= (blocks * blk)[:, None]).astype(I32), axis=1)
    block_e = jnp.minimum(block_e, N_EXPERTS - 1)
    block_e = jnp.where(blocks < n_used, block_e, block_e[n_used - 1])
    dest_by_k = dest.reshape(n, TOP_K).T.reshape(-1)
    return slot_tok, block_e, n_used.reshape(1), dest_by_k


def _moe_layer(x2, norm_g, router_w, router_b, w_gate, w_up, w_down, out_g, final_norm):
    n, d = x2.shape
    rw = jnp.zeros((d, LANES), F32).at[:, :N_EXPERTS].set(router_w)
    rb = jnp.zeros((1, LANES), F32).at[0, :N_EXPERTS].set(router_b)
    h, top_e, gate, rank, counts = _router(x2, norm_g[None, :], rw, rb)
    slot_tok, block_e, n_used, dest_by_k = _routing_tables(top_e, rank, counts[0, :N_EXPERTS], TILES["expert"].rows)
    yb = _expert_ffn(h, slot_tok, block_e, n_used, w_gate.astype(BF16), w_up.astype(BF16), w_down.astype(BF16))
    return _combine(x2, gate, yb, dest_by_k, out_g[None, :], final_norm)


def _mixer_layer(x2, bsz, length, norm_g, w_in, b_in, hy_conv_w, hy_conv_b, hy_w1, hy_b1, hy_w2, hy_b2,
                 hy_w3, hy_freq, hy_decay, hy_skip, ml_conv_w, ml_conv_b, ml_norm_g, w_a, w_b, w_o,
                 ctab, stab, twiddle, zpos):
    n, d = x2.shape
    wh = hy_skip.shape[1]
    wm = ml_norm_g.shape[0]
    off_qk = (HYENA_ORDER + 1) * wh
    off_v = off_qk + 2 * wm
    off_o = off_v + wm
    off_gates = off_o + wm
    off_br = off_gates + 4 * MLSTM_HEADS
    w_cat = jnp.concatenate([w_in[:, off_br:], w_in[:, :off_gates]], axis=1).astype(BF16)
    b_cat = jnp.concatenate([b_in[off_br:], b_in[:off_gates]])[None, :]
    col_hy = 2 * d
    col_q = col_hy + off_qk
    col_k = col_q + wm
    col_v = col_hy + off_v
    col_o = col_hy + off_o

    p_big, gates = _in_proj(x2, norm_g[None, :], w_in[:, off_gates:off_br], b_in[None, off_gates:off_br],
                            w_cat, b_cat)
    p3 = p_big.reshape(bsz, length, p_big.shape[1])

    kp = LANES
    pad2 = lambda a, r, c: jnp.zeros((r, c), F32).at[:a.shape[0], :a.shape[1]].set(a)
    zp = pad2(zpos, length, kp)
    htab, hmid = _hyena_spectrum(
        zp, pad2(hy_w1, kp, kp), pad2(hy_b1[None, :], 1, kp), pad2(hy_w2, kp, kp), pad2(hy_b2[None, :], 1, kp),
        pad2(hy_freq[None, :], 1, kp), pad2(hy_w3, kp, hy_w3.shape[1]), hy_decay[None, :],
        ctab, stab, twiddle, wh)
    y_hy = _hyena_conv(p3, col_hy, hy_conv_w, hy_conv_b[None, :], htab, hmid, hy_skip[:, None, :], ctab, stab, wh)

    gcol, grow = _gate_prep(gates.reshape(bsz, length, 4 * MLSTM_HEADS))
    y_ml = _mlstm(p3, col_q, col_k, col_v, col_o, ml_conv_w, ml_conv_b[None, :], ml_norm_g[None, :],
                  gcol, grow)

    return _merge(y_hy.reshape(n, wh), y_ml.reshape(n, wm), p_big, x2,
                  w_a.astype(BF16), w_b.astype(BF16), w_o.astype(BF16))


def kernel(x, mix_norm_g, mix_w_in, mix_b_in, hy_conv_w, hy_conv_b, hy_filt_w1, hy_filt_b1, hy_filt_w2,
           hy_filt_b2, hy_filt_w3, hy_filt_freq, hy_filt_decay, hy_skip, ml_conv_w, ml_conv_b, ml_norm_g,
           mix_w_a, mix_w_b, mix_w_o, ffn_norm_g, dense_w_gate, dense_w_up, dense_w_down, moe_router_w,
           moe_router_b, moe_w_gate, moe_w_up, moe_w_down, final_norm_g):
    bsz, length, d = x.shape
    depth = mix_norm_g.shape[0]
    x2 = x.reshape(bsz * length, d)
    ctab, stab, twiddle = _dft_tables(length // 2)
    zpos = _hyena_positions(length)
    normed = False
    for layer in range(depth):
        x2 = _mixer_layer(x2, bsz, length, mix_norm_g[layer], mix_w_in[layer], mix_b_in[layer],
                          hy_conv_w[layer], hy_conv_b[layer], hy_filt_w1[layer], hy_filt_b1[layer],
                          hy_filt_w2[layer], hy_filt_b2[layer], hy_filt_w3[layer], hy_filt_freq[layer],
                          hy_filt_decay[layer], hy_skip[layer], ml_conv_w[layer], ml_conv_b[layer],
                          ml_norm_g[layer], mix_w_a[layer], mix_w_b[layer], mix_w_o[layer], ctab, stab, twiddle,
                          zpos)
        j = layer // 2
        if layer % 2 == 0:
            x2 = _dense_ffn(x2, ffn_norm_g[layer][None, :], dense_w_gate[j], dense_w_up[j], dense_w_down[j])
        else:
            normed = layer == depth - 1
            x2 = _moe_layer(x2, ffn_norm_g[layer], moe_router_w[j], moe_router_b[j], moe_w_gate[j],
                            moe_w_up[j], moe_w_down[j], final_norm_g, normed)
    if not normed:
        x2 = _final_norm(x2, final_norm_g[None, :])
    return x2.reshape(bsz, length, d)
```

```python
import functools
import math
from typing import NamedTuple

import jax
import jax.numpy as jnp
import numpy as np
from jax import lax
from jax.experimental import pallas as pl
from jax.experimental.pallas import tpu as pltpu

F32, BF16, I32 = jnp.float32, jnp.bfloat16, jnp.int32
HIGHEST = lax.Precision.HIGHEST

HYENA_ORDER = 2
HYENA_POS_BANDS = 16
MLSTM_HEADS = 8
N_EXPERTS = 8
TOP_K = 2
EPS = 1e-6

V7X_VMEM_BYTES = 64 * 1024 * 1024
LANES = 128
SUBLANES = 8
MXU_DIM = 256


class _Tile(NamedTuple):
    rows: int
    cols: int
    vmem_mib: int


TILES = dict(
    in_proj=_Tile(1024, 4 * MXU_DIM, 48),
    hyena_spectrum=_Tile(0, MXU_DIM, 48),
    hyena_conv=_Tile(512, MXU_DIM, 56),
    mlstm_gate_prep=_Tile(256, 0, 32),
    mlstm=_Tile(256, 0, 40),
    merge=_Tile(256, 0, 48),
    dense=_Tile(1024, 256, 56),
    router=_Tile(512, 0, 32),
    expert=_Tile(512, 512, 52),
    combine=_Tile(512, 0, 48),
    final_norm=_Tile(512, 0, 32),
)


def _cparams(semantics, tile):
    assert tile.vmem_mib * 1024 * 1024 <= V7X_VMEM_BYTES
    return pltpu.CompilerParams(dimension_semantics=semantics, vmem_limit_bytes=tile.vmem_mib * 1024 * 1024)


def _const_spec(shape):
    nd = len(shape)
    return pl.BlockSpec(shape, lambda *_: (0,) * nd, pipeline_mode=pl.Buffered(1))


def _rms(x, g):
    return x * lax.rsqrt(jnp.mean(x * x, axis=-1, keepdims=True) + EPS) * g


def _shift_rows(u, direction):
    n = u.shape[0]
    row = lax.broadcasted_iota(I32, u.shape, 0)
    if direction > 0:
        return jnp.where(row == 0, 0.0, pltpu.roll(u, 1, axis=0))
    return jnp.where(row == n - 1, 0.0, pltpu.roll(u, n - 1, axis=0))


def _short_conv(u, w, b):
    return b + _shift_rows(u, 1) * w[0:1] + u * w[1:2] + _shift_rows(u, -1) * w[2:3]


def _log_sigmoid(x):
    return jnp.minimum(x, 0.0) - jnp.log1p(jnp.exp(-jnp.abs(x)))


def _dot_split(a, w_hi, w_lo):
    a_hi = a.astype(BF16)
    a_lo = (a - a_hi.astype(F32)).astype(BF16)
    return (jnp.dot(a_hi, w_hi, preferred_element_type=F32) + jnp.dot(a_lo, w_hi, preferred_element_type=F32)
            + jnp.dot(a_hi, w_lo, preferred_element_type=F32))


def _split_bf16(w):
    w_hi = w.astype(BF16)
    return w_hi, (w - w_hi.astype(F32)).astype(BF16)


def _in_proj_kernel(x_ref, g_ref, wg_ref, bg_ref, w_ref, b_ref, o_ref, gates_ref, h_s):
    @pl.when(pl.program_id(1) == 0)
    def _():
        h_s[...] = _rms(x_ref[...], g_ref[...]).astype(BF16)
        gates_ref[...] = jnp.dot(h_s[...], wg_ref[...], preferred_element_type=F32) + bg_ref[...]

    acc = jnp.dot(h_s[...], w_ref[...], preferred_element_type=F32) + b_ref[...]
    o_ref[...] = acc.astype(o_ref.dtype)


def _in_proj(x2, g, w_gates, b_gates, w, b, tile=TILES["in_proj"]):
    tm, tn = tile.rows, tile.cols
    n, d = x2.shape
    nc = w.shape[1]
    ng = w_gates.shape[1]
    return pl.pallas_call(
        _in_proj_kernel,
        grid=(n // tm, nc // tn),
        in_specs=[pl.BlockSpec((tm, d), lambda i, j: (i, 0)),
                  pl.BlockSpec((1, d), lambda i, j: (0, 0)),
                  pl.BlockSpec((d, ng), lambda i, j: (0, 0)),
                  pl.BlockSpec((1, ng), lambda i, j: (0, 0)),
                  pl.BlockSpec((d, tn), lambda i, j: (0, j)),
                  pl.BlockSpec((1, tn), lambda i, j: (0, j))],
        out_specs=[pl.BlockSpec((tm, tn), lambda i, j: (i, j)),
                   pl.BlockSpec((tm, ng), lambda i, j: (i, 0))],
        out_shape=[jax.ShapeDtypeStruct((n, nc), BF16), jax.ShapeDtypeStruct((n, ng), F32)],
        scratch_shapes=[pltpu.VMEM((tm, d), BF16)],
        compiler_params=_cparams(("parallel", "arbitrary"), tile),
        name="in_proj",
    )(x2, g, w_gates.astype(BF16), b_gates, w, b)


def _dft_tables(half):
    k = jnp.arange(half, dtype=I32)

    def cos_sin(rows):
        ang = ((rows[:, None] * k[None, :]) % (2 * half)).astype(F32) * (math.pi / half)
        return jnp.cos(ang), jnp.sin(ang)

    split = math.gcd(half, 32)
    c1, s1 = cos_sin(jnp.arange(half // split, dtype=I32) * split)
    c0, s0 = cos_sin(jnp.arange(split, dtype=I32))
    ctab = (c1[:, None, :] * c0[None, :, :] - s1[:, None, :] * s0[None, :, :]).reshape(half, half)
    stab = (s1[:, None, :] * c0[None, :, :] + c1[:, None, :] * s0[None, :, :]).reshape(half, half)
    tw = k.astype(F32)[:, None] * (math.pi / (2 * half))
    twiddle = jnp.stack([jnp.broadcast_to(jnp.cos(tw), (half, LANES)), jnp.broadcast_to(jnp.sin(tw), (half, LANES))])
    return ctab.astype(BF16), stab.astype(BF16), twiddle


def _hyena_positions(length):
    t = np.linspace(0.0, 1.0, length)[:, None]
    n = np.arange(length, dtype=np.float64)[:, None]
    bands = np.linspace(1e-4, HYENA_POS_BANDS - 1, HYENA_POS_BANDS)[None, :]
    ang = (2.0 * np.pi / length) * n * bands
    return jnp.asarray(np.concatenate([t, np.cos(ang), -np.sin(ang)], axis=-1), dtype=F32)


def _split_rows(val, slab_s):
    half = val.shape[0] // 2
    n_slabs = slab_s.shape[0]
    for s in range(n_slabs):
        slab_s[s] = val[:, s * LANES:(s + 1) * LANES]
    pick = lambda p: jnp.concatenate([slab_s[s, pl.ds(p, half, stride=2), :] for s in range(n_slabs)], axis=1)
    return pick(0), pick(1)


def _merge_rows(even, odd, slab_s):
    half = even.shape[0]
    n_slabs = slab_s.shape[0]
    for s in range(n_slabs):
        slab_s[s, pl.ds(0, half, stride=2), :] = even[:, s * LANES:(s + 1) * LANES]
        slab_s[s, pl.ds(1, half, stride=2), :] = odd[:, s * LANES:(s + 1) * LANES]
    return jnp.concatenate([slab_s[s] for s in range(n_slabs)], axis=1)


def _alt_sum(x):
    row = lax.broadcasted_iota(I32, (x.shape[0], 1), 0)
    return jnp.sum(jnp.where(row % 2 == 0, x, -x), axis=0, keepdims=True)


def _hyena_spectrum_kernel(z_ref, w1_ref, b1_ref, w2_ref, b2_ref, fr_ref, w3f_ref, w3b_ref,
                           decf_ref, decb_ref, c_ref, s_ref, tw_ref, tab_ref, mid_ref, slab_s, hid_s):
    half = c_ref.shape[0]
    length = 2 * half
    ct = w3f_ref.shape[1]
    dot_hi = functools.partial(jnp.dot, precision=HIGHEST, preferred_element_type=F32)
    dot = functools.partial(jnp.dot, preferred_element_type=F32)
    z = z_ref[...]

    @pl.when(jnp.logical_and(pl.program_id(0) == 0, pl.program_id(1) == 0))
    def _():
        fr = fr_ref[...]
        hid = jnp.sin(fr * (dot_hi(z, w1_ref[...]) + b1_ref[...]))
        hid_s[...] = jnp.sin(fr * (dot_hi(hid, w2_ref[...]) + b2_ref[...]))

    hid = hid_s[...]
    t = z[:, 0:1]
    ff = dot_hi(hid, w3f_ref[...]) * jnp.exp(-t * jnp.abs(decf_ref[...]))
    fb = dot_hi(hid, w3b_ref[...]) * jnp.exp(-t * jnp.abs(decb_ref[...]))
    row = lax.broadcasted_iota(I32, (length, 1), 0)
    l1 = jnp.sum(jnp.where(row == 0, jnp.abs(ff + fb), jnp.abs(ff) + jnp.abs(fb)), axis=0, keepdims=True)
    inv = 1.0 / l1
    se, so = _split_rows((ff + fb) * inv, slab_s)
    de, do = _split_rows((ff - fb) * inv, slab_s)
    wide = lambda a: jnp.concatenate([a] * (ct // LANES), axis=1)
    cw, sw = wide(tw_ref[0]), wide(tw_ref[1])
    c_t, s_t = c_ref[...], s_ref[...]
    bf = lambda a: a.astype(BF16)
    e_r = dot(c_t, bf(se))
    o_r, o_s = dot(c_t, bf(so)), dot(s_t, bf(so))
    t_r = cw * o_r - sw * o_s
    d_s = dot(s_t, bf(de))
    p_r, p_s = dot(c_t, bf(do)), dot(s_t, bf(do))
    t_i = -(cw * p_s + sw * p_r)
    krow = lax.broadcasted_iota(I32, (half, 1), 0)
    scale = jnp.where(krow == 0, 1.0 / (2 * length), 1.0 / length)
    lo_r, lo_i = (e_r + t_r) * scale, (t_i - d_s) * scale
    hi_r, hi_i = (e_r - t_r) * scale, (t_i + d_s) * scale

    def emit(p, lr, li, hr, hi):
        a_r, a_i = lr + hr, li - hi
        d_r, d_i = lr - hr, li + hi
        tab_ref[4 * p + 0] = a_r
        tab_ref[4 * p + 1] = a_i
        tab_ref[4 * p + 2] = d_r * cw + d_i * sw
        tab_ref[4 * p + 3] = d_i * cw - d_r * sw

    emit(0, lo_r, lo_i, hi_r, hi_i)
    emit(1, lo_r * cw - lo_i * sw, lo_r * sw + lo_i * cw, -(hi_r * cw + hi_i * sw), -(hi_i * cw - hi_r * sw))
    mid_ref[0:1, :] = _alt_sum(se) * (1.0 / length)
    mid_ref[1:2, :] = -_alt_sum(do) * (1.0 / length)


def _hyena_spectrum(z, w1, b1, w2, b2, freq, w3, decay, ctab, stab, twiddle, width, tile=TILES["hyena_spectrum"]):
    ct = tile.cols
    half = ctab.shape[0]
    kp = w1.shape[1]
    nct = width // ct
    col_f = lambda o, c: (0, o * 2 * nct + c)
    col_b = lambda o, c: (0, o * 2 * nct + nct + c)
    return pl.pallas_call(
        _hyena_spectrum_kernel,
        grid=(HYENA_ORDER, nct),
        in_specs=[_const_spec(z.shape), _const_spec(w1.shape), _const_spec(b1.shape),
                  _const_spec(w2.shape), _const_spec(b2.shape), _const_spec(freq.shape),
                  pl.BlockSpec((kp, ct), col_f), pl.BlockSpec((kp, ct), col_b),
                  pl.BlockSpec((1, ct), col_f), pl.BlockSpec((1, ct), col_b),
                  _const_spec(ctab.shape), _const_spec(stab.shape), _const_spec(twiddle.shape)],
        out_specs=[pl.BlockSpec((None, 8, half, ct), lambda o, c: (o, 0, 0, c)),
                   pl.BlockSpec((None, 2, ct), lambda o, c: (o, 0, c))],
        out_shape=[jax.ShapeDtypeStruct((HYENA_ORDER, 8, half, width), F32),
                   jax.ShapeDtypeStruct((HYENA_ORDER, 2, width), F32)],
        scratch_shapes=[pltpu.VMEM((ct // LANES, 2 * half, LANES), F32), pltpu.VMEM((2 * half, kp), F32)],
        compiler_params=_cparams(("arbitrary", "arbitrary"), tile),
        name="hyena_spectrum",
    )(z, w1, b1, w2, b2, freq, w3, w3, decay, decay, ctab, stab, twiddle)


def _hyena_conv_kernel(pv_ref, p1_ref, p2_ref, wv_ref, w1_ref, w2_ref, bv_ref, b1_ref, b2_ref,
                       tab_ref, mid_ref, skip_ref, c_ref, s_ref, y_ref,
                       slab_s, z_s, g1_s, g2_s, zb_s, gr_s, gn_s, *, rc):
    half = c_ref.shape[0]
    dot = functools.partial(jnp.dot, preferred_element_type=F32)
    for dst, p_ref, w_ref, b_ref in ((z_s, pv_ref, wv_ref, bv_ref), (g1_s, p1_ref, w1_ref, b1_ref),
                                     (g2_s, p2_ref, w2_ref, b2_ref)):
        even, odd = _split_rows(_short_conv(p_ref[...].astype(F32), w_ref[...], b_ref[...]), slab_s)
        dst[0] = even
        dst[1] = odd
    gates = (g1_s, g2_s)
    row = lax.broadcasted_iota(I32, (half, 1), 0)
    sign = jnp.where(row % 2 == 0, 1.0, -1.0)
    for o in range(HYENA_ORDER):
        zb_s[0] = z_s[0].astype(BF16)
        zb_s[1] = z_s[1].astype(BF16)
        a_e, a_o = _alt_sum(z_s[0]), _alt_sum(z_s[1])
        mid = (mid_ref[o, 0:1, :] * a_e + mid_ref[o, 1:2, :] * a_o,
               mid_ref[o, 0:1, :] * a_o - mid_ref[o, 1:2, :] * a_e)
        for r in range(half // rc):
            rows = slice(r * rc, (r + 1) * rc)
            e_r, e_s = dot(c_ref[rows, :], zb_s[0]), dot(s_ref[rows, :], zb_s[0])
            o_r, o_s = dot(c_ref[rows, :], zb_s[1]), dot(s_ref[rows, :], zb_s[1])
            for p in range(2):
                a_r, a_i = tab_ref[o, 4 * p + 0, rows, :], tab_ref[o, 4 * p + 1, rows, :]
                b_r, b_i = tab_ref[o, 4 * p + 2, rows, :], tab_ref[o, 4 * p + 3, rows, :]
                gr_s[p, rows, :] = (a_r * e_r + a_i * e_s + b_r * o_r + b_i * o_s).astype(BF16)
                gn_s[p, rows, :] = (a_r * e_s - a_i * e_r + b_r * o_s - b_i * o_r).astype(BF16)
        for p in range(2):
            for r in range(half // rc):
                rows = slice(r * rc, (r + 1) * rc)
                conv = dot(c_ref[rows, :], gr_s[p]) + dot(s_ref[rows, :], gn_s[p]) + sign[rows] * mid[p]
                z_s[p, rows, :] = gates[o][p, rows, :] * (conv + skip_ref[o] * z_s[p, rows, :])
    y_ref[...] = _merge_rows(z_s[0], z_s[1], slab_s).astype(y_ref.dtype)


def _hyena_conv(p3, col0, conv_w, conv_b, tab, mid, skip, ctab, stab, width, tile=TILES["hyena_conv"]):
    rc, ct = tile.rows, tile.cols
    bsz, length, _ = p3.shape
    half = length // 2
    nct = width // ct
    c0 = col0 // ct

    def pspec(part):
        return pl.BlockSpec((None, length, ct), lambda c, b: (b, 0, c0 + part * nct + c))

    def wspec(rows, part):
        return pl.BlockSpec((rows, ct), lambda c, b: (0, part * nct + c))

    scratch = ([pltpu.VMEM((ct // LANES, length, LANES), F32)] + [pltpu.VMEM((2, half, ct), F32)] * 3
               + [pltpu.VMEM((2, half, ct), BF16)] * 3)
    return pl.pallas_call(
        functools.partial(_hyena_conv_kernel, rc=rc),
        grid=(nct, bsz),
        in_specs=[pspec(0), pspec(1), pspec(2),
                  wspec(3, 0), wspec(3, 1), wspec(3, 2), wspec(1, 0), wspec(1, 1), wspec(1, 2),
                  pl.BlockSpec((HYENA_ORDER, 8, half, ct), lambda c, b: (0, 0, 0, c),
                               pipeline_mode=pl.Buffered(1)),
                  pl.BlockSpec((HYENA_ORDER, 2, ct), lambda c, b: (0, 0, c)),
                  pl.BlockSpec((HYENA_ORDER, 1, ct), lambda c, b: (0, 0, c)),
                  _const_spec(ctab.shape), _const_spec(stab.shape)],
        out_specs=pl.BlockSpec((None, length, ct), lambda c, b: (b, 0, c)),
        out_shape=jax.ShapeDtypeStruct((bsz, length, width), BF16),
        scratch_shapes=scratch,
        compiler_params=_cparams(("parallel", "parallel"), tile),
        name="hyena_conv",
    )(p3, p3, p3, conv_w, conv_w, conv_w, conv_b, conv_b, conv_b, tab, mid, skip, ctab, stab)


def _gate_prep_kernel(ic_ref, fc_ref, ir_ref, fr_ref, col_ref, row_ref, *, chunk):
    length, nch = ic_ref.shape
    nh = nch // 2
    r = lax.broadcasted_iota(I32, (chunk, chunk), 0)
    c = lax.broadcasted_iota(I32, (chunk, chunk), 1)
    lower = (c <= r).astype(F32)
    upper = (c >= r).astype(F32)
    dot_hi = functools.partial(jnp.dot, precision=HIGHEST, preferred_element_type=F32)
    fwd_col = lax.broadcasted_iota(I32, (chunk, nch), 1) < nh
    fwd_row = lax.broadcasted_iota(I32, (nch, chunk), 0) < nh
    pos = lax.broadcasted_iota(I32, (chunk, nch), 0)
    for ch in range(length // chunk):
        rows = slice(ch * chunk, (ch + 1) * chunk)
        lf = _log_sigmoid(fc_ref[rows, :])
        b = jnp.where(fwd_col, dot_hi(lower, lf), dot_hi(upper, lf))
        u = ic_ref[rows, :] - b
        cm_f, cm_b = u, u
        shift = 1
        while shift < chunk:
            cm_f = jnp.maximum(cm_f, jnp.where(pos >= shift, pltpu.roll(cm_f, shift, axis=0), -jnp.inf))
            cm_b = jnp.maximum(cm_b, jnp.where(pos + shift < chunk, pltpu.roll(cm_b, chunk - shift, axis=0), -jnp.inf))
            shift *= 2
        col_ref[0, rows, :] = b
        col_ref[1, rows, :] = u
        col_ref[2, rows, :] = jnp.where(fwd_col, cm_f, cm_b)
        lfr = _log_sigmoid(fr_ref[:, rows])
        b_r = jnp.where(fwd_row, dot_hi(lfr, upper), dot_hi(lfr, lower))
        row_ref[:, rows] = ir_ref[:, rows] - b_r


def _gate_prep(gates3, tile=TILES["mlstm_gate_prep"]):
    chunk = tile.rows
    bsz, length, nch4 = gates3.shape
    nh = nch4 // 4
    i_col = jnp.concatenate([gates3[..., :nh], gates3[..., 2 * nh:3 * nh]], axis=-1)
    f_col = jnp.concatenate([gates3[..., nh:2 * nh], gates3[..., 3 * nh:]], axis=-1)
    nch = 2 * nh
    cspec = pl.BlockSpec((None, length, nch), lambda b: (b, 0, 0))
    rspec = pl.BlockSpec((None, nch, length), lambda b: (b, 0, 0))
    col, row = pl.pallas_call(
        functools.partial(_gate_prep_kernel, chunk=chunk),
        grid=(bsz,),
        in_specs=[cspec, cspec, rspec, rspec],
        out_specs=[pl.BlockSpec((None, 3, length, nch), lambda b: (b, 0, 0, 0)), rspec],
        out_shape=[jax.ShapeDtypeStruct((bsz, 3, length, nch), F32),
                   jax.ShapeDtypeStruct((bsz, nch, length), F32)],
        compiler_params=_cparams(("parallel",), tile),
        name="mlstm_gate_prep",
    )(i_col, f_col, jnp.swapaxes(i_col, 1, 2), jnp.swapaxes(f_col, 1, 2))
    col = col.reshape(bsz, 3, length, 2, nh).transpose(0, 4, 2, 3, 1).reshape(bsz, nh, length, 6)
    row = row.reshape(bsz, 2, nh, length).transpose(0, 2, 1, 3)
    return col, row


def _mlstm_kernel(pq_ref, pk_ref, pv_ref, po_ref, cwq_ref, cwk_ref, cbq_ref, cbk_ref, ng_ref,
                  gcol_ref, grow_ref, y_ref, q_s, k_s, v1_s, tab_s, hf_s, hb_s, *, chunk):
    length, dk = pq_ref.shape
    nc = length // chunk
    rep = chunk // LANES

    def conv_silu(p_ref, w_ref, b_ref):
        c = _short_conv(p_ref[...].astype(F32), w_ref[...], b_ref[...])
        return c * jax.nn.sigmoid(c)

    q_s[...] = (conv_silu(pq_ref, cwq_ref, cbq_ref) * (dk ** -0.5)).astype(BF16)
    k_s[...] = conv_silu(pk_ref, cwk_ref, cbk_ref).astype(BF16)
    v1_s[:, :dk] = pv_ref[...]
    v1_s[:, dk:] = jnp.ones((length, dk), BF16)
    for t in range(tab_s.shape[0]):
        tab_s[t] = jnp.broadcast_to(gcol_ref[:, t:t + 1], (length, LANES))

    row_i = lax.broadcasted_iota(I32, (chunk, chunk), 0)
    col_i = lax.broadcasted_iota(I32, (chunk, chunk), 1)
    wide = lambda a: jnp.concatenate([a] * rep, axis=1)
    both = lambda a: jnp.concatenate([a, a], axis=1)

    def step(c, reverse, state, m):
        d = 1 if reverse else 0
        rows = slice(c * chunk, (c + 1) * chunk)
        edge = c * chunk if reverse else (c + 1) * chunk - 1
        q, k, v1 = q_s[rows, :], k_s[rows, :], v1_s[rows, :]
        b_t, u_t, cm_t = tab_s[3 * d, rows, :], tab_s[3 * d + 1, rows, :], tab_s[3 * d + 2, rows, :]
        u_row = grow_ref[d:d + 1, rows]
        g = tab_s[3 * d, edge:edge + 1, :]
        u_max = tab_s[3 * d + 2, edge:edge + 1, :]
        mask = (col_i >= row_i) if reverse else (col_i <= row_i)
        mm = jnp.maximum(cm_t, m)
        p = jnp.exp(jnp.where(mask, u_row - wide(mm), -jnp.inf))
        qk = lax.dot_general(q, k, (((1,), (1,)), ((), ())), preferred_element_type=F32)
        intra = jnp.dot((qk * p).astype(BF16), v1, preferred_element_type=F32)
        inter = jnp.dot(q, state.astype(BF16), preferred_element_type=F32)
        nd = intra + both(jnp.exp(m - mm)) * inter
        h = nd[:, :dk] / jnp.maximum(jnp.abs(nd[:, dk:]), jnp.exp(-(b_t + mm)))
        kw = (k.astype(F32) * jnp.exp(u_t - u_max)).astype(BF16)
        upd = lax.dot_general(kw, v1, (((0,), (0,)), ((), ())), preferred_element_type=F32)
        m_loc = g + u_max
        m_new = jnp.maximum(g + m, m_loc)
        state = both(jnp.exp(g + m - m_new)) * state + both(jnp.exp(m_loc - m_new)) * upd
        return h, state, m_new

    zero = (jnp.zeros((dk, 2 * dk), F32), jnp.zeros((1, LANES), F32))
    st_f, m_f = zero
    st_b, m_b = zero
    for c in range(nc):
        h, st_f, m_f = step(c, False, st_f, m_f)
        hf_s[c * chunk:(c + 1) * chunk, :] = h
        cb = nc - 1 - c
        h, st_b, m_b = step(cb, True, st_b, m_b)
        hb_s[cb * chunk:(cb + 1) * chunk, :] = h

    ht = hf_s[...] + hb_s[...]
    y_ref[...] = (_rms(ht, ng_ref[...]) * jax.nn.sigmoid(po_ref[...].astype(F32))).astype(y_ref.dtype)


def _mlstm(p3, col_q, col_k, col_v, col_o, conv_w, conv_b, norm_g, gcol, grow, tile=TILES["mlstm"]):
    chunk = tile.rows
    bsz, length, _ = p3.shape
    nh, dh = MLSTM_HEADS, norm_g.shape[1] // MLSTM_HEADS
    assert dh == LANES and chunk % LANES == 0

    def pspec(col):
        return pl.BlockSpec((None, length, dh), lambda b, h: (b, 0, col // dh + h))

    def wspec(rows, part):
        return pl.BlockSpec((rows, dh), lambda b, h: (0, part * nh + h))

    return pl.pallas_call(
        functools.partial(_mlstm_kernel, chunk=chunk),
        grid=(bsz, nh),
        in_specs=[pspec(col_q), pspec(col_k), pspec(col_v), pspec(col_o),
                  wspec(3, 0), wspec(3, 1), wspec(1, 0), wspec(1, 1),
                  pl.BlockSpec((1, dh), lambda b, h: (0, h)),
                  pl.BlockSpec((None, None, length, 6), lambda b, h: (b, h, 0, 0)),
                  pl.BlockSpec((None, None, 2, length), lambda b, h: (b, h, 0, 0))],
        out_specs=pl.BlockSpec((None, length, dh), lambda b, h: (b, 0, h)),
        out_shape=jax.ShapeDtypeStruct((bsz, length, nh * dh), BF16),
        scratch_shapes=[pltpu.VMEM((length, dh), BF16), pltpu.VMEM((length, dh), BF16),
                        pltpu.VMEM((length, 2 * dh), BF16), pltpu.VMEM((6, length, LANES), F32),
                        pltpu.VMEM((length, dh), F32), pltpu.VMEM((length, dh), F32)],
        compiler_params=_cparams(("parallel", "parallel"), tile),
        name="mlstm",
    )(p3, p3, p3, p3, conv_w, conv_w, conv_b, conv_b, norm_g, gcol, grow)


def _merge_kernel(yh_ref, ym_ref, gh_ref, gm_ref, x_ref, wa_ref, wb_ref, wo_ref, o_ref):
    a = jnp.dot(yh_ref[...], wa_ref[...], preferred_element_type=F32)
    b = jnp.dot(ym_ref[...], wb_ref[...], preferred_element_type=F32)
    t = jax.nn.sigmoid(gh_ref[...].astype(F32)) * a + jax.nn.sigmoid(gm_ref[...].astype(F32)) * b
    o_ref[...] = x_ref[...] + jnp.dot(t.astype(BF16), wo_ref[...], preferred_element_type=F32)


def _merge(y_hy, y_ml, p_big, x2, w_a, w_b, w_o, tile=TILES["merge"]):
    tm = tile.rows
    n, d = x2.shape
    wh, wm = y_hy.shape[1], y_ml.shape[1]
    return pl.pallas_call(
        _merge_kernel,
        grid=(n // tm,),
        in_specs=[pl.BlockSpec((tm, wh), lambda i: (i, 0)),
                  pl.BlockSpec((tm, wm), lambda i: (i, 0)),
                  pl.BlockSpec((tm, d), lambda i: (i, 0)),
                  pl.BlockSpec((tm, d), lambda i: (i, 1)),
                  pl.BlockSpec((tm, d), lambda i: (i, 0)),
                  _const_spec(w_a.shape), _const_spec(w_b.shape), _const_spec(w_o.shape)],
        out_specs=pl.BlockSpec((tm, d), lambda i: (i, 0)),
        out_shape=jax.ShapeDtypeStruct((n, d), F32),
        compiler_params=_cparams(("parallel",), tile),
        name="branch_merge",
    )(y_hy, y_ml, p_big, p_big, x2, w_a, w_b, w_o)


def _dense_ffn_kernel(x_ref, g_ref, wg_ref, wu_ref, wd_ref, o_ref, h_s):
    @pl.when(pl.program_id(1) == 0)
    def _():
        x = x_ref[...]
        h_s[...] = _rms(x, g_ref[...]).astype(BF16)
        o_ref[...] = x

    h = h_s[...]
    gt = jnp.dot(h, wg_ref[...].astype(BF16), preferred_element_type=F32)
    up = jnp.dot(h, wu_ref[...].astype(BF16), preferred_element_type=F32)
    act = (gt * jax.nn.sigmoid(gt) * up).astype(BF16)
    o_ref[...] += jnp.dot(act, wd_ref[...].astype(BF16), preferred_element_type=F32)


def _dense_ffn(x2, g, w_gate, w_up, w_down, tile=TILES["dense"]):
    tm, tf = tile.rows, tile.cols
    n, d = x2.shape
    f = w_gate.shape[1]
    return pl.pallas_call(
        _dense_ffn_kernel,
        grid=(n // tm, f // tf),
        in_specs=[pl.BlockSpec((tm, d), lambda i, j: (i, 0)),
                  pl.BlockSpec((1, d), lambda i, j: (0, 0)),
                  pl.BlockSpec((d, tf), lambda i, j: (0, j)),
                  pl.BlockSpec((d, tf), lambda i, j: (0, j)),
                  pl.BlockSpec((tf, d), lambda i, j: (j, 0))],
        out_specs=pl.BlockSpec((tm, d), lambda i, j: (i, 0)),
        out_shape=jax.ShapeDtypeStruct((n, d), F32),
        scratch_shapes=[pltpu.VMEM((tm, d), BF16)],
        compiler_params=_cparams(("parallel", "arbitrary"), tile),
        name="dense_swiglu",
    )(x2, g, w_gate, w_up, w_down)


def _store_rows_3d(ref3, val2):
    for s in range(ref3.shape[1]):
        ref3[:, s, :] = val2[:, s * LANES:(s + 1) * LANES]


def _router_kernel(x_ref, g_ref, rwh_ref, rwl_ref, rb_ref, h_ref, e_ref, gate_ref, rank_ref, cnt_ref, base_s):
    @pl.when(pl.program_id(0) == 0)
    def _():
        base_s[...] = jnp.zeros_like(base_s)

    hn = _rms(x_ref[...], g_ref[...])
    _store_rows_3d(h_ref, hn)
    logits = _dot_split(hn, rwh_ref[...], rwl_ref[...]) + rb_ref[...]
    lane = lax.broadcasted_iota(I32, logits.shape, 1)
    logits = jnp.where(lane < N_EXPERTS, logits, -jnp.inf)
    m1 = jnp.max(logits, axis=-1, keepdims=True)
    i1 = jnp.min(jnp.where(logits == m1, lane, LANES), axis=-1, keepdims=True)
    rest = jnp.where(lane == i1, -jnp.inf, logits)
    m2 = jnp.max(rest, axis=-1, keepdims=True)
    i2 = jnp.min(jnp.where(rest == m2, lane, LANES), axis=-1, keepdims=True)
    e = jnp.exp(m2 - m1)
    slot = lax.broadcasted_iota(I32, e_ref.shape, 1)
    e_ref[...] = jnp.where(slot == 0, i1, i2)
    gate_ref[...] = jnp.where(slot == 0, 1.0 / (1.0 + e), e / (1.0 + e))
    tm = logits.shape[0]
    oh1 = (lane == i1).astype(F32)
    oh2 = (lane == i2).astype(F32)
    both = oh1 + oh2
    earlier = (lax.broadcasted_iota(I32, (tm, tm), 1) < lax.broadcasted_iota(I32, (tm, tm), 0)).astype(BF16)
    before = jnp.dot(earlier, both.astype(BF16), preferred_element_type=F32) + base_s[...]
    r1 = jnp.sum(before * oh1, axis=-1, keepdims=True)
    r2 = jnp.sum(before * oh2, axis=-1, keepdims=True)
    rank_ref[...] = jnp.where(slot == 0, r1, r2).astype(I32)
    base_s[...] += jnp.sum(both, axis=0, keepdims=True)
    cnt_ref[...] = base_s[...].astype(I32)


def _router(x2, g, rw_pad, rb_pad, tile=TILES["router"]):
    tm = tile.rows
    n, d = x2.shape
    s = d // LANES
    rw_hi, rw_lo = _split_bf16(rw_pad)
    return pl.pallas_call(
        _router_kernel,
        grid=(n // tm,),
        in_specs=[pl.BlockSpec((tm, d), lambda i: (i, 0)),
                  pl.BlockSpec((1, d), lambda i: (0, 0)),
                  pl.BlockSpec((d, LANES), lambda i: (0, 0)),
                  pl.BlockSpec((d, LANES), lambda i: (0, 0)),
                  pl.BlockSpec((1, LANES), lambda i: (0, 0))],
        out_specs=[pl.BlockSpec((tm, s, LANES), lambda i: (i, 0, 0)),
                   pl.BlockSpec((tm, TOP_K), lambda i: (i, 0)),
                   pl.BlockSpec((tm, TOP_K), lambda i: (i, 0)),
                   pl.BlockSpec((tm, TOP_K), lambda i: (i, 0)),
                   pl.BlockSpec((1, LANES), lambda i: (0, 0))],
        out_shape=[jax.ShapeDtypeStruct((n, s, LANES), F32),
                   jax.ShapeDtypeStruct((n, TOP_K), I32),
                   jax.ShapeDtypeStruct((n, TOP_K), F32),
                   jax.ShapeDtypeStruct((n, TOP_K), I32),
                   jax.ShapeDtypeStruct((1, LANES), I32)],
        scratch_shapes=[pltpu.VMEM((1, LANES), F32)],
        compiler_params=_cparams(("arbitrary",), tile),
        name="moe_router",
    )(x2, g, rw_hi, rw_lo, rb_pad)


def _slab_pitch(s):
    return s if (s // SUBLANES) % 2 == 1 else s + SUBLANES


def _row_gather(src_hbm, idx_ref, idx_base, dst, sem, rows):
    slab = src_hbm.shape[1]

    def copy(r):
        return pltpu.make_async_copy(src_hbm.at[idx_ref[idx_base + r]], dst.at[r, pl.ds(0, slab)], sem)

    def start_one(r, carry):
        copy(r).start()
        return carry

    def wait_one(r, carry):
        copy(r).wait()
        return carry

    return (lambda: lax.fori_loop(0, rows, start_one, 0, unroll=8),
            lambda: lax.fori_loop(0, rows, wait_one, 0, unroll=8))


def _expert_ffn_kernel(be_ref, nu_ref, tok_ref, h_hbm, wg_ref, wu_ref, wd_ref, o_ref,
                       xg_s, x2_s, h_s, acc_s, sems):
    i, j = pl.program_id(0), pl.program_id(1)
    blk = xg_s.shape[1]
    n_used = nu_ref[0]
    last = pl.num_programs(1) - 1
    mid = pl.num_programs(1) // 2
    used = i < n_used
    has_next = i + 1 < n_used
    both = jnp.logical_and

    def gather(block):
        slot = block % 2
        return _row_gather(h_hbm, tok_ref, block * blk, xg_s.at[slot], sems.at[slot], blk)

    def unpack(block):
        slot = block % 2
        for s in range(h_hbm.shape[1]):
            x2_s[:, s * LANES:(s + 1) * LANES] = xg_s[slot, :, s, :]
        h_s[slot] = x2_s[...].astype(BF16)

    def partial_out():
        h = h_s[i % 2]
        gt = jnp.dot(h, wg_ref[...], preferred_element_type=F32)
        up = jnp.dot(h, wu_ref[...], preferred_element_type=F32)
        act = (gt * jax.nn.sigmoid(gt) * up).astype(BF16)
        return jnp.dot(act, wd_ref[...], preferred_element_type=F32)

    @pl.when(both(i == 0, j == 0))
    def _():
        start, wait = gather(0)
        start()
        wait()
        unpack(0)

    @pl.when(both(j == 0, has_next))
    def _():
        gather(i + 1)[0]()

    @pl.when(both(used, j == 0))
    def _():
        acc_s[...] = partial_out()

    @pl.when(both(used, both(both(j > 0, j < last), jnp.logical_or(j != mid, jnp.logical_not(has_next)))))
    def _():
        acc_s[...] += partial_out()

    @pl.when(both(both(used, has_next), j == mid))
    def _():
        gather(i + 1)[1]()
        unpack(i + 1)
        acc_s[...] += partial_out()

    @pl.when(both(used, j == last))
    def _():
        res = acc_s[...] + partial_out()
        for s in range(o_ref.shape[1]):
            o_ref[:, s, :] = res[:, s * LANES:(s + 1) * LANES]

    @pl.when(both(jnp.logical_not(used), j == last))
    def _():
        o_ref[...] = jnp.zeros_like(o_ref)


def _expert_ffn(h3, slot_tok, block_e, n_used, w_gate, w_up, w_down, tile=TILES["expert"]):
    blk, tf = tile.rows, tile.cols
    ns = slot_tok.shape[0]
    s = h3.shape[1]
    d = s * LANES
    f = w_gate.shape[2]
    nj = f // tf
    assert nj >= 3

    def jj(i, j, nu):
        return jnp.where(i < nu[0], j, nj - 1)

    return pl.pallas_call(
        _expert_ffn_kernel,
        grid_spec=pltpu.PrefetchScalarGridSpec(
            num_scalar_prefetch=3,
            grid=(ns // blk, nj),
            in_specs=[pl.BlockSpec(memory_space=pl.ANY),
                      pl.BlockSpec((None, d, tf), lambda i, j, be, nu, tok: (be[i], 0, jj(i, j, nu))),
                      pl.BlockSpec((None, d, tf), lambda i, j, be, nu, tok: (be[i], 0, jj(i, j, nu))),
                      pl.BlockSpec((None, tf, d), lambda i, j, be, nu, tok: (be[i], jj(i, j, nu), 0))],
            out_specs=pl.BlockSpec((blk, s, LANES), lambda i, j, be, nu, tok: (i, 0, 0)),
            scratch_shapes=[pltpu.VMEM((2, blk, _slab_pitch(s), LANES), F32), pltpu.VMEM((blk, d), F32),
                            pltpu.VMEM((2, blk, d), BF16), pltpu.VMEM((blk, d), F32),
                            pltpu.SemaphoreType.DMA((2,))],
        ),
        out_shape=jax.ShapeDtypeStruct((ns, s, LANES), F32),
        compiler_params=_cparams(("arbitrary", "arbitrary"), tile),
        name="expert_swiglu",
    )(block_e, n_used, slot_tok, h3, w_gate, w_up, w_down)


def _combine_kernel(dest_ref, x_ref, gate_ref, y_hbm, g_ref, o_ref, yg_s, sems, *, final_norm):
    i, n_tiles = pl.program_id(0), pl.num_programs(0)
    tm = x_ref.shape[0]

    def gather(tile):
        slot = tile % 2
        parts = [_row_gather(y_hbm, dest_ref, k * n_tiles * tm + tile * tm, yg_s.at[slot, k], sems.at[slot], tm)
                 for k in range(TOP_K)]
        return (lambda: [p[0]() for p in parts]), (lambda: [p[1]() for p in parts])

    @pl.when(i == 0)
    def _():
        gather(0)[0]()

    @pl.when(i + 1 < n_tiles)
    def _():
        gather(i + 1)[0]()

    gather(i)[1]()
    slot = i % 2
    gates = [jnp.broadcast_to(gate_ref[:, k:k + 1], (tm, LANES)) for k in range(TOP_K)]
    for s in range(y_hbm.shape[1]):
        cols = slice(s * LANES, (s + 1) * LANES)
        acc = x_ref[:, cols]
        for k in range(TOP_K):
            acc = acc + gates[k] * yg_s[slot, k, :, s, :]
        o_ref[:, cols] = acc
    if final_norm:
        o_ref[...] = _rms(o_ref[...], g_ref[...])


def _combine(x2, gate, yb, dest_by_k, g, final_norm, tile=TILES["combine"]):
    tm = tile.rows
    n, d = x2.shape
    s = d // LANES
    return pl.pallas_call(
        functools.partial(_combine_kernel, final_norm=final_norm),
        grid_spec=pltpu.PrefetchScalarGridSpec(
            num_scalar_prefetch=1,
            grid=(n // tm,),
            in_specs=[pl.BlockSpec((tm, d), lambda i, dest: (i, 0)),
                      pl.BlockSpec((tm, TOP_K), lambda i, dest: (i, 0)),
                      pl.BlockSpec(memory_space=pl.ANY),
                      pl.BlockSpec((1, d), lambda i, dest: (0, 0))],
            out_specs=pl.BlockSpec((tm, d), lambda i, dest: (i, 0)),
            scratch_shapes=[pltpu.VMEM((2, TOP_K, tm, _slab_pitch(s), LANES), F32),
                            pltpu.SemaphoreType.DMA((2,))],
        ),
        out_shape=jax.ShapeDtypeStruct((n, d), F32),
        compiler_params=_cparams(("arbitrary",), tile),
        name="moe_combine",
    )(dest_by_k, x2, gate, yb, g)


def _final_norm_kernel(x_ref, g_ref, o_ref):
    o_ref[...] = _rms(x_ref[...], g_ref[...])


def _final_norm(x2, g, tile=TILES["final_norm"]):
    tm = tile.rows
    n, d = x2.shape
    return pl.pallas_call(
        _final_norm_kernel,
        grid=(n // tm,),
        in_specs=[pl.BlockSpec((tm, d), lambda i: (i, 0)), pl.BlockSpec((1, d), lambda i: (0, 0))],
        out_specs=pl.BlockSpec((tm, d), lambda i: (i, 0)),
        out_shape=jax.ShapeDtypeStruct((n, d), F32),
        compiler_params=_cparams(("parallel",), tile),
        name="final_norm",
    )(x2, g)


def _routing_tables(top_e, rank, counts, blk):
    n = top_e.shape[0]
    nk = n * TOP_K
    flat_e = top_e.reshape(-1)
    padded = (counts + blk - 1) // blk * blk
    p_end = jnp.cumsum(padded)
    p_start = p_end - padded
    dest = (p_start[flat_e] + rank.reshape(-1)).astype(I32)
    n_blocks = -(-nk // blk) + N_EXPERTS
    flat_tok = jnp.arange(nk, dtype=I32) // TOP_K
    slot_tok = jnp.zeros((n_blocks * blk,), I32).at[dest].set(flat_tok, unique_indices=True)
    n_used = (p_end[-1] // blk).astype(I32)
    blocks = jnp.arange(n_blocks, dtype=I32)
    block_e = jnp.sum((p_end[None, :] <= (blocks * blk)[:, None]).astype(I32), axis=1)
    block_e = jnp.minimum(block_e, N_EXPERTS - 1)
    block_e = jnp.where(blocks < n_used, block_e, block_e[n_used - 1])
    dest_by_k = dest.reshape(n, TOP_K).T.reshape(-1)
    return slot_tok, block_e, n_used.reshape(1), dest_by_k


def _moe_layer(x2, norm_g, router_w, router_b, w_gate, w_up, w_down, out_g, final_norm):
    n, d = x2.shape
    rw = jnp.zeros((d, LANES), F32).at[:, :N_EXPERTS].set(router_w)
    rb = jnp.zeros((1, LANES), F32).at[0, :N_EXPERTS].set(router_b)
    h, top_e, gate, rank, counts = _router(x2, norm_g[None, :], rw, rb)
    slot_tok, block_e, n_used, dest_by_k = _routing_tables(top_e, rank, counts[0, :N_EXPERTS], TILES["expert"].rows)
    yb = _expert_ffn(h, slot_tok, block_e, n_used, w_gate.astype(BF16), w_up.astype(BF16), w_down.astype(BF16))
    return _combine(x2, gate, yb, dest_by_k, out_g[None, :], final_norm)


def _mixer_layer(x2, bsz, length, norm_g, w_in, b_in, hy_conv_w, hy_conv_b, hy_w1, hy_b1, hy_w2, hy_b2,
                 hy_w3, hy_freq, hy_decay, hy_skip, ml_conv_w, ml_conv_b, ml_norm_g, w_a, w_b, w_o,
                 ctab, stab, twiddle, zpos):
    n, d = x2.shape
    wh = hy_skip.shape[1]
    wm = ml_norm_g.shape[0]
    off_qk = (HYENA_ORDER + 1) * wh
    off_v = off_qk + 2 * wm
    off_o = off_v + wm
    off_gates = off_o + wm
    off_br = off_gates + 4 * MLSTM_HEADS
    w_cat = jnp.concatenate([w_in[:, off_br:], w_in[:, :off_gates]], axis=1).astype(BF16)
    b_cat = jnp.concatenate([b_in[off_br:], b_in[:off_gates]])[None, :]
    col_hy = 2 * d
    col_q = col_hy + off_qk
    col_k = col_q + wm
    col_v = col_hy + off_v
    col_o = col_hy + off_o

    p_big, gates = _in_proj(x2, norm_g[None, :], w_in[:, off_gates:off_br], b_in[None, off_gates:off_br],
                            w_cat, b_cat)
    p3 = p_big.reshape(bsz, length, p_big.shape[1])

    kp = LANES
    pad2 = lambda a, r, c: jnp.zeros((r, c), F32).at[:a.shape[0], :a.shape[1]].set(a)
    zp = pad2(zpos, length, kp)
    htab, hmid = _hyena_spectrum(
        zp, pad2(hy_w1, kp, kp), pad2(hy_b1[None, :], 1, kp), pad2(hy_w2, kp, kp), pad2(hy_b2[None, :], 1, kp),
        pad2(hy_freq[None, :], 1, kp), pad2(hy_w3, kp, hy_w3.shape[1]), hy_decay[None, :],
        ctab, stab, twiddle, wh)
    y_hy = _hyena_conv(p3, col_hy, hy_conv_w, hy_conv_b[None, :], htab, hmid, hy_skip[:, None, :], ctab, stab, wh)

    gcol, grow = _gate_prep(gates.reshape(bsz, length, 4 * MLSTM_HEADS))
    y_ml = _mlstm(p3, col_q, col_k, col_v, col_o, ml_conv_w, ml_conv_b[None, :], ml_norm_g[None, :],
                  gcol, grow)

    return _merge(y_hy.reshape(n, wh), y_ml.reshape(n, wm), p_big, x2,
                  w_a.astype(BF16), w_b.astype(BF16), w_o.astype(BF16))


def kernel(x, mix_norm_g, mix_w_in, mix_b_in, hy_conv_w, hy_conv_b, hy_filt_w1, hy_filt_b1, hy_filt_w2,
           hy_filt_b2, hy_filt_w3, hy_filt_freq, hy_filt_decay, hy_skip, ml_conv_w, ml_conv_b, ml_norm_g,
           mix_w_a, mix_w_b, mix_w_o, ffn_norm_g, dense_w_gate, dense_w_up, dense_w_down, moe_router_w,
           moe_router_b, moe_w_gate, moe_w_up, moe_w_down, final_norm_g):
    bsz, length, d = x.shape
    depth = mix_norm_g.shape[0]
    x2 = x.reshape(bsz * length, d)
    ctab, stab, twiddle = _dft_tables(length // 2)
    zpos = _hyena_positions(length)
    normed = False
    for layer in range(depth):
        x2 = _mixer_layer(x2, bsz, length, mix_norm_g[layer], mix_w_in[layer], mix_b_in[layer],
                          hy_conv_w[layer], hy_conv_b[layer], hy_filt_w1[layer], hy_filt_b1[layer],
                          hy_filt_w2[layer], hy_filt_b2[layer], hy_filt_w3[layer], hy_filt_freq[layer],
                          hy_filt_decay[layer], hy_skip[layer], ml_conv_w[layer], ml_conv_b[layer],
                          ml_norm_g[layer], mix_w_a[layer], mix_w_b[layer], mix_w_o[layer], ctab, stab, twiddle,
                          zpos)
        j = layer // 2
        if layer % 2 == 0:
            x2 = _dense_ffn(x2, ffn_norm_g[layer][None, :], dense_w_gate[j], dense_w_up[j], dense_w_down[j])
        else:
            normed = layer == depth - 1
            x2 = _moe_layer(x2, ffn_norm_g[layer], moe_router_w[j], moe_router_b[j], moe_w_gate[j],
                            moe_w_up[j], moe_w_down[j], final_norm_g, normed)
    if not normed:
        x2 = _final_norm(x2, final_norm_g[None, :])
    return x2.reshape(bsz, length, d)
```

```python
import functools
import math
from typing import NamedTuple

import jax
import jax.numpy as jnp
import numpy as np
from jax import lax
from jax.experimental import pallas as pl
from jax.experimental.pallas import tpu as pltpu

F32, BF16, I32 = jnp.float32, jnp.bfloat16, jnp.int32
HIGHEST = lax.Precision.HIGHEST

HYENA_ORDER = 2
HYENA_POS_BANDS = 16
MLSTM_HEADS = 8
N_EXPERTS = 8
TOP_K = 2
EPS = 1e-6

V7X_VMEM_BYTES = 64 * 1024 * 1024
LANES = 128
SUBLANES = 8
MXU_DIM = 256


class _Tile(NamedTuple):
    rows: int
    cols: int
    vmem_mib: int


TILES = dict(
    in_proj=_Tile(1024, 4 * MXU_DIM, 48),
    hyena_spectrum=_Tile(0, MXU_DIM, 48),
    hyena_conv=_Tile(512, MXU_DIM, 56),
    mlstm_gate_prep=_Tile(256, 0, 32),
    mlstm=_Tile(256, 0, 40),
    merge=_Tile(256, 0, 48),
    dense=_Tile(1024, 256, 56),
    router=_Tile(512, 0, 32),
    expert=_Tile(512, 512, 52),
    combine=_Tile(512, 0, 48),
    final_norm=_Tile(512, 0, 32),
)


def _cparams(semantics, tile):
    assert tile.vmem_mib * 1024 * 1024 <= V7X_VMEM_BYTES
    return pltpu.CompilerParams(dimension_semantics=semantics, vmem_limit_bytes=tile.vmem_mib * 1024 * 1024)


def _const_spec(shape):
    nd = len(shape)
    return pl.BlockSpec(shape, lambda *_: (0,) * nd, pipeline_mode=pl.Buffered(1))


def _rms(x, g):
    return x * lax.rsqrt(jnp.mean(x * x, axis=-1, keepdims=True) + EPS) * g


def _shift_rows(u, direction):
    n = u.shape[0]
    row = lax.broadcasted_iota(I32, u.shape, 0)
    if direction > 0:
        return jnp.where(row == 0, 0.0, pltpu.roll(u, 1, axis=0))
    return jnp.where(row == n - 1, 0.0, pltpu.roll(u, n - 1, axis=0))


def _short_conv(u, w, b):
    return b + _shift_rows(u, 1) * w[0:1] + u * w[1:2] + _shift_rows(u, -1) * w[2:3]


def _log_sigmoid(x):
    return jnp.minimum(x, 0.0) - jnp.log1p(jnp.exp(-jnp.abs(x)))


def _dot_split(a, w_hi, w_lo):
    a_hi = a.astype(BF16)
    a_lo = (a - a_hi.astype(F32)).astype(BF16)
    return (jnp.dot(a_hi, w_hi, preferred_element_type=F32) + jnp.dot(a_lo, w_hi, preferred_element_type=F32)
            + jnp.dot(a_hi, w_lo, preferred_element_type=F32))


def _split_bf16(w):
    w_hi = w.astype(BF16)
    return w_hi, (w - w_hi.astype(F32)).astype(BF16)


def _in_proj_kernel(x_ref, g_ref, wg_ref, bg_ref, w_ref, b_ref, o_ref, gates_ref, h_s):
    @pl.when(pl.program_id(1) == 0)
    def _():
        h_s[...] = _rms(x_ref[...], g_ref[...]).astype(BF16)
        gates_ref[...] = jnp.dot(h_s[...], wg_ref[...], preferred_element_type=F32) + bg_ref[...]

    acc = jnp.dot(h_s[...], w_ref[...], preferred_element_type=F32) + b_ref[...]
    o_ref[...] = acc.astype(o_ref.dtype)


def _in_proj(x2, g, w_gates, b_gates, w, b, tile=TILES["in_proj"]):
    tm, tn = tile.rows, tile.cols
    n, d = x2.shape
    nc = w.shape[1]
    ng = w_gates.shape[1]
    return pl.pallas_call(
        _in_proj_kernel,
        grid=(n // tm, nc // tn),
        in_specs=[pl.BlockSpec((tm, d), lambda i, j: (i, 0)),
                  pl.BlockSpec((1, d), lambda i, j: (0, 0)),
                  pl.BlockSpec((d, ng), lambda i, j: (0, 0)),
                  pl.BlockSpec((1, ng), lambda i, j: (0, 0)),
                  pl.BlockSpec((d, tn), lambda i, j: (0, j)),
                  pl.BlockSpec((1, tn), lambda i, j: (0, j))],
        out_specs=[pl.BlockSpec((tm, tn), lambda i, j: (i, j)),
                   pl.BlockSpec((tm, ng), lambda i, j: (i, 0))],
        out_shape=[jax.ShapeDtypeStruct((n, nc), BF16), jax.ShapeDtypeStruct((n, ng), F32)],
        scratch_shapes=[pltpu.VMEM((tm, d), BF16)],
        compiler_params=_cparams(("parallel", "arbitrary"), tile),
        name="in_proj",
    )(x2, g, w_gates.astype(BF16), b_gates, w, b)


def _dft_tables(half):
    k = jnp.arange(half, dtype=I32)

    def cos_sin(rows):
        ang = ((rows[:, None] * k[None, :]) % (2 * half)).astype(F32) * (math.pi / half)
        return jnp.cos(ang), jnp.sin(ang)

    split = math.gcd(half, 32)
    c1, s1 = cos_sin(jnp.arange(half // split, dtype=I32) * split)
    c0, s0 = cos_sin(jnp.arange(split, dtype=I32))
    ctab = (c1[:, None, :] * c0[None, :, :] - s1[:, None, :] * s0[None, :, :]).reshape(half, half)
    stab = (s1[:, None, :] * c0[None, :, :] + c1[:, None, :] * s0[None, :, :]).reshape(half, half)
    tw = k.astype(F32)[:, None] * (math.pi / (2 * half))
    twiddle = jnp.stack([jnp.broadcast_to(jnp.cos(tw), (half, LANES)), jnp.broadcast_to(jnp.sin(tw), (half, LANES))])
    return ctab.astype(BF16), stab.astype(BF16), twiddle


def _hyena_positions(length):
    t = np.linspace(0.0, 1.0, length)[:, None]
    n = np.arange(length, dtype=np.float64)[:, None]
    bands = np.linspace(1e-4, HYENA_POS_BANDS - 1, HYENA_POS_BANDS)[None, :]
    ang = (2.0 * np.pi / length) * n * bands
    return jnp.asarray(np.concatenate([t, np.cos(ang), -np.sin(ang)], axis=-1), dtype=F32)


def _split_rows(val, slab_s):
    half = val.shape[0] // 2
    n_slabs = slab_s.shape[0]
    for s in range(n_slabs):
        slab_s[s] = val[:, s * LANES:(s + 1) * LANES]
    pick = lambda p: jnp.concatenate([slab_s[s, pl.ds(p, half, stride=2), :] for s in range(n_slabs)], axis=1)
    return pick(0), pick(1)


def _merge_rows(even, odd, slab_s):
    half = even.shape[0]
    n_slabs = slab_s.shape[0]
    for s in range(n_slabs):
        slab_s[s, pl.ds(0, half, stride=2), :] = even[:, s * LANES:(s + 1) * LANES]
        slab_s[s, pl.ds(1, half, stride=2), :] = odd[:, s * LANES:(s + 1) * LANES]
    return jnp.concatenate([slab_s[s] for s in range(n_slabs)], axis=1)


def _alt_sum(x):
    row = lax.broadcasted_iota(I32, (x.shape[0], 1), 0)
    return jnp.sum(jnp.where(row % 2 == 0, x, -x), axis=0, keepdims=True)


def _hyena_spectrum_kernel(z_ref, w1_ref, b1_ref, w2_ref, b2_ref, fr_ref, w3f_ref, w3b_ref,
                           decf_ref, decb_ref, c_ref, s_ref, tw_ref, tab_ref, mid_ref, slab_s, hid_s):
    half = c_ref.shape[0]
    length = 2 * half
    ct = w3f_ref.shape[1]
    dot_hi = functools.partial(jnp.dot, precision=HIGHEST, preferred_element_type=F32)
    dot = functools.partial(jnp.dot, preferred_element_type=F32)
    z = z_ref[...]

    @pl.when(jnp.logical_and(pl.program_id(0) == 0, pl.program_id(1) == 0))
    def _():
        fr = fr_ref[...]
        hid = jnp.sin(fr * (dot_hi(z, w1_ref[...]) + b1_ref[...]))
        hid_s[...] = jnp.sin(fr * (dot_hi(hid, w2_ref[...]) + b2_ref[...]))

    hid = hid_s[...]
    t = z[:, 0:1]
    ff = dot_hi(hid, w3f_ref[...]) * jnp.exp(-t * jnp.abs(decf_ref[...]))
    fb = dot_hi(hid, w3b_ref[...]) * jnp.exp(-t * jnp.abs(decb_ref[...]))
    row = lax.broadcasted_iota(I32, (length, 1), 0)
    l1 = jnp.sum(jnp.where(row == 0, jnp.abs(ff + fb), jnp.abs(ff) + jnp.abs(fb)), axis=0, keepdims=True)
    inv = 1.0 / l1
    se, so = _split_rows((ff + fb) * inv, slab_s)
    de, do = _split_rows((ff - fb) * inv, slab_s)
    wide = lambda a: jnp.concatenate([a] * (ct // LANES), axis=1)
    cw, sw = wide(tw_ref[0]), wide(tw_ref[1])
    c_t, s_t = c_ref[...], s_ref[...]
    bf = lambda a: a.astype(BF16)
    e_r = dot(c_t, bf(se))
    o_r, o_s = dot(c_t, bf(so)), dot(s_t, bf(so))
    t_r = cw * o_r - sw * o_s
    d_s = dot(s_t, bf(de))
    p_r, p_s = dot(c_t, bf(do)), dot(s_t, bf(do))
    t_i = -(cw * p_s + sw * p_r)
    krow = lax.broadcasted_iota(I32, (half, 1), 0)
    scale = jnp.where(krow == 0, 1.0 / (2 * length), 1.0 / length)
    lo_r, lo_i = (e_r + t_r) * scale, (t_i - d_s) * scale
    hi_r, hi_i = (e_r - t_r) * scale, (t_i + d_s) * scale

    def emit(p, lr, li, hr, hi):
        a_r, a_i = lr + hr, li - hi
        d_r, d_i = lr - hr, li + hi
        tab_ref[4 * p + 0] = a_r.astype(tab_ref.dtype)
        tab_ref[4 * p + 1] = a_i.astype(tab_ref.dtype)
        tab_ref[4 * p + 2] = (d_r * cw + d_i * sw).astype(tab_ref.dtype)
        tab_ref[4 * p + 3] = (d_i * cw - d_r * sw).astype(tab_ref.dtype)

    emit(0, lo_r, lo_i, hi_r, hi_i)
    emit(1, lo_r * cw - lo_i * sw, lo_r * sw + lo_i * cw, -(hi_r * cw + hi_i * sw), -(hi_i * cw - hi_r * sw))
    mid_ref[0:1, :] = _alt_sum(se) * (1.0 / length)
    mid_ref[1:2, :] = -_alt_sum(do) * (1.0 / length)


def _hyena_spectrum(z, w1, b1, w2, b2, freq, w3, decay, ctab, stab, twiddle, width, tile=TILES["hyena_spectrum"]):
    ct = tile.cols
    half = ctab.shape[0]
    kp = w1.shape[1]
    nct = width // ct
    col_f = lambda o, c: (0, o * 2 * nct + c)
    col_b = lambda o, c: (0, o * 2 * nct + nct + c)
    return pl.pallas_call(
        _hyena_spectrum_kernel,
        grid=(HYENA_ORDER, nct),
        in_specs=[_const_spec(z.shape), _const_spec(w1.shape), _const_spec(b1.shape),
                  _const_spec(w2.shape), _const_spec(b2.shape), _const_spec(freq.shape),
                  pl.BlockSpec((kp, ct), col_f), pl.BlockSpec((kp, ct), col_b),
                  pl.BlockSpec((1, ct), col_f), pl.BlockSpec((1, ct), col_b),
                  _const_spec(ctab.shape), _const_spec(stab.shape), _const_spec(twiddle.shape)],
        out_specs=[pl.BlockSpec((None, 8, half, ct), lambda o, c: (o, 0, 0, c)),
                   pl.BlockSpec((None, 2, ct), lambda o, c: (o, 0, c))],
        out_shape=[jax.ShapeDtypeStruct((HYENA_ORDER, 8, half, width), BF16),
                   jax.ShapeDtypeStruct((HYENA_ORDER, 2, width), F32)],
        scratch_shapes=[pltpu.VMEM((ct // LANES, 2 * half, LANES), F32), pltpu.VMEM((2 * half, kp), F32)],
        compiler_params=_cparams(("arbitrary", "arbitrary"), tile),
        name="hyena_spectrum",
    )(z, w1, b1, w2, b2, freq, w3, w3, decay, decay, ctab, stab, twiddle)


def _hyena_conv_kernel(pv_ref, p1_ref, p2_ref, wv_ref, w1_ref, w2_ref, bv_ref, b1_ref, b2_ref,
                       tab_ref, mid_ref, skip_ref, c_ref, s_ref, y_ref,
                       slab_s, z_s, g1_s, g2_s, zb_s, gr_s, gn_s, *, rc):
    half = c_ref.shape[0]
    dot = functools.partial(jnp.dot, preferred_element_type=F32)
    for dst, p_ref, w_ref, b_ref in ((z_s, pv_ref, wv_ref, bv_ref), (g1_s, p1_ref, w1_ref, b1_ref),
                                     (g2_s, p2_ref, w2_ref, b2_ref)):
        even, odd = _split_rows(_short_conv(p_ref[...].astype(F32), w_ref[...], b_ref[...]), slab_s)
        dst[0] = even
        dst[1] = odd
    gates = (g1_s, g2_s)
    row = lax.broadcasted_iota(I32, (half, 1), 0)
    sign = jnp.where(row % 2 == 0, 1.0, -1.0)
    for o in range(HYENA_ORDER):
        zb_s[0] = z_s[0].astype(BF16)
        zb_s[1] = z_s[1].astype(BF16)
        a_e, a_o = _alt_sum(z_s[0]), _alt_sum(z_s[1])
        mid = (mid_ref[o, 0:1, :] * a_e + mid_ref[o, 1:2, :] * a_o,
               mid_ref[o, 0:1, :] * a_o - mid_ref[o, 1:2, :] * a_e)
        for r in range(half // rc):
            rows = slice(r * rc, (r + 1) * rc)
            e_r, e_s = dot(c_ref[rows, :], zb_s[0]).astype(BF16), dot(s_ref[rows, :], zb_s[0]).astype(BF16)
            o_r, o_s = dot(c_ref[rows, :], zb_s[1]).astype(BF16), dot(s_ref[rows, :], zb_s[1]).astype(BF16)
            for p in range(2):
                a_r, a_i = tab_ref[o, 4 * p + 0, rows, :], tab_ref[o, 4 * p + 1, rows, :]
                b_r, b_i = tab_ref[o, 4 * p + 2, rows, :], tab_ref[o, 4 * p + 3, rows, :]
                gr_s[p, rows, :] = (a_r * e_r + a_i * e_s) + (b_r * o_r + b_i * o_s)
                gn_s[p, rows, :] = (a_r * e_s - a_i * e_r) + (b_r * o_s - b_i * o_r)
        for p in range(2):
            for r in range(half // rc):
                rows = slice(r * rc, (r + 1) * rc)
                conv = dot(c_ref[rows, :], gr_s[p]) + dot(s_ref[rows, :], gn_s[p]) + sign[rows] * mid[p]
                z_s[p, rows, :] = gates[o][p, rows, :] * (conv + skip_ref[o] * z_s[p, rows, :])
    y_ref[...] = _merge_rows(z_s[0], z_s[1], slab_s).astype(y_ref.dtype)


def _hyena_conv(p3, col0, conv_w, conv_b, tab, mid, skip, ctab, stab, width, tile=TILES["hyena_conv"]):
    rc, ct = tile.rows, tile.cols
    bsz, length, _ = p3.shape
    half = length // 2
    nct = width // ct
    c0 = col0 // ct

    def pspec(part):
        return pl.BlockSpec((None, length, ct), lambda c, b: (b, 0, c0 + part * nct + c))

    def wspec(rows, part):
        return pl.BlockSpec((rows, ct), lambda c, b: (0, part * nct + c))

    scratch = ([pltpu.VMEM((ct // LANES, length, LANES), F32)] + [pltpu.VMEM((2, half, ct), F32)] * 3
               + [pltpu.VMEM((2, half, ct), BF16)] * 3)
    return pl.pallas_call(
        functools.partial(_hyena_conv_kernel, rc=rc),
        grid=(nct, bsz),
        in_specs=[pspec(0), pspec(1), pspec(2),
                  wspec(3, 0), wspec(3, 1), wspec(3, 2), wspec(1, 0), wspec(1, 1), wspec(1, 2),
                  pl.BlockSpec((HYENA_ORDER, 8, half, ct), lambda c, b: (0, 0, 0, c),
                               pipeline_mode=pl.Buffered(1)),
                  pl.BlockSpec((HYENA_ORDER, 2, ct), lambda c, b: (0, 0, c)),
                  pl.BlockSpec((HYENA_ORDER, 1, ct), lambda c, b: (0, 0, c)),
                  _const_spec(ctab.shape), _const_spec(stab.shape)],
        out_specs=pl.BlockSpec((None, length, ct), lambda c, b: (b, 0, c)),
        out_shape=jax.ShapeDtypeStruct((bsz, length, width), BF16),
        scratch_shapes=scratch,
        compiler_params=_cparams(("parallel", "parallel"), tile),
        name="hyena_conv",
    )(p3, p3, p3, conv_w, conv_w, conv_w, conv_b, conv_b, conv_b, tab, mid, skip, ctab, stab)


def _gate_prep_kernel(ic_ref, fc_ref, ir_ref, fr_ref, col_ref, row_ref, *, chunk):
    length, nch = ic_ref.shape
    nh = nch // 2
    r = lax.broadcasted_iota(I32, (chunk, chunk), 0)
    c = lax.broadcasted_iota(I32, (chunk, chunk), 1)
    lower = (c <= r).astype(F32)
    upper = (c >= r).astype(F32)
    dot_hi = functools.partial(jnp.dot, precision=HIGHEST, preferred_element_type=F32)
    fwd_col = lax.broadcasted_iota(I32, (chunk, nch), 1) < nh
    fwd_row = lax.broadcasted_iota(I32, (nch, chunk), 0) < nh
    pos = lax.broadcasted_iota(I32, (chunk, nch), 0)
    for ch in range(length // chunk):
        rows = slice(ch * chunk, (ch + 1) * chunk)
        lf = _log_sigmoid(fc_ref[rows, :])
        b = jnp.where(fwd_col, dot_hi(lower, lf), dot_hi(upper, lf))
        u = ic_ref[rows, :] - b
        cm_f, cm_b = u, u
        shift = 1
        while shift < chunk:
            cm_f = jnp.maximum(cm_f, jnp.where(pos >= shift, pltpu.roll(cm_f, shift, axis=0), -jnp.inf))
            cm_b = jnp.maximum(cm_b, jnp.where(pos + shift < chunk, pltpu.roll(cm_b, chunk - shift, axis=0), -jnp.inf))
            shift *= 2
        col_ref[0, rows, :] = b
        col_ref[1, rows, :] = u
        col_ref[2, rows, :] = jnp.where(fwd_col, cm_f, cm_b)
        lfr = _log_sigmoid(fr_ref[:, rows])
        b_r = jnp.where(fwd_row, dot_hi(lfr, upper), dot_hi(lfr, lower))
        row_ref[:, rows] = ir_ref[:, rows] - b_r


def _gate_prep(gates3, tile=TILES["mlstm_gate_prep"]):
    chunk = tile.rows
    bsz, length, nch4 = gates3.shape
    nh = nch4 // 4
    i_col = jnp.concatenate([gates3[..., :nh], gates3[..., 2 * nh:3 * nh]], axis=-1)
    f_col = jnp.concatenate([gates3[..., nh:2 * nh], gates3[..., 3 * nh:]], axis=-1)
    nch = 2 * nh
    cspec = pl.BlockSpec((None, length, nch), lambda b: (b, 0, 0))
    rspec = pl.BlockSpec((None, nch, length), lambda b: (b, 0, 0))
    col, row = pl.pallas_call(
        functools.partial(_gate_prep_kernel, chunk=chunk),
        grid=(bsz,),
        in_specs=[cspec, cspec, rspec, rspec],
        out_specs=[pl.BlockSpec((None, 3, length, nch), lambda b: (b, 0, 0, 0)), rspec],
        out_shape=[jax.ShapeDtypeStruct((bsz, 3, length, nch), F32),
                   jax.ShapeDtypeStruct((bsz, nch, length), F32)],
        compiler_params=_cparams(("parallel",), tile),
        name="mlstm_gate_prep",
    )(i_col, f_col, jnp.swapaxes(i_col, 1, 2), jnp.swapaxes(f_col, 1, 2))
    col = col.reshape(bsz, 3, length, 2, nh).transpose(0, 4, 2, 3, 1).reshape(bsz, nh, length, 6)
    row = row.reshape(bsz, 2, nh, length).transpose(0, 2, 1, 3)
    return col, row


def _mlstm_kernel(pq_ref, pk_ref, pv_ref, po_ref, cwq_ref, cwk_ref, cbq_ref, cbk_ref, ng_ref,
                  gcol_ref, grow_ref, y_ref, q_s, k_s, v1_s, tab_s, hf_s, hb_s, *, chunk):
    length, dk = pq_ref.shape
    nc = length // chunk
    rep = chunk // LANES

    def conv_silu(p_ref, w_ref, b_ref):
        c = _short_conv(p_ref[...].astype(F32), w_ref[...], b_ref[...])
        return c * jax.nn.sigmoid(c)

    q_s[...] = (conv_silu(pq_ref, cwq_ref, cbq_ref) * (dk ** -0.5)).astype(BF16)
    k_s[...] = conv_silu(pk_ref, cwk_ref, cbk_ref).astype(BF16)
    v1_s[:, :dk] = pv_ref[...]
    v1_s[:, dk:] = jnp.ones((length, dk), BF16)
    for t in range(tab_s.shape[0]):
        tab_s[t] = jnp.broadcast_to(gcol_ref[:, t:t + 1], (length, LANES))

    row_i = lax.broadcasted_iota(I32, (chunk, chunk), 0)
    col_i = lax.broadcasted_iota(I32, (chunk, chunk), 1)
    wide = lambda a: jnp.concatenate([a] * rep, axis=1)
    both = lambda a: jnp.concatenate([a, a], axis=1)

    def step(c, reverse, state, m):
        d = 1 if reverse else 0
        rows = slice(c * chunk, (c + 1) * chunk)
        edge = c * chunk if reverse else (c + 1) * chunk - 1
        q, k, v1 = q_s[rows, :], k_s[rows, :], v1_s[rows, :]
        b_t, u_t, cm_t = tab_s[3 * d, rows, :], tab_s[3 * d + 1, rows, :], tab_s[3 * d + 2, rows, :]
        u_row = grow_ref[d:d + 1, rows]
        g = tab_s[3 * d, edge:edge + 1, :]
        u_max = tab_s[3 * d + 2, edge:edge + 1, :]
        mask = (col_i >= row_i) if reverse else (col_i <= row_i)
        mm = jnp.maximum(cm_t, m)
        p = jnp.exp(jnp.where(mask, u_row - wide(mm), -jnp.inf))
        qk = lax.dot_general(q, k, (((1,), (1,)), ((), ())), preferred_element_type=F32)
        intra = jnp.dot((qk * p).astype(BF16), v1, preferred_element_type=F32)
        inter = jnp.dot(q, state.astype(BF16), preferred_element_type=F32)
        nd = intra + both(jnp.exp(m - mm)) * inter
        h = nd[:, :dk] / jnp.maximum(jnp.abs(nd[:, dk:]), jnp.exp(-(b_t + mm)))
        kw = (k.astype(F32) * jnp.exp(u_t - u_max)).astype(BF16)
        upd = lax.dot_general(kw, v1, (((0,), (0,)), ((), ())), preferred_element_type=F32)
        m_loc = g + u_max
        m_new = jnp.maximum(g + m, m_loc)
        state = both(jnp.exp(g + m - m_new)) * state + both(jnp.exp(m_loc - m_new)) * upd
        return h, state, m_new

    zero = (jnp.zeros((dk, 2 * dk), F32), jnp.zeros((1, LANES), F32))
    st_f, m_f = zero
    st_b, m_b = zero
    for c in range(nc):
        h, st_f, m_f = step(c, False, st_f, m_f)
        hf_s[c * chunk:(c + 1) * chunk, :] = h
        cb = nc - 1 - c
        h, st_b, m_b = step(cb, True, st_b, m_b)
        hb_s[cb * chunk:(cb + 1) * chunk, :] = h

    ht = hf_s[...] + hb_s[...]
    y_ref[...] = (_rms(ht, ng_ref[...]) * jax.nn.sigmoid(po_ref[...].astype(F32))).astype(y_ref.dtype)


def _mlstm(p3, col_q, col_k, col_v, col_o, conv_w, conv_b, norm_g, gcol, grow, tile=TILES["mlstm"]):
    chunk = tile.rows
    bsz, length, _ = p3.shape
    nh, dh = MLSTM_HEADS, norm_g.shape[1] // MLSTM_HEADS
    assert dh == LANES and chunk % LANES == 0

    def pspec(col):
        return pl.BlockSpec((None, length, dh), lambda b, h: (b, 0, col // dh + h))

    def wspec(rows, part):
        return pl.BlockSpec((rows, dh), lambda b, h: (0, part * nh + h))

    return pl.pallas_call(
        functools.partial(_mlstm_kernel, chunk=chunk),
        grid=(bsz, nh),
        in_specs=[pspec(col_q), pspec(col_k), pspec(col_v), pspec(col_o),
                  wspec(3, 0), wspec(3, 1), wspec(1, 0), wspec(1, 1),
                  pl.BlockSpec((1, dh), lambda b, h: (0, h)),
                  pl.BlockSpec((None, None, length, 6), lambda b, h: (b, h, 0, 0)),
                  pl.BlockSpec((None, None, 2, length), lambda b, h: (b, h, 0, 0))],
        out_specs=pl.BlockSpec((None, length, dh), lambda b, h: (b, 0, h)),
        out_shape=jax.ShapeDtypeStruct((bsz, length, nh * dh), BF16),
        scratch_shapes=[pltpu.VMEM((length, dh), BF16), pltpu.VMEM((length, dh), BF16),
                        pltpu.VMEM((length, 2 * dh), BF16), pltpu.VMEM((6, length, LANES), F32),
                        pltpu.VMEM((length, dh), F32), pltpu.VMEM((length, dh), F32)],
        compiler_params=_cparams(("parallel", "parallel"), tile),
        name="mlstm",
    )(p3, p3, p3, p3, conv_w, conv_w, conv_b, conv_b, norm_g, gcol, grow)


def _merge_kernel(yh_ref, ym_ref, gh_ref, gm_ref, x_ref, wa_ref, wb_ref, wo_ref, o_ref):
    a = jnp.dot(yh_ref[...], wa_ref[...], preferred_element_type=F32)
    b = jnp.dot(ym_ref[...], wb_ref[...], preferred_element_type=F32)
    t = jax.nn.sigmoid(gh_ref[...].astype(F32)) * a + jax.nn.sigmoid(gm_ref[...].astype(F32)) * b
    o_ref[...] = x_ref[...] + jnp.dot(t.astype(BF16), wo_ref[...], preferred_element_type=F32)


def _merge(y_hy, y_ml, p_big, x2, w_a, w_b, w_o, tile=TILES["merge"]):
    tm = tile.rows
    n, d = x2.shape
    wh, wm = y_hy.shape[1], y_ml.shape[1]
    return pl.pallas_call(
        _merge_kernel,
        grid=(n // tm,),
        in_specs=[pl.BlockSpec((tm, wh), lambda i: (i, 0)),
                  pl.BlockSpec((tm, wm), lambda i: (i, 0)),
                  pl.BlockSpec((tm, d), lambda i: (i, 0)),
                  pl.BlockSpec((tm, d), lambda i: (i, 1)),
                  pl.BlockSpec((tm, d), lambda i: (i, 0)),
                  _const_spec(w_a.shape), _const_spec(w_b.shape), _const_spec(w_o.shape)],
        out_specs=pl.BlockSpec((tm, d), lambda i: (i, 0)),
        out_shape=jax.ShapeDtypeStruct((n, d), F32),
        compiler_params=_cparams(("parallel",), tile),
        name="branch_merge",
    )(y_hy, y_ml, p_big, p_big, x2, w_a, w_b, w_o)


def _dense_ffn_kernel(x_ref, g_ref, wg_ref, wu_ref, wd_ref, o_ref, h_s):
    @pl.when(pl.program_id(1) == 0)
    def _():
        x = x_ref[...]
        h_s[...] = _rms(x, g_ref[...]).astype(BF16)
        o_ref[...] = x

    h = h_s[...]
    gt = jnp.dot(h, wg_ref[...].astype(BF16), preferred_element_type=F32)
    up = jnp.dot(h, wu_ref[...].astype(BF16), preferred_element_type=F32)
    act = (gt * jax.nn.sigmoid(gt) * up).astype(BF16)
    o_ref[...] += jnp.dot(act, wd_ref[...].astype(BF16), preferred_element_type=F32)


def _dense_ffn(x2, g, w_gate, w_up, w_down, tile=TILES["dense"]):
    tm, tf = tile.rows, tile.cols
    n, d = x2.shape
    f = w_gate.shape[1]
    return pl.pallas_call(
        _dense_ffn_kernel,
        grid=(n // tm, f // tf),
        in_specs=[pl.BlockSpec((tm, d), lambda i, j: (i, 0)),
                  pl.BlockSpec((1, d), lambda i, j: (0, 0)),
                  pl.BlockSpec((d, tf), lambda i, j: (0, j)),
                  pl.BlockSpec((d, tf), lambda i, j: (0, j)),
                  pl.BlockSpec((tf, d), lambda i, j: (j, 0))],
        out_specs=pl.BlockSpec((tm, d), lambda i, j: (i, 0)),
        out_shape=jax.ShapeDtypeStruct((n, d), F32),
        scratch_shapes=[pltpu.VMEM((tm, d), BF16)],
        compiler_params=_cparams(("parallel", "arbitrary"), tile),
        name="dense_swiglu",
    )(x2, g, w_gate, w_up, w_down)


def _store_rows_3d(ref3, val2):
    for s in range(ref3.shape[1]):
        ref3[:, s, :] = val2[:, s * LANES:(s + 1) * LANES]


def _router_kernel(x_ref, g_ref, rwh_ref, rwl_ref, rb_ref, h_ref, e_ref, gate_ref, rank_ref, cnt_ref, base_s):
    @pl.when(pl.program_id(0) == 0)
    def _():
        base_s[...] = jnp.zeros_like(base_s)

    hn = _rms(x_ref[...], g_ref[...])
    _store_rows_3d(h_ref, hn)
    logits = _dot_split(hn, rwh_ref[...], rwl_ref[...]) + rb_ref[...]
    lane = lax.broadcasted_iota(I32, logits.shape, 1)
    logits = jnp.where(lane < N_EXPERTS, logits, -jnp.inf)
    m1 = jnp.max(logits, axis=-1, keepdims=True)
    i1 = jnp.min(jnp.where(logits == m1, lane, LANES), axis=-1, keepdims=True)
    rest = jnp.where(lane == i1, -jnp.inf, logits)
    m2 = jnp.max(rest, axis=-1, keepdims=True)
    i2 = jnp.min(jnp.where(rest == m2, lane, LANES), axis=-1, keepdims=True)
    e = jnp.exp(m2 - m1)
    slot = lax.broadcasted_iota(I32, e_ref.shape, 1)
    e_ref[...] = jnp.where(slot == 0, i1, i2)
    gate_ref[...] = jnp.where(slot == 0, 1.0 / (1.0 + e), e / (1.0 + e))
    tm = logits.shape[0]
    oh1 = (lane == i1).astype(F32)
    oh2 = (lane == i2).astype(F32)
    both = oh1 + oh2
    earlier = (lax.broadcasted_iota(I32, (tm, tm), 1) < lax.broadcasted_iota(I32, (tm, tm), 0)).astype(BF16)
    before = jnp.dot(earlier, both.astype(BF16), preferred_element_type=F32) + base_s[...]
    r1 = jnp.sum(before * oh1, axis=-1, keepdims=True)
    r2 = jnp.sum(before * oh2, axis=-1, keepdims=True)
    rank_ref[...] = jnp.where(slot == 0, r1, r2).astype(I32)
    base_s[...] += jnp.sum(both, axis=0, keepdims=True)
    cnt_ref[...] = base_s[...].astype(I32)


def _router(x2, g, rw_pad, rb_pad, tile=TILES["router"]):
    tm = tile.rows
    n, d = x2.shape
    s = d // LANES
    rw_hi, rw_lo = _split_bf16(rw_pad)
    return pl.pallas_call(
        _router_kernel,
        grid=(n // tm,),
        in_specs=[pl.BlockSpec((tm, d), lambda i: (i, 0)),
                  pl.BlockSpec((1, d), lambda i: (0, 0)),
                  pl.BlockSpec((d, LANES), lambda i: (0, 0)),
                  pl.BlockSpec((d, LANES), lambda i: (0, 0)),
                  pl.BlockSpec((1, LANES), lambda i: (0, 0))],
        out_specs=[pl.BlockSpec((tm, s, LANES), lambda i: (i, 0, 0)),
                   pl.BlockSpec((tm, TOP_K), lambda i: (i, 0)),
                   pl.BlockSpec((tm, TOP_K), lambda i: (i, 0)),
                   pl.BlockSpec((tm, TOP_K), lambda i: (i, 0)),
                   pl.BlockSpec((1, LANES), lambda i: (0, 0))],
        out_shape=[jax.ShapeDtypeStruct((n, s, LANES), F32),
                   jax.ShapeDtypeStruct((n, TOP_K), I32),
                   jax.ShapeDtypeStruct((n, TOP_K), F32),
                   jax.ShapeDtypeStruct((n, TOP_K), I32),
                   jax.ShapeDtypeStruct((1, LANES), I32)],
        scratch_shapes=[pltpu.VMEM((1, LANES), F32)],
        compiler_params=_cparams(("arbitrary",), tile),
        name="moe_router",
    )(x2, g, rw_hi, rw_lo, rb_pad)


def _slab_pitch(s):
    return s if (s // SUBLANES) % 2 == 1 else s + SUBLANES


def _row_gather(src_hbm, idx_ref, idx_base, dst, sem, rows):
    slab = src_hbm.shape[1]

    def copy(r):
        return pltpu.make_async_copy(src_hbm.at[idx_ref[idx_base + r]], dst.at[r, pl.ds(0, slab)], sem)

    def start_one(r, carry):
        copy(r).start()
        return carry

    def wait_one(r, carry):
        copy(r).wait()
        return carry

    return (lambda: lax.fori_loop(0, rows, start_one, 0, unroll=8),
            lambda: lax.fori_loop(0, rows, wait_one, 0, unroll=8))


def _expert_ffn_kernel(be_ref, nu_ref, tok_ref, h_hbm, wg_ref, wu_ref, wd_ref, o_ref,
                       xg_s, x2_s, h_s, acc_s, sems):
    i, j = pl.program_id(0), pl.program_id(1)
    blk = xg_s.shape[1]
    n_used = nu_ref[0]
    last = pl.num_programs(1) - 1
    mid = pl.num_programs(1) // 2
    used = i < n_used
    has_next = i + 1 < n_used
    both = jnp.logical_and

    def gather(block):
        slot = block % 2
        return _row_gather(h_hbm, tok_ref, block * blk, xg_s.at[slot], sems.at[slot], blk)

    def unpack(block):
        slot = block % 2
        for s in range(h_hbm.shape[1]):
            x2_s[:, s * LANES:(s + 1) * LANES] = xg_s[slot, :, s, :]
        h_s[slot] = x2_s[...].astype(BF16)

    def partial_out():
        h = h_s[i % 2]
        gt = jnp.dot(h, wg_ref[...], preferred_element_type=F32)
        up = jnp.dot(h, wu_ref[...], preferred_element_type=F32)
        act = (gt * jax.nn.sigmoid(gt) * up).astype(BF16)
        return jnp.dot(act, wd_ref[...], preferred_element_type=F32)

    @pl.when(both(i == 0, j == 0))
    def _():
        start, wait = gather(0)
        start()
        wait()
        unpack(0)

    @pl.when(both(j == 0, has_next))
    def _():
        gather(i + 1)[0]()

    @pl.when(both(used, j == 0))
    def _():
        acc_s[...] = partial_out()

    @pl.when(both(used, both(both(j > 0, j < last), jnp.logical_or(j != mid, jnp.logical_not(has_next)))))
    def _():
        acc_s[...] += partial_out()

    @pl.when(both(both(used, has_next), j == mid))
    def _():
        gather(i + 1)[1]()
        unpack(i + 1)
        acc_s[...] += partial_out()

    @pl.when(both(used, j == last))
    def _():
        res = acc_s[...] + partial_out()
        for s in range(o_ref.shape[1]):
            o_ref[:, s, :] = res[:, s * LANES:(s + 1) * LANES]

    @pl.when(both(jnp.logical_not(used), j == last))
    def _():
        o_ref[...] = jnp.zeros_like(o_ref)


def _expert_ffn(h3, slot_tok, block_e, n_used, w_gate, w_up, w_down, tile=TILES["expert"]):
    blk, tf = tile.rows, tile.cols
    ns = slot_tok.shape[0]
    s = h3.shape[1]
    d = s * LANES
    f = w_gate.shape[2]
    nj = f // tf
    assert nj >= 3

    def jj(i, j, nu):
        return jnp.where(i < nu[0], j, nj - 1)

    return pl.pallas_call(
        _expert_ffn_kernel,
        grid_spec=pltpu.PrefetchScalarGridSpec(
            num_scalar_prefetch=3,
            grid=(ns // blk, nj),
            in_specs=[pl.BlockSpec(memory_space=pl.ANY),
                      pl.BlockSpec((None, d, tf), lambda i, j, be, nu, tok: (be[i], 0, jj(i, j, nu))),
                      pl.BlockSpec((None, d, tf), lambda i, j, be, nu, tok: (be[i], 0, jj(i, j, nu))),
                      pl.BlockSpec((None, tf, d), lambda i, j, be, nu, tok: (be[i], jj(i, j, nu), 0))],
            out_specs=pl.BlockSpec((blk, s, LANES), lambda i, j, be, nu, tok: (i, 0, 0)),
            scratch_shapes=[pltpu.VMEM((2, blk, _slab_pitch(s), LANES), F32), pltpu.VMEM((blk, d), F32),
                            pltpu.VMEM((2, blk, d), BF16), pltpu.VMEM((blk, d), F32),
                            pltpu.SemaphoreType.DMA((2,))],
        ),
        out_shape=jax.ShapeDtypeStruct((ns, s, LANES), F32),
        compiler_params=_cparams(("arbitrary", "arbitrary"), tile),
        name="expert_swiglu",
    )(block_e, n_used, slot_tok, h3, w_gate, w_up, w_down)


def _combine_kernel(dest_ref, x_ref, gate_ref, y_hbm, g_ref, o_ref, yg_s, sems, *, final_norm):
    i, n_tiles = pl.program_id(0), pl.num_programs(0)
    tm = x_ref.shape[0]

    def gather(tile):
        slot = tile % 2
        parts = [_row_gather(y_hbm, dest_ref, k * n_tiles * tm + tile * tm, yg_s.at[slot, k], sems.at[slot], tm)
                 for k in range(TOP_K)]
        return (lambda: [p[0]() for p in parts]), (lambda: [p[1]() for p in parts])

    @pl.when(i == 0)
    def _():
        gather(0)[0]()

    @pl.when(i + 1 < n_tiles)
    def _():
        gather(i + 1)[0]()

    gather(i)[1]()
    slot = i % 2
    gates = [jnp.broadcast_to(gate_ref[:, k:k + 1], (tm, LANES)) for k in range(TOP_K)]
    for s in range(y_hbm.shape[1]):
        cols = slice(s * LANES, (s + 1) * LANES)
        acc = x_ref[:, cols]
        for k in range(TOP_K):
            acc = acc + gates[k] * yg_s[slot, k, :, s, :]
        o_ref[:, cols] = acc
    if final_norm:
        o_ref[...] = _rms(o_ref[...], g_ref[...])


def _combine(x2, gate, yb, dest_by_k, g, final_norm, tile=TILES["combine"]):
    tm = tile.rows
    n, d = x2.shape
    s = d // LANES
    return pl.pallas_call(
        functools.partial(_combine_kernel, final_norm=final_norm),
        grid_spec=pltpu.PrefetchScalarGridSpec(
            num_scalar_prefetch=1,
            grid=(n // tm,),
            in_specs=[pl.BlockSpec((tm, d), lambda i, dest: (i, 0)),
                      pl.BlockSpec((tm, TOP_K), lambda i, dest: (i, 0)),
                      pl.BlockSpec(memory_space=pl.ANY),
                      pl.BlockSpec((1, d), lambda i, dest: (0, 0))],
            out_specs=pl.BlockSpec((tm, d), lambda i, dest: (i, 0)),
            scratch_shapes=[pltpu.VMEM((2, TOP_K, tm, _slab_pitch(s), LANES), F32),
                            pltpu.SemaphoreType.DMA((2,))],
        ),
        out_shape=jax.ShapeDtypeStruct((n, d), F32),
        compiler_params=_cparams(("arbitrary",), tile),
        name="moe_combine",
    )(dest_by_k, x2, gate, yb, g)


def _final_norm_kernel(x_ref, g_ref, o_ref):
    o_ref[...] = _rms(x_ref[...], g_ref[...])


def _final_norm(x2, g, tile=TILES["final_norm"]):
    tm = tile.rows
    n, d = x2.shape
    return pl.pallas_call(
        _final_norm_kernel,
        grid=(n // tm,),
        in_specs=[pl.BlockSpec((tm, d), lambda i: (i, 0)), pl.BlockSpec((1, d), lambda i: (0, 0))],
        out_specs=pl.BlockSpec((tm, d), lambda i: (i, 0)),
        out_shape=jax.ShapeDtypeStruct((n, d), F32),
        compiler_params=_cparams(("parallel",), tile),
        name="final_norm",
    )(x2, g)


def _routing_tables(top_e, rank, counts, blk):
    n = top_e.shape[0]
    nk = n * TOP_K
    flat_e = top_e.reshape(-1)
    padded = (counts + blk - 1) // blk * blk
    p_end = jnp.cumsum(padded)
    p_start = p_end - padded
    dest = (p_start[flat_e] + rank.reshape(-1)).astype(I32)
    n_blocks = -(-nk // blk) + N_EXPERTS
    flat_tok = jnp.arange(nk, dtype=I32) // TOP_K
    slot_tok = jnp.zeros((n_blocks * blk,), I32).at[dest].set(flat_tok, unique_indices=True)
    n_used = (p_end[-1] // blk).astype(I32)
    blocks = jnp.arange(n_blocks, dtype=I32)
    block_e = jnp.sum((p_end[None, :] <= (blocks * blk)[:, None]).astype(I32), axis=1)
    block_e = jnp.minimum(block_e, N_EXPERTS - 1)
    block_e = jnp.where(blocks < n_used, block_e, block_e[n_used - 1])
    dest_by_k = dest.reshape(n, TOP_K).T.reshape(-1)
    return slot_tok, block_e, n_used.reshape(1), dest_by_k


def _moe_layer(x2, norm_g, router_w, router_b, w_gate, w_up, w_down, out_g, final_norm):
    n, d = x2.shape
    rw = jnp.zeros((d, LANES), F32).at[:, :N_EXPERTS].set(router_w)
    rb = jnp.zeros((1, LANES), F32).at[0, :N_EXPERTS].set(router_b)
    h, top_e, gate, rank, counts = _router(x2, norm_g[None, :], rw, rb)
    slot_tok, block_e, n_used, dest_by_k = _routing_tables(top_e, rank, counts[0, :N_EXPERTS], TILES["expert"].rows)
    yb = _expert_ffn(h, slot_tok, block_e, n_used, w_gate.astype(BF16), w_up.astype(BF16), w_down.astype(BF16))
    return _combine(x2, gate, yb, dest_by_k, out_g[None, :], final_norm)


def _mixer_layer(x2, bsz, length, norm_g, w_in, b_in, hy_conv_w, hy_conv_b, hy_w1, hy_b1, hy_w2, hy_b2,
                 hy_w3, hy_freq, hy_decay, hy_skip, ml_conv_w, ml_conv_b, ml_norm_g, w_a, w_b, w_o,
                 ctab, stab, twiddle, zpos):
    n, d = x2.shape
    wh = hy_skip.shape[1]
    wm = ml_norm_g.shape[0]
    off_qk = (HYENA_ORDER + 1) * wh
    off_v = off_qk + 2 * wm
    off_o = off_v + wm
    off_gates = off_o + wm
    off_br = off_gates + 4 * MLSTM_HEADS
    w_cat = jnp.concatenate([w_in[:, off_br:], w_in[:, :off_gates]], axis=1).astype(BF16)
    b_cat = jnp.concatenate([b_in[off_br:], b_in[:off_gates]])[None, :]
    col_hy = 2 * d
    col_q = col_hy + off_qk
    col_k = col_q + wm
    col_v = col_hy + off_v
    col_o = col_hy + off_o

    p_big, gates = _in_proj(x2, norm_g[None, :], w_in[:, off_gates:off_br], b_in[None, off_gates:off_br],
                            w_cat, b_cat)
    p3 = p_big.reshape(bsz, length, p_big.shape[1])

    kp = LANES
    pad2 = lambda a, r, c: jnp.zeros((r, c), F32).at[:a.shape[0], :a.shape[1]].set(a)
    zp = pad2(zpos, length, kp)
    htab, hmid = _hyena_spectrum(
        zp, pad2(hy_w1, kp, kp), pad2(hy_b1[None, :], 1, kp), pad2(hy_w2, kp, kp), pad2(hy_b2[None, :], 1, kp),
        pad2(hy_freq[None, :], 1, kp), pad2(hy_w3, kp, hy_w3.shape[1]), hy_decay[None, :],
        ctab, stab, twiddle, wh)
    y_hy = _hyena_conv(p3, col_hy, hy_conv_w, hy_conv_b[None, :], htab, hmid, hy_skip[:, None, :], ctab, stab, wh)

    gcol, grow = _gate_prep(gates.reshape(bsz, length, 4 * MLSTM_HEADS))
    y_ml = _mlstm(p3, col_q, col_k, col_v, col_o, ml_conv_w, ml_conv_b[None, :], ml_norm_g[None, :],
                  gcol, grow)

    return _merge(y_hy.reshape(n, wh), y_ml.reshape(n, wm), p_big, x2,
                  w_a.astype(BF16), w_b.astype(BF16), w_o.astype(BF16))


def kernel(x, mix_norm_g, mix_w_in, mix_b_in, hy_conv_w, hy_conv_b, hy_filt_w1, hy_filt_b1, hy_filt_w2,
           hy_filt_b2, hy_filt_w3, hy_filt_freq, hy_filt_decay, hy_skip, ml_conv_w, ml_conv_b, ml_norm_g,
           mix_w_a, mix_w_b, mix_w_o, ffn_norm_g, dense_w_gate, dense_w_up, dense_w_down, moe_router_w,
           moe_router_b, moe_w_gate, moe_w_up, moe_w_down, final_norm_g):
    bsz, length, d = x.shape
    depth = mix_norm_g.shape[0]
    x2 = x.reshape(bsz * length, d)
    ctab, stab, twiddle = _dft_tables(length // 2)
    zpos = _hyena_positions(length)
    normed = False
    for layer in range(depth):
        x2 = _mixer_layer(x2, bsz, length, mix_norm_g[layer], mix_w_in[layer], mix_b_in[layer],
                          hy_conv_w[layer], hy_conv_b[layer], hy_filt_w1[layer], hy_filt_b1[layer],
                          hy_filt_w2[layer], hy_filt_b2[layer], hy_filt_w3[layer], hy_filt_freq[layer],
                          hy_filt_decay[layer], hy_skip[layer], ml_conv_w[layer], ml_conv_b[layer],
                          ml_norm_g[layer], mix_w_a[layer], mix_w_b[layer], mix_w_o[layer], ctab, stab, twiddle,
                          zpos)
        j = layer // 2
        if layer % 2 == 0:
            x2 = _dense_ffn(x2, ffn_norm_g[layer][None, :], dense_w_gate[j], dense_w_up[j], dense_w_down[j])
        else:
            normed = layer == depth - 1
            x2 = _moe_layer(x2, ffn_norm_g[layer], moe_router_w[j], moe_router_b[j], moe_w_gate[j],
                            moe_w_up[j], moe_w_down[j], final_norm_g, normed)
    if not normed:
        x2 = _final_norm(x2, final_norm_g[None, :])
    return x2.reshape(bsz, length, d)
```

```python
import functools
import math
from typing import NamedTuple

import jax
import jax.numpy as jnp
import numpy as np
from jax import lax
from jax.experimental import pallas as pl
from jax.experimental.pallas import tpu as pltpu

F32, BF16, I32 = jnp.float32, jnp.bfloat16, jnp.int32
HIGHEST = lax.Precision.HIGHEST

HYENA_ORDER = 2
HYENA_POS_BANDS = 16
MLSTM_HEADS = 8
N_EXPERTS = 8
TOP_K = 2
EPS = 1e-6

V7X_VMEM_BYTES = 64 * 1024 * 1024
LANES = 128
SUBLANES = 8
MXU_DIM = 256


class _Tile(NamedTuple):
    rows: int
    cols: int
    vmem_mib: int


TILES = dict(
    in_proj=_Tile(1024, 4 * MXU_DIM, 48),
    hyena_spectrum=_Tile(0, MXU_DIM, 48),
    hyena_conv=_Tile(1024, MXU_DIM, 56),
    mlstm_gate_prep=_Tile(256, 0, 32),
    mlstm=_Tile(256, 0, 40),
    merge=_Tile(256, 0, 48),
    dense=_Tile(1024, 256, 56),
    router=_Tile(512, 0, 32),
    expert=_Tile(512, 512, 52),
    combine=_Tile(512, 0, 48),
    final_norm=_Tile(512, 0, 32),
)


def _cparams(semantics, tile):
    assert tile.vmem_mib * 1024 * 1024 <= V7X_VMEM_BYTES
    return pltpu.CompilerParams(dimension_semantics=semantics, vmem_limit_bytes=tile.vmem_mib * 1024 * 1024)


def _const_spec(shape):
    nd = len(shape)
    return pl.BlockSpec(shape, lambda *_: (0,) * nd, pipeline_mode=pl.Buffered(1))


def _rms(x, g):
    return x * lax.rsqrt(jnp.mean(x * x, axis=-1, keepdims=True) + EPS) * g


def _shift_rows(u, direction):
    n = u.shape[0]
    row = lax.broadcasted_iota(I32, u.shape, 0)
    if direction > 0:
        return jnp.where(row == 0, 0.0, pltpu.roll(u, 1, axis=0))
    return jnp.where(row == n - 1, 0.0, pltpu.roll(u, n - 1, axis=0))


def _short_conv(u, w, b):
    return b + _shift_rows(u, 1) * w[0:1] + u * w[1:2] + _shift_rows(u, -1) * w[2:3]


def _log_sigmoid(x):
    return jnp.minimum(x, 0.0) - jnp.log1p(jnp.exp(-jnp.abs(x)))


def _dot_split(a, w_hi, w_lo):
    a_hi = a.astype(BF16)
    a_lo = (a - a_hi.astype(F32)).astype(BF16)
    return (jnp.dot(a_hi, w_hi, preferred_element_type=F32) + jnp.dot(a_lo, w_hi, preferred_element_type=F32)
            + jnp.dot(a_hi, w_lo, preferred_element_type=F32))


def _split_bf16(w):
    w_hi = w.astype(BF16)
    return w_hi, (w - w_hi.astype(F32)).astype(BF16)


def _in_proj_kernel(x_ref, g_ref, wg_ref, bg_ref, w_ref, b_ref, o_ref, gates_ref, h_s):
    @pl.when(pl.program_id(1) == 0)
    def _():
        h_s[...] = _rms(x_ref[...], g_ref[...]).astype(BF16)
        gates_ref[...] = jnp.dot(h_s[...], wg_ref[...], preferred_element_type=F32) + bg_ref[...]

    acc = jnp.dot(h_s[...], w_ref[...], preferred_element_type=F32) + b_ref[...]
    o_ref[...] = acc.astype(o_ref.dtype)


def _in_proj(x2, g, w_gates, b_gates, w, b, tile=TILES["in_proj"]):
    tm, tn = tile.rows, tile.cols
    n, d = x2.shape
    nc = w.shape[1]
    ng = w_gates.shape[1]
    return pl.pallas_call(
        _in_proj_kernel,
        grid=(n // tm, nc // tn),
        in_specs=[pl.BlockSpec((tm, d), lambda i, j: (i, 0)),
                  pl.BlockSpec((1, d), lambda i, j: (0, 0)),
                  pl.BlockSpec((d, ng), lambda i, j: (0, 0)),
                  pl.BlockSpec((1, ng), lambda i, j: (0, 0)),
                  pl.BlockSpec((d, tn), lambda i, j: (0, j)),
                  pl.BlockSpec((1, tn), lambda i, j: (0, j))],
        out_specs=[pl.BlockSpec((tm, tn), lambda i, j: (i, j)),
                   pl.BlockSpec((tm, ng), lambda i, j: (i, 0))],
        out_shape=[jax.ShapeDtypeStruct((n, nc), BF16), jax.ShapeDtypeStruct((n, ng), F32)],
        scratch_shapes=[pltpu.VMEM((tm, d), BF16)],
        compiler_params=_cparams(("parallel", "arbitrary"), tile),
        name="in_proj",
    )(x2, g, w_gates.astype(BF16), b_gates, w, b)


def _dft_tables(half):
    k = jnp.arange(half, dtype=I32)

    def cos_sin(rows):
        ang = ((rows[:, None] * k[None, :]) % (2 * half)).astype(F32) * (math.pi / half)
        return jnp.cos(ang), jnp.sin(ang)

    split = math.gcd(half, 32)
    c1, s1 = cos_sin(jnp.arange(half // split, dtype=I32) * split)
    c0, s0 = cos_sin(jnp.arange(split, dtype=I32))
    ctab = (c1[:, None, :] * c0[None, :, :] - s1[:, None, :] * s0[None, :, :]).reshape(half, half)
    stab = (s1[:, None, :] * c0[None, :, :] + c1[:, None, :] * s0[None, :, :]).reshape(half, half)
    tw = k.astype(F32)[:, None] * (math.pi / (2 * half))
    twiddle = jnp.stack([jnp.broadcast_to(jnp.cos(tw), (half, LANES)), jnp.broadcast_to(jnp.sin(tw), (half, LANES))])
    return ctab.astype(BF16), stab.astype(BF16), twiddle


def _hyena_positions(length):
    t = np.linspace(0.0, 1.0, length)[:, None]
    n = np.arange(length, dtype=np.float64)[:, None]
    bands = np.linspace(1e-4, HYENA_POS_BANDS - 1, HYENA_POS_BANDS)[None, :]
    ang = (2.0 * np.pi / length) * n * bands
    return jnp.asarray(np.concatenate([t, np.cos(ang), -np.sin(ang)], axis=-1), dtype=F32)


def _split_rows(val, slab_s):
    half = val.shape[0] // 2
    n_slabs = slab_s.shape[0]
    for s in range(n_slabs):
        slab_s[s] = val[:, s * LANES:(s + 1) * LANES]
    pick = lambda p: jnp.concatenate([slab_s[s, pl.ds(p, half, stride=2), :] for s in range(n_slabs)], axis=1)
    return pick(0), pick(1)


def _merge_rows(even, odd, slab_s):
    half = even.shape[0]
    n_slabs = slab_s.shape[0]
    for s in range(n_slabs):
        slab_s[s, pl.ds(0, half, stride=2), :] = even[:, s * LANES:(s + 1) * LANES]
        slab_s[s, pl.ds(1, half, stride=2), :] = odd[:, s * LANES:(s + 1) * LANES]
    return jnp.concatenate([slab_s[s] for s in range(n_slabs)], axis=1)


def _alt_sum(x):
    row = lax.broadcasted_iota(I32, (x.shape[0], 1), 0)
    return jnp.sum(jnp.where(row % 2 == 0, x, -x), axis=0, keepdims=True)


def _hyena_spectrum_kernel(z_ref, w1_ref, b1_ref, w2_ref, b2_ref, fr_ref, w3f_ref, w3b_ref,
                           decf_ref, decb_ref, c_ref, s_ref, tw_ref, tab_ref, mid_ref, slab_s, hid_s):
    half = c_ref.shape[0]
    length = 2 * half
    ct = w3f_ref.shape[1]
    dot_hi = functools.partial(jnp.dot, precision=HIGHEST, preferred_element_type=F32)
    dot = functools.partial(jnp.dot, preferred_element_type=F32)
    z = z_ref[...]

    @pl.when(jnp.logical_and(pl.program_id(0) == 0, pl.program_id(1) == 0))
    def _():
        fr = fr_ref[...]
        hid = jnp.sin(fr * (dot_hi(z, w1_ref[...]) + b1_ref[...]))
        hid_s[...] = jnp.sin(fr * (dot_hi(hid, w2_ref[...]) + b2_ref[...]))

    hid = hid_s[...]
    t = z[:, 0:1]
    ff = dot_hi(hid, w3f_ref[...]) * jnp.exp(-t * jnp.abs(decf_ref[...]))
    fb = dot_hi(hid, w3b_ref[...]) * jnp.exp(-t * jnp.abs(decb_ref[...]))
    row = lax.broadcasted_iota(I32, (length, 1), 0)
    l1 = jnp.sum(jnp.where(row == 0, jnp.abs(ff + fb), jnp.abs(ff) + jnp.abs(fb)), axis=0, keepdims=True)
    inv = 1.0 / l1
    se, so = _split_rows((ff + fb) * inv, slab_s)
    de, do = _split_rows((ff - fb) * inv, slab_s)
    wide = lambda a: jnp.concatenate([a] * (ct // LANES), axis=1)
    cw, sw = wide(tw_ref[0]), wide(tw_ref[1])
    c_t, s_t = c_ref[...], s_ref[...]
    bf = lambda a: a.astype(BF16)
    e_r = dot(c_t, bf(se))
    o_r, o_s = dot(c_t, bf(so)), dot(s_t, bf(so))
    t_r = cw * o_r - sw * o_s
    d_s = dot(s_t, bf(de))
    p_r, p_s = dot(c_t, bf(do)), dot(s_t, bf(do))
    t_i = -(cw * p_s + sw * p_r)
    krow = lax.broadcasted_iota(I32, (half, 1), 0)
    scale = jnp.where(krow == 0, 1.0 / (2 * length), 1.0 / length)
    lo_r, lo_i = (e_r + t_r) * scale, (t_i - d_s) * scale
    hi_r, hi_i = (e_r - t_r) * scale, (t_i + d_s) * scale

    def emit(p, lr, li, hr, hi):
        a_r, a_i = lr + hr, li - hi
        d_r, d_i = lr - hr, li + hi
        tab_ref[4 * p + 0] = a_r.astype(tab_ref.dtype)
        tab_ref[4 * p + 1] = a_i.astype(tab_ref.dtype)
        tab_ref[4 * p + 2] = (d_r * cw + d_i * sw).astype(tab_ref.dtype)
        tab_ref[4 * p + 3] = (d_i * cw - d_r * sw).astype(tab_ref.dtype)

    emit(0, lo_r, lo_i, hi_r, hi_i)
    emit(1, lo_r * cw - lo_i * sw, lo_r * sw + lo_i * cw, -(hi_r * cw + hi_i * sw), -(hi_i * cw - hi_r * sw))
    mid_ref[0:1, :] = _alt_sum(se) * (1.0 / length)
    mid_ref[1:2, :] = -_alt_sum(do) * (1.0 / length)


def _hyena_spectrum(z, w1, b1, w2, b2, freq, w3, decay, ctab, stab, twiddle, width, tile=TILES["hyena_spectrum"]):
    ct = tile.cols
    half = ctab.shape[0]
    kp = w1.shape[1]
    nct = width // ct
    col_f = lambda o, c: (0, o * 2 * nct + c)
    col_b = lambda o, c: (0, o * 2 * nct + nct + c)
    return pl.pallas_call(
        _hyena_spectrum_kernel,
        grid=(HYENA_ORDER, nct),
        in_specs=[_const_spec(z.shape), _const_spec(w1.shape), _const_spec(b1.shape),
                  _const_spec(w2.shape), _const_spec(b2.shape), _const_spec(freq.shape),
                  pl.BlockSpec((kp, ct), col_f), pl.BlockSpec((kp, ct), col_b),
                  pl.BlockSpec((1, ct), col_f), pl.BlockSpec((1, ct), col_b),
                  _const_spec(ctab.shape), _const_spec(stab.shape), _const_spec(twiddle.shape)],
        out_specs=[pl.BlockSpec((None, 8, half, ct), lambda o, c: (o, 0, 0, c)),
                   pl.BlockSpec((None, 2, ct), lambda o, c: (o, 0, c))],
        out_shape=[jax.ShapeDtypeStruct((HYENA_ORDER, 8, half, width), BF16),
                   jax.ShapeDtypeStruct((HYENA_ORDER, 2, width), F32)],
        scratch_shapes=[pltpu.VMEM((ct // LANES, 2 * half, LANES), F32), pltpu.VMEM((2 * half, kp), F32)],
        compiler_params=_cparams(("arbitrary", "arbitrary"), tile),
        name="hyena_spectrum",
    )(z, w1, b1, w2, b2, freq, w3, w3, decay, decay, ctab, stab, twiddle)


def _hyena_conv_kernel(pv_ref, p1_ref, p2_ref, wv_ref, w1_ref, w2_ref, bv_ref, b1_ref, b2_ref,
                       tab_ref, mid_ref, skip_ref, c_ref, s_ref, y_ref,
                       slab_s, z_s, g1_s, g2_s, zb_s, gr_s, gn_s, *, rc):
    half = c_ref.shape[0]
    dot = functools.partial(jnp.dot, preferred_element_type=F32)
    for dst, p_ref, w_ref, b_ref in ((z_s, pv_ref, wv_ref, bv_ref), (g1_s, p1_ref, w1_ref, b1_ref),
                                     (g2_s, p2_ref, w2_ref, b2_ref)):
        even, odd = _split_rows(_short_conv(p_ref[...].astype(F32), w_ref[...], b_ref[...]), slab_s)
        dst[0] = even
        dst[1] = odd
    gates = (g1_s, g2_s)
    row = lax.broadcasted_iota(I32, (half, 1), 0)
    sign = jnp.where(row % 2 == 0, 1.0, -1.0)
    for o in range(HYENA_ORDER):
        zb_s[0] = z_s[0].astype(BF16)
        zb_s[1] = z_s[1].astype(BF16)
        a_e, a_o = _alt_sum(z_s[0]), _alt_sum(z_s[1])
        mid = (mid_ref[o, 0:1, :] * a_e + mid_ref[o, 1:2, :] * a_o,
               mid_ref[o, 0:1, :] * a_o - mid_ref[o, 1:2, :] * a_e)
        for r in range(half // rc):
            rows = slice(r * rc, (r + 1) * rc)
            e_r, e_s = dot(c_ref[rows, :], zb_s[0]).astype(BF16), dot(s_ref[rows, :], zb_s[0]).astype(BF16)
            o_r, o_s = dot(c_ref[rows, :], zb_s[1]).astype(BF16), dot(s_ref[rows, :], zb_s[1]).astype(BF16)
            for p in range(2):
                a_r, a_i = tab_ref[o, 4 * p + 0, rows, :], tab_ref[o, 4 * p + 1, rows, :]
                b_r, b_i = tab_ref[o, 4 * p + 2, rows, :], tab_ref[o, 4 * p + 3, rows, :]
                gr_s[p, rows, :] = (a_r * e_r + a_i * e_s) + (b_r * o_r + b_i * o_s)
                gn_s[p, rows, :] = (a_r * e_s - a_i * e_r) + (b_r * o_s - b_i * o_r)
        for p in range(2):
            for r in range(half // rc):
                rows = slice(r * rc, (r + 1) * rc)
                conv = dot(c_ref[rows, :], gr_s[p]) + dot(s_ref[rows, :], gn_s[p]) + sign[rows] * mid[p]
                z_s[p, rows, :] = gates[o][p, rows, :] * (conv + skip_ref[o] * z_s[p, rows, :])
    y_ref[...] = _merge_rows(z_s[0], z_s[1], slab_s).astype(y_ref.dtype)


def _hyena_conv(p3, col0, conv_w, conv_b, tab, mid, skip, ctab, stab, width, tile=TILES["hyena_conv"]):
    rc, ct = tile.rows, tile.cols
    bsz, length, _ = p3.shape
    half = length // 2
    nct = width // ct
    c0 = col0 // ct

    def pspec(part):
        return pl.BlockSpec((None, length, ct), lambda c, b: (b, 0, c0 + part * nct + c))

    def wspec(rows, part):
        return pl.BlockSpec((rows, ct), lambda c, b: (0, part * nct + c))

    scratch = ([pltpu.VMEM((ct // LANES, length, LANES), F32)] + [pltpu.VMEM((2, half, ct), F32)] * 3
               + [pltpu.VMEM((2, half, ct), BF16)] * 3)
    return pl.pallas_call(
        functools.partial(_hyena_conv_kernel, rc=rc),
        grid=(nct, bsz),
        in_specs=[pspec(0), pspec(1), pspec(2),
                  wspec(3, 0), wspec(3, 1), wspec(3, 2), wspec(1, 0), wspec(1, 1), wspec(1, 2),
                  pl.BlockSpec((HYENA_ORDER, 8, half, ct), lambda c, b: (0, 0, 0, c),
                               pipeline_mode=pl.Buffered(1)),
                  pl.BlockSpec((HYENA_ORDER, 2, ct), lambda c, b: (0, 0, c)),
                  pl.BlockSpec((HYENA_ORDER, 1, ct), lambda c, b: (0, 0, c)),
                  _const_spec(ctab.shape), _const_spec(stab.shape)],
        out_specs=pl.BlockSpec((None, length, ct), lambda c, b: (b, 0, c)),
        out_shape=jax.ShapeDtypeStruct((bsz, length, width), BF16),
        scratch_shapes=scratch,
        compiler_params=_cparams(("parallel", "parallel"), tile),
        name="hyena_conv",
    )(p3, p3, p3, conv_w, conv_w, conv_w, conv_b, conv_b, conv_b, tab, mid, skip, ctab, stab)


def _gate_prep_kernel(ic_ref, fc_ref, ir_ref, fr_ref, col_ref, row_ref, *, chunk):
    length, nch = ic_ref.shape
    nh = nch // 2
    r = lax.broadcasted_iota(I32, (chunk, chunk), 0)
    c = lax.broadcasted_iota(I32, (chunk, chunk), 1)
    lower = (c <= r).astype(F32)
    upper = (c >= r).astype(F32)
    dot_hi = functools.partial(jnp.dot, precision=HIGHEST, preferred_element_type=F32)
    fwd_col = lax.broadcasted_iota(I32, (chunk, nch), 1) < nh
    fwd_row = lax.broadcasted_iota(I32, (nch, chunk), 0) < nh
    pos = lax.broadcasted_iota(I32, (chunk, nch), 0)
    for ch in range(length // chunk):
        rows = slice(ch * chunk, (ch + 1) * chunk)
        lf = _log_sigmoid(fc_ref[rows, :])
        b = jnp.where(fwd_col, dot_hi(lower, lf), dot_hi(upper, lf))
        u = ic_ref[rows, :] - b
        cm_f, cm_b = u, u
        shift = 1
        while shift < chunk:
            cm_f = jnp.maximum(cm_f, jnp.where(pos >= shift, pltpu.roll(cm_f, shift, axis=0), -jnp.inf))
            cm_b = jnp.maximum(cm_b, jnp.where(pos + shift < chunk, pltpu.roll(cm_b, chunk - shift, axis=0), -jnp.inf))
            shift *= 2
        col_ref[0, rows, :] = b
        col_ref[1, rows, :] = u
        col_ref[2, rows, :] = jnp.where(fwd_col, cm_f, cm_b)
        lfr = _log_sigmoid(fr_ref[:, rows])
        b_r = jnp.where(fwd_row, dot_hi(lfr, upper), dot_hi(lfr, lower))
        row_ref[:, rows] = ir_ref[:, rows] - b_r


def _gate_prep(gates3, tile=TILES["mlstm_gate_prep"]):
    chunk = tile.rows
    bsz, length, nch4 = gates3.shape
    nh = nch4 // 4
    i_col = jnp.concatenate([gates3[..., :nh], gates3[..., 2 * nh:3 * nh]], axis=-1)
    f_col = jnp.concatenate([gates3[..., nh:2 * nh], gates3[..., 3 * nh:]], axis=-1)
    nch = 2 * nh
    cspec = pl.BlockSpec((None, length, nch), lambda b: (b, 0, 0))
    rspec = pl.BlockSpec((None, nch, length), lambda b: (b, 0, 0))
    col, row = pl.pallas_call(
        functools.partial(_gate_prep_kernel, chunk=chunk),
        grid=(bsz,),
        in_specs=[cspec, cspec, rspec, rspec],
        out_specs=[pl.BlockSpec((None, 3, length, nch), lambda b: (b, 0, 0, 0)), rspec],
        out_shape=[jax.ShapeDtypeStruct((bsz, 3, length, nch), F32),
                   jax.ShapeDtypeStruct((bsz, nch, length), F32)],
        compiler_params=_cparams(("parallel",), tile),
        name="mlstm_gate_prep",
    )(i_col, f_col, jnp.swapaxes(i_col, 1, 2), jnp.swapaxes(f_col, 1, 2))
    col = col.reshape(bsz, 3, length, 2, nh).transpose(0, 4, 2, 3, 1).reshape(bsz, nh, length, 6)
    row = row.reshape(bsz, 2, nh, length).transpose(0, 2, 1, 3)
    return col, row


def _mlstm_kernel(pq_ref, pk_ref, pv_ref, po_ref, cwq_ref, cwk_ref, cbq_ref, cbk_ref, ng_ref,
                  gcol_ref, grow_ref, y_ref, q_s, k_s, v1_s, tab_s, hf_s, hb_s, *, chunk):
    length, dk = pq_ref.shape
    nc = length // chunk
    rep = chunk // LANES

    def conv_silu(p_ref, w_ref, b_ref):
        c = _short_conv(p_ref[...].astype(F32), w_ref[...], b_ref[...])
        return c * jax.nn.sigmoid(c)

    q_s[...] = (conv_silu(pq_ref, cwq_ref, cbq_ref) * (dk ** -0.5)).astype(BF16)
    k_s[...] = conv_silu(pk_ref, cwk_ref, cbk_ref).astype(BF16)
    v1_s[:, :dk] = pv_ref[...]
    v1_s[:, dk:] = jnp.ones((length, dk), BF16)
    for t in range(tab_s.shape[0]):
        tab_s[t] = jnp.broadcast_to(gcol_ref[:, t:t + 1], (length, LANES))

    row_i = lax.broadcasted_iota(I32, (chunk, chunk), 0)
    col_i = lax.broadcasted_iota(I32, (chunk, chunk), 1)
    wide = lambda a: jnp.concatenate([a] * rep, axis=1)
    both = lambda a: jnp.concatenate([a, a], axis=1)

    def step(c, reverse, state, m):
        d = 1 if reverse else 0
        rows = slice(c * chunk, (c + 1) * chunk)
        edge = c * chunk if reverse else (c + 1) * chunk - 1
        q, k, v1 = q_s[rows, :], k_s[rows, :], v1_s[rows, :]
        b_t, u_t, cm_t = tab_s[3 * d, rows, :], tab_s[3 * d + 1, rows, :], tab_s[3 * d + 2, rows, :]
        u_row = grow_ref[d:d + 1, rows]
        g = tab_s[3 * d, edge:edge + 1, :]
        u_max = tab_s[3 * d + 2, edge:edge + 1, :]
        mask = (col_i >= row_i) if reverse else (col_i <= row_i)
        mm = jnp.maximum(cm_t, m)
        p = jnp.exp(jnp.where(mask, u_row - wide(mm), -jnp.inf))
        qk = lax.dot_general(q, k, (((1,), (1,)), ((), ())), preferred_element_type=F32)
        intra = jnp.dot((qk * p).astype(BF16), v1, preferred_element_type=F32)
        inter = jnp.dot(q, state.astype(BF16), preferred_element_type=F32)
        nd = intra + both(jnp.exp(m - mm)) * inter
        h = nd[:, :dk] / jnp.maximum(jnp.abs(nd[:, dk:]), jnp.exp(-(b_t + mm)))
        kw = (k.astype(F32) * jnp.exp(u_t - u_max)).astype(BF16)
        upd = lax.dot_general(kw, v1, (((0,), (0,)), ((), ())), preferred_element_type=F32)
        m_loc = g + u_max
        m_new = jnp.maximum(g + m, m_loc)
        state = both(jnp.exp(g + m - m_new)) * state + both(jnp.exp(m_loc - m_new)) * upd
        return h, state, m_new

    zero = (jnp.zeros((dk, 2 * dk), F32), jnp.zeros((1, LANES), F32))
    st_f, m_f = zero
    st_b, m_b = zero
    for c in range(nc):
        h, st_f, m_f = step(c, False, st_f, m_f)
        hf_s[c * chunk:(c + 1) * chunk, :] = h
        cb = nc - 1 - c
        h, st_b, m_b = step(cb, True, st_b, m_b)
        hb_s[cb * chunk:(cb + 1) * chunk, :] = h

    ht = hf_s[...] + hb_s[...]
    y_ref[...] = (_rms(ht, ng_ref[...]) * jax.nn.sigmoid(po_ref[...].astype(F32))).astype(y_ref.dtype)


def _mlstm(p3, col_q, col_k, col_v, col_o, conv_w, conv_b, norm_g, gcol, grow, tile=TILES["mlstm"]):
    chunk = tile.rows
    bsz, length, _ = p3.shape
    nh, dh = MLSTM_HEADS, norm_g.shape[1] // MLSTM_HEADS
    assert dh == LANES and chunk % LANES == 0

    def pspec(col):
        return pl.BlockSpec((None, length, dh), lambda b, h: (b, 0, col // dh + h))

    def wspec(rows, part):
        return pl.BlockSpec((rows, dh), lambda b, h: (0, part * nh + h))

    return pl.pallas_call(
        functools.partial(_mlstm_kernel, chunk=chunk),
        grid=(bsz, nh),
        in_specs=[pspec(col_q), pspec(col_k), pspec(col_v), pspec(col_o),
                  wspec(3, 0), wspec(3, 1), wspec(1, 0), wspec(1, 1),
                  pl.BlockSpec((1, dh), lambda b, h: (0, h)),
                  pl.BlockSpec((None, None, length, 6), lambda b, h: (b, h, 0, 0)),
                  pl.BlockSpec((None, None, 2, length), lambda b, h: (b, h, 0, 0))],
        out_specs=pl.BlockSpec((None, length, dh), lambda b, h: (b, 0, h)),
        out_shape=jax.ShapeDtypeStruct((bsz, length, nh * dh), BF16),
        scratch_shapes=[pltpu.VMEM((length, dh), BF16), pltpu.VMEM((length, dh), BF16),
                        pltpu.VMEM((length, 2 * dh), BF16), pltpu.VMEM((6, length, LANES), F32),
                        pltpu.VMEM((length, dh), F32), pltpu.VMEM((length, dh), F32)],
        compiler_params=_cparams(("parallel", "parallel"), tile),
        name="mlstm",
    )(p3, p3, p3, p3, conv_w, conv_w, conv_b, conv_b, norm_g, gcol, grow)


def _merge_kernel(yh_ref, ym_ref, gh_ref, gm_ref, x_ref, wa_ref, wb_ref, wo_ref, o_ref):
    a = jnp.dot(yh_ref[...], wa_ref[...], preferred_element_type=F32)
    b = jnp.dot(ym_ref[...], wb_ref[...], preferred_element_type=F32)
    t = jax.nn.sigmoid(gh_ref[...].astype(F32)) * a + jax.nn.sigmoid(gm_ref[...].astype(F32)) * b
    o_ref[...] = x_ref[...] + jnp.dot(t.astype(BF16), wo_ref[...], preferred_element_type=F32)


def _merge(y_hy, y_ml, p_big, x2, w_a, w_b, w_o, tile=TILES["merge"]):
    tm = tile.rows
    n, d = x2.shape
    wh, wm = y_hy.shape[1], y_ml.shape[1]
    return pl.pallas_call(
        _merge_kernel,
        grid=(n // tm,),
        in_specs=[pl.BlockSpec((tm, wh), lambda i: (i, 0)),
                  pl.BlockSpec((tm, wm), lambda i: (i, 0)),
                  pl.BlockSpec((tm, d), lambda i: (i, 0)),
                  pl.BlockSpec((tm, d), lambda i: (i, 1)),
                  pl.BlockSpec((tm, d), lambda i: (i, 0)),
                  _const_spec(w_a.shape), _const_spec(w_b.shape), _const_spec(w_o.shape)],
        out_specs=pl.BlockSpec((tm, d), lambda i: (i, 0)),
        out_shape=jax.ShapeDtypeStruct((n, d), F32),
        compiler_params=_cparams(("parallel",), tile),
        name="branch_merge",
    )(y_hy, y_ml, p_big, p_big, x2, w_a, w_b, w_o)


def _dense_ffn_kernel(x_ref, g_ref, wg_ref, wu_ref, wd_ref, o_ref, h_s):
    @pl.when(pl.program_id(1) == 0)
    def _():
        x = x_ref[...]
        h_s[...] = _rms(x, g_ref[...]).astype(BF16)
        o_ref[...] = x

    h = h_s[...]
    gt = jnp.dot(h, wg_ref[...].astype(BF16), preferred_element_type=F32)
    up = jnp.dot(h, wu_ref[...].astype(BF16), preferred_element_type=F32)
    act = (gt * jax.nn.sigmoid(gt) * up).astype(BF16)
    o_ref[...] += jnp.dot(act, wd_ref[...].astype(BF16), preferred_element_type=F32)


def _dense_ffn(x2, g, w_gate, w_up, w_down, tile=TILES["dense"]):
    tm, tf = tile.rows, tile.cols
    n, d = x2.shape
    f = w_gate.shape[1]
    return pl.pallas_call(
        _dense_ffn_kernel,
        grid=(n // tm, f // tf),
        in_specs=[pl.BlockSpec((tm, d), lambda i, j: (i, 0)),
                  pl.BlockSpec((1, d), lambda i, j: (0, 0)),
                  pl.BlockSpec((d, tf), lambda i, j: (0, j)),
                  pl.BlockSpec((d, tf), lambda i, j: (0, j)),
                  pl.BlockSpec((tf, d), lambda i, j: (j, 0))],
        out_specs=pl.BlockSpec((tm, d), lambda i, j: (i, 0)),
        out_shape=jax.ShapeDtypeStruct((n, d), F32),
        scratch_shapes=[pltpu.VMEM((tm, d), BF16)],
        compiler_params=_cparams(("parallel", "arbitrary"), tile),
        name="dense_swiglu",
    )(x2, g, w_gate, w_up, w_down)


def _store_rows_3d(ref3, val2):
    for s in range(ref3.shape[1]):
        ref3[:, s, :] = val2[:, s * LANES:(s + 1) * LANES]


def _router_kernel(x_ref, g_ref, rwh_ref, rwl_ref, rb_ref, h_ref, e_ref, gate_ref, rank_ref, cnt_ref, base_s):
    @pl.when(pl.program_id(0) == 0)
    def _():
        base_s[...] = jnp.zeros_like(base_s)

    hn = _rms(x_ref[...], g_ref[...])
    _store_rows_3d(h_ref, hn)
    logits = _dot_split(hn, rwh_ref[...], rwl_ref[...]) + rb_ref[...]
    lane = lax.broadcasted_iota(I32, logits.shape, 1)
    logits = jnp.where(lane < N_EXPERTS, logits, -jnp.inf)
    m1 = jnp.max(logits, axis=-1, keepdims=True)
    i1 = jnp.min(jnp.where(logits == m1, lane, LANES), axis=-1, keepdims=True)
    rest = jnp.where(lane == i1, -jnp.inf, logits)
    m2 = jnp.max(rest, axis=-1, keepdims=True)
    i2 = jnp.min(jnp.where(rest == m2, lane, LANES), axis=-1, keepdims=True)
    e = jnp.exp(m2 - m1)
    slot = lax.broadcasted_iota(I32, e_ref.shape, 1)
    e_ref[...] = jnp.where(slot == 0, i1, i2)
    gate_ref[...] = jnp.where(slot == 0, 1.0 / (1.0 + e), e / (1.0 + e))
    tm = logits.shape[0]
    oh1 = (lane == i1).astype(F32)
    oh2 = (lane == i2).astype(F32)
    both = oh1 + oh2
    earlier = (lax.broadcasted_iota(I32, (tm, tm), 1) < lax.broadcasted_iota(I32, (tm, tm), 0)).astype(BF16)
    before = jnp.dot(earlier, both.astype(BF16), preferred_element_type=F32) + base_s[...]
    r1 = jnp.sum(before * oh1, axis=-1, keepdims=True)
    r2 = jnp.sum(before * oh2, axis=-1, keepdims=True)
    rank_ref[...] = jnp.where(slot == 0, r1, r2).astype(I32)
    base_s[...] += jnp.sum(both, axis=0, keepdims=True)
    cnt_ref[...] = base_s[...].astype(I32)


def _router(x2, g, rw_pad, rb_pad, tile=TILES["router"]):
    tm = tile.rows
    n, d = x2.shape
    s = d // LANES
    rw_hi, rw_lo = _split_bf16(rw_pad)
    return pl.pallas_call(
        _router_kernel,
        grid=(n // tm,),
        in_specs=[pl.BlockSpec((tm, d), lambda i: (i, 0)),
                  pl.BlockSpec((1, d), lambda i: (0, 0)),
                  pl.BlockSpec((d, LANES), lambda i: (0, 0)),
                  pl.BlockSpec((d, LANES), lambda i: (0, 0)),
                  pl.BlockSpec((1, LANES), lambda i: (0, 0))],
        out_specs=[pl.BlockSpec((tm, s, LANES), lambda i: (i, 0, 0)),
                   pl.BlockSpec((tm, TOP_K), lambda i: (i, 0)),
                   pl.BlockSpec((tm, TOP_K), lambda i: (i, 0)),
                   pl.BlockSpec((tm, TOP_K), lambda i: (i, 0)),
                   pl.BlockSpec((1, LANES), lambda i: (0, 0))],
        out_shape=[jax.ShapeDtypeStruct((n, s, LANES), F32),
                   jax.ShapeDtypeStruct((n, TOP_K), I32),
                   jax.ShapeDtypeStruct((n, TOP_K), F32),
                   jax.ShapeDtypeStruct((n, TOP_K), I32),
                   jax.ShapeDtypeStruct((1, LANES), I32)],
        scratch_shapes=[pltpu.VMEM((1, LANES), F32)],
        compiler_params=_cparams(("arbitrary",), tile),
        name="moe_router",
    )(x2, g, rw_hi, rw_lo, rb_pad)


def _slab_pitch(s):
    return s if (s // SUBLANES) % 2 == 1 else s + SUBLANES


def _row_gather(src_hbm, idx_ref, idx_base, dst, sem, rows):
    slab = src_hbm.shape[1]

    def copy(r):
        return pltpu.make_async_copy(src_hbm.at[idx_ref[idx_base + r]], dst.at[r, pl.ds(0, slab)], sem)

    def start_one(r, carry):
        copy(r).start()
        return carry

    def wait_one(r, carry):
        copy(r).wait()
        return carry

    return (lambda: lax.fori_loop(0, rows, start_one, 0, unroll=8),
            lambda: lax.fori_loop(0, rows, wait_one, 0, unroll=8))


def _expert_ffn_kernel(be_ref, nu_ref, tok_ref, h_hbm, wg_ref, wu_ref, wd_ref, o_ref,
                       xg_s, x2_s, h_s, acc_s, sems):
    i, j = pl.program_id(0), pl.program_id(1)
    blk = xg_s.shape[1]
    n_used = nu_ref[0]
    last = pl.num_programs(1) - 1
    mid = pl.num_programs(1) // 2
    used = i < n_used
    has_next = i + 1 < n_used
    both = jnp.logical_and

    def gather(block):
        slot = block % 2
        return _row_gather(h_hbm, tok_ref, block * blk, xg_s.at[slot], sems.at[slot], blk)

    def unpack(block):
        slot = block % 2
        for s in range(h_hbm.shape[1]):
            x2_s[:, s * LANES:(s + 1) * LANES] = xg_s[slot, :, s, :]
        h_s[slot] = x2_s[...].astype(BF16)

    def partial_out():
        h = h_s[i % 2]
        gt = jnp.dot(h, wg_ref[...], preferred_element_type=F32)
        up = jnp.dot(h, wu_ref[...], preferred_element_type=F32)
        act = (gt * jax.nn.sigmoid(gt) * up).astype(BF16)
        return jnp.dot(act, wd_ref[...], preferred_element_type=F32)

    @pl.when(both(i == 0, j == 0))
    def _():
        start, wait = gather(0)
        start()
        wait()
        unpack(0)

    @pl.when(both(j == 0, has_next))
    def _():
        gather(i + 1)[0]()

    @pl.when(both(used, j == 0))
    def _():
        acc_s[...] = partial_out()

    @pl.when(both(used, both(both(j > 0, j < last), jnp.logical_or(j != mid, jnp.logical_not(has_next)))))
    def _():
        acc_s[...] += partial_out()

    @pl.when(both(both(used, has_next), j == mid))
    def _():
        gather(i + 1)[1]()
        unpack(i + 1)
        acc_s[...] += partial_out()

    @pl.when(both(used, j == last))
    def _():
        res = acc_s[...] + partial_out()
        for s in range(o_ref.shape[1]):
            o_ref[:, s, :] = res[:, s * LANES:(s + 1) * LANES]

    @pl.when(both(jnp.logical_not(used), j == last))
    def _():
        o_ref[...] = jnp.zeros_like(o_ref)


def _expert_ffn(h3, slot_tok, block_e, n_used, w_gate, w_up, w_down, tile=TILES["expert"]):
    blk, tf = tile.rows, tile.cols
    ns = slot_tok.shape[0]
    s = h3.shape[1]
    d = s * LANES
    f = w_gate.shape[2]
    nj = f // tf
    assert nj >= 3

    def jj(i, j, nu):
        return jnp.where(i < nu[0], j, nj - 1)

    return pl.pallas_call(
        _expert_ffn_kernel,
        grid_spec=pltpu.PrefetchScalarGridSpec(
            num_scalar_prefetch=3,
            grid=(ns // blk, nj),
            in_specs=[pl.BlockSpec(memory_space=pl.ANY),
                      pl.BlockSpec((None, d, tf), lambda i, j, be, nu, tok: (be[i], 0, jj(i, j, nu))),
                      pl.BlockSpec((None, d, tf), lambda i, j, be, nu, tok: (be[i], 0, jj(i, j, nu))),
                      pl.BlockSpec((None, tf, d), lambda i, j, be, nu, tok: (be[i], jj(i, j, nu), 0))],
            out_specs=pl.BlockSpec((blk, s, LANES), lambda i, j, be, nu, tok: (i, 0, 0)),
            scratch_shapes=[pltpu.VMEM((2, blk, _slab_pitch(s), LANES), F32), pltpu.VMEM((blk, d), F32),
                            pltpu.VMEM((2, blk, d), BF16), pltpu.VMEM((blk, d), F32),
                            pltpu.SemaphoreType.DMA((2,))],
        ),
        out_shape=jax.ShapeDtypeStruct((ns, s, LANES), F32),
        compiler_params=_cparams(("arbitrary", "arbitrary"), tile),
        name="expert_swiglu",
    )(block_e, n_used, slot_tok, h3, w_gate, w_up, w_down)


def _combine_kernel(dest_ref, x_ref, gate_ref, y_hbm, g_ref, o_ref, yg_s, sems, *, final_norm):
    i, n_tiles = pl.program_id(0), pl.num_programs(0)
    tm = x_ref.shape[0]

    def gather(tile):
        slot = tile % 2
        parts = [_row_gather(y_hbm, dest_ref, k * n_tiles * tm + tile * tm, yg_s.at[slot, k], sems.at[slot], tm)
                 for k in range(TOP_K)]
        return (lambda: [p[0]() for p in parts]), (lambda: [p[1]() for p in parts])

    @pl.when(i == 0)
    def _():
        gather(0)[0]()

    @pl.when(i + 1 < n_tiles)
    def _():
        gather(i + 1)[0]()

    gather(i)[1]()
    slot = i % 2
    gates = [jnp.broadcast_to(gate_ref[:, k:k + 1], (tm, LANES)) for k in range(TOP_K)]
    for s in range(y_hbm.shape[1]):
        cols = slice(s * LANES, (s + 1) * LANES)
        acc = x_ref[:, cols]
        for k in range(TOP_K):
            acc = acc + gates[k] * yg_s[slot, k, :, s, :]
        o_ref[:, cols] = acc
    if final_norm:
        o_ref[...] = _rms(o_ref[...], g_ref[...])


def _combine(x2, gate, yb, dest_by_k, g, final_norm, tile=TILES["combine"]):
    tm = tile.rows
    n, d = x2.shape
    s = d // LANES
    return pl.pallas_call(
        functools.partial(_combine_kernel, final_norm=final_norm),
        grid_spec=pltpu.PrefetchScalarGridSpec(
            num_scalar_prefetch=1,
            grid=(n // tm,),
            in_specs=[pl.BlockSpec((tm, d), lambda i, dest: (i, 0)),
                      pl.BlockSpec((tm, TOP_K), lambda i, dest: (i, 0)),
                      pl.BlockSpec(memory_space=pl.ANY),
                      pl.BlockSpec((1, d), lambda i, dest: (0, 0))],
            out_specs=pl.BlockSpec((tm, d), lambda i, dest: (i, 0)),
            scratch_shapes=[pltpu.VMEM((2, TOP_K, tm, _slab_pitch(s), LANES), F32),
                            pltpu.SemaphoreType.DMA((2,))],
        ),
        out_shape=jax.ShapeDtypeStruct((n, d), F32),
        compiler_params=_cparams(("arbitrary",), tile),
        name="moe_combine",
    )(dest_by_k, x2, gate, yb, g)


def _final_norm_kernel(x_ref, g_ref, o_ref):
    o_ref[...] = _rms(x_ref[...], g_ref[...])


def _final_norm(x2, g, tile=TILES["final_norm"]):
    tm = tile.rows
    n, d = x2.shape
    return pl.pallas_call(
        _final_norm_kernel,
        grid=(n // tm,),
        in_specs=[pl.BlockSpec((tm, d), lambda i: (i, 0)), pl.BlockSpec((1, d), lambda i: (0, 0))],
        out_specs=pl.BlockSpec((tm, d), lambda i: (i, 0)),
        out_shape=jax.ShapeDtypeStruct((n, d), F32),
        compiler_params=_cparams(("parallel",), tile),
        name="final_norm",
    )(x2, g)


def _routing_tables(top_e, rank, counts, blk):
    n = top_e.shape[0]
    nk = n * TOP_K
    flat_e = top_e.reshape(-1)
    padded = (counts + blk - 1) // blk * blk
    p_end = jnp.cumsum(padded)
    p_start = p_end - padded
    dest = (p_start[flat_e] + rank.reshape(-1)).astype(I32)
    n_blocks = -(-nk // blk) + N_EXPERTS
    flat_tok = jnp.arange(nk, dtype=I32) // TOP_K
    slot_tok = jnp.zeros((n_blocks * blk,), I32).at[dest].set(flat_tok, unique_indices=True)
    n_used = (p_end[-1] // blk).astype(I32)
    blocks = jnp.arange(n_blocks, dtype=I32)
    block_e = jnp.sum((p_end[None, :] <= (blocks * blk)[:, None]).astype(I32), axis=1)
    block_e = jnp.minimum(block_e, N_EXPERTS - 1)
    block_e = jnp.where(blocks < n_used, block_e, block_e[n_used - 1])
    dest_by_k = dest.reshape(n, TOP_K).T.reshape(-1)
    return slot_tok, block_e, n_used.reshape(1), dest_by_k


def _moe_layer(x2, norm_g, router_w, router_b, w_gate, w_up, w_down, out_g, final_norm):
    n, d = x2.shape
    rw = jnp.zeros((d, LANES), F32).at[:, :N_EXPERTS].set(router_w)
    rb = jnp.zeros((1, LANES), F32).at[0, :N_EXPERTS].set(router_b)
    h, top_e, gate, rank, counts = _router(x2, norm_g[None, :], rw, rb)
    slot_tok, block_e, n_used, dest_by_k = _routing_tables(top_e, rank, counts[0, :N_EXPERTS], TILES["expert"].rows)
    yb = _expert_ffn(h, slot_tok, block_e, n_used, w_gate.astype(BF16), w_up.astype(BF16), w_down.astype(BF16))
    return _combine(x2, gate, yb, dest_by_k, out_g[None, :], final_norm)


def _mixer_layer(x2, bsz, length, norm_g, w_in, b_in, hy_conv_w, hy_conv_b, hy_w1, hy_b1, hy_w2, hy_b2,
                 hy_w3, hy_freq, hy_decay, hy_skip, ml_conv_w, ml_conv_b, ml_norm_g, w_a, w_b, w_o,
                 ctab, stab, twiddle, zpos):
    n, d = x2.shape
    wh = hy_skip.shape[1]
    wm = ml_norm_g.shape[0]
    off_qk = (HYENA_ORDER + 1) * wh
    off_v = off_qk + 2 * wm
    off_o = off_v + wm
    off_gates = off_o + wm
    off_br = off_gates + 4 * MLSTM_HEADS
    w_cat = jnp.concatenate([w_in[:, off_br:], w_in[:, :off_gates]], axis=1).astype(BF16)
    b_cat = jnp.concatenate([b_in[off_br:], b_in[:off_gates]])[None, :]
    col_hy = 2 * d
    col_q = col_hy + off_qk
    col_k = col_q + wm
    col_v = col_hy + off_v
    col_o = col_hy + off_o

    p_big, gates = _in_proj(x2, norm_g[None, :], w_in[:, off_gates:off_br], b_in[None, off_gates:off_br],
                            w_cat, b_cat)
    p3 = p_big.reshape(bsz, length, p_big.shape[1])

    kp = LANES
    pad2 = lambda a, r, c: jnp.zeros((r, c), F32).at[:a.shape[0], :a.shape[1]].set(a)
    zp = pad2(zpos, length, kp)
    htab, hmid = _hyena_spectrum(
        zp, pad2(hy_w1, kp, kp), pad2(hy_b1[None, :], 1, kp), pad2(hy_w2, kp, kp), pad2(hy_b2[None, :], 1, kp),
        pad2(hy_freq[None, :], 1, kp), pad2(hy_w3, kp, hy_w3.shape[1]), hy_decay[None, :],
        ctab, stab, twiddle, wh)
    y_hy = _hyena_conv(p3, col_hy, hy_conv_w, hy_conv_b[None, :], htab, hmid, hy_skip[:, None, :], ctab, stab, wh)

    gcol, grow = _gate_prep(gates.reshape(bsz, length, 4 * MLSTM_HEADS))
    y_ml = _mlstm(p3, col_q, col_k, col_v, col_o, ml_conv_w, ml_conv_b[None, :], ml_norm_g[None, :],
                  gcol, grow)

    return _merge(y_hy.reshape(n, wh), y_ml.reshape(n, wm), p_big, x2,
                  w_a.astype(BF16), w_b.astype(BF16), w_o.astype(BF16))


def kernel(x, mix_norm_g, mix_w_in, mix_b_in, hy_conv_w, hy_conv_b, hy_filt_w1, hy_filt_b1, hy_filt_w2,
           hy_filt_b2, hy_filt_w3, hy_filt_freq, hy_filt_decay, hy_skip, ml_conv_w, ml_conv_b, ml_norm_g,
           mix_w_a, mix_w_b, mix_w_o, ffn_norm_g, dense_w_gate, dense_w_up, dense_w_down, moe_router_w,
           moe_router_b, moe_w_gate, moe_w_up, moe_w_down, final_norm_g):
    bsz, length, d = x.shape
    depth = mix_norm_g.shape[0]
    x2 = x.reshape(bsz * length, d)
    ctab, stab, twiddle = _dft_tables(length // 2)
    zpos = _hyena_positions(length)
    normed = False
    for layer in range(depth):
        x2 = _mixer_layer(x2, bsz, length, mix_norm_g[layer], mix_w_in[layer], mix_b_in[layer],
                          hy_conv_w[layer], hy_conv_b[layer], hy_filt_w1[layer], hy_filt_b1[layer],
                          hy_filt_w2[layer], hy_filt_b2[layer], hy_filt_w3[layer], hy_filt_freq[layer],
                          hy_filt_decay[layer], hy_skip[layer], ml_conv_w[layer], ml_conv_b[layer],
                          ml_norm_g[layer], mix_w_a[layer], mix_w_b[layer], mix_w_o[layer], ctab, stab, twiddle,
                          zpos)
        j = layer // 2
        if layer % 2 == 0:
            x2 = _dense_ffn(x2, ffn_norm_g[layer][None, :], dense_w_gate[j], dense_w_up[j], dense_w_down[j])
        else:
            normed = layer == depth - 1
            x2 = _moe_layer(x2, ffn_norm_g[layer], moe_router_w[j], moe_router_b[j], moe_w_gate[j],
                            moe_w_up[j], moe_w_down[j], final_norm_g, normed)
    if not normed:
        x2 = _final_norm(x2, final_norm_g[None, :])
    return x2.reshape(bsz, length, d)
```

```python
import functools
import math
from typing import NamedTuple

import jax
import jax.numpy as jnp
import numpy as np
from jax import lax
from jax.experimental import pallas as pl
from jax.experimental.pallas import tpu as pltpu

F32, BF16, I32 = jnp.float32, jnp.bfloat16, jnp.int32
HIGHEST = lax.Precision.HIGHEST

HYENA_ORDER = 2
HYENA_POS_BANDS = 16
MLSTM_HEADS = 8
N_EXPERTS = 8
TOP_K = 2
EPS = 1e-6

V7X_VMEM_BYTES = 64 * 1024 * 1024
LANES = 128
SUBLANES = 8
MXU_DIM = 256


class _Tile(NamedTuple):
    rows: int
    cols: int
    vmem_mib: int


TILES = dict(
    in_proj=_Tile(1024, 4 * MXU_DIM, 48),
    hyena_spectrum=_Tile(0, MXU_DIM, 48),
    hyena_conv=_Tile(1024, MXU_DIM, 56),
    mlstm_gate_prep=_Tile(256, 0, 32),
    mlstm=_Tile(256, 0, 40),
    merge=_Tile(512, 0, 60),
    dense=_Tile(1024, 256, 56),
    router=_Tile(512, 0, 32),
    expert=_Tile(512, 512, 52),
    combine=_Tile(512, 0, 48),
    final_norm=_Tile(512, 0, 32),
)


def _cparams(semantics, tile):
    assert tile.vmem_mib * 1024 * 1024 <= V7X_VMEM_BYTES
    return pltpu.CompilerParams(dimension_semantics=semantics, vmem_limit_bytes=tile.vmem_mib * 1024 * 1024)


def _const_spec(shape):
    nd = len(shape)
    return pl.BlockSpec(shape, lambda *_: (0,) * nd, pipeline_mode=pl.Buffered(1))


def _rms(x, g):
    return x * lax.rsqrt(jnp.mean(x * x, axis=-1, keepdims=True) + EPS) * g


def _shift_rows(u, direction):
    n = u.shape[0]
    row = lax.broadcasted_iota(I32, u.shape, 0)
    if direction > 0:
        return jnp.where(row == 0, 0.0, pltpu.roll(u, 1, axis=0))
    return jnp.where(row == n - 1, 0.0, pltpu.roll(u, n - 1, axis=0))


def _short_conv(u, w, b):
    return b + _shift_rows(u, 1) * w[0:1] + u * w[1:2] + _shift_rows(u, -1) * w[2:3]


def _log_sigmoid(x):
    return jnp.minimum(x, 0.0) - jnp.log1p(jnp.exp(-jnp.abs(x)))


def _dot_split(a, w_hi, w_lo):
    a_hi = a.astype(BF16)
    a_lo = (a - a_hi.astype(F32)).astype(BF16)
    return (jnp.dot(a_hi, w_hi, preferred_element_type=F32) + jnp.dot(a_lo, w_hi, preferred_element_type=F32)
            + jnp.dot(a_hi, w_lo, preferred_element_type=F32))


def _split_bf16(w):
    w_hi = w.astype(BF16)
    return w_hi, (w - w_hi.astype(F32)).astype(BF16)


def _in_proj_kernel(x_ref, g_ref, wg_ref, bg_ref, w_ref, b_ref, o_ref, gates_ref, h_s):
    @pl.when(pl.program_id(1) == 0)
    def _():
        h_s[...] = _rms(x_ref[...], g_ref[...]).astype(BF16)
        gates_ref[...] = jnp.dot(h_s[...], wg_ref[...], preferred_element_type=F32) + bg_ref[...]

    acc = jnp.dot(h_s[...], w_ref[...], preferred_element_type=F32) + b_ref[...]
    o_ref[...] = acc.astype(o_ref.dtype)


def _in_proj(x2, g, w_gates, b_gates, w, b, tile=TILES["in_proj"]):
    tm, tn = tile.rows, tile.cols
    n, d = x2.shape
    nc = w.shape[1]
    ng = w_gates.shape[1]
    return pl.pallas_call(
        _in_proj_kernel,
        grid=(n // tm, nc // tn),
        in_specs=[pl.BlockSpec((tm, d), lambda i, j: (i, 0)),
                  pl.BlockSpec((1, d), lambda i, j: (0, 0)),
                  pl.BlockSpec((d, ng), lambda i, j: (0, 0)),
                  pl.BlockSpec((1, ng), lambda i, j: (0, 0)),
                  pl.BlockSpec((d, tn), lambda i, j: (0, j)),
                  pl.BlockSpec((1, tn), lambda i, j: (0, j))],
        out_specs=[pl.BlockSpec((tm, tn), lambda i, j: (i, j)),
                   pl.BlockSpec((tm, ng), lambda i, j: (i, 0))],
        out_shape=[jax.ShapeDtypeStruct((n, nc), BF16), jax.ShapeDtypeStruct((n, ng), F32)],
        scratch_shapes=[pltpu.VMEM((tm, d), BF16)],
        compiler_params=_cparams(("parallel", "arbitrary"), tile),
        name="in_proj",
    )(x2, g, w_gates.astype(BF16), b_gates, w, b)


def _dft_tables(half):
    k = jnp.arange(half, dtype=I32)

    def cos_sin(rows):
        ang = ((rows[:, None] * k[None, :]) % (2 * half)).astype(F32) * (math.pi / half)
        return jnp.cos(ang), jnp.sin(ang)

    split = math.gcd(half, 32)
    c1, s1 = cos_sin(jnp.arange(half // split, dtype=I32) * split)
    c0, s0 = cos_sin(jnp.arange(split, dtype=I32))
    ctab = (c1[:, None, :] * c0[None, :, :] - s1[:, None, :] * s0[None, :, :]).reshape(half, half)
    stab = (s1[:, None, :] * c0[None, :, :] + c1[:, None, :] * s0[None, :, :]).reshape(half, half)
    tw = k.astype(F32)[:, None] * (math.pi / (2 * half))
    twiddle = jnp.stack([jnp.broadcast_to(jnp.cos(tw), (half, LANES)), jnp.broadcast_to(jnp.sin(tw), (half, LANES))])
    return ctab.astype(BF16), stab.astype(BF16), twiddle


def _hyena_positions(length):
    t = np.linspace(0.0, 1.0, length)[:, None]
    n = np.arange(length, dtype=np.float64)[:, None]
    bands = np.linspace(1e-4, HYENA_POS_BANDS - 1, HYENA_POS_BANDS)[None, :]
    ang = (2.0 * np.pi / length) * n * bands
    return jnp.asarray(np.concatenate([t, np.cos(ang), -np.sin(ang)], axis=-1), dtype=F32)


def _split_rows(val, slab_s):
    half = val.shape[0] // 2
    n_slabs = slab_s.shape[0]
    for s in range(n_slabs):
        slab_s[s] = val[:, s * LANES:(s + 1) * LANES]
    pick = lambda p: jnp.concatenate([slab_s[s, pl.ds(p, half, stride=2), :] for s in range(n_slabs)], axis=1)
    return pick(0), pick(1)


def _merge_rows(even, odd, slab_s):
    half = even.shape[0]
    n_slabs = slab_s.shape[0]
    for s in range(n_slabs):
        slab_s[s, pl.ds(0, half, stride=2), :] = even[:, s * LANES:(s + 1) * LANES]
        slab_s[s, pl.ds(1, half, stride=2), :] = odd[:, s * LANES:(s + 1) * LANES]
    return jnp.concatenate([slab_s[s] for s in range(n_slabs)], axis=1)


def _alt_sum(x):
    row = lax.broadcasted_iota(I32, (x.shape[0], 1), 0)
    return jnp.sum(jnp.where(row % 2 == 0, x, -x), axis=0, keepdims=True)


def _hyena_spectrum_kernel(z_ref, w1_ref, b1_ref, w2_ref, b2_ref, fr_ref, w3f_ref, w3b_ref,
                           decf_ref, decb_ref, c_ref, s_ref, tw_ref, tab_ref, mid_ref, slab_s, hid_s):
    half = c_ref.shape[0]
    length = 2 * half
    ct = w3f_ref.shape[1]
    dot_hi = functools.partial(jnp.dot, precision=HIGHEST, preferred_element_type=F32)
    dot = functools.partial(jnp.dot, preferred_element_type=F32)
    z = z_ref[...]

    @pl.when(jnp.logical_and(pl.program_id(0) == 0, pl.program_id(1) == 0))
    def _():
        fr = fr_ref[...]
        hid = jnp.sin(fr * (dot_hi(z, w1_ref[...]) + b1_ref[...]))
        hid_s[...] = jnp.sin(fr * (dot_hi(hid, w2_ref[...]) + b2_ref[...]))

    hid = hid_s[...]
    t = z[:, 0:1]
    ff = dot_hi(hid, w3f_ref[...]) * jnp.exp(-t * jnp.abs(decf_ref[...]))
    fb = dot_hi(hid, w3b_ref[...]) * jnp.exp(-t * jnp.abs(decb_ref[...]))
    row = lax.broadcasted_iota(I32, (length, 1), 0)
    l1 = jnp.sum(jnp.where(row == 0, jnp.abs(ff + fb), jnp.abs(ff) + jnp.abs(fb)), axis=0, keepdims=True)
    inv = 1.0 / l1
    se, so = _split_rows((ff + fb) * inv, slab_s)
    de, do = _split_rows((ff - fb) * inv, slab_s)
    wide = lambda a: jnp.concatenate([a] * (ct // LANES), axis=1)
    cw, sw = wide(tw_ref[0]), wide(tw_ref[1])
    c_t, s_t = c_ref[...], s_ref[...]
    bf = lambda a: a.astype(BF16)
    e_r = dot(c_t, bf(se))
    o_r, o_s = dot(c_t, bf(so)), dot(s_t, bf(so))
    t_r = cw * o_r - sw * o_s
    d_s = dot(s_t, bf(de))
    p_r, p_s = dot(c_t, bf(do)), dot(s_t, bf(do))
    t_i = -(cw * p_s + sw * p_r)
    krow = lax.broadcasted_iota(I32, (half, 1), 0)
    scale = jnp.where(krow == 0, 1.0 / (2 * length), 1.0 / length)
    lo_r, lo_i = (e_r + t_r) * scale, (t_i - d_s) * scale
    hi_r, hi_i = (e_r - t_r) * scale, (t_i + d_s) * scale

    def emit(p, lr, li, hr, hi):
        a_r, a_i = lr + hr, li - hi
        d_r, d_i = lr - hr, li + hi
        tab_ref[4 * p + 0] = a_r.astype(tab_ref.dtype)
        tab_ref[4 * p + 1] = a_i.astype(tab_ref.dtype)
        tab_ref[4 * p + 2] = (d_r * cw + d_i * sw).astype(tab_ref.dtype)
        tab_ref[4 * p + 3] = (d_i * cw - d_r * sw).astype(tab_ref.dtype)

    emit(0, lo_r, lo_i, hi_r, hi_i)
    emit(1, lo_r * cw - lo_i * sw, lo_r * sw + lo_i * cw, -(hi_r * cw + hi_i * sw), -(hi_i * cw - hi_r * sw))
    mid_ref[0:1, :] = _alt_sum(se) * (1.0 / length)
    mid_ref[1:2, :] = -_alt_sum(do) * (1.0 / length)


def _hyena_spectrum(z, w1, b1, w2, b2, freq, w3, decay, ctab, stab, twiddle, width, tile=TILES["hyena_spectrum"]):
    ct = tile.cols
    half = ctab.shape[0]
    kp = w1.shape[1]
    nct = width // ct
    col_f = lambda o, c: (0, o * 2 * nct + c)
    col_b = lambda o, c: (0, o * 2 * nct + nct + c)
    return pl.pallas_call(
        _hyena_spectrum_kernel,
        grid=(HYENA_ORDER, nct),
        in_specs=[_const_spec(z.shape), _const_spec(w1.shape), _const_spec(b1.shape),
                  _const_spec(w2.shape), _const_spec(b2.shape), _const_spec(freq.shape),
                  pl.BlockSpec((kp, ct), col_f), pl.BlockSpec((kp, ct), col_b),
                  pl.BlockSpec((1, ct), col_f), pl.BlockSpec((1, ct), col_b),
                  _const_spec(ctab.shape), _const_spec(stab.shape), _const_spec(twiddle.shape)],
        out_specs=[pl.BlockSpec((None, 8, half, ct), lambda o, c: (o, 0, 0, c)),
                   pl.BlockSpec((None, 2, ct), lambda o, c: (o, 0, c))],
        out_shape=[jax.ShapeDtypeStruct((HYENA_ORDER, 8, half, width), BF16),
                   jax.ShapeDtypeStruct((HYENA_ORDER, 2, width), F32)],
        scratch_shapes=[pltpu.VMEM((ct // LANES, 2 * half, LANES), F32), pltpu.VMEM((2 * half, kp), F32)],
        compiler_params=_cparams(("arbitrary", "arbitrary"), tile),
        name="hyena_spectrum",
    )(z, w1, b1, w2, b2, freq, w3, w3, decay, decay, ctab, stab, twiddle)


def _hyena_conv_kernel(pv_ref, p1_ref, p2_ref, wv_ref, w1_ref, w2_ref, bv_ref, b1_ref, b2_ref,
                       tab_ref, mid_ref, skip_ref, c_ref, s_ref, y_ref,
                       slab_s, z_s, g1_s, g2_s, zb_s, gr_s, gn_s, *, rc):
    half = c_ref.shape[0]
    dot = functools.partial(jnp.dot, preferred_element_type=F32)
    for dst, p_ref, w_ref, b_ref in ((z_s, pv_ref, wv_ref, bv_ref), (g1_s, p1_ref, w1_ref, b1_ref),
                                     (g2_s, p2_ref, w2_ref, b2_ref)):
        even, odd = _split_rows(_short_conv(p_ref[...].astype(F32), w_ref[...], b_ref[...]), slab_s)
        dst[0] = even
        dst[1] = odd
    gates = (g1_s, g2_s)
    row = lax.broadcasted_iota(I32, (half, 1), 0)
    sign = jnp.where(row % 2 == 0, 1.0, -1.0)
    for o in range(HYENA_ORDER):
        zb_s[0] = z_s[0].astype(BF16)
        zb_s[1] = z_s[1].astype(BF16)
        a_e, a_o = _alt_sum(z_s[0]), _alt_sum(z_s[1])
        mid = (mid_ref[o, 0:1, :] * a_e + mid_ref[o, 1:2, :] * a_o,
               mid_ref[o, 0:1, :] * a_o - mid_ref[o, 1:2, :] * a_e)
        for r in range(half // rc):
            rows = slice(r * rc, (r + 1) * rc)
            e_r, e_s = dot(c_ref[rows, :], zb_s[0]).astype(BF16), dot(s_ref[rows, :], zb_s[0]).astype(BF16)
            o_r, o_s = dot(c_ref[rows, :], zb_s[1]).astype(BF16), dot(s_ref[rows, :], zb_s[1]).astype(BF16)
            for p in range(2):
                a_r, a_i = tab_ref[o, 4 * p + 0, rows, :], tab_ref[o, 4 * p + 1, rows, :]
                b_r, b_i = tab_ref[o, 4 * p + 2, rows, :], tab_ref[o, 4 * p + 3, rows, :]
                gr_s[p, rows, :] = (a_r * e_r + a_i * e_s) + (b_r * o_r + b_i * o_s)
                gn_s[p, rows, :] = (a_r * e_s - a_i * e_r) + (b_r * o_s - b_i * o_r)
        for p in range(2):
            for r in range(half // rc):
                rows = slice(r * rc, (r + 1) * rc)
                conv = dot(c_ref[rows, :], gr_s[p]) + dot(s_ref[rows, :], gn_s[p]) + sign[rows] * mid[p]
                z_s[p, rows, :] = gates[o][p, rows, :] * (conv + skip_ref[o] * z_s[p, rows, :])
    y_ref[...] = _merge_rows(z_s[0], z_s[1], slab_s).astype(y_ref.dtype)


def _hyena_conv(p3, col0, conv_w, conv_b, tab, mid, skip, ctab, stab, width, tile=TILES["hyena_conv"]):
    rc, ct = tile.rows, tile.cols
    bsz, length, _ = p3.shape
    half = length // 2
    nct = width // ct
    c0 = col0 // ct

    def pspec(part):
        return pl.BlockSpec((None, length, ct), lambda c, b: (b, 0, c0 + part * nct + c))

    def wspec(rows, part):
        return pl.BlockSpec((rows, ct), lambda c, b: (0, part * nct + c))

    scratch = ([pltpu.VMEM((ct // LANES, length, LANES), F32)] + [pltpu.VMEM((2, half, ct), F32)] * 3
               + [pltpu.VMEM((2, half, ct), BF16)] * 3)
    return pl.pallas_call(
        functools.partial(_hyena_conv_kernel, rc=rc),
        grid=(nct, bsz),
        in_specs=[pspec(0), pspec(1), pspec(2),
                  wspec(3, 0), wspec(3, 1), wspec(3, 2), wspec(1, 0), wspec(1, 1), wspec(1, 2),
                  pl.BlockSpec((HYENA_ORDER, 8, half, ct), lambda c, b: (0, 0, 0, c),
                               pipeline_mode=pl.Buffered(1)),
                  pl.BlockSpec((HYENA_ORDER, 2, ct), lambda c, b: (0, 0, c)),
                  pl.BlockSpec((HYENA_ORDER, 1, ct), lambda c, b: (0, 0, c)),
                  _const_spec(ctab.shape), _const_spec(stab.shape)],
        out_specs=pl.BlockSpec((None, length, ct), lambda c, b: (b, 0, c)),
        out_shape=jax.ShapeDtypeStruct((bsz, length, width), BF16),
        scratch_shapes=scratch,
        compiler_params=_cparams(("parallel", "parallel"), tile),
        name="hyena_conv",
    )(p3, p3, p3, conv_w, conv_w, conv_w, conv_b, conv_b, conv_b, tab, mid, skip, ctab, stab)


def _gate_prep_kernel(ic_ref, fc_ref, ir_ref, fr_ref, col_ref, row_ref, *, chunk):
    length, nch = ic_ref.shape
    nh = nch // 2
    r = lax.broadcasted_iota(I32, (chunk, chunk), 0)
    c = lax.broadcasted_iota(I32, (chunk, chunk), 1)
    lower = (c <= r).astype(F32)
    upper = (c >= r).astype(F32)
    dot_hi = functools.partial(jnp.dot, precision=HIGHEST, preferred_element_type=F32)
    fwd_col = lax.broadcasted_iota(I32, (chunk, nch), 1) < nh
    fwd_row = lax.broadcasted_iota(I32, (nch, chunk), 0) < nh
    pos = lax.broadcasted_iota(I32, (chunk, nch), 0)
    for ch in range(length // chunk):
        rows = slice(ch * chunk, (ch + 1) * chunk)
        lf = _log_sigmoid(fc_ref[rows, :])
        b = jnp.where(fwd_col, dot_hi(lower, lf), dot_hi(upper, lf))
        u = ic_ref[rows, :] - b
        cm_f, cm_b = u, u
        shift = 1
        while shift < chunk:
            cm_f = jnp.maximum(cm_f, jnp.where(pos >= shift, pltpu.roll(cm_f, shift, axis=0), -jnp.inf))
            cm_b = jnp.maximum(cm_b, jnp.where(pos + shift < chunk, pltpu.roll(cm_b, chunk - shift, axis=0), -jnp.inf))
            shift *= 2
        col_ref[0, rows, :] = b
        col_ref[1, rows, :] = u
        col_ref[2, rows, :] = jnp.where(fwd_col, cm_f, cm_b)
        lfr = _log_sigmoid(fr_ref[:, rows])
        b_r = jnp.where(fwd_row, dot_hi(lfr, upper), dot_hi(lfr, lower))
        row_ref[:, rows] = ir_ref[:, rows] - b_r


def _gate_prep(gates3, tile=TILES["mlstm_gate_prep"]):
    chunk = tile.rows
    bsz, length, nch4 = gates3.shape
    nh = nch4 // 4
    i_col = jnp.concatenate([gates3[..., :nh], gates3[..., 2 * nh:3 * nh]], axis=-1)
    f_col = jnp.concatenate([gates3[..., nh:2 * nh], gates3[..., 3 * nh:]], axis=-1)
    nch = 2 * nh
    cspec = pl.BlockSpec((None, length, nch), lambda b: (b, 0, 0))
    rspec = pl.BlockSpec((None, nch, length), lambda b: (b, 0, 0))
    col, row = pl.pallas_call(
        functools.partial(_gate_prep_kernel, chunk=chunk),
        grid=(bsz,),
        in_specs=[cspec, cspec, rspec, rspec],
        out_specs=[pl.BlockSpec((None, 3, length, nch), lambda b: (b, 0, 0, 0)), rspec],
        out_shape=[jax.ShapeDtypeStruct((bsz, 3, length, nch), F32),
                   jax.ShapeDtypeStruct((bsz, nch, length), F32)],
        compiler_params=_cparams(("parallel",), tile),
        name="mlstm_gate_prep",
    )(i_col, f_col, jnp.swapaxes(i_col, 1, 2), jnp.swapaxes(f_col, 1, 2))
    col = col.reshape(bsz, 3, length, 2, nh).transpose(0, 4, 2, 3, 1).reshape(bsz, nh, length, 6)
    row = row.reshape(bsz, 2, nh, length).transpose(0, 2, 1, 3)
    return col, row


def _mlstm_kernel(pq_ref, pk_ref, pv_ref, po_ref, cwq_ref, cwk_ref, cbq_ref, cbk_ref, ng_ref,
                  gcol_ref, grow_ref, y_ref, q_s, k_s, v1_s, tab_s, hf_s, hb_s, *, chunk):
    length, dk = pq_ref.shape
    nc = length // chunk
    rep = chunk // LANES

    def conv_silu(p_ref, w_ref, b_ref):
        c = _short_conv(p_ref[...].astype(F32), w_ref[...], b_ref[...])
        return c * jax.nn.sigmoid(c)

    q_s[...] = (conv_silu(pq_ref, cwq_ref, cbq_ref) * (dk ** -0.5)).astype(BF16)
    k_s[...] = conv_silu(pk_ref, cwk_ref, cbk_ref).astype(BF16)
    v1_s[:, :dk] = pv_ref[...]
    v1_s[:, dk:] = jnp.ones((length, dk), BF16)
    for t in range(tab_s.shape[0]):
        tab_s[t] = jnp.broadcast_to(gcol_ref[:, t:t + 1], (length, LANES))

    row_i = lax.broadcasted_iota(I32, (chunk, chunk), 0)
    col_i = lax.broadcasted_iota(I32, (chunk, chunk), 1)
    wide = lambda a: jnp.concatenate([a] * rep, axis=1)
    both = lambda a: jnp.concatenate([a, a], axis=1)

    def step(c, reverse, state, m):
        d = 1 if reverse else 0
        rows = slice(c * chunk, (c + 1) * chunk)
        edge = c * chunk if reverse else (c + 1) * chunk - 1
        q, k, v1 = q_s[rows, :], k_s[rows, :], v1_s[rows, :]
        b_t, u_t, cm_t = tab_s[3 * d, rows, :], tab_s[3 * d + 1, rows, :], tab_s[3 * d + 2, rows, :]
        u_row = grow_ref[d:d + 1, rows]
        g = tab_s[3 * d, edge:edge + 1, :]
        u_max = tab_s[3 * d + 2, edge:edge + 1, :]
        mask = (col_i >= row_i) if reverse else (col_i <= row_i)
        mm = jnp.maximum(cm_t, m)
        p = jnp.exp(jnp.where(mask, u_row - wide(mm), -jnp.inf))
        qk = lax.dot_general(q, k, (((1,), (1,)), ((), ())), preferred_element_type=F32)
        intra = jnp.dot((qk * p).astype(BF16), v1, preferred_element_type=F32)
        inter = jnp.dot(q, state.astype(BF16), preferred_element_type=F32)
        nd = intra + both(jnp.exp(m - mm)) * inter
        h = nd[:, :dk] / jnp.maximum(jnp.abs(nd[:, dk:]), jnp.exp(-(b_t + mm)))
        kw = (k.astype(F32) * jnp.exp(u_t - u_max)).astype(BF16)
        upd = lax.dot_general(kw, v1, (((0,), (0,)), ((), ())), preferred_element_type=F32)
        m_loc = g + u_max
        m_new = jnp.maximum(g + m, m_loc)
        state = both(jnp.exp(g + m - m_new)) * state + both(jnp.exp(m_loc - m_new)) * upd
        return h, state, m_new

    zero = (jnp.zeros((dk, 2 * dk), F32), jnp.zeros((1, LANES), F32))
    st_f, m_f = zero
    st_b, m_b = zero
    for c in range(nc):
        h, st_f, m_f = step(c, False, st_f, m_f)
        hf_s[c * chunk:(c + 1) * chunk, :] = h
        cb = nc - 1 - c
        h, st_b, m_b = step(cb, True, st_b, m_b)
        hb_s[cb * chunk:(cb + 1) * chunk, :] = h

    ht = hf_s[...] + hb_s[...]
    y_ref[...] = (_rms(ht, ng_ref[...]) * jax.nn.sigmoid(po_ref[...].astype(F32))).astype(y_ref.dtype)


def _mlstm(p3, col_q, col_k, col_v, col_o, conv_w, conv_b, norm_g, gcol, grow, tile=TILES["mlstm"]):
    chunk = tile.rows
    bsz, length, _ = p3.shape
    nh, dh = MLSTM_HEADS, norm_g.shape[1] // MLSTM_HEADS
    assert dh == LANES and chunk % LANES == 0

    def pspec(col):
        return pl.BlockSpec((None, length, dh), lambda b, h: (b, 0, col // dh + h))

    def wspec(rows, part):
        return pl.BlockSpec((rows, dh), lambda b, h: (0, part * nh + h))

    return pl.pallas_call(
        functools.partial(_mlstm_kernel, chunk=chunk),
        grid=(bsz, nh),
        in_specs=[pspec(col_q), pspec(col_k), pspec(col_v), pspec(col_o),
                  wspec(3, 0), wspec(3, 1), wspec(1, 0), wspec(1, 1),
                  pl.BlockSpec((1, dh), lambda b, h: (0, h)),
                  pl.BlockSpec((None, None, length, 6), lambda b, h: (b, h, 0, 0)),
                  pl.BlockSpec((None, None, 2, length), lambda b, h: (b, h, 0, 0))],
        out_specs=pl.BlockSpec((None, length, dh), lambda b, h: (b, 0, h)),
        out_shape=jax.ShapeDtypeStruct((bsz, length, nh * dh), BF16),
        scratch_shapes=[pltpu.VMEM((length, dh), BF16), pltpu.VMEM((length, dh), BF16),
                        pltpu.VMEM((length, 2 * dh), BF16), pltpu.VMEM((6, length, LANES), F32),
                        pltpu.VMEM((length, dh), F32), pltpu.VMEM((length, dh), F32)],
        compiler_params=_cparams(("parallel", "parallel"), tile),
        name="mlstm",
    )(p3, p3, p3, p3, conv_w, conv_w, conv_b, conv_b, norm_g, gcol, grow)


def _merge_kernel(yh_ref, ym_ref, gh_ref, gm_ref, x_ref, wa_ref, wb_ref, wo_ref, o_ref):
    a = jnp.dot(yh_ref[...], wa_ref[...], preferred_element_type=F32)
    b = jnp.dot(ym_ref[...], wb_ref[...], preferred_element_type=F32)
    t = jax.nn.sigmoid(gh_ref[...].astype(F32)) * a + jax.nn.sigmoid(gm_ref[...].astype(F32)) * b
    o_ref[...] = x_ref[...] + jnp.dot(t.astype(BF16), wo_ref[...], preferred_element_type=F32)


def _merge(y_hy, y_ml, p_big, x2, w_a, w_b, w_o, tile=TILES["merge"]):
    tm = tile.rows
    n, d = x2.shape
    wh, wm = y_hy.shape[1], y_ml.shape[1]
    return pl.pallas_call(
        _merge_kernel,
        grid=(n // tm,),
        in_specs=[pl.BlockSpec((tm, wh), lambda i: (i, 0)),
                  pl.BlockSpec((tm, wm), lambda i: (i, 0)),
                  pl.BlockSpec((tm, d), lambda i: (i, 0)),
                  pl.BlockSpec((tm, d), lambda i: (i, 1)),
                  pl.BlockSpec((tm, d), lambda i: (i, 0)),
                  _const_spec(w_a.shape), _const_spec(w_b.shape), _const_spec(w_o.shape)],
        out_specs=pl.BlockSpec((tm, d), lambda i: (i, 0)),
        out_shape=jax.ShapeDtypeStruct((n, d), F32),
        compiler_params=_cparams(("parallel",), tile),
        name="branch_merge",
    )(y_hy, y_ml, p_big, p_big, x2, w_a, w_b, w_o)


def _dense_ffn_kernel(x_ref, g_ref, wg_ref, wu_ref, wd_ref, o_ref, h_s):
    @pl.when(pl.program_id(1) == 0)
    def _():
        x = x_ref[...]
        h_s[...] = _rms(x, g_ref[...]).astype(BF16)
        o_ref[...] = x

    h = h_s[...]
    gt = jnp.dot(h, wg_ref[...].astype(BF16), preferred_element_type=F32)
    up = jnp.dot(h, wu_ref[...].astype(BF16), preferred_element_type=F32)
    act = (gt * jax.nn.sigmoid(gt) * up).astype(BF16)
    o_ref[...] += jnp.dot(act, wd_ref[...].astype(BF16), preferred_element_type=F32)


def _dense_ffn(x2, g, w_gate, w_up, w_down, tile=TILES["dense"]):
    tm, tf = tile.rows, tile.cols
    n, d = x2.shape
    f = w_gate.shape[1]
    return pl.pallas_call(
        _dense_ffn_kernel,
        grid=(n // tm, f // tf),
        in_specs=[pl.BlockSpec((tm, d), lambda i, j: (i, 0)),
                  pl.BlockSpec((1, d), lambda i, j: (0, 0)),
                  pl.BlockSpec((d, tf), lambda i, j: (0, j)),
                  pl.BlockSpec((d, tf), lambda i, j: (0, j)),
                  pl.BlockSpec((tf, d), lambda i, j: (j, 0))],
        out_specs=pl.BlockSpec((tm, d), lambda i, j: (i, 0)),
        out_shape=jax.ShapeDtypeStruct((n, d), F32),
        scratch_shapes=[pltpu.VMEM((tm, d), BF16)],
        compiler_params=_cparams(("parallel", "arbitrary"), tile),
        name="dense_swiglu",
    )(x2, g, w_gate, w_up, w_down)


def _store_rows_3d(ref3, val2):
    for s in range(ref3.shape[1]):
        ref3[:, s, :] = val2[:, s * LANES:(s + 1) * LANES]


def _router_kernel(x_ref, g_ref, rwh_ref, rwl_ref, rb_ref, h_ref, e_ref, gate_ref, rank_ref, cnt_ref, base_s):
    @pl.when(pl.program_id(0) == 0)
    def _():
        base_s[...] = jnp.zeros_like(base_s)

    hn = _rms(x_ref[...], g_ref[...])
    _store_rows_3d(h_ref, hn)
    logits = _dot_split(hn, rwh_ref[...], rwl_ref[...]) + rb_ref[...]
    lane = lax.broadcasted_iota(I32, logits.shape, 1)
    logits = jnp.where(lane < N_EXPERTS, logits, -jnp.inf)
    m1 = jnp.max(logits, axis=-1, keepdims=True)
    i1 = jnp.min(jnp.where(logits == m1, lane, LANES), axis=-1, keepdims=True)
    rest = jnp.where(lane == i1, -jnp.inf, logits)
    m2 = jnp.max(rest, axis=-1, keepdims=True)
    i2 = jnp.min(jnp.where(rest == m2, lane, LANES), axis=-1, keepdims=True)
    e = jnp.exp(m2 - m1)
    slot = lax.broadcasted_iota(I32, e_ref.shape, 1)
    e_ref[...] = jnp.where(slot == 0, i1, i2)
    gate_ref[...] = jnp.where(slot == 0, 1.0 / (1.0 + e), e / (1.0 + e))
    tm = logits.shape[0]
    oh1 = (lane == i1).astype(F32)
    oh2 = (lane == i2).astype(F32)
    both = oh1 + oh2
    earlier = (lax.broadcasted_iota(I32, (tm, tm), 1) < lax.broadcasted_iota(I32, (tm, tm), 0)).astype(BF16)
    before = jnp.dot(earlier, both.astype(BF16), preferred_element_type=F32) + base_s[...]
    r1 = jnp.sum(before * oh1, axis=-1, keepdims=True)
    r2 = jnp.sum(before * oh2, axis=-1, keepdims=True)
    rank_ref[...] = jnp.where(slot == 0, r1, r2).astype(I32)
    base_s[...] += jnp.sum(both, axis=0, keepdims=True)
    cnt_ref[...] = base_s[...].astype(I32)


def _router(x2, g, rw_pad, rb_pad, tile=TILES["router"]):
    tm = tile.rows
    n, d = x2.shape
    s = d // LANES
    rw_hi, rw_lo = _split_bf16(rw_pad)
    return pl.pallas_call(
        _router_kernel,
        grid=(n // tm,),
        in_specs=[pl.BlockSpec((tm, d), lambda i: (i, 0)),
                  pl.BlockSpec((1, d), lambda i: (0, 0)),
                  pl.BlockSpec((d, LANES), lambda i: (0, 0)),
                  pl.BlockSpec((d, LANES), lambda i: (0, 0)),
                  pl.BlockSpec((1, LANES), lambda i: (0, 0))],
        out_specs=[pl.BlockSpec((tm, s, LANES), lambda i: (i, 0, 0)),
                   pl.BlockSpec((tm, TOP_K), lambda i: (i, 0)),
                   pl.BlockSpec((tm, TOP_K), lambda i: (i, 0)),
                   pl.BlockSpec((tm, TOP_K), lambda i: (i, 0)),
                   pl.BlockSpec((1, LANES), lambda i: (0, 0))],
        out_shape=[jax.ShapeDtypeStruct((n, s, LANES), F32),
                   jax.ShapeDtypeStruct((n, TOP_K), I32),
                   jax.ShapeDtypeStruct((n, TOP_K), F32),
                   jax.ShapeDtypeStruct((n, TOP_K), I32),
                   jax.ShapeDtypeStruct((1, LANES), I32)],
        scratch_shapes=[pltpu.VMEM((1, LANES), F32)],
        compiler_params=_cparams(("arbitrary",), tile),
        name="moe_router",
    )(x2, g, rw_hi, rw_lo, rb_pad)


def _slab_pitch(s):
    return s if (s // SUBLANES) % 2 == 1 else s + SUBLANES


def _row_gather(src_hbm, idx_ref, idx_base, dst, sem, rows):
    slab = src_hbm.shape[1]

    def copy(r):
        return pltpu.make_async_copy(src_hbm.at[idx_ref[idx_base + r]], dst.at[r, pl.ds(0, slab)], sem)

    def start_one(r, carry):
        copy(r).start()
        return carry

    def wait_one(r, carry):
        copy(r).wait()
        return carry

    return (lambda: lax.fori_loop(0, rows, start_one, 0, unroll=8),
            lambda: lax.fori_loop(0, rows, wait_one, 0, unroll=8))


def _expert_ffn_kernel(be_ref, nu_ref, tok_ref, h_hbm, wg_ref, wu_ref, wd_ref, o_ref,
                       xg_s, x2_s, h_s, acc_s, sems):
    i, j = pl.program_id(0), pl.program_id(1)
    blk = xg_s.shape[1]
    n_used = nu_ref[0]
    last = pl.num_programs(1) - 1
    mid = pl.num_programs(1) // 2
    used = i < n_used
    has_next = i + 1 < n_used
    both = jnp.logical_and

    def gather(block):
        slot = block % 2
        return _row_gather(h_hbm, tok_ref, block * blk, xg_s.at[slot], sems.at[slot], blk)

    def unpack(block):
        slot = block % 2
        for s in range(h_hbm.shape[1]):
            x2_s[:, s * LANES:(s + 1) * LANES] = xg_s[slot, :, s, :]
        h_s[slot] = x2_s[...].astype(BF16)

    def partial_out():
        h = h_s[i % 2]
        gt = jnp.dot(h, wg_ref[...], preferred_element_type=F32)
        up = jnp.dot(h, wu_ref[...], preferred_element_type=F32)
        act = (gt * jax.nn.sigmoid(gt) * up).astype(BF16)
        return jnp.dot(act, wd_ref[...], preferred_element_type=F32)

    @pl.when(both(i == 0, j == 0))
    def _():
        start, wait = gather(0)
        start()
        wait()
        unpack(0)

    @pl.when(both(j == 0, has_next))
    def _():
        gather(i + 1)[0]()

    @pl.when(both(used, j == 0))
    def _():
        acc_s[...] = partial_out()

    @pl.when(both(used, both(both(j > 0, j < last), jnp.logical_or(j != mid, jnp.logical_not(has_next)))))
    def _():
        acc_s[...] += partial_out()

    @pl.when(both(both(used, has_next), j == mid))
    def _():
        gather(i + 1)[1]()
        unpack(i + 1)
        acc_s[...] += partial_out()

    @pl.when(both(used, j == last))
    def _():
        res = acc_s[...] + partial_out()
        for s in range(o_ref.shape[1]):
            o_ref[:, s, :] = res[:, s * LANES:(s + 1) * LANES]

    @pl.when(both(jnp.logical_not(used), j == last))
    def _():
        o_ref[...] = jnp.zeros_like(o_ref)


def _expert_ffn(h3, slot_tok, block_e, n_used, w_gate, w_up, w_down, tile=TILES["expert"]):
    blk, tf = tile.rows, tile.cols
    ns = slot_tok.shape[0]
    s = h3.shape[1]
    d = s * LANES
    f = w_gate.shape[2]
    nj = f // tf
    assert nj >= 3

    def jj(i, j, nu):
        return jnp.where(i < nu[0], j, nj - 1)

    return pl.pallas_call(
        _expert_ffn_kernel,
        grid_spec=pltpu.PrefetchScalarGridSpec(
            num_scalar_prefetch=3,
            grid=(ns // blk, nj),
            in_specs=[pl.BlockSpec(memory_space=pl.ANY),
                      pl.BlockSpec((None, d, tf), lambda i, j, be, nu, tok: (be[i], 0, jj(i, j, nu))),
                      pl.BlockSpec((None, d, tf), lambda i, j, be, nu, tok: (be[i], 0, jj(i, j, nu))),
                      pl.BlockSpec((None, tf, d), lambda i, j, be, nu, tok: (be[i], jj(i, j, nu), 0))],
            out_specs=pl.BlockSpec((blk, s, LANES), lambda i, j, be, nu, tok: (i, 0, 0)),
            scratch_shapes=[pltpu.VMEM((2, blk, _slab_pitch(s), LANES), F32), pltpu.VMEM((blk, d), F32),
                            pltpu.VMEM((2, blk, d), BF16), pltpu.VMEM((blk, d), F32),
                            pltpu.SemaphoreType.DMA((2,))],
        ),
        out_shape=jax.ShapeDtypeStruct((ns, s, LANES), F32),
        compiler_params=_cparams(("arbitrary", "arbitrary"), tile),
        name="expert_swiglu",
    )(block_e, n_used, slot_tok, h3, w_gate, w_up, w_down)


def _combine_kernel(dest_ref, x_ref, gate_ref, y_hbm, g_ref, o_ref, yg_s, sems, *, final_norm):
    i, n_tiles = pl.program_id(0), pl.num_programs(0)
    tm = x_ref.shape[0]

    def gather(tile):
        slot = tile % 2
        parts = [_row_gather(y_hbm, dest_ref, k * n_tiles * tm + tile * tm, yg_s.at[slot, k], sems.at[slot], tm)
                 for k in range(TOP_K)]
        return (lambda: [p[0]() for p in parts]), (lambda: [p[1]() for p in parts])

    @pl.when(i == 0)
    def _():
        gather(0)[0]()

    @pl.when(i + 1 < n_tiles)
    def _():
        gather(i + 1)[0]()

    gather(i)[1]()
    slot = i % 2
    gates = [jnp.broadcast_to(gate_ref[:, k:k + 1], (tm, LANES)) for k in range(TOP_K)]
    for s in range(y_hbm.shape[1]):
        cols = slice(s * LANES, (s + 1) * LANES)
        acc = x_ref[:, cols]
        for k in range(TOP_K):
            acc = acc + gates[k] * yg_s[slot, k, :, s, :]
        o_ref[:, cols] = acc
    if final_norm:
        o_ref[...] = _rms(o_ref[...], g_ref[...])


def _combine(x2, gate, yb, dest_by_k, g, final_norm, tile=TILES["combine"]):
    tm = tile.rows
    n, d = x2.shape
    s = d // LANES
    return pl.pallas_call(
        functools.partial(_combine_kernel, final_norm=final_norm),
        grid_spec=pltpu.PrefetchScalarGridSpec(
            num_scalar_prefetch=1,
            grid=(n // tm,),
            in_specs=[pl.BlockSpec((tm, d), lambda i, dest: (i, 0)),
                      pl.BlockSpec((tm, TOP_K), lambda i, dest: (i, 0)),
                      pl.BlockSpec(memory_space=pl.ANY),
                      pl.BlockSpec((1, d), lambda i, dest: (0, 0))],
            out_specs=pl.BlockSpec((tm, d), lambda i, dest: (i, 0)),
            scratch_shapes=[pltpu.VMEM((2, TOP_K, tm, _slab_pitch(s), LANES), F32),
                            pltpu.SemaphoreType.DMA((2,))],
        ),
        out_shape=jax.ShapeDtypeStruct((n, d), F32),
        compiler_params=_cparams(("arbitrary",), tile),
        name="moe_combine",
    )(dest_by_k, x2, gate, yb, g)


def _final_norm_kernel(x_ref, g_ref, o_ref):
    o_ref[...] = _rms(x_ref[...], g_ref[...])


def _final_norm(x2, g, tile=TILES["final_norm"]):
    tm = tile.rows
    n, d = x2.shape
    return pl.pallas_call(
        _final_norm_kernel,
        grid=(n // tm,),
        in_specs=[pl.BlockSpec((tm, d), lambda i: (i, 0)), pl.BlockSpec((1, d), lambda i: (0, 0))],
        out_specs=pl.BlockSpec((tm, d), lambda i: (i, 0)),
        out_shape=jax.ShapeDtypeStruct((n, d), F32),
        compiler_params=_cparams(("parallel",), tile),
        name="final_norm",
    )(x2, g)


def _routing_tables(top_e, rank, counts, blk):
    n = top_e.shape[0]
    nk = n * TOP_K
    flat_e = top_e.reshape(-1)
    padded = (counts + blk - 1) // blk * blk
    p_end = jnp.cumsum(padded)
    p_start = p_end - padded
    dest = (p_start[flat_e] + rank.reshape(-1)).astype(I32)
    n_blocks = -(-nk // blk) + N_EXPERTS
    flat_tok = jnp.arange(nk, dtype=I32) // TOP_K
    slot_tok = jnp.zeros((n_blocks * blk,), I32).at[dest].set(flat_tok, unique_indices=True)
    n_used = (p_end[-1] // blk).astype(I32)
    blocks = jnp.arange(n_blocks, dtype=I32)
    block_e = jnp.sum((p_end[None, :] <= (blocks * blk)[:, None]).astype(I32), axis=1)
    block_e = jnp.minimum(block_e, N_EXPERTS - 1)
    block_e = jnp.where(blocks < n_used, block_e, block_e[n_used - 1])
    dest_by_k = dest.reshape(n, TOP_K).T.reshape(-1)
    return slot_tok, block_e, n_used.reshape(1), dest_by_k


def _moe_layer(x2, norm_g, router_w, router_b, w_gate, w_up, w_down, out_g, final_norm):
    n, d = x2.shape
    rw = jnp.zeros((d, LANES), F32).at[:, :N_EXPERTS].set(router_w)
    rb = jnp.zeros((1, LANES), F32).at[0, :N_EXPERTS].set(router_b)
    h, top_e, gate, rank, counts = _router(x2, norm_g[None, :], rw, rb)
    slot_tok, block_e, n_used, dest_by_k = _routing_tables(top_e, rank, counts[0, :N_EXPERTS], TILES["expert"].rows)
    yb = _expert_ffn(h, slot_tok, block_e, n_used, w_gate.astype(BF16), w_up.astype(BF16), w_down.astype(BF16))
    return _combine(x2, gate, yb, dest_by_k, out_g[None, :], final_norm)


def _mixer_layer(x2, bsz, length, norm_g, w_in, b_in, hy_conv_w, hy_conv_b, hy_w1, hy_b1, hy_w2, hy_b2,
                 hy_w3, hy_freq, hy_decay, hy_skip, ml_conv_w, ml_conv_b, ml_norm_g, w_a, w_b, w_o,
                 ctab, stab, twiddle, zpos):
    n, d = x2.shape
    wh = hy_skip.shape[1]
    wm = ml_norm_g.shape[0]
    off_qk = (HYENA_ORDER + 1) * wh
    off_v = off_qk + 2 * wm
    off_o = off_v + wm
    off_gates = off_o + wm
    off_br = off_gates + 4 * MLSTM_HEADS
    w_cat = jnp.concatenate([w_in[:, off_br:], w_in[:, :off_gates]], axis=1).astype(BF16)
    b_cat = jnp.concatenate([b_in[off_br:], b_in[:off_gates]])[None, :]
    col_hy = 2 * d
    col_q = col_hy + off_qk
    col_k = col_q + wm
    col_v = col_hy + off_v
    col_o = col_hy + off_o

    p_big, gates = _in_proj(x2, norm_g[None, :], w_in[:, off_gates:off_br], b_in[None, off_gates:off_br],
                            w_cat, b_cat)
    p3 = p_big.reshape(bsz, length, p_big.shape[1])

    kp = LANES
    pad2 = lambda a, r, c: jnp.zeros((r, c), F32).at[:a.shape[0], :a.shape[1]].set(a)
    zp = pad2(zpos, length, kp)
    htab, hmid = _hyena_spectrum(
        zp, pad2(hy_w1, kp, kp), pad2(hy_b1[None, :], 1, kp), pad2(hy_w2, kp, kp), pad2(hy_b2[None, :], 1, kp),
        pad2(hy_freq[None, :], 1, kp), pad2(hy_w3, kp, hy_w3.shape[1]), hy_decay[None, :],
        ctab, stab, twiddle, wh)
    y_hy = _hyena_conv(p3, col_hy, hy_conv_w, hy_conv_b[None, :], htab, hmid, hy_skip[:, None, :], ctab, stab, wh)

    gcol, grow = _gate_prep(gates.reshape(bsz, length, 4 * MLSTM_HEADS))
    y_ml = _mlstm(p3, col_q, col_k, col_v, col_o, ml_conv_w, ml_conv_b[None, :], ml_norm_g[None, :],
                  gcol, grow)

    return _merge(y_hy.reshape(n, wh), y_ml.reshape(n, wm), p_big, x2,
                  w_a.astype(BF16), w_b.astype(BF16), w_o.astype(BF16))


def kernel(x, mix_norm_g, mix_w_in, mix_b_in, hy_conv_w, hy_conv_b, hy_filt_w1, hy_filt_b1, hy_filt_w2,
           hy_filt_b2, hy_filt_w3, hy_filt_freq, hy_filt_decay, hy_skip, ml_conv_w, ml_conv_b, ml_norm_g,
           mix_w_a, mix_w_b, mix_w_o, ffn_norm_g, dense_w_gate, dense_w_up, dense_w_down, moe_router_w,
           moe_router_b, moe_w_gate, moe_w_up, moe_w_down, final_norm_g):
    bsz, length, d = x.shape
    depth = mix_norm_g.shape[0]
    x2 = x.reshape(bsz * length, d)
    ctab, stab, twiddle = _dft_tables(length // 2)
    zpos = _hyena_positions(length)
    normed = False
    for layer in range(depth):
        x2 = _mixer_layer(x2, bsz, length, mix_norm_g[layer], mix_w_in[layer], mix_b_in[layer],
                          hy_conv_w[layer], hy_conv_b[layer], hy_filt_w1[layer], hy_filt_b1[layer],
                          hy_filt_w2[layer], hy_filt_b2[layer], hy_filt_w3[layer], hy_filt_freq[layer],
                          hy_filt_decay[layer], hy_skip[layer], ml_conv_w[layer], ml_conv_b[layer],
                          ml_norm_g[layer], mix_w_a[layer], mix_w_b[layer], mix_w_o[layer], ctab, stab, twiddle,
                          zpos)
        j = layer // 2
        if layer % 2 == 0:
            x2 = _dense_ffn(x2, ffn_norm_g[layer][None, :], dense_w_gate[j], dense_w_up[j], dense_w_down[j])
        else:
            normed = layer == depth - 1
            x2 = _moe_layer(x2, ffn_norm_g[layer], moe_router_w[j], moe_router_b[j], moe_w_gate[j],
                            moe_w_up[j], moe_w_down[j], final_norm_g, normed)
    if not normed:
        x2 = _final_norm(x2, final_norm_g[None, :])
    return x2.reshape(bsz, length, d)
```
